```python
import jax
import jax.numpy as jnp
from jax import lax
import numpy as np

D_MODEL = 2048
BATCH = 2
SEQ = 8192
DEPTH = 1

CHUNK = 64
EPS = 1e-6
NEG_INF = -1e30
GMLP_BLOCK = 128
GMLP_GROUPS = 8
GMLP_WIDTH = 1024
GMLP_GROUP_DIM = GMLP_WIDTH // GMLP_GROUPS
ATT_HEADS = 16
ATT_HEAD_DIM = 64
ATT_WIDTH = ATT_HEADS * ATT_HEAD_DIM
LEFT_CHUNKS = 8
BAND = LEFT_CHUNKS + 1
MAX_REL = 256
N_BRANCH = 2
IN_COLS = 2 * GMLP_WIDTH + 3 * ATT_WIDTH + N_BRANCH * D_MODEL
N_GROUPS = 4
EXPERTS_PER_GROUP = 8
TOP_K = 2
D_EXPERT = 512
N_MOD = 6

kernel_name = 'hybrid_gmlp_bandattn_hmoe_adaln'


def _rmsnorm(x, g):
    xf = x.astype(jnp.float32)
    y = xf * lax.rsqrt(jnp.mean(xf * xf, axis=-1, keepdims=True) + EPS)
    return (y * g.astype(jnp.float32)).astype(x.dtype)


def _layernorm(x, g, b):
    xf = x.astype(jnp.float32)
    mu = jnp.mean(xf, axis=-1, keepdims=True)
    var = jnp.mean(jnp.square(xf - mu), axis=-1, keepdims=True)
    y = (xf - mu) * lax.rsqrt(var + EPS)
    return (y * g.astype(jnp.float32) + b.astype(jnp.float32)).astype(x.dtype)


def _gmlp_mask():
    pos = np.arange(GMLP_BLOCK)
    return (pos[None, :] // CHUNK) <= (pos[:, None] // CHUNK)


def _rel_index():
    i = np.arange(CHUNK)[:, None, None]
    j = np.arange(BAND)[None, :, None]
    l = np.arange(CHUNK)[None, None, :]
    dist = (LEFT_CHUNKS - j) * CHUNK + i - l
    return (np.clip(dist, -MAX_REL, MAX_REL) + MAX_REL).reshape(CHUNK, BAND * CHUNK).astype(np.int32)


def _spatial_gating(u, v, w_s, b_s, ln_g, ln_b):
    b, s, _ = v.shape
    nb = s // GMLP_BLOCK
    v = _layernorm(v, ln_g, ln_b).reshape(b, nb, GMLP_BLOCK, GMLP_GROUPS, GMLP_GROUP_DIM)
    w = jnp.where(jnp.asarray(_gmlp_mask())[None], w_s, jnp.zeros_like(w_s))
    mixed = jnp.einsum('gts,bnsgc->bntgc', w, v) + b_s.T[:, :, None]
    return u * mixed.reshape(b, s, GMLP_WIDTH)


def _band_attention(q, k, v, rel_table):
    b, s, h, dh = q.shape
    nc = s // CHUNK
    q = q.reshape(b, nc, CHUNK, h, dh)
    pad = ((0, 0), (LEFT_CHUNKS, 0), (0, 0), (0, 0), (0, 0))
    kp = jnp.pad(k.reshape(b, nc, CHUNK, h, dh), pad)
    vp = jnp.pad(v.reshape(b, nc, CHUNK, h, dh), pad)
    scores = jnp.concatenate(
        [jnp.einsum('bcqhd,bckhd->bhcqk', q, kp[:, j:j + nc]) for j in range(BAND)],
        axis=-1).astype(jnp.float32)
    bias = rel_table.astype(jnp.float32)[:, jnp.asarray(_rel_index())]
    scores = scores * (dh ** -0.5) + bias[None, :, None]
    key_chunk = jnp.arange(nc)[:, None] + jnp.arange(BAND)[None, :] - LEFT_CHUNKS
    key_valid = jnp.repeat(key_chunk >= 0, CHUNK, axis=1)
    scores = jnp.where(key_valid[None, None, :, None, :], scores, NEG_INF)
    probs = jax.nn.softmax(scores, axis=-1).astype(v.dtype).reshape(b, h, nc, CHUNK, BAND, CHUNK)
    out = jnp.einsum('bhcqk,bckhd->bcqhd', probs[:, :, :, :, 0], vp[:, 0:nc])
    for j in range(1, BAND):
        out = out + jnp.einsum('bhcqk,bckhd->bcqhd', probs[:, :, :, :, j], vp[:, j:j + nc])
    return out.reshape(b, s, h * dh)


def _hierarchical_moe(h, w_group, w_expert, w1, w3, w2):
    g_logits = (h @ w_group).astype(jnp.float32)
    g_probs = jax.nn.softmax(g_logits, axis=-1)
    _, g_idx = lax.top_k(g_logits, 1)
    g_onehot = jax.nn.one_hot(g_idx[:, 0], N_GROUPS, dtype=jnp.float32)
    g_weight = jnp.sum(g_probs * g_onehot, axis=-1, keepdims=True)
    e_logits = jnp.einsum('nd,gde->nge', h, w_expert).astype(jnp.float32)
    e_sel = jnp.einsum('nge,ng->ne', e_logits, g_onehot)
    e_top, e_idx = lax.top_k(e_sel, TOP_K)
    e_w = jax.nn.softmax(e_top, axis=-1)
    e_weight = jnp.einsum('nk,nke->ne', e_w, jax.nn.one_hot(e_idx, EXPERTS_PER_GROUP, dtype=jnp.float32))
    combine = ((g_weight * g_onehot)[:, :, None] * e_weight[:, None, :]).astype(h.dtype)
    out = jnp.zeros_like(h)
    for g in range(N_GROUPS):
        a = jnp.einsum('nd,edf->nef', h, w1[g])
        bgate = jnp.einsum('nd,edf->nef', h, w3[g])
        act = jax.nn.silu(a) * bgate * combine[:, g, :, None]
        out = out + jnp.einsum('nef,efd->nd', act, w2[g])
    return out


def setup_inputs(seed: int = 0) -> dict:
    key = jax.random.key(seed)
    ks = jax.random.split(key, 24)
    f32 = jnp.float32
    L, D = DEPTH, D_MODEL

    def nrm(k, shape, scale):
        return jax.random.normal(k, shape, f32) * scale

    return {
        'x': nrm(ks[0], (BATCH, SEQ, D), 1.0),
        'c': nrm(ks[1], (BATCH, D), 1.0),
        'ada_w': nrm(ks[2], (L, D, N_MOD * D), 0.5 * D ** -0.5),
        'ada_b': nrm(ks[3], (L, N_MOD * D), 0.02),
        'norm1_g': 1.0 + nrm(ks[4], (L, D), 0.1),
        'w_in': nrm(ks[5], (L, D, IN_COLS), D ** -0.5),
        'gmlp_ln_g': 1.0 + nrm(ks[6], (L, GMLP_WIDTH), 0.1),
        'gmlp_ln_b': nrm(ks[7], (L, GMLP_WIDTH), 0.02),
        'gmlp_w_s': nrm(ks[8], (L, GMLP_GROUPS, GMLP_BLOCK, GMLP_BLOCK), GMLP_BLOCK ** -0.5),
        'gmlp_b_s': 1.0 + nrm(ks[9], (L, GMLP_GROUPS, GMLP_BLOCK), 0.1),
        'rel_bias': nrm(ks[10], (L, ATT_HEADS, 2 * MAX_REL + 1), 0.5),
        'w_branch_a': nrm(ks[11], (L, GMLP_WIDTH, D), GMLP_WIDTH ** -0.5),
        'w_branch_b': nrm(ks[12], (L, ATT_WIDTH, D), ATT_WIDTH ** -0.5),
        'w_out': nrm(ks[13], (L, D, D), D ** -0.5),
        'norm2_g': 1.0 + nrm(ks[14], (L, D), 0.1),
        'w_group': nrm(ks[15], (L, D, N_GROUPS), D ** -0.5),
        'w_expert': nrm(ks[16], (L, N_GROUPS, D, EXPERTS_PER_GROUP), D ** -0.5),
        'w1': nrm(ks[17], (L, N_GROUPS, EXPERTS_PER_GROUP, D, D_EXPERT), D ** -0.5),
        'w3': nrm(ks[18], (L, N_GROUPS, EXPERTS_PER_GROUP, D, D_EXPERT), D ** -0.5),
        'w2': nrm(ks[19], (L, N_GROUPS, EXPERTS_PER_GROUP, D_EXPERT, D), D_EXPERT ** -0.5),
        'final_g': 1.0 + nrm(ks[20], (D,), 0.1),
    }


def reference(x, c, ada_w, ada_b, norm1_g, w_in, gmlp_ln_g, gmlp_ln_b, gmlp_w_s, gmlp_b_s,
              rel_bias, w_branch_a, w_branch_b, w_out, norm2_g, w_group, w_expert, w1, w3, w2,
              final_g):
    b, s, d = x.shape
    cut = [GMLP_WIDTH, 2 * GMLP_WIDTH, 2 * GMLP_WIDTH + ATT_WIDTH,
           2 * GMLP_WIDTH + 2 * ATT_WIDTH, 2 * GMLP_WIDTH + 3 * ATT_WIDTH]
    for layer in range(DEPTH):
        mod = jax.nn.silu(c) @ ada_w[layer] + ada_b[layer]
        shift1, scale1, gate1, shift2, scale2, gate2 = jnp.split(mod[:, None, :], N_MOD, axis=-1)

        h = _rmsnorm(x, norm1_g[layer]) * (1 + scale1) + shift1
        proj = h @ w_in[layer]
        u_a, v_a, q, k, v_b, gates = jnp.split(proj, cut, axis=-1)
        y_a = _spatial_gating(jax.nn.gelu(u_a, approximate=False), jax.nn.gelu(v_a, approximate=False),
                              gmlp_w_s[layer], gmlp_b_s[layer], gmlp_ln_g[layer], gmlp_ln_b[layer])
        y_a = y_a @ w_branch_a[layer]
        hs = (b, s, ATT_HEADS, ATT_HEAD_DIM)
        y_b = _band_attention(q.reshape(hs), k.reshape(hs), v_b.reshape(hs), rel_bias[layer])
        y_b = y_b @ w_branch_b[layer]
        gate_a, gate_b = jnp.split(jax.nn.sigmoid(gates), N_BRANCH, axis=-1)
        mixed = (gate_a * y_a + gate_b * y_b) @ w_out[layer]
        x = x + gate1 * mixed

        h = _rmsnorm(x, norm2_g[layer]) * (1 + scale2) + shift2
        y = _hierarchical_moe(h.reshape(b * s, d), w_group[layer], w_expert[layer],
                              w1[layer], w3[layer], w2[layer]).reshape(b, s, d)
        x = x + gate2 * y
    return _rmsnorm(x, final_g)
```

```python
import functools

import numpy as np
import jax
import jax.numpy as jnp
from jax import lax
from jax.experimental import pallas as pl
from jax.experimental.pallas import tpu as pltpu

F32 = jnp.float32
BF16 = jnp.bfloat16
I32 = jnp.int32

D_MODEL = 2048
CHUNK = 64
EPS = 1e-6
NEG_INF = -1e30
GMLP_BLOCK = 128
GMLP_GROUPS = 8
GMLP_WIDTH = 1024
ATT_HEADS = 16
ATT_HEAD_DIM = 64
ATT_WIDTH = ATT_HEADS * ATT_HEAD_DIM
LEFT_CHUNKS = 8
MAX_REL = 256
N_GROUPS = 4
EXPERTS_PER_GROUP = 8
N_EXPERTS = N_GROUPS * EXPERTS_PER_GROUP
D_EXPERT = 512
N_MOD = 6

LANES = 128
SUBLANES = 8
VMEM_LIMIT = 56 * 1024 * 1024

COL_TILE = 1024
PROJ_COLS = 2 * D_MODEL + 2 * GMLP_WIDTH + 3 * ATT_WIDTH
GATE_TILES = 2 * D_MODEL // COL_TILE
U_TILE = GATE_TILES
V_TILE = GATE_TILES + 1
Q_TILE = GATE_TILES + 2
K_TILE = GATE_TILES + 3
VB_TILE = GATE_TILES + 4

ROUTER_ROWS = SUBLANES + N_EXPERTS

ATT_QBLK = 512
ATT_SUB = 2 * CHUNK
ATT_WIN = ATT_SUB + LEFT_CHUNKS * CHUNK

EXPERT_TILE = 256


def _params(sem, vmem=VMEM_LIMIT):
    return pltpu.CompilerParams(dimension_semantics=sem, vmem_limit_bytes=vmem)


def _rms_mod(x, g, scale, shift):
    y = x * lax.rsqrt(jnp.mean(x * x, axis=-1, keepdims=True) + EPS)
    return (y * g) * (1.0 + scale) + shift


def _ada_kernel(cb_ref, w_ref, b_ref, o_ref):
    nb, d, _ = cb_ref.shape
    tn = w_ref.shape[1]
    reps = tn // LANES

    def body(i, accs):
        r = pl.ds(pl.multiple_of(i * SUBLANES, SUBLANES), SUBLANES)
        w8 = w_ref[r, :]
        out = []
        for b in range(nb):
            cb = cb_ref[b, r, :]
            s = cb * jax.nn.sigmoid(cb)
            out.append(accs[b] + w8 * jnp.concatenate([s] * reps, axis=1))
        return tuple(out)

    init = tuple(jnp.zeros((SUBLANES, tn), F32) for _ in range(nb))
    accs = lax.fori_loop(0, d // SUBLANES, body, init)
    for b in range(nb):
        o_ref[b:b + 1, :] = jnp.sum(accs[b], axis=0, keepdims=True) + b_ref[...]


def _ada(c, ada_w, ada_b):
    nb, d = c.shape
    n = ada_w.shape[1]
    tn = 1024
    cb = jnp.broadcast_to(c[:, :, None], (nb, d, LANES))
    return pl.pallas_call(
        _ada_kernel,
        grid=(n // tn,),
        in_specs=[pl.BlockSpec((nb, d, LANES), lambda j: (0, 0, 0)),
                  pl.BlockSpec((d, tn), lambda j: (0, j)),
                  pl.BlockSpec((1, tn), lambda j: (0, j))],
        out_specs=pl.BlockSpec((nb, tn), lambda j: (0, j)),
        out_shape=jax.ShapeDtypeStruct((nb, n), F32),
        compiler_params=_params(("arbitrary",)),
        name="ada",
    )(cb, ada_w, ada_b.reshape(1, n))


IN_TM = 1024
IN_RB = 256


def _gelu(a):
    return 0.5 * a * (1.0 + lax.erf(a * np.float32(np.sqrt(0.5))))


def _in_proj_kernel(mod_ref, g_ref, x_ref, w_ref, o_ref, h_scr):
    j = pl.program_id(1)
    nrb = IN_TM // IN_RB

    @pl.when(j == 0)
    def _():
        shift = mod_ref[0, 0:1, :]
        scale = mod_ref[0, 1:2, :]
        g = g_ref[...]

        def body(rb, c):
            rows = pl.ds(pl.multiple_of(rb * IN_RB, IN_RB), IN_RB)
            h_scr[rows, :] = _rms_mod(x_ref[rows, :], g, scale, shift).astype(BF16)
            return c
        lax.fori_loop(0, nrb, body, 0)

    def run(epilogue):
        def body(rb, c):
            rows = pl.ds(pl.multiple_of(rb * IN_RB, IN_RB), IN_RB)
            acc = jnp.dot(h_scr[rows, :], w_ref[...], preferred_element_type=F32)
            o_ref[rows, :] = epilogue(acc).astype(o_ref.dtype)
            return c
        lax.fori_loop(0, nrb, body, 0)

    @pl.when(j < GATE_TILES)
    def _():
        run(jax.nn.sigmoid)

    @pl.when((j == U_TILE) | (j == V_TILE))
    def _():
        run(_gelu)

    @pl.when(j == Q_TILE)
    def _():
        run(lambda a: a * np.float32(ATT_HEAD_DIM ** -0.5))

    @pl.when(j > Q_TILE)
    def _():
        run(lambda a: a)


def _in_proj(x2d, mod3, norm_g, w_in_bf16, seq):
    n, d = x2d.shape
    cols = w_in_bf16.shape[1]
    tiles_per_seq = seq // IN_TM
    return pl.pallas_call(
        _in_proj_kernel,
        grid=(n // IN_TM, cols // COL_TILE),
        in_specs=[pl.BlockSpec((1, N_MOD, d), lambda i, j: (i // tiles_per_seq, 0, 0)),
                  pl.BlockSpec((1, d), lambda i, j: (0, 0)),
                  pl.BlockSpec((IN_TM, d), lambda i, j: (i, 0)),
                  pl.BlockSpec((d, COL_TILE), lambda i, j: (0, j))],
        out_specs=pl.BlockSpec((IN_TM, COL_TILE), lambda i, j: (i, j)),
        out_shape=jax.ShapeDtypeStruct((n, cols), BF16),
        scratch_shapes=[pltpu.VMEM((IN_TM, d), BF16)],
        compiler_params=_params(("arbitrary", "arbitrary")),
        name="in_proj",
    )(mod3, norm_g.reshape(1, d), x2d, w_in_bf16)


GM_TM = 256


def _gmlp_kernel(u_ref, v_ref, lng_ref, lnb_ref, ws_ref, bs_ref, o_ref):
    t = lax.broadcasted_iota(I32, (GMLP_BLOCK, GMLP_BLOCK), 0)
    s = lax.broadcasted_iota(I32, (GMLP_BLOCK, GMLP_BLOCK), 1)
    causal = (s // CHUNK) <= (t // CHUNK)
    lng = lng_ref[...]
    lnb = lnb_ref[...]
    for blk in range(GM_TM // GMLP_BLOCK):
        rows = slice(blk * GMLP_BLOCK, (blk + 1) * GMLP_BLOCK)
        v = v_ref[rows, :].astype(F32)
        mu = jnp.mean(v, axis=-1, keepdims=True)
        vc = v - mu
        var = jnp.mean(vc * vc, axis=-1, keepdims=True)
        vln = ((vc * lax.rsqrt(var + EPS)) * lng + lnb).astype(BF16)
        for g in range(GMLP_GROUPS):
            cols = slice(g * LANES, (g + 1) * LANES)
            w = jnp.where(causal, ws_ref[g], 0.0).astype(BF16)
            mixed = jnp.dot(w, vln[:, cols], preferred_element_type=F32) + bs_ref[g]
            o_ref[rows, cols] = (u_ref[rows, cols].astype(F32) * mixed).astype(o_ref.dtype)


def _gmlp(proj, ln_g, ln_b, w_s, b_s):
    n = proj.shape[0]
    return pl.pallas_call(
        _gmlp_kernel,
        grid=(n // GM_TM,),
        in_specs=[pl.BlockSpec((GM_TM, GMLP_WIDTH), lambda i: (i, U_TILE)),
                  pl.BlockSpec((GM_TM, GMLP_WIDTH), lambda i: (i, V_TILE)),
                  pl.BlockSpec((1, GMLP_WIDTH), lambda i: (0, 0)),
                  pl.BlockSpec((1, GMLP_WIDTH), lambda i: (0, 0)),
                  pl.BlockSpec((GMLP_GROUPS, GMLP_BLOCK, GMLP_BLOCK), lambda i: (0, 0, 0)),
                  pl.BlockSpec((GMLP_GROUPS, GMLP_BLOCK, 1), lambda i: (0, 0, 0))],
        out_specs=pl.BlockSpec((GM_TM, GMLP_WIDTH), lambda i: (i, 0)),
        out_shape=jax.ShapeDtypeStruct((n, GMLP_WIDTH), BF16),
        compiler_params=_params(("arbitrary",)),
        name="gmlp",
    )(proj, proj, ln_g.reshape(1, -1), ln_b.reshape(1, -1), w_s,
      b_s.reshape(GMLP_GROUPS, GMLP_BLOCK, 1))


def _band_bias(rel_table):
    r = np.arange(ATT_SUB)[:, None]
    w = np.arange(ATT_WIN)[None, :]
    j = w // CHUNK - r // CHUNK
    dist = (LEFT_CHUNKS - j) * CHUNK + r % CHUNK - w % CHUNK
    idx = np.clip(dist, -MAX_REL, MAX_REL) + MAX_REL
    valid = (j >= 0) & (j <= LEFT_CHUNKS)
    bias = rel_table.astype(F32)[:, jnp.asarray(idx.astype(np.int32))]
    return jnp.where(jnp.asarray(valid)[None], bias, NEG_INF)


def _attn_kernel(q_ref, kp_ref, kc_ref, vp_ref, vc_ref, bias_ref, o_ref, k_scr, v_scr, *, blocks_per_seq):
    first = (pl.program_id(0) % blocks_per_seq) == 0
    k_scr[0:ATT_QBLK, :] = kp_ref[...]
    k_scr[ATT_QBLK:, :] = kc_ref[...]
    v_scr[0:ATT_QBLK, :] = vp_ref[...]
    v_scr[ATT_QBLK:, :] = vc_ref[...]
    lane = lax.broadcasted_iota(I32, (ATT_SUB, LANES), 1)
    low = lane < ATT_HEAD_DIM
    col = lax.broadcasted_iota(I32, (ATT_SUB, ATT_WIN), 1)
    for a in range(ATT_QBLK // ATT_SUB):
        q = q_ref[a * ATT_SUB:(a + 1) * ATT_SUB, :]
        win = slice(a * ATT_SUB, a * ATT_SUB + ATT_WIN)
        k = k_scr[win, :]
        v = v_scr[win, :]
        before = first & (col < ATT_QBLK - a * ATT_SUB)
        outs = []
        for hh in range(2):
            qh = jnp.where(low if hh == 0 else ~low, q, jnp.zeros_like(q))
            s = lax.dot_general(qh, k, (((1,), (1,)), ((), ())), preferred_element_type=F32)
            s = s + bias_ref[hh]
            s = jnp.where(before, NEG_INF, s)
            m = jnp.max(s, axis=-1, keepdims=True)
            p = jnp.exp(s - m)
            l = jnp.sum(p, axis=-1, keepdims=True)
            pv = jnp.dot(p.astype(BF16), v, preferred_element_type=F32)
            outs.append(pv / l)
        o_ref[a * ATT_SUB:(a + 1) * ATT_SUB, :] = jnp.where(low, outs[0], outs[1]).astype(o_ref.dtype)


def _attn(proj, rel_table, seq):
    n = proj.shape[0]
    bps = seq // ATT_QBLK
    pairs = ATT_WIDTH // LANES
    qc, kc, vc = (Q_TILE * COL_TILE // LANES, K_TILE * COL_TILE // LANES, VB_TILE * COL_TILE // LANES)
    bias = _band_bias(rel_table)

    def prev(i):
        return jnp.where(i % bps == 0, i, i - 1)

    blk = (ATT_QBLK, LANES)
    return pl.pallas_call(
        functools.partial(_attn_kernel, blocks_per_seq=bps),
        grid=(n // ATT_QBLK, pairs),
        in_specs=[pl.BlockSpec(blk, lambda i, h: (i, qc + h)),
                  pl.BlockSpec(blk, lambda i, h: (prev(i), kc + h)),
                  pl.BlockSpec(blk, lambda i, h: (i, kc + h)),
                  pl.BlockSpec(blk, lambda i, h: (prev(i), vc + h)),
                  pl.BlockSpec(blk, lambda i, h: (i, vc + h)),
                  pl.BlockSpec((2, ATT_SUB, ATT_WIN), lambda i, h: (h, 0, 0))],
        out_specs=pl.BlockSpec(blk, lambda i, h: (i, h)),
        out_shape=jax.ShapeDtypeStruct((n, ATT_WIDTH), BF16),
        scratch_shapes=[pltpu.VMEM((2 * ATT_QBLK, LANES), BF16),
                        pltpu.VMEM((2 * ATT_QBLK, LANES), BF16)],
        compiler_params=_params(("arbitrary", "arbitrary")),
        name="attn",
    )(proj, proj, proj, proj, proj, bias)


MG_TM = 256


def _merge_kernel(mod_ref, g_ref, x_ref, ya_ref, yb_ref, ga_ref, gb_ref, wa_ref, wb_ref, wo_ref, wr_ref,
                  x1_ref, h2_ref, lt_ref):
    ya = jnp.dot(ya_ref[...], wa_ref[...], preferred_element_type=F32)
    yb = jnp.dot(yb_ref[...], wb_ref[...], preferred_element_type=F32)
    m = ga_ref[...].astype(F32) * ya + gb_ref[...].astype(F32) * yb
    mixed = jnp.dot(m.astype(BF16), wo_ref[...], preferred_element_type=F32)
    gate1 = mod_ref[0, 2:3, :]
    shift2 = mod_ref[0, 3:4, :]
    scale2 = mod_ref[0, 4:5, :]
    x1 = x_ref[...] + gate1 * mixed
    x1_ref[...] = x1
    h2 = _rms_mod(x1, g_ref[...], scale2, shift2)
    h2_ref[...] = h2
    lt_ref[...] = lax.dot_general(wr_ref[...], h2.astype(BF16), (((1,), (1,)), ((), ())),
                                  preferred_element_type=F32)


def _merge(x2d, mod3, norm_g, ya, yb, proj, wa, wb, wo, wr_t, seq):
    n, d = x2d.shape
    tps = seq // MG_TM
    const = lambda shape: pl.BlockSpec(shape, lambda i: (0,) * len(shape), pipeline_mode=pl.Buffered(1))
    return pl.pallas_call(
        _merge_kernel,
        grid=(n // MG_TM,),
        in_specs=[pl.BlockSpec((1, N_MOD, d), lambda i: (i // tps, 0, 0)),
                  pl.BlockSpec((1, d), lambda i: (0, 0)),
                  pl.BlockSpec((MG_TM, d), lambda i: (i, 0)),
                  pl.BlockSpec((MG_TM, GMLP_WIDTH), lambda i: (i, 0)),
                  pl.BlockSpec((MG_TM, ATT_WIDTH), lambda i: (i, 0)),
                  pl.BlockSpec((MG_TM, d), lambda i: (i, 0)),
                  pl.BlockSpec((MG_TM, d), lambda i: (i, 1)),
                  const((GMLP_WIDTH, d)), const((ATT_WIDTH, d)), const((d, d)),
                  const((ROUTER_ROWS, d))],
        out_specs=[pl.BlockSpec((MG_TM, d), lambda i: (i, 0)),
                   pl.BlockSpec((MG_TM, d), lambda i: (i, 0)),
                   pl.BlockSpec((ROUTER_ROWS, MG_TM), lambda i: (0, i))],
        out_shape=[jax.ShapeDtypeStruct((n, d), F32),
                   jax.ShapeDtypeStruct((n, d), F32),
                   jax.ShapeDtypeStruct((ROUTER_ROWS, n), F32)],
        compiler_params=_params(("arbitrary",)),
        name="merge",
    )(mod3, norm_g.reshape(1, d), x2d, ya, yb, proj, proj, wa, wb, wo, wr_t)


RT_TN = 512


def _first_argmax(vals, vmax, nrows):
    rows = lax.broadcasted_iota(I32, vals.shape, 0)
    return jnp.min(jnp.where(vals == vmax, rows, nrows), axis=0, keepdims=True)


def _route_kernel(lt_ref, e_ref, r_ref, w_ref, cnt_ref, carry_scr):
    @pl.when(pl.program_id(0) == 0)
    def _():
        carry_scr[...] = jnp.zeros_like(carry_scr)

    gl = lt_ref[0:N_GROUPS, :]
    gmax = jnp.max(gl, axis=0, keepdims=True)
    gidx = _first_argmax(gl, gmax, N_GROUPS)
    gw = 1.0 / jnp.sum(jnp.exp(gl - gmax), axis=0, keepdims=True)

    esel = lt_ref[SUBLANES:SUBLANES + EXPERTS_PER_GROUP, :]
    for g in range(1, N_GROUPS):
        lo = SUBLANES + g * EXPERTS_PER_GROUP
        esel = jnp.where(gidx == g, lt_ref[lo:lo + EXPERTS_PER_GROUP, :], esel)
    rows8 = lax.broadcasted_iota(I32, esel.shape, 0)
    m1 = jnp.max(esel, axis=0, keepdims=True)
    i1 = _first_argmax(esel, m1, EXPERTS_PER_GROUP)
    rest = jnp.where(rows8 == i1, -jnp.inf, esel)
    m2 = jnp.max(rest, axis=0, keepdims=True)
    i2 = _first_argmax(rest, m2, EXPERTS_PER_GROUP)
    z = jnp.exp(m2 - m1)
    w_top = 1.0 / (1.0 + z)
    e0 = gidx * EXPERTS_PER_GROUP + i1
    e1 = gidx * EXPERTS_PER_GROUP + i2
    e_ref[0:1, :] = e0
    e_ref[1:2, :] = e1
    w_ref[0:1, :] = gw * w_top
    w_ref[1:2, :] = gw * (z * w_top)

    rows_e = lax.broadcasted_iota(I32, (N_EXPERTS, RT_TN), 0)
    oh0 = rows_e == e0
    oh1 = rows_e == e1
    oh = jnp.where(oh0 | oh1, 1.0, 0.0)
    src = lax.broadcasted_iota(I32, (RT_TN, RT_TN), 0)
    dst = lax.broadcasted_iota(I32, (RT_TN, RT_TN), 1)
    before = jnp.where(src < dst, 1.0, 0.0).astype(BF16)
    carry = carry_scr[...]
    prefix = jnp.dot(oh.astype(BF16), before, preferred_element_type=F32) + carry[:, 0:1]
    r_ref[0:1, :] = jnp.sum(jnp.where(oh0, prefix, 0.0), axis=0, keepdims=True).astype(I32)
    r_ref[1:2, :] = jnp.sum(jnp.where(oh1, prefix, 0.0), axis=0, keepdims=True).astype(I32)
    carry = carry + jnp.sum(oh, axis=1, keepdims=True)
    carry_scr[...] = carry
    cnt_ref[...] = carry.astype(I32)


def _route(logits_t):
    n = logits_t.shape[1]
    slot = pl.BlockSpec((2, RT_TN), lambda i: (0, i))
    return pl.pallas_call(
        _route_kernel,
        grid=(n // RT_TN,),
        in_specs=[pl.BlockSpec((ROUTER_ROWS, RT_TN), lambda i: (0, i))],
        out_specs=[slot, slot, slot, pl.BlockSpec((N_EXPERTS, LANES), lambda i: (0, 0))],
        out_shape=[jax.ShapeDtypeStruct((2, n), I32),
                   jax.ShapeDtypeStruct((2, n), I32),
                   jax.ShapeDtypeStruct((2, n), F32),
                   jax.ShapeDtypeStruct((N_EXPERTS, LANES), I32)],
        scratch_shapes=[pltpu.VMEM((N_EXPERTS, LANES), F32)],
        compiler_params=_params(("arbitrary",)),
        name="route",
    )(logits_t)


DP_TM = 256


def _row_copy(src_ref, src_row, dst_ref, dst_row, sem):
    return pltpu.make_async_copy(src_ref.at[pl.ds(src_row, 1)], dst_ref.at[pl.ds(dst_row, 1)], sem)


def _dispatch_kernel(pos0_ref, pos1_ref, pad_start_ref, pad_n_ref, nt_ref, h_ref, xs_ref, zero_scr, sem, zsem, tsem):
    i = pl.program_id(0)
    n_tiles_max = xs_ref.shape[0] // EXPERT_TILE

    def tail_copy(t):
        return pltpu.make_async_copy(zero_scr, xs_ref.at[pl.ds(t * EXPERT_TILE, EXPERT_TILE)], tsem)

    @pl.when(i == 0)
    def _():
        zero_scr[...] = jnp.zeros_like(zero_scr)
        for e in range(N_EXPERTS):
            start = pad_start_ref[e]

            def issue(r, c, start=start):
                _row_copy(zero_scr, 0, xs_ref, start + r, zsem).start()
                return c
            lax.fori_loop(0, pad_n_ref[e], issue, 0)

        def issue_tail(t, c):
            tail_copy(t).start()
            return c
        lax.fori_loop(nt_ref[0], n_tiles_max, issue_tail, 0)

        for e in range(N_EXPERTS):
            def drain(r, c):
                _row_copy(zero_scr, 0, xs_ref, 0, zsem).wait()
                return c
            lax.fori_loop(0, pad_n_ref[e], drain, 0)

        def drain_tail(t, c):
            tail_copy(t).wait()
            return c
        lax.fori_loop(nt_ref[0], n_tiles_max, drain_tail, 0)

    base = i * DP_TM

    def issue(r, c):
        _row_copy(h_ref, r, xs_ref, pos0_ref[base + r], sem).start()
        _row_copy(h_ref, r, xs_ref, pos1_ref[base + r], sem).start()
        return c
    lax.fori_loop(0, DP_TM, issue, 0)

    def drain(r, c):
        _row_copy(h_ref, 0, xs_ref, 0, sem).wait()
        _row_copy(h_ref, 0, xs_ref, 0, sem).wait()
        return c
    lax.fori_loop(0, DP_TM, drain, 0)


def _dispatch(h2, pos0, pos1, pad_start, pad_n, n_tiles, rows_out):
    n, d = h2.shape
    grid_spec = pltpu.PrefetchScalarGridSpec(
        num_scalar_prefetch=5,
        grid=(n // DP_TM,),
        in_specs=[pl.BlockSpec((DP_TM, d), lambda i, *_: (i, 0))],
        out_specs=pl.BlockSpec(memory_space=pl.ANY),
        scratch_shapes=[pltpu.VMEM((EXPERT_TILE, d), F32),
                        pltpu.SemaphoreType.DMA(()), pltpu.SemaphoreType.DMA(()),
                        pltpu.SemaphoreType.DMA(())],
    )
    return pl.pallas_call(
        _dispatch_kernel,
        grid_spec=grid_spec,
        out_shape=jax.ShapeDtypeStruct((rows_out, d), F32),
        compiler_params=pltpu.CompilerParams(dimension_semantics=("arbitrary",),
                                             vmem_limit_bytes=VMEM_LIMIT, has_side_effects=True),
        name="dispatch",
    )(pos0, pos1, pad_start, pad_n, n_tiles, h2)


def _experts_kernel(te_ref, tr_ref, nt_ref, xs_ref, w1_ref, w3_ref, w2_ref, ys_ref):
    i = pl.program_id(0)

    @pl.when(i < nt_ref[0])
    def _():
        x = xs_ref[...].astype(BF16)
        a = jnp.dot(x, w1_ref[0], preferred_element_type=F32)
        b = jnp.dot(x, w3_ref[0], preferred_element_type=F32)
        act = (a * jax.nn.sigmoid(a)) * b
        ys_ref[...] = jnp.dot(act.astype(BF16), w2_ref[0], preferred_element_type=F32)

    @pl.when(i >= nt_ref[0])
    def _():
        ys_ref[...] = jnp.zeros_like(ys_ref)


def _experts(xs, tile_expert, tile_row, n_tiles, w1, w3, w2):
    rows, d = xs.shape
    f = w1.shape[2]
    grid_spec = pltpu.PrefetchScalarGridSpec(
        num_scalar_prefetch=3,
        grid=(rows // EXPERT_TILE,),
        in_specs=[pl.BlockSpec((EXPERT_TILE, d), lambda i, te, tr, nt: (tr[i], 0)),
                  pl.BlockSpec((1, d, f), lambda i, te, tr, nt: (te[i], 0, 0)),
                  pl.BlockSpec((1, d, f), lambda i, te, tr, nt: (te[i], 0, 0)),
                  pl.BlockSpec((1, f, d), lambda i, te, tr, nt: (te[i], 0, 0))],
        out_specs=pl.BlockSpec((EXPERT_TILE, d), lambda i, te, tr, nt: (i, 0)),
    )
    return pl.pallas_call(
        _experts_kernel,
        grid_spec=grid_spec,
        out_shape=jax.ShapeDtypeStruct((rows, d), F32),
        compiler_params=_params(("arbitrary",)),
        name="experts",
    )(tile_expert, tile_row, n_tiles, xs, w1, w3, w2)


CB_TM = 256


def _combine_kernel(pos0_ref, pos1_ref, mod_ref, fg_ref, x1_ref, w0_ref, w1_ref, ys_ref, o_ref,
                    y0_scr, y1_scr, sem):
    base = pl.program_id(0) * CB_TM

    def issue(r, c):
        _row_copy(ys_ref, pos0_ref[base + r], y0_scr, r, sem).start()
        _row_copy(ys_ref, pos1_ref[base + r], y1_scr, r, sem).start()
        return c
    lax.fori_loop(0, CB_TM, issue, 0)

    def drain(r, c):
        _row_copy(ys_ref, 0, y0_scr, 0, sem).wait()
        _row_copy(ys_ref, 0, y1_scr, 0, sem).wait()
        return c
    lax.fori_loop(0, CB_TM, drain, 0)

    gate2 = mod_ref[0, 5:6, :]
    y = w0_ref[...] * y0_scr[...] + w1_ref[...] * y1_scr[...]
    x2 = x1_ref[...] + gate2 * y
    o_ref[...] = (x2 * lax.rsqrt(jnp.mean(x2 * x2, axis=-1, keepdims=True) + EPS)) * fg_ref[...]


def _combine(x1, mod3, final_g, ys, pos0, pos1, cw0, cw1, seq):
    n, d = x1.shape
    tps = seq // CB_TM
    grid_spec = pltpu.PrefetchScalarGridSpec(
        num_scalar_prefetch=2,
        grid=(n // CB_TM,),
        in_specs=[pl.BlockSpec((1, N_MOD, d), lambda i, *_: (i // tps, 0, 0)),
                  pl.BlockSpec((1, d), lambda i, *_: (0, 0)),
                  pl.BlockSpec((CB_TM, d), lambda i, *_: (i, 0)),
                  pl.BlockSpec((CB_TM, 1), lambda i, *_: (i, 0)),
                  pl.BlockSpec((CB_TM, 1), lambda i, *_: (i, 0)),
                  pl.BlockSpec(memory_space=pl.ANY)],
        out_specs=pl.BlockSpec((CB_TM, d), lambda i, *_: (i, 0)),
        scratch_shapes=[pltpu.VMEM((CB_TM, d), F32), pltpu.VMEM((CB_TM, d), F32),
                        pltpu.SemaphoreType.DMA(())],
    )
    return pl.pallas_call(
        _combine_kernel,
        grid_spec=grid_spec,
        out_shape=jax.ShapeDtypeStruct((n, d), F32),
        compiler_params=_params(("arbitrary",)),
        name="combine",
    )(pos0, pos1, mod3, final_g.reshape(1, d), x1, cw0, cw1, ys)


def _layer(x2d, c, seq, ada_w, ada_b, norm1_g, w_in, gmlp_ln_g, gmlp_ln_b, gmlp_w_s, gmlp_b_s, rel_bias,
           w_branch_a, w_branch_b, w_out, norm2_g, w_group, w_expert, w1, w3, w2, final_g):
    n, d = x2d.shape
    nb = c.shape[0]
    mod3 = _ada(c, ada_w, ada_b).reshape(nb, N_MOD, d)

    n_gate = 2 * d
    w_in_perm = jnp.concatenate([w_in[:, PROJ_COLS - n_gate:], w_in[:, :PROJ_COLS - n_gate]], axis=1).astype(BF16)
    proj = _in_proj(x2d, mod3, norm1_g, w_in_perm, seq)
    ya = _gmlp(proj, gmlp_ln_g, gmlp_ln_b, gmlp_w_s, gmlp_b_s)
    yb = _attn(proj, rel_bias, seq)

    wr_t = jnp.concatenate([w_group.T, jnp.zeros((SUBLANES - N_GROUPS, d), F32),
                            w_expert.transpose(0, 2, 1).reshape(N_EXPERTS, d)], axis=0).astype(BF16)
    x1, h2, logits_t = _merge(x2d, mod3, norm2_g, ya, yb, proj, w_branch_a.astype(BF16),
                              w_branch_b.astype(BF16), w_out.astype(BF16), wr_t, seq)

    eidx, rank, cw, counts = _route(logits_t)
    counts = counts[:, 0]
    padded = ((counts + EXPERT_TILE - 1) // EXPERT_TILE) * EXPERT_TILE
    ends = jnp.cumsum(padded)
    offs = ends - padded
    pos = offs[eidx] + rank
    rows_out = 2 * n + N_EXPERTS * EXPERT_TILE
    n_tiles_max = rows_out // EXPERT_TILE
    n_tiles = (ends[-1] // EXPERT_TILE).astype(I32)
    tile_row = jnp.minimum(jnp.arange(n_tiles_max, dtype=I32), n_tiles - 1)
    tile_expert = jnp.minimum(jnp.searchsorted(ends, tile_row * EXPERT_TILE, side="right"),
                              N_EXPERTS - 1).astype(I32)

    n_tiles = n_tiles.reshape(1)
    xs = _dispatch(h2, pos[0], pos[1], (offs + counts).astype(I32), (padded - counts).astype(I32), n_tiles,
                   rows_out)
    ys = _experts(xs, tile_expert, tile_row, n_tiles,
                  w1.reshape(N_EXPERTS, d, D_EXPERT).astype(BF16),
                  w3.reshape(N_EXPERTS, d, D_EXPERT).astype(BF16),
                  w2.reshape(N_EXPERTS, D_EXPERT, d).astype(BF16))
    return _combine(x1, mod3, final_g, ys, pos[0], pos[1], cw[0].reshape(n, 1), cw[1].reshape(n, 1), seq)


def kernel(x, c, ada_w, ada_b, norm1_g, w_in, gmlp_ln_g, gmlp_ln_b, gmlp_w_s, gmlp_b_s, rel_bias, w_branch_a,
           w_branch_b, w_out, norm2_g, w_group, w_expert, w1, w3, w2, final_g):
    b, s, d = x.shape
    out = _layer(x.reshape(b * s, d), c, s, ada_w[0], ada_b[0], norm1_g[0], w_in[0], gmlp_ln_g[0], gmlp_ln_b[0],
                 gmlp_w_s[0], gmlp_b_s[0], rel_bias[0], w_branch_a[0], w_branch_b[0], w_out[0], norm2_g[0],
                 w_group[0], w_expert[0], w1[0], w3[0], w2[0], final_g)
    return out.reshape(b, s, d)
```

```python
import functools

import numpy as np
import jax
import jax.numpy as jnp
from jax import lax
from jax.experimental import pallas as pl
from jax.experimental.pallas import tpu as pltpu

F32 = jnp.float32
BF16 = jnp.bfloat16
I32 = jnp.int32

D_MODEL = 2048
CHUNK = 64
EPS = 1e-6
NEG_INF = -1e30
GMLP_BLOCK = 128
GMLP_GROUPS = 8
GMLP_WIDTH = 1024
ATT_HEADS = 16
ATT_HEAD_DIM = 64
ATT_WIDTH = ATT_HEADS * ATT_HEAD_DIM
LEFT_CHUNKS = 8
MAX_REL = 256
N_GROUPS = 4
EXPERTS_PER_GROUP = 8
N_EXPERTS = N_GROUPS * EXPERTS_PER_GROUP
D_EXPERT = 512
N_MOD = 6

LANES = 128
SUBLANES = 8
VMEM_LIMIT = 56 * 1024 * 1024

COL_TILE = 1024
PROJ_COLS = 2 * D_MODEL + 2 * GMLP_WIDTH + 3 * ATT_WIDTH
GATE_TILES = 2 * D_MODEL // COL_TILE
U_TILE = GATE_TILES
V_TILE = GATE_TILES + 1
Q_TILE = GATE_TILES + 2
K_TILE = GATE_TILES + 3
VB_TILE = GATE_TILES + 4

ROUTER_ROWS = SUBLANES + N_EXPERTS

ATT_QBLK = 512
ATT_SUB = 2 * CHUNK
ATT_WIN = ATT_SUB + LEFT_CHUNKS * CHUNK

EXPERT_TILE = 256


def _params(sem, vmem=VMEM_LIMIT):
    return pltpu.CompilerParams(dimension_semantics=sem, vmem_limit_bytes=vmem)


def _rms_mod(x, g, scale, shift):
    y = x * lax.rsqrt(jnp.mean(x * x, axis=-1, keepdims=True) + EPS)
    return (y * g) * (1.0 + scale) + shift


def _ada_kernel(cb_ref, w_ref, b_ref, o_ref):
    nb, d, _ = cb_ref.shape
    tn = w_ref.shape[1]
    reps = tn // LANES

    def body(i, accs):
        r = pl.ds(pl.multiple_of(i * SUBLANES, SUBLANES), SUBLANES)
        w8 = w_ref[r, :]
        out = []
        for b in range(nb):
            cb = cb_ref[b, r, :]
            s = cb * jax.nn.sigmoid(cb)
            out.append(accs[b] + w8 * jnp.concatenate([s] * reps, axis=1))
        return tuple(out)

    init = tuple(jnp.zeros((SUBLANES, tn), F32) for _ in range(nb))
    accs = lax.fori_loop(0, d // SUBLANES, body, init)
    for b in range(nb):
        o_ref[b:b + 1, :] = jnp.sum(accs[b], axis=0, keepdims=True) + b_ref[...]


def _ada(c, ada_w, ada_b):
    nb, d = c.shape
    n = ada_w.shape[1]
    tn = 1024
    cb = jnp.broadcast_to(c[:, :, None], (nb, d, LANES))
    return pl.pallas_call(
        _ada_kernel,
        grid=(n // tn,),
        in_specs=[pl.BlockSpec((nb, d, LANES), lambda j: (0, 0, 0)),
                  pl.BlockSpec((d, tn), lambda j: (0, j)),
                  pl.BlockSpec((1, tn), lambda j: (0, j))],
        out_specs=pl.BlockSpec((nb, tn), lambda j: (0, j)),
        out_shape=jax.ShapeDtypeStruct((nb, n), F32),
        compiler_params=_params(("arbitrary",)),
        name="ada",
    )(cb, ada_w, ada_b.reshape(1, n))


IN_TM = 1024
IN_RB = 256


def _gelu(a):
    return 0.5 * a * (1.0 + lax.erf(a * np.float32(np.sqrt(0.5))))


def _in_proj_kernel(mod_ref, g_ref, x_ref, w_ref, o_ref, h_scr):
    j = pl.program_id(1)
    nrb = IN_TM // IN_RB

    @pl.when(j == 0)
    def _():
        shift = mod_ref[0, 0:1, :]
        scale = mod_ref[0, 1:2, :]
        g = g_ref[...]

        def body(rb, c):
            rows = pl.ds(pl.multiple_of(rb * IN_RB, IN_RB), IN_RB)
            h_scr[rows, :] = _rms_mod(x_ref[rows, :], g, scale, shift).astype(BF16)
            return c
        lax.fori_loop(0, nrb, body, 0)

    def run(epilogue):
        def body(rb, c):
            rows = pl.ds(pl.multiple_of(rb * IN_RB, IN_RB), IN_RB)
            acc = jnp.dot(h_scr[rows, :], w_ref[...], preferred_element_type=F32)
            o_ref[rows, :] = epilogue(acc).astype(o_ref.dtype)
            return c
        lax.fori_loop(0, nrb, body, 0)

    @pl.when(j < GATE_TILES)
    def _():
        run(jax.nn.sigmoid)

    @pl.when((j == U_TILE) | (j == V_TILE))
    def _():
        run(_gelu)

    @pl.when(j == Q_TILE)
    def _():
        run(lambda a: a * np.float32(ATT_HEAD_DIM ** -0.5))

    @pl.when(j > Q_TILE)
    def _():
        run(lambda a: a)


def _in_proj(x2d, mod3, norm_g, w_in_bf16, seq):
    n, d = x2d.shape
    cols = w_in_bf16.shape[1]
    n_tiles = cols // COL_TILE
    tiles_per_seq = seq // IN_TM
    return pl.pallas_call(
        _in_proj_kernel,
        grid=(n // IN_TM, n_tiles),
        in_specs=[pl.BlockSpec((1, N_MOD, d), lambda i, j: (i // tiles_per_seq, 0, 0)),
                  pl.BlockSpec((1, d), lambda i, j: (0, 0)),
                  pl.BlockSpec((IN_TM, d), lambda i, j: (i, 0)),
                  pl.BlockSpec((d, COL_TILE), lambda i, j: (0, (j + n_tiles - GATE_TILES) % n_tiles))],
        out_specs=pl.BlockSpec((IN_TM, COL_TILE), lambda i, j: (i, j)),
        out_shape=jax.ShapeDtypeStruct((n, cols), BF16),
        scratch_shapes=[pltpu.VMEM((IN_TM, d), BF16)],
        compiler_params=_params(("arbitrary", "arbitrary")),
        name="in_proj",
    )(mod3, norm_g.reshape(1, d), x2d, w_in_bf16)


GM_TM = 256


def _gmlp_kernel(u_ref, v_ref, lng_ref, lnb_ref, ws_ref, bs_ref, o_ref):
    t = lax.broadcasted_iota(I32, (GMLP_BLOCK, GMLP_BLOCK), 0)
    s = lax.broadcasted_iota(I32, (GMLP_BLOCK, GMLP_BLOCK), 1)
    causal = (s // CHUNK) <= (t // CHUNK)
    lng = lng_ref[...]
    lnb = lnb_ref[...]
    for blk in range(GM_TM // GMLP_BLOCK):
        rows = slice(blk * GMLP_BLOCK, (blk + 1) * GMLP_BLOCK)
        v = v_ref[rows, :].astype(F32)
        mu = jnp.mean(v, axis=-1, keepdims=True)
        vc = v - mu
        var = jnp.mean(vc * vc, axis=-1, keepdims=True)
        vln = ((vc * lax.rsqrt(var + EPS)) * lng + lnb).astype(BF16)
        for g in range(GMLP_GROUPS):
            cols = slice(g * LANES, (g + 1) * LANES)
            w = jnp.where(causal, ws_ref[g], 0.0).astype(BF16)
            mixed = jnp.dot(w, vln[:, cols], preferred_element_type=F32) + bs_ref[g]
            o_ref[rows, cols] = (u_ref[rows, cols].astype(F32) * mixed).astype(o_ref.dtype)


def _gmlp(proj, ln_g, ln_b, w_s, b_s):
    n = proj.shape[0]
    return pl.pallas_call(
        _gmlp_kernel,
        grid=(n // GM_TM,),
        in_specs=[pl.BlockSpec((GM_TM, GMLP_WIDTH), lambda i: (i, U_TILE)),
                  pl.BlockSpec((GM_TM, GMLP_WIDTH), lambda i: (i, V_TILE)),
                  pl.BlockSpec((1, GMLP_WIDTH), lambda i: (0, 0)),
                  pl.BlockSpec((1, GMLP_WIDTH), lambda i: (0, 0)),
                  pl.BlockSpec((GMLP_GROUPS, GMLP_BLOCK, GMLP_BLOCK), lambda i: (0, 0, 0)),
                  pl.BlockSpec((GMLP_GROUPS, GMLP_BLOCK, 1), lambda i: (0, 0, 0))],
        out_specs=pl.BlockSpec((GM_TM, GMLP_WIDTH), lambda i: (i, 0)),
        out_shape=jax.ShapeDtypeStruct((n, GMLP_WIDTH), BF16),
        compiler_params=_params(("arbitrary",)),
        name="gmlp",
    )(proj, proj, ln_g.reshape(1, -1), ln_b.reshape(1, -1), w_s,
      b_s.reshape(GMLP_GROUPS, GMLP_BLOCK, 1))


def _band_bias(rel_table):
    heads = rel_table.shape[0]
    r = np.arange(ATT_SUB)[:, None]
    w = np.arange(ATT_WIN)[None, :]
    j = w // CHUNK - r // CHUNK
    valid = (j >= 0) & (j <= LEFT_CHUNKS)
    far = LEFT_CHUNKS * CHUNK + ATT_SUB - 1
    n_clipped = far - MAX_REL + 1
    table = rel_table.astype(F32)
    lo = MAX_REL - (ATT_WIN - 1 - LEFT_CHUNKS * CHUNK)
    diag = jnp.concatenate([jnp.broadcast_to(table[:, 2 * MAX_REL:], (heads, n_clipped)),
                            jnp.flip(table[:, lo:2 * MAX_REL], axis=1)], axis=1)
    rows = [diag[:, ATT_SUB - 1 - q:ATT_SUB - 1 - q + ATT_WIN] for q in range(ATT_SUB)]
    bias = jnp.stack(rows, axis=1)
    return jnp.where(jnp.asarray(valid)[None], bias, NEG_INF)


def _attn_kernel(q_ref, kp_ref, kc_ref, vp_ref, vc_ref, bias_ref, o_ref, k_scr, v_scr, *, blocks_per_seq):
    first = (pl.program_id(0) % blocks_per_seq) == 0
    k_scr[0:ATT_QBLK, :] = kp_ref[...]
    k_scr[ATT_QBLK:, :] = kc_ref[...]
    v_scr[0:ATT_QBLK, :] = vp_ref[...]
    v_scr[ATT_QBLK:, :] = vc_ref[...]
    lane = lax.broadcasted_iota(I32, (ATT_SUB, LANES), 1)
    low = lane < ATT_HEAD_DIM
    col = lax.broadcasted_iota(I32, (ATT_SUB, ATT_WIN), 1)
    for a in range(ATT_QBLK // ATT_SUB):
        q = q_ref[a * ATT_SUB:(a + 1) * ATT_SUB, :]
        win = slice(a * ATT_SUB, a * ATT_SUB + ATT_WIN)
        k = k_scr[win, :]
        v = v_scr[win, :]
        before = first & (col < ATT_QBLK - a * ATT_SUB)
        outs = []
        for hh in range(2):
            qh = jnp.where(low if hh == 0 else ~low, q, jnp.zeros_like(q))
            s = lax.dot_general(qh, k, (((1,), (1,)), ((), ())), preferred_element_type=F32)
            s = s + bias_ref[hh]
            s = jnp.where(before, NEG_INF, s)
            m = jnp.max(s, axis=-1, keepdims=True)
            p = jnp.exp(s - m)
            l = jnp.sum(p, axis=-1, keepdims=True)
            pv = jnp.dot(p.astype(BF16), v, preferred_element_type=F32)
            outs.append(pv / l)
        o_ref[a * ATT_SUB:(a + 1) * ATT_SUB, :] = jnp.where(low, outs[0], outs[1]).astype(o_ref.dtype)


def _attn(proj, rel_table, seq):
    n = proj.shape[0]
    bps = seq // ATT_QBLK
    pairs = ATT_WIDTH // LANES
    qc, kc, vc = (Q_TILE * COL_TILE // LANES, K_TILE * COL_TILE // LANES, VB_TILE * COL_TILE // LANES)
    bias = _band_bias(rel_table)

    def prev(i):
        return jnp.where(i % bps == 0, i, i - 1)

    blk = (ATT_QBLK, LANES)
    return pl.pallas_call(
        functools.partial(_attn_kernel, blocks_per_seq=bps),
        grid=(n // ATT_QBLK, pairs),
        in_specs=[pl.BlockSpec(blk, lambda i, h: (i, qc + h)),
                  pl.BlockSpec(blk, lambda i, h: (prev(i), kc + h)),
                  pl.BlockSpec(blk, lambda i, h: (i, kc + h)),
                  pl.BlockSpec(blk, lambda i, h: (prev(i), vc + h)),
                  pl.BlockSpec(blk, lambda i, h: (i, vc + h)),
                  pl.BlockSpec((2, ATT_SUB, ATT_WIN), lambda i, h: (h, 0, 0))],
        out_specs=pl.BlockSpec(blk, lambda i, h: (i, h)),
        out_shape=jax.ShapeDtypeStruct((n, ATT_WIDTH), BF16),
        scratch_shapes=[pltpu.VMEM((2 * ATT_QBLK, LANES), BF16),
                        pltpu.VMEM((2 * ATT_QBLK, LANES), BF16)],
        compiler_params=_params(("arbitrary", "arbitrary")),
        name="attn",
    )(proj, proj, proj, proj, proj, bias)


MG_TM = 256


def _merge_kernel(mod_ref, g_ref, x_ref, ya_ref, yb_ref, ga_ref, gb_ref, wa_ref, wb_ref, wo_ref, wr_ref,
                  x1_ref, h2_ref, lt_ref):
    ya = jnp.dot(ya_ref[...], wa_ref[...], preferred_element_type=F32)
    yb = jnp.dot(yb_ref[...], wb_ref[...], preferred_element_type=F32)
    m = ga_ref[...].astype(F32) * ya + gb_ref[...].astype(F32) * yb
    mixed = jnp.dot(m.astype(BF16), wo_ref[...], preferred_element_type=F32)
    gate1 = mod_ref[0, 2:3, :]
    shift2 = mod_ref[0, 3:4, :]
    scale2 = mod_ref[0, 4:5, :]
    x1 = x_ref[...] + gate1 * mixed
    x1_ref[...] = x1
    h2 = _rms_mod(x1, g_ref[...], scale2, shift2)
    h2_ref[...] = h2
    lt_ref[...] = lax.dot_general(wr_ref[...], h2.astype(BF16), (((1,), (1,)), ((), ())),
                                  preferred_element_type=F32)


def _merge(x2d, mod3, norm_g, ya, yb, proj, wa, wb, wo, wr_t, seq):
    n, d = x2d.shape
    tps = seq // MG_TM
    const = lambda shape: pl.BlockSpec(shape, lambda i: (0,) * len(shape), pipeline_mode=pl.Buffered(1))
    return pl.pallas_call(
        _merge_kernel,
        grid=(n // MG_TM,),
        in_specs=[pl.BlockSpec((1, N_MOD, d), lambda i: (i // tps, 0, 0)),
                  pl.BlockSpec((1, d), lambda i: (0, 0)),
                  pl.BlockSpec((MG_TM, d), lambda i: (i, 0)),
                  pl.BlockSpec((MG_TM, GMLP_WIDTH), lambda i: (i, 0)),
                  pl.BlockSpec((MG_TM, ATT_WIDTH), lambda i: (i, 0)),
                  pl.BlockSpec((MG_TM, d), lambda i: (i, 0)),
                  pl.BlockSpec((MG_TM, d), lambda i: (i, 1)),
                  const((GMLP_WIDTH, d)), const((ATT_WIDTH, d)), const((d, d)),
                  const((ROUTER_ROWS, d))],
        out_specs=[pl.BlockSpec((MG_TM, d), lambda i: (i, 0)),
                   pl.BlockSpec((MG_TM, d), lambda i: (i, 0)),
                   pl.BlockSpec((ROUTER_ROWS, MG_TM), lambda i: (0, i))],
        out_shape=[jax.ShapeDtypeStruct((n, d), F32),
                   jax.ShapeDtypeStruct((n, d), F32),
                   jax.ShapeDtypeStruct((ROUTER_ROWS, n), F32)],
        compiler_params=_params(("arbitrary",)),
        name="merge",
    )(mod3, norm_g.reshape(1, d), x2d, ya, yb, proj, proj, wa, wb, wo, wr_t)


RT_TN = 512


def _first_argmax(vals, vmax, nrows):
    rows = lax.broadcasted_iota(I32, vals.shape, 0)
    return jnp.min(jnp.where(vals == vmax, rows, nrows), axis=0, keepdims=True)


def _route_kernel(lt_ref, e_ref, r_ref, w_ref, cnt_ref, carry_scr):
    @pl.when(pl.program_id(0) == 0)
    def _():
        carry_scr[...] = jnp.zeros_like(carry_scr)

    gl = lt_ref[0:N_GROUPS, :]
    gmax = jnp.max(gl, axis=0, keepdims=True)
    gidx = _first_argmax(gl, gmax, N_GROUPS)
    gw = 1.0 / jnp.sum(jnp.exp(gl - gmax), axis=0, keepdims=True)

    esel = lt_ref[SUBLANES:SUBLANES + EXPERTS_PER_GROUP, :]
    for g in range(1, N_GROUPS):
        lo = SUBLANES + g * EXPERTS_PER_GROUP
        esel = jnp.where(gidx == g, lt_ref[lo:lo + EXPERTS_PER_GROUP, :], esel)
    rows8 = lax.broadcasted_iota(I32, esel.shape, 0)
    m1 = jnp.max(esel, axis=0, keepdims=True)
    i1 = _first_argmax(esel, m1, EXPERTS_PER_GROUP)
    rest = jnp.where(rows8 == i1, -jnp.inf, esel)
    m2 = jnp.max(rest, axis=0, keepdims=True)
    i2 = _first_argmax(rest, m2, EXPERTS_PER_GROUP)
    z = jnp.exp(m2 - m1)
    w_top = 1.0 / (1.0 + z)
    e0 = gidx * EXPERTS_PER_GROUP + i1
    e1 = gidx * EXPERTS_PER_GROUP + i2
    e_ref[0:1, :] = e0
    e_ref[1:2, :] = e1
    w_ref[0:1, :] = gw * w_top
    w_ref[1:2, :] = gw * (z * w_top)

    rows_e = lax.broadcasted_iota(I32, (N_EXPERTS, RT_TN), 0)
    oh0 = rows_e == e0
    oh1 = rows_e == e1
    oh = jnp.where(oh0 | oh1, 1.0, 0.0)
    src = lax.broadcasted_iota(I32, (RT_TN, RT_TN), 0)
    dst = lax.broadcasted_iota(I32, (RT_TN, RT_TN), 1)
    before = jnp.where(src < dst, 1.0, 0.0).astype(BF16)
    carry = carry_scr[...]
    prefix = jnp.dot(oh.astype(BF16), before, preferred_element_type=F32) + carry[:, 0:1]
    r_ref[0:1, :] = jnp.sum(jnp.where(oh0, prefix, 0.0), axis=0, keepdims=True).astype(I32)
    r_ref[1:2, :] = jnp.sum(jnp.where(oh1, prefix, 0.0), axis=0, keepdims=True).astype(I32)
    carry = carry + jnp.sum(oh, axis=1, keepdims=True)
    carry_scr[...] = carry
    cnt_ref[...] = carry.astype(I32)


def _route(logits_t):
    n = logits_t.shape[1]
    slot = pl.BlockSpec((2, RT_TN), lambda i: (0, i))
    return pl.pallas_call(
        _route_kernel,
        grid=(n // RT_TN,),
        in_specs=[pl.BlockSpec((ROUTER_ROWS, RT_TN), lambda i: (0, i))],
        out_specs=[slot, slot, slot, pl.BlockSpec((N_EXPERTS, LANES), lambda i: (0, 0))],
        out_shape=[jax.ShapeDtypeStruct((2, n), I32),
                   jax.ShapeDtypeStruct((2, n), I32),
                   jax.ShapeDtypeStruct((2, n), F32),
                   jax.ShapeDtypeStruct((N_EXPERTS, LANES), I32)],
        scratch_shapes=[pltpu.VMEM((N_EXPERTS, LANES), F32)],
        compiler_params=_params(("arbitrary",)),
        name="route",
    )(logits_t)


DP_TM = 256


def _row_copy(src_ref, src_row, dst_ref, dst_row, sem):
    return pltpu.make_async_copy(src_ref.at[pl.ds(src_row, 1)], dst_ref.at[pl.ds(dst_row, 1)], sem)


def _dispatch_kernel(pos0_ref, pos1_ref, pad_start_ref, pad_n_ref, nt_ref, h_ref, xs_ref, zero_scr, sem, zsem, tsem):
    i = pl.program_id(0)
    n_tiles_max = xs_ref.shape[0] // EXPERT_TILE

    def tail_copy(t):
        return pltpu.make_async_copy(zero_scr, xs_ref.at[pl.ds(t * EXPERT_TILE, EXPERT_TILE)], tsem)

    @pl.when(i == 0)
    def _():
        zero_scr[...] = jnp.zeros_like(zero_scr)
        for e in range(N_EXPERTS):
            start = pad_start_ref[e]

            def issue(r, c, start=start):
                _row_copy(zero_scr, 0, xs_ref, start + r, zsem).start()
                return c
            lax.fori_loop(0, pad_n_ref[e], issue, 0)

        def issue_tail(t, c):
            tail_copy(t).start()
            return c
        lax.fori_loop(nt_ref[0], n_tiles_max, issue_tail, 0)

        for e in range(N_EXPERTS):
            def drain(r, c):
                _row_copy(zero_scr, 0, xs_ref, 0, zsem).wait()
                return c
            lax.fori_loop(0, pad_n_ref[e], drain, 0)

        def drain_tail(t, c):
            tail_copy(t).wait()
            return c
        lax.fori_loop(nt_ref[0], n_tiles_max, drain_tail, 0)

    base = i * DP_TM

    def issue(r, c):
        _row_copy(h_ref, r, xs_ref, pos0_ref[base + r], sem).start()
        _row_copy(h_ref, r, xs_ref, pos1_ref[base + r], sem).start()
        return c
    lax.fori_loop(0, DP_TM, issue, 0)

    def drain(r, c):
        _row_copy(h_ref, 0, xs_ref, 0, sem).wait()
        _row_copy(h_ref, 0, xs_ref, 0, sem).wait()
        return c
    lax.fori_loop(0, DP_TM, drain, 0)


def _dispatch(h2, pos0, pos1, pad_start, pad_n, n_tiles, rows_out):
    n, d = h2.shape
    grid_spec = pltpu.PrefetchScalarGridSpec(
        num_scalar_prefetch=5,
        grid=(n // DP_TM,),
        in_specs=[pl.BlockSpec((DP_TM, d), lambda i, *_: (i, 0))],
        out_specs=pl.BlockSpec(memory_space=pl.ANY),
        scratch_shapes=[pltpu.VMEM((EXPERT_TILE, d), F32),
                        pltpu.SemaphoreType.DMA(()), pltpu.SemaphoreType.DMA(()),
                        pltpu.SemaphoreType.DMA(())],
    )
    return pl.pallas_call(
        _dispatch_kernel,
        grid_spec=grid_spec,
        out_shape=jax.ShapeDtypeStruct((rows_out, d), F32),
        compiler_params=pltpu.CompilerParams(dimension_semantics=("arbitrary",),
                                             vmem_limit_bytes=VMEM_LIMIT, has_side_effects=True),
        name="dispatch",
    )(pos0, pos1, pad_start, pad_n, n_tiles, h2)


EXPERT_CAST_ROWS = 256


def _cast_rows(src_ref, dst_ref):
    rows_total = dst_ref.shape[0]

    def body(r, c):
        rows = pl.ds(pl.multiple_of(r * EXPERT_CAST_ROWS, EXPERT_CAST_ROWS), EXPERT_CAST_ROWS)
        dst_ref[rows, :] = src_ref[0, rows, :].astype(BF16)
        return c
    lax.fori_loop(0, rows_total // EXPERT_CAST_ROWS, body, 0)


def _experts_kernel(te_ref, tr_ref, nt_ref, xs_ref, w1_ref, w3_ref, w2_ref, ys_ref, w1_scr, w3_scr, w2_scr):
    i = pl.program_id(0)

    @pl.when((i == 0) | (te_ref[i] != te_ref[jnp.maximum(i - 1, 0)]))
    def _():
        _cast_rows(w1_ref, w1_scr)
        _cast_rows(w3_ref, w3_scr)
        _cast_rows(w2_ref, w2_scr)

    @pl.when(i < nt_ref[0])
    def _():
        x = xs_ref[...].astype(BF16)
        a = jnp.dot(x, w1_scr[...], preferred_element_type=F32)
        b = jnp.dot(x, w3_scr[...], preferred_element_type=F32)
        act = (a * jax.nn.sigmoid(a)) * b
        ys_ref[...] = jnp.dot(act.astype(BF16), w2_scr[...], preferred_element_type=F32)

    @pl.when(i >= nt_ref[0])
    def _():
        ys_ref[...] = jnp.zeros_like(ys_ref)


def _experts(xs, tile_expert, tile_row, n_tiles, w1, w3, w2):
    rows, d = xs.shape
    f = w1.shape[2]
    grid_spec = pltpu.PrefetchScalarGridSpec(
        num_scalar_prefetch=3,
        grid=(rows // EXPERT_TILE,),
        in_specs=[pl.BlockSpec((EXPERT_TILE, d), lambda i, te, tr, nt: (tr[i], 0)),
                  pl.BlockSpec((1, d, f), lambda i, te, tr, nt: (te[i], 0, 0)),
                  pl.BlockSpec((1, d, f), lambda i, te, tr, nt: (te[i], 0, 0)),
                  pl.BlockSpec((1, f, d), lambda i, te, tr, nt: (te[i], 0, 0))],
        out_specs=pl.BlockSpec((EXPERT_TILE, d), lambda i, te, tr, nt: (i, 0)),
        scratch_shapes=[pltpu.VMEM((d, f), BF16), pltpu.VMEM((d, f), BF16), pltpu.VMEM((f, d), BF16)],
    )
    return pl.pallas_call(
        _experts_kernel,
        grid_spec=grid_spec,
        out_shape=jax.ShapeDtypeStruct((rows, d), F32),
        compiler_params=_params(("arbitrary",)),
        name="experts",
    )(tile_expert, tile_row, n_tiles, xs, w1, w3, w2)


CB_TM = 256


def _combine_kernel(pos0_ref, pos1_ref, mod_ref, fg_ref, x1_ref, w0_ref, w1_ref, ys_ref, o_ref,
                    y0_scr, y1_scr, sem):
    base = pl.program_id(0) * CB_TM

    def issue(r, c):
        _row_copy(ys_ref, pos0_ref[base + r], y0_scr, r, sem).start()
        _row_copy(ys_ref, pos1_ref[base + r], y1_scr, r, sem).start()
        return c
    lax.fori_loop(0, CB_TM, issue, 0)

    def drain(r, c):
        _row_copy(ys_ref, 0, y0_scr, 0, sem).wait()
        _row_copy(ys_ref, 0, y1_scr, 0, sem).wait()
        return c
    lax.fori_loop(0, CB_TM, drain, 0)

    gate2 = mod_ref[0, 5:6, :]
    y = w0_ref[...] * y0_scr[...] + w1_ref[...] * y1_scr[...]
    x2 = x1_ref[...] + gate2 * y
    o_ref[...] = (x2 * lax.rsqrt(jnp.mean(x2 * x2, axis=-1, keepdims=True) + EPS)) * fg_ref[...]


def _combine(x1, mod3, final_g, ys, pos0, pos1, cw0, cw1, seq):
    n, d = x1.shape
    tps = seq // CB_TM
    grid_spec = pltpu.PrefetchScalarGridSpec(
        num_scalar_prefetch=2,
        grid=(n // CB_TM,),
        in_specs=[pl.BlockSpec((1, N_MOD, d), lambda i, *_: (i // tps, 0, 0)),
                  pl.BlockSpec((1, d), lambda i, *_: (0, 0)),
                  pl.BlockSpec((CB_TM, d), lambda i, *_: (i, 0)),
                  pl.BlockSpec((CB_TM, 1), lambda i, *_: (i, 0)),
                  pl.BlockSpec((CB_TM, 1), lambda i, *_: (i, 0)),
                  pl.BlockSpec(memory_space=pl.ANY)],
        out_specs=pl.BlockSpec((CB_TM, d), lambda i, *_: (i, 0)),
        scratch_shapes=[pltpu.VMEM((CB_TM, d), F32), pltpu.VMEM((CB_TM, d), F32),
                        pltpu.SemaphoreType.DMA(())],
    )
    return pl.pallas_call(
        _combine_kernel,
        grid_spec=grid_spec,
        out_shape=jax.ShapeDtypeStruct((n, d), F32),
        compiler_params=_params(("arbitrary",)),
        name="combine",
    )(pos0, pos1, mod3, final_g.reshape(1, d), x1, cw0, cw1, ys)


def _layer(x2d, c, seq, ada_w, ada_b, norm1_g, w_in, gmlp_ln_g, gmlp_ln_b, gmlp_w_s, gmlp_b_s, rel_bias,
           w_branch_a, w_branch_b, w_out, norm2_g, w_group, w_expert, w1, w3, w2, final_g):
    n, d = x2d.shape
    nb = c.shape[0]
    mod3 = _ada(c, ada_w, ada_b).reshape(nb, N_MOD, d)

    proj = _in_proj(x2d, mod3, norm1_g, w_in.astype(BF16), seq)
    ya = _gmlp(proj, gmlp_ln_g, gmlp_ln_b, gmlp_w_s, gmlp_b_s)
    yb = _attn(proj, rel_bias, seq)

    wr_t = jnp.concatenate([w_group.T, jnp.zeros((SUBLANES - N_GROUPS, d), F32),
                            w_expert.transpose(0, 2, 1).reshape(N_EXPERTS, d)], axis=0).astype(BF16)
    x1, h2, logits_t = _merge(x2d, mod3, norm2_g, ya, yb, proj, w_branch_a.astype(BF16),
                              w_branch_b.astype(BF16), w_out.astype(BF16), wr_t, seq)

    eidx, rank, cw, counts = _route(logits_t)
    counts = counts[:, 0]
    padded = ((counts + EXPERT_TILE - 1) // EXPERT_TILE) * EXPERT_TILE
    ends = jnp.cumsum(padded)
    offs = ends - padded
    experts = jnp.arange(N_EXPERTS, dtype=I32)
    pos = jnp.sum(jnp.where(eidx[:, :, None] == experts, offs, 0), axis=-1) + rank
    rows_out = 2 * n + N_EXPERTS * EXPERT_TILE
    n_tiles_max = rows_out // EXPERT_TILE
    n_tiles = (ends[-1] // EXPERT_TILE).astype(I32)
    tile_row = jnp.minimum(jnp.arange(n_tiles_max, dtype=I32), n_tiles - 1)
    tile_expert = jnp.minimum(jnp.sum(ends[None, :] <= (tile_row * EXPERT_TILE)[:, None], axis=-1),
                              N_EXPERTS - 1).astype(I32)

    n_tiles = n_tiles.reshape(1)
    xs = _dispatch(h2, pos[0], pos[1], (offs + counts).astype(I32), (padded - counts).astype(I32), n_tiles,
                   rows_out)
    ys = _experts(xs, tile_expert, tile_row, n_tiles,
                  w1.reshape(N_EXPERTS, d, D_EXPERT), w3.reshape(N_EXPERTS, d, D_EXPERT),
                  w2.reshape(N_EXPERTS, D_EXPERT, d))
    return _combine(x1, mod3, final_g, ys, pos[0], pos[1], cw[0].reshape(n, 1), cw[1].reshape(n, 1), seq)


def kernel(x, c, ada_w, ada_b, norm1_g, w_in, gmlp_ln_g, gmlp_ln_b, gmlp_w_s, gmlp_b_s, rel_bias, w_branch_a,
           w_branch_b, w_out, norm2_g, w_group, w_expert, w1, w3, w2, final_g):
    b, s, d = x.shape
    out = _layer(x.reshape(b * s, d), c, s, ada_w[0], ada_b[0], norm1_g[0], w_in[0], gmlp_ln_g[0], gmlp_ln_b[0],
                 gmlp_w_s[0], gmlp_b_s[0], rel_bias[0], w_branch_a[0], w_branch_b[0], w_out[0], norm2_g[0],
                 w_group[0], w_expert[0], w1[0], w3[0], w2[0], final_g)
    return out.reshape(b, s, d)
```

```python
import numpy as np
import jax
import jax.numpy as jnp
from jax import lax
from jax.experimental import pallas as pl
from jax.experimental.pallas import tpu as pltpu

F32 = jnp.float32
BF16 = jnp.bfloat16
I32 = jnp.int32

D_MODEL = 2048
CHUNK = 64
EPS = 1e-6
NEG_INF = -1e30
GMLP_BLOCK = 128
GMLP_GROUPS = 8
GMLP_WIDTH = 1024
ATT_HEADS = 16
ATT_HEAD_DIM = 64
ATT_WIDTH = ATT_HEADS * ATT_HEAD_DIM
LEFT_CHUNKS = 8
MAX_REL = 256
N_GROUPS = 4
EXPERTS_PER_GROUP = 8
N_EXPERTS = N_GROUPS * EXPERTS_PER_GROUP
D_EXPERT = 512
N_MOD = 6

LANES = 128
SUBLANES = 8
VMEM_LIMIT = 56 * 1024 * 1024

COL_TILE = 1024
PROJ_COLS = 2 * D_MODEL + 2 * GMLP_WIDTH + 3 * ATT_WIDTH
GATE_TILES = 2 * D_MODEL // COL_TILE
U_TILE = GATE_TILES
V_TILE = GATE_TILES + 1
Q_TILE = GATE_TILES + 2
K_TILE = GATE_TILES + 3
VB_TILE = GATE_TILES + 4

ROUTER_ROWS = SUBLANES + N_EXPERTS

ATT_QBLK = 512
ATT_SUB = 2 * CHUNK
ATT_NSUB = ATT_QBLK // ATT_SUB
ATT_WIN = ATT_SUB + LEFT_CHUNKS * CHUNK

EXPERT_TILE = 256
TOKEN_ROWS = D_MODEL // LANES


def _params(sem, vmem=VMEM_LIMIT):
    return pltpu.CompilerParams(dimension_semantics=sem, vmem_limit_bytes=vmem)


def _rms_mod(x, g, scale, shift):
    y = x * lax.rsqrt(jnp.mean(x * x, axis=-1, keepdims=True) + EPS)
    return (y * g) * (1.0 + scale) + shift


def _ada_kernel(cb_ref, w_ref, b_ref, o_ref):
    nb, d, _ = cb_ref.shape
    tn = w_ref.shape[1]
    reps = tn // LANES

    def body(i, accs):
        r = pl.ds(pl.multiple_of(i * SUBLANES, SUBLANES), SUBLANES)
        w8 = w_ref[r, :]
        out = []
        for b in range(nb):
            cb = cb_ref[b, r, :]
            s = cb * jax.nn.sigmoid(cb)
            out.append(accs[b] + w8 * jnp.concatenate([s] * reps, axis=1))
        return tuple(out)

    init = tuple(jnp.zeros((SUBLANES, tn), F32) for _ in range(nb))
    accs = lax.fori_loop(0, d // SUBLANES, body, init)
    for b in range(nb):
        o_ref[b:b + 1, :] = jnp.sum(accs[b], axis=0, keepdims=True) + b_ref[...]


def _ada(c, ada_w, ada_b):
    nb, d = c.shape
    n = ada_w.shape[1]
    tn = 1024
    cb = jnp.broadcast_to(c[:, :, None], (nb, d, LANES))
    return pl.pallas_call(
        _ada_kernel,
        grid=(n // tn,),
        in_specs=[pl.BlockSpec((nb, d, LANES), lambda j: (0, 0, 0)),
                  pl.BlockSpec((d, tn), lambda j: (0, j)),
                  pl.BlockSpec((1, tn), lambda j: (0, j))],
        out_specs=pl.BlockSpec((nb, tn), lambda j: (0, j)),
        out_shape=jax.ShapeDtypeStruct((nb, n), F32),
        compiler_params=_params(("arbitrary",)),
        name="ada",
    )(cb, ada_w, ada_b.reshape(1, n))


IN_TM = 1024
IN_RB = 256


def _gelu(a):
    return 0.5 * a * (1.0 + lax.erf(a * np.float32(np.sqrt(0.5))))


def _in_proj_kernel(mod_ref, g_ref, x_ref, w_ref, o_ref, h_scr):
    j = pl.program_id(1)
    nrb = IN_TM // IN_RB

    @pl.when(j == 0)
    def _():
        shift = mod_ref[0, 0:1, :]
        scale = mod_ref[0, 1:2, :]
        g = g_ref[...]

        def body(rb, c):
            rows = pl.ds(pl.multiple_of(rb * IN_RB, IN_RB), IN_RB)
            h_scr[rows, :] = _rms_mod(x_ref[rows, :], g, scale, shift).astype(BF16)
            return c
        lax.fori_loop(0, nrb, body, 0)

    def run(epilogue):
        def body(rb, c):
            rows = pl.ds(pl.multiple_of(rb * IN_RB, IN_RB), IN_RB)
            acc = jnp.dot(h_scr[rows, :], w_ref[...], preferred_element_type=F32)
            o_ref[rows, :] = epilogue(acc).astype(o_ref.dtype)
            return c
        lax.fori_loop(0, nrb, body, 0)

    @pl.when(j < GATE_TILES)
    def _():
        run(jax.nn.sigmoid)

    @pl.when((j == U_TILE) | (j == V_TILE))
    def _():
        run(_gelu)

    @pl.when(j == Q_TILE)
    def _():
        run(lambda a: a * np.float32(ATT_HEAD_DIM ** -0.5))

    @pl.when(j > Q_TILE)
    def _():
        run(lambda a: a)


def _in_proj(x2d, mod3, norm_g, w_in_bf16, seq):
    n, d = x2d.shape
    cols = w_in_bf16.shape[1]
    n_tiles = cols // COL_TILE
    tiles_per_seq = seq // IN_TM
    return pl.pallas_call(
        _in_proj_kernel,
        grid=(n // IN_TM, n_tiles),
        in_specs=[pl.BlockSpec((1, N_MOD, d), lambda i, j: (i // tiles_per_seq, 0, 0)),
                  pl.BlockSpec((1, d), lambda i, j: (0, 0)),
                  pl.BlockSpec((IN_TM, d), lambda i, j: (i, 0)),
                  pl.BlockSpec((d, COL_TILE), lambda i, j: (0, (j + n_tiles - GATE_TILES) % n_tiles))],
        out_specs=pl.BlockSpec((IN_TM, COL_TILE), lambda i, j: (i, j)),
        out_shape=jax.ShapeDtypeStruct((n, cols), BF16),
        scratch_shapes=[pltpu.VMEM((IN_TM, d), BF16)],
        compiler_params=_params(("arbitrary", "arbitrary")),
        name="in_proj",
    )(mod3, norm_g.reshape(1, d), x2d, w_in_bf16)


GM_TM = 256


def _gmlp_kernel(u_ref, v_ref, lng_ref, lnb_ref, ws_ref, bs_ref, o_ref):
    t = lax.broadcasted_iota(I32, (GMLP_BLOCK, GMLP_BLOCK), 0)
    s = lax.broadcasted_iota(I32, (GMLP_BLOCK, GMLP_BLOCK), 1)
    causal = (s // CHUNK) <= (t // CHUNK)
    lng = lng_ref[...]
    lnb = lnb_ref[...]
    for blk in range(GM_TM // GMLP_BLOCK):
        rows = slice(blk * GMLP_BLOCK, (blk + 1) * GMLP_BLOCK)
        v = v_ref[rows, :].astype(F32)
        mu = jnp.mean(v, axis=-1, keepdims=True)
        vc = v - mu
        var = jnp.mean(vc * vc, axis=-1, keepdims=True)
        vln = ((vc * lax.rsqrt(var + EPS)) * lng + lnb).astype(BF16)
        for g in range(GMLP_GROUPS):
            cols = slice(g * LANES, (g + 1) * LANES)
            w = jnp.where(causal, ws_ref[g], 0.0).astype(BF16)
            mixed = jnp.dot(w, vln[:, cols], preferred_element_type=F32) + bs_ref[g]
            o_ref[rows, cols] = (u_ref[rows, cols].astype(F32) * mixed).astype(o_ref.dtype)


def _gmlp(proj, ln_g, ln_b, w_s, b_s):
    n = proj.shape[0]
    return pl.pallas_call(
        _gmlp_kernel,
        grid=(n // GM_TM,),
        in_specs=[pl.BlockSpec((GM_TM, GMLP_WIDTH), lambda i: (i, U_TILE)),
                  pl.BlockSpec((GM_TM, GMLP_WIDTH), lambda i: (i, V_TILE)),
                  pl.BlockSpec((1, GMLP_WIDTH), lambda i: (0, 0)),
                  pl.BlockSpec((1, GMLP_WIDTH), lambda i: (0, 0)),
                  pl.BlockSpec((GMLP_GROUPS, GMLP_BLOCK, GMLP_BLOCK), lambda i: (0, 0, 0)),
                  pl.BlockSpec((GMLP_GROUPS, GMLP_BLOCK, 1), lambda i: (0, 0, 0))],
        out_specs=pl.BlockSpec((GM_TM, GMLP_WIDTH), lambda i: (i, 0)),
        out_shape=jax.ShapeDtypeStruct((n, GMLP_WIDTH), BF16),
        compiler_params=_params(("arbitrary",)),
        name="gmlp",
    )(proj, proj, ln_g.reshape(1, -1), ln_b.reshape(1, -1), w_s,
      b_s.reshape(GMLP_GROUPS, GMLP_BLOCK, 1))


def _band_bias(rel_table):
    heads = rel_table.shape[0]
    r = np.arange(ATT_SUB)[:, None]
    w = np.arange(ATT_WIN)[None, :]
    j = w // CHUNK - r // CHUNK
    in_band = (j >= 0) & (j <= LEFT_CHUNKS)
    a = np.arange(ATT_NSUB)[:, None, None]
    in_seq = np.broadcast_to(w[None] >= ATT_QBLK - a * ATT_SUB, (ATT_NSUB, ATT_SUB, ATT_WIN))
    visible = np.stack([np.broadcast_to(in_band, in_seq.shape), in_band[None] & in_seq])[:, None]
    far = LEFT_CHUNKS * CHUNK + ATT_SUB - 1
    n_clipped = far - MAX_REL + 1
    table = rel_table.astype(F32)
    lo = MAX_REL - (ATT_WIN - 1 - LEFT_CHUNKS * CHUNK)
    diag = jnp.concatenate([jnp.broadcast_to(table[:, 2 * MAX_REL:], (heads, n_clipped)),
                            jnp.flip(table[:, lo:2 * MAX_REL], axis=1),
                            jnp.zeros((heads, 1), F32)], axis=1)
    span = diag.shape[1] - 1
    shifted = jnp.tile(diag, (1, ATT_SUB))[:, :ATT_SUB * span].reshape(heads, ATT_SUB, span)
    bias = shifted[:, :, ATT_SUB - 1:ATT_SUB - 1 + ATT_WIN]
    return jnp.where(jnp.asarray(visible), bias[None, :, None], NEG_INF)


def _attn_kernel(q_ref, kp_ref, kc_ref, vp_ref, vc_ref, bias_ref, o_ref, k_scr, v_scr):
    k_scr[0:ATT_QBLK, :] = kp_ref[...]
    k_scr[ATT_QBLK:, :] = kc_ref[...]
    v_scr[0:ATT_QBLK, :] = vp_ref[...]
    v_scr[ATT_QBLK:, :] = vc_ref[...]
    lane = lax.broadcasted_iota(I32, (ATT_SUB, LANES), 1)
    low = lane < ATT_HEAD_DIM
    for a in range(ATT_NSUB):
        q = q_ref[a * ATT_SUB:(a + 1) * ATT_SUB, :]
        zero = jnp.zeros_like(q)
        q2 = jnp.concatenate([jnp.where(low, q, zero), jnp.where(low, zero, q)], axis=0)
        win = slice(a * ATT_SUB, a * ATT_SUB + ATT_WIN)
        s = lax.dot_general(q2, k_scr[win, :], (((1,), (1,)), ((), ())), preferred_element_type=F32)
        s = s + bias_ref[0, :, a].reshape(2 * ATT_SUB, ATT_WIN)
        m = jnp.max(s, axis=-1, keepdims=True)
        p = jnp.exp(s - m)
        l = jnp.sum(p, axis=-1, keepdims=True)
        pv = jnp.dot(p.astype(BF16), v_scr[win, :], preferred_element_type=F32) / l
        o_ref[a * ATT_SUB:(a + 1) * ATT_SUB, :] = jnp.where(low, pv[:ATT_SUB], pv[ATT_SUB:]).astype(o_ref.dtype)


def _attn(proj, rel_table, seq):
    n = proj.shape[0]
    bps = seq // ATT_QBLK
    pairs = ATT_WIDTH // LANES
    qc, kc, vc = (Q_TILE * COL_TILE // LANES, K_TILE * COL_TILE // LANES, VB_TILE * COL_TILE // LANES)
    bias = _band_bias(rel_table)

    def prev(i):
        return jnp.where(i % bps == 0, i, i - 1)

    def first(i):
        return jnp.where(i % bps == 0, 1, 0)

    blk = (ATT_QBLK, LANES)
    return pl.pallas_call(
        _attn_kernel,
        grid=(pairs, n // ATT_QBLK),
        in_specs=[pl.BlockSpec(blk, lambda h, i: (i, qc + h)),
                  pl.BlockSpec(blk, lambda h, i: (prev(i), kc + h)),
                  pl.BlockSpec(blk, lambda h, i: (i, kc + h)),
                  pl.BlockSpec(blk, lambda h, i: (prev(i), vc + h)),
                  pl.BlockSpec(blk, lambda h, i: (i, vc + h)),
                  pl.BlockSpec((1, 2, ATT_NSUB, ATT_SUB, ATT_WIN), lambda h, i: (first(i), h, 0, 0, 0))],
        out_specs=pl.BlockSpec(blk, lambda h, i: (i, h)),
        out_shape=jax.ShapeDtypeStruct((n, ATT_WIDTH), BF16),
        scratch_shapes=[pltpu.VMEM((2 * ATT_QBLK, LANES), BF16),
                        pltpu.VMEM((2 * ATT_QBLK, LANES), BF16)],
        compiler_params=_params(("arbitrary", "arbitrary")),
        name="attn",
    )(proj, proj, proj, proj, proj, bias)


MG_TM = 256


def _merge_kernel(mod_ref, g_ref, x_ref, ya_ref, yb_ref, ga_ref, gb_ref, wa_ref, wb_ref, wo_ref, wr_ref,
                  x1_ref, h2_ref, lt_ref):
    ya = jnp.dot(ya_ref[...], wa_ref[...], preferred_element_type=F32)
    yb = jnp.dot(yb_ref[...], wb_ref[...], preferred_element_type=F32)
    m = ga_ref[...].astype(F32) * ya + gb_ref[...].astype(F32) * yb
    mixed = jnp.dot(m.astype(BF16), wo_ref[...], preferred_element_type=F32)
    gate1 = mod_ref[0, 2:3, :]
    shift2 = mod_ref[0, 3:4, :]
    scale2 = mod_ref[0, 4:5, :]
    x1 = x_ref[...] + gate1 * mixed
    x1_ref[...] = x1
    h2 = _rms_mod(x1, g_ref[...], scale2, shift2).astype(BF16)
    h2_ref[...] = h2.reshape(h2_ref.shape)
    lt_ref[...] = lax.dot_general(wr_ref[...], h2, (((1,), (1,)), ((), ())), preferred_element_type=F32)


def _merge(x2d, mod3, norm_g, ya, yb, proj, wa, wb, wo, wr_t, seq):
    n, d = x2d.shape
    tps = seq // MG_TM
    const = lambda shape: pl.BlockSpec(shape, lambda i: (0,) * len(shape), pipeline_mode=pl.Buffered(1))
    return pl.pallas_call(
        _merge_kernel,
        grid=(n // MG_TM,),
        in_specs=[pl.BlockSpec((1, N_MOD, d), lambda i: (i // tps, 0, 0)),
                  pl.BlockSpec((1, d), lambda i: (0, 0)),
                  pl.BlockSpec((MG_TM, d), lambda i: (i, 0)),
                  pl.BlockSpec((MG_TM, GMLP_WIDTH), lambda i: (i, 0)),
                  pl.BlockSpec((MG_TM, ATT_WIDTH), lambda i: (i, 0)),
                  pl.BlockSpec((MG_TM, d), lambda i: (i, 0)),
                  pl.BlockSpec((MG_TM, d), lambda i: (i, 1)),
                  const((GMLP_WIDTH, d)), const((ATT_WIDTH, d)), const((d, d)),
                  const((ROUTER_ROWS, d))],
        out_specs=[pl.BlockSpec((MG_TM, d), lambda i: (i, 0)),
                   pl.BlockSpec((MG_TM, TOKEN_ROWS, LANES), lambda i: (i, 0, 0)),
                   pl.BlockSpec((ROUTER_ROWS, MG_TM), lambda i: (0, i))],
        out_shape=[jax.ShapeDtypeStruct((n, d), F32),
                   jax.ShapeDtypeStruct((n, TOKEN_ROWS, LANES), BF16),
                   jax.ShapeDtypeStruct((ROUTER_ROWS, n), F32)],
        compiler_params=_params(("arbitrary",)),
        name="merge",
    )(mod3, norm_g.reshape(1, d), x2d, ya, yb, proj, proj, wa, wb, wo, wr_t)


RT_TN = 512


def _first_argmax(vals, vmax, nrows):
    rows = lax.broadcasted_iota(I32, vals.shape, 0)
    return jnp.min(jnp.where(vals == vmax, rows, nrows), axis=0, keepdims=True)


def _route_kernel(lt_ref, e_ref, r_ref, w_ref, cnt_ref, carry_scr):
    @pl.when(pl.program_id(0) == 0)
    def _():
        carry_scr[...] = jnp.zeros_like(carry_scr)

    gl = lt_ref[0:N_GROUPS, :]
    gmax = jnp.max(gl, axis=0, keepdims=True)
    gidx = _first_argmax(gl, gmax, N_GROUPS)
    gw = 1.0 / jnp.sum(jnp.exp(gl - gmax), axis=0, keepdims=True)

    esel = lt_ref[SUBLANES:SUBLANES + EXPERTS_PER_GROUP, :]
    for g in range(1, N_GROUPS):
        lo = SUBLANES + g * EXPERTS_PER_GROUP
        esel = jnp.where(gidx == g, lt_ref[lo:lo + EXPERTS_PER_GROUP, :], esel)
    rows8 = lax.broadcasted_iota(I32, esel.shape, 0)
    m1 = jnp.max(esel, axis=0, keepdims=True)
    i1 = _first_argmax(esel, m1, EXPERTS_PER_GROUP)
    rest = jnp.where(rows8 == i1, -jnp.inf, esel)
    m2 = jnp.max(rest, axis=0, keepdims=True)
    i2 = _first_argmax(rest, m2, EXPERTS_PER_GROUP)
    z = jnp.exp(m2 - m1)
    w_top = 1.0 / (1.0 + z)
    e0 = gidx * EXPERTS_PER_GROUP + i1
    e1 = gidx * EXPERTS_PER_GROUP + i2
    e_ref[0:1, :] = e0
    e_ref[1:2, :] = e1
    w_ref[0:1, :] = gw * w_top
    w_ref[1:2, :] = gw * (z * w_top)

    rows_e = lax.broadcasted_iota(I32, (N_EXPERTS, RT_TN), 0)
    oh0 = rows_e == e0
    oh1 = rows_e == e1
    oh = jnp.where(oh0 | oh1, 1.0, 0.0)
    src = lax.broadcasted_iota(I32, (RT_TN, RT_TN), 0)
    dst = lax.broadcasted_iota(I32, (RT_TN, RT_TN), 1)
    before = jnp.where(src < dst, 1.0, 0.0).astype(BF16)
    carry = carry_scr[...]
    prefix = jnp.dot(oh.astype(BF16), before, preferred_element_type=F32) + carry[:, 0:1]
    r_ref[0:1, :] = jnp.sum(jnp.where(oh0, prefix, 0.0), axis=0, keepdims=True).astype(I32)
    r_ref[1:2, :] = jnp.sum(jnp.where(oh1, prefix, 0.0), axis=0, keepdims=True).astype(I32)
    carry = carry + jnp.sum(oh, axis=1, keepdims=True)
    carry_scr[...] = carry
    cnt_ref[...] = carry.astype(I32)


def _route(logits_t):
    n = logits_t.shape[1]
    slot = pl.BlockSpec((2, RT_TN), lambda i: (0, i))
    return pl.pallas_call(
        _route_kernel,
        grid=(n // RT_TN,),
        in_specs=[pl.BlockSpec((ROUTER_ROWS, RT_TN), lambda i: (0, i))],
        out_specs=[slot, slot, slot, pl.BlockSpec((N_EXPERTS, LANES), lambda i: (0, 0))],
        out_shape=[jax.ShapeDtypeStruct((2, n), I32),
                   jax.ShapeDtypeStruct((2, n), I32),
                   jax.ShapeDtypeStruct((2, n), F32),
                   jax.ShapeDtypeStruct((N_EXPERTS, LANES), I32)],
        scratch_shapes=[pltpu.VMEM((N_EXPERTS, LANES), F32)],
        compiler_params=_params(("arbitrary",)),
        name="route",
    )(logits_t)


DP_TM = 256


DMA_UNROLL = 8


def _token_copy(src_ref, src_row, dst_ref, dst_row, sem):
    return pltpu.make_async_copy(src_ref.at[src_row], dst_ref.at[dst_row], sem)


def _dispatch_kernel(pos0_ref, pos1_ref, pad_start_ref, pad_n_ref, nt_ref, h_ref, xs_ref, zero_scr, sem, zsem, tsem):
    i = pl.program_id(0)
    n_tiles_max = xs_ref.shape[0] // EXPERT_TILE

    def tail_copy(t):
        return pltpu.make_async_copy(zero_scr, xs_ref.at[pl.ds(t * EXPERT_TILE, EXPERT_TILE)], tsem)

    @pl.when(i == 0)
    def _():
        zero_scr[...] = jnp.zeros_like(zero_scr)
        for e in range(N_EXPERTS):
            start = pad_start_ref[e]

            def issue(r, c, start=start):
                _token_copy(zero_scr, 0, xs_ref, start + r, zsem).start()
                return c
            lax.fori_loop(0, pad_n_ref[e], issue, 0)

        def issue_tail(t, c):
            tail_copy(t).start()
            return c
        lax.fori_loop(nt_ref[0], n_tiles_max, issue_tail, 0)

        for e in range(N_EXPERTS):
            def drain(r, c):
                _token_copy(zero_scr, 0, xs_ref, 0, zsem).wait()
                return c
            lax.fori_loop(0, pad_n_ref[e], drain, 0)

        def drain_tail(t, c):
            tail_copy(t).wait()
            return c
        lax.fori_loop(nt_ref[0], n_tiles_max, drain_tail, 0)

    base = i * DP_TM

    def issue(r, c):
        _token_copy(h_ref, r, xs_ref, pos0_ref[base + r], sem).start()
        _token_copy(h_ref, r, xs_ref, pos1_ref[base + r], sem).start()
        return c
    lax.fori_loop(0, DP_TM, issue, 0, unroll=DMA_UNROLL)

    def drain(r, c):
        _token_copy(h_ref, 0, xs_ref, 0, sem).wait()
        _token_copy(h_ref, 0, xs_ref, 0, sem).wait()
        return c
    lax.fori_loop(0, DP_TM, drain, 0, unroll=DMA_UNROLL)


def _dispatch(h2, pos0, pos1, pad_start, pad_n, n_tiles, rows_out):
    n = h2.shape[0]
    grid_spec = pltpu.PrefetchScalarGridSpec(
        num_scalar_prefetch=5,
        grid=(n // DP_TM,),
        in_specs=[pl.BlockSpec((DP_TM, TOKEN_ROWS, LANES), lambda i, *_: (i, 0, 0))],
        out_specs=pl.BlockSpec(memory_space=pl.ANY),
        scratch_shapes=[pltpu.VMEM((EXPERT_TILE, TOKEN_ROWS, LANES), BF16),
                        pltpu.SemaphoreType.DMA(()), pltpu.SemaphoreType.DMA(()),
                        pltpu.SemaphoreType.DMA(())],
    )
    return pl.pallas_call(
        _dispatch_kernel,
        grid_spec=grid_spec,
        out_shape=jax.ShapeDtypeStruct((rows_out, TOKEN_ROWS, LANES), BF16),
        compiler_params=pltpu.CompilerParams(dimension_semantics=("arbitrary",),
                                             vmem_limit_bytes=VMEM_LIMIT, has_side_effects=True),
        name="dispatch",
    )(pos0, pos1, pad_start, pad_n, n_tiles, h2)


EXPERT_CAST_ROWS = 256


def _cast_rows(src_ref, dst_ref):
    rows_total = dst_ref.shape[0]

    def body(r, c):
        rows = pl.ds(pl.multiple_of(r * EXPERT_CAST_ROWS, EXPERT_CAST_ROWS), EXPERT_CAST_ROWS)
        dst_ref[rows, :] = src_ref[0, rows, :].astype(BF16)
        return c
    lax.fori_loop(0, rows_total // EXPERT_CAST_ROWS, body, 0)


def _experts_kernel(te_ref, tr_ref, nt_ref, xs_ref, w1_ref, w3_ref, w2_ref, ys_ref, w1_scr, w3_scr, w2_scr):
    i = pl.program_id(0)

    @pl.when((i == 0) | (te_ref[i] != te_ref[jnp.maximum(i - 1, 0)]))
    def _():
        _cast_rows(w1_ref, w1_scr)
        _cast_rows(w3_ref, w3_scr)
        _cast_rows(w2_ref, w2_scr)

    @pl.when(i < nt_ref[0])
    def _():
        x = xs_ref[...].reshape(EXPERT_TILE, D_MODEL)
        a = jnp.dot(x, w1_scr[...], preferred_element_type=F32)
        b = jnp.dot(x, w3_scr[...], preferred_element_type=F32)
        act = (a * jax.nn.sigmoid(a)) * b
        y = jnp.dot(act.astype(BF16), w2_scr[...], preferred_element_type=F32)
        ys_ref[...] = y.astype(BF16).reshape(ys_ref.shape)

    @pl.when(i >= nt_ref[0])
    def _():
        ys_ref[...] = jnp.zeros_like(ys_ref)


def _experts(xs, tile_expert, tile_row, n_tiles, w1, w3, w2):
    rows = xs.shape[0]
    _, d, f = w1.shape
    tile = (EXPERT_TILE, TOKEN_ROWS, LANES)
    grid_spec = pltpu.PrefetchScalarGridSpec(
        num_scalar_prefetch=3,
        grid=(rows // EXPERT_TILE,),
        in_specs=[pl.BlockSpec(tile, lambda i, te, tr, nt: (tr[i], 0, 0)),
                  pl.BlockSpec((1, d, f), lambda i, te, tr, nt: (te[i], 0, 0)),
                  pl.BlockSpec((1, d, f), lambda i, te, tr, nt: (te[i], 0, 0)),
                  pl.BlockSpec((1, f, d), lambda i, te, tr, nt: (te[i], 0, 0))],
        out_specs=pl.BlockSpec(tile, lambda i, te, tr, nt: (i, 0, 0)),
        scratch_shapes=[pltpu.VMEM((d, f), BF16), pltpu.VMEM((d, f), BF16), pltpu.VMEM((f, d), BF16)],
    )
    return pl.pallas_call(
        _experts_kernel,
        grid_spec=grid_spec,
        out_shape=jax.ShapeDtypeStruct(xs.shape, BF16),
        compiler_params=_params(("arbitrary",)),
        name="experts",
    )(tile_expert, tile_row, n_tiles, xs, w1, w3, w2)


CB_TM = 256


def _combine_kernel(pos0_ref, pos1_ref, mod_ref, fg_ref, x1_ref, w0_ref, w1_ref, ys_ref, o_ref,
                    y0_scr, y1_scr, sems):
    i = pl.program_id(0)
    slot = i % 2

    def gather(step, buf):
        base = step * CB_TM

        def issue(r, c):
            _token_copy(ys_ref, pos0_ref[base + r], y0_scr.at[buf], r, sems.at[buf]).start()
            _token_copy(ys_ref, pos1_ref[base + r], y1_scr.at[buf], r, sems.at[buf]).start()
            return c
        lax.fori_loop(0, CB_TM, issue, 0, unroll=DMA_UNROLL)

    @pl.when(i == 0)
    def _():
        gather(i, slot)

    @pl.when(i + 1 < pl.num_programs(0))
    def _():
        gather(i + 1, 1 - slot)

    def drain(r, c):
        _token_copy(ys_ref, 0, y0_scr.at[slot], 0, sems.at[slot]).wait()
        _token_copy(ys_ref, 0, y1_scr.at[slot], 0, sems.at[slot]).wait()
        return c
    lax.fori_loop(0, CB_TM, drain, 0, unroll=DMA_UNROLL)

    gate2 = mod_ref[0, 5:6, :]
    y0 = y0_scr[slot].reshape(CB_TM, D_MODEL).astype(F32)
    y1 = y1_scr[slot].reshape(CB_TM, D_MODEL).astype(F32)
    y = w0_ref[...] * y0 + w1_ref[...] * y1
    x2 = x1_ref[...] + gate2 * y
    o_ref[...] = (x2 * lax.rsqrt(jnp.mean(x2 * x2, axis=-1, keepdims=True) + EPS)) * fg_ref[...]


def _combine(x1, mod3, final_g, ys, pos0, pos1, cw0, cw1, seq):
    n, d = x1.shape
    tps = seq // CB_TM
    grid_spec = pltpu.PrefetchScalarGridSpec(
        num_scalar_prefetch=2,
        grid=(n // CB_TM,),
        in_specs=[pl.BlockSpec((1, N_MOD, d), lambda i, *_: (i // tps, 0, 0)),
                  pl.BlockSpec((1, d), lambda i, *_: (0, 0)),
                  pl.BlockSpec((CB_TM, d), lambda i, *_: (i, 0)),
                  pl.BlockSpec((CB_TM, 1), lambda i, *_: (i, 0)),
                  pl.BlockSpec((CB_TM, 1), lambda i, *_: (i, 0)),
                  pl.BlockSpec(memory_space=pl.ANY)],
        out_specs=pl.BlockSpec((CB_TM, d), lambda i, *_: (i, 0)),
        scratch_shapes=[pltpu.VMEM((2, CB_TM, TOKEN_ROWS, LANES), BF16),
                        pltpu.VMEM((2, CB_TM, TOKEN_ROWS, LANES), BF16),
                        pltpu.SemaphoreType.DMA((2,))],
    )
    return pl.pallas_call(
        _combine_kernel,
        grid_spec=grid_spec,
        out_shape=jax.ShapeDtypeStruct((n, d), F32),
        compiler_params=_params(("arbitrary",)),
        name="combine",
    )(pos0, pos1, mod3, final_g.reshape(1, d), x1, cw0, cw1, ys)


def _layer(x2d, c, seq, ada_w, ada_b, norm1_g, w_in, gmlp_ln_g, gmlp_ln_b, gmlp_w_s, gmlp_b_s, rel_bias,
           w_branch_a, w_branch_b, w_out, norm2_g, w_group, w_expert, w1, w3, w2, final_g):
    n, d = x2d.shape
    nb = c.shape[0]
    mod3 = _ada(c, ada_w, ada_b).reshape(nb, N_MOD, d)

    proj = _in_proj(x2d, mod3, norm1_g, w_in.astype(BF16), seq)
    ya = _gmlp(proj, gmlp_ln_g, gmlp_ln_b, gmlp_w_s, gmlp_b_s)
    yb = _attn(proj, rel_bias, seq)

    wr_t = jnp.concatenate([w_group.T, jnp.zeros((SUBLANES - N_GROUPS, d), F32),
                            w_expert.transpose(0, 2, 1).reshape(N_EXPERTS, d)], axis=0).astype(BF16)
    x1, h2, logits_t = _merge(x2d, mod3, norm2_g, ya, yb, proj, w_branch_a.astype(BF16),
                              w_branch_b.astype(BF16), w_out.astype(BF16), wr_t, seq)

    eidx, rank, cw, counts = _route(logits_t)
    counts = counts[:, 0]
    padded = ((counts + EXPERT_TILE - 1) // EXPERT_TILE) * EXPERT_TILE
    ends = jnp.cumsum(padded)
    offs = ends - padded
    experts = jnp.arange(N_EXPERTS, dtype=I32)
    pos = jnp.sum(jnp.where(eidx[:, :, None] == experts, offs, 0), axis=-1) + rank
    rows_out = 2 * n + N_EXPERTS * EXPERT_TILE
    n_tiles_max = rows_out // EXPERT_TILE
    n_tiles = (ends[-1] // EXPERT_TILE).astype(I32)
    tile_row = jnp.minimum(jnp.arange(n_tiles_max, dtype=I32), n_tiles - 1)
    tile_expert = jnp.minimum(jnp.sum(ends[None, :] <= (tile_row * EXPERT_TILE)[:, None], axis=-1),
                              N_EXPERTS - 1).astype(I32)

    n_tiles = n_tiles.reshape(1)
    xs = _dispatch(h2, pos[0], pos[1], (offs + counts).astype(I32), (padded - counts).astype(I32), n_tiles,
                   rows_out)
    ys = _experts(xs, tile_expert, tile_row, n_tiles,
                  w1.reshape(N_EXPERTS, d, D_EXPERT), w3.reshape(N_EXPERTS, d, D_EXPERT),
                  w2.reshape(N_EXPERTS, D_EXPERT, d))
    return _combine(x1, mod3, final_g, ys, pos[0], pos[1], cw[0].reshape(n, 1), cw[1].reshape(n, 1), seq)


def kernel(x, c, ada_w, ada_b, norm1_g, w_in, gmlp_ln_g, gmlp_ln_b, gmlp_w_s, gmlp_b_s, rel_bias, w_branch_a,
           w_branch_b, w_out, norm2_g, w_group, w_expert, w1, w3, w2, final_g):
    b, s, d = x.shape
    out = _layer(x.reshape(b * s, d), c, s, ada_w[0], ada_b[0], norm1_g[0], w_in[0], gmlp_ln_g[0], gmlp_ln_b[0],
                 gmlp_w_s[0], gmlp_b_s[0], rel_bias[0], w_branch_a[0], w_branch_b[0], w_out[0], norm2_g[0],
                 w_group[0], w_expert[0], w1[0], w3[0], w2[0], final_g)
    return out.reshape(b, s, d)
```

```python
import numpy as np
import jax
import jax.numpy as jnp
from jax import lax
from jax.experimental import pallas as pl
from jax.experimental.pallas import tpu as pltpu

F32 = jnp.float32
BF16 = jnp.bfloat16
I32 = jnp.int32

D_MODEL = 2048
CHUNK = 64
EPS = 1e-6
NEG_INF = -1e30
GMLP_BLOCK = 128
GMLP_GROUPS = 8
GMLP_WIDTH = 1024
ATT_HEADS = 16
ATT_HEAD_DIM = 64
ATT_WIDTH = ATT_HEADS * ATT_HEAD_DIM
LEFT_CHUNKS = 8
MAX_REL = 256
N_GROUPS = 4
EXPERTS_PER_GROUP = 8
N_EXPERTS = N_GROUPS * EXPERTS_PER_GROUP
D_EXPERT = 512
N_MOD = 6

LANES = 128
SUBLANES = 8
VMEM_LIMIT = 56 * 1024 * 1024

COL_TILE = 1024
PROJ_COLS = 2 * D_MODEL + 2 * GMLP_WIDTH + 3 * ATT_WIDTH
GATE_TILES = 2 * D_MODEL // COL_TILE
U_TILE = GATE_TILES
V_TILE = GATE_TILES + 1
Q_TILE = GATE_TILES + 2
K_TILE = GATE_TILES + 3
VB_TILE = GATE_TILES + 4

ROUTER_ROWS = SUBLANES + N_EXPERTS

ATT_QBLK = 512
ATT_SUB = 2 * CHUNK
ATT_NSUB = ATT_QBLK // ATT_SUB
ATT_WIN = ATT_SUB + LEFT_CHUNKS * CHUNK

EXPERT_TILE = 256
TOKEN_ROWS = D_MODEL // LANES


def _params(sem, vmem=VMEM_LIMIT):
    return pltpu.CompilerParams(dimension_semantics=sem, vmem_limit_bytes=vmem)


def _rms_mod(x, g, scale, shift):
    y = x * lax.rsqrt(jnp.mean(x * x, axis=-1, keepdims=True) + EPS)
    return (y * g) * (1.0 + scale) + shift


def _ada_kernel(cb_ref, w_ref, b_ref, o_ref):
    nb, d, _ = cb_ref.shape
    tn = w_ref.shape[1]
    reps = tn // LANES

    def body(i, accs):
        r = pl.ds(pl.multiple_of(i * SUBLANES, SUBLANES), SUBLANES)
        w8 = w_ref[r, :]
        out = []
        for b in range(nb):
            cb = cb_ref[b, r, :]
            s = cb * jax.nn.sigmoid(cb)
            out.append(accs[b] + w8 * jnp.concatenate([s] * reps, axis=1))
        return tuple(out)

    init = tuple(jnp.zeros((SUBLANES, tn), F32) for _ in range(nb))
    accs = lax.fori_loop(0, d // SUBLANES, body, init)
    for b in range(nb):
        o_ref[b:b + 1, :] = jnp.sum(accs[b], axis=0, keepdims=True) + b_ref[...]


def _ada(c, ada_w, ada_b):
    nb, d = c.shape
    n = ada_w.shape[1]
    tn = 1024
    cb = jnp.broadcast_to(c[:, :, None], (nb, d, LANES))
    return pl.pallas_call(
        _ada_kernel,
        grid=(n // tn,),
        in_specs=[pl.BlockSpec((nb, d, LANES), lambda j: (0, 0, 0)),
                  pl.BlockSpec((d, tn), lambda j: (0, j)),
                  pl.BlockSpec((1, tn), lambda j: (0, j))],
        out_specs=pl.BlockSpec((nb, tn), lambda j: (0, j)),
        out_shape=jax.ShapeDtypeStruct((nb, n), F32),
        compiler_params=_params(("arbitrary",)),
        name="ada",
    )(cb, ada_w, ada_b.reshape(1, n))


IN_TM = 1024
IN_RB = 512
IN_STAT_RB = 128
IN_NORM_RB = 16


def _gelu(a):
    return 0.5 * a * (1.0 + lax.erf(a * np.float32(np.sqrt(0.5))))


def _in_proj_kernel(mod_ref, g_ref, x_ref, w_ref, o_ref, h_scr, gain_scr, shift_scr, inv_scr):
    j = pl.program_id(1)
    nrb = IN_TM // IN_RB

    @pl.when(j == 0)
    def _():
        d = x_ref.shape[1]
        gain_scr[...] = jnp.broadcast_to(g_ref[...] * (1.0 + mod_ref[0, 1:2, :]), gain_scr.shape)
        shift_scr[...] = jnp.broadcast_to(mod_ref[0, 0:1, :], shift_scr.shape)

        def stats(rb, c):
            rows = pl.ds(pl.multiple_of(rb * IN_STAT_RB, IN_STAT_RB), IN_STAT_RB)
            sq = jnp.zeros((IN_STAT_RB, LANES), F32)
            for k in range(d // LANES):
                xk = x_ref[rows, k * LANES:(k + 1) * LANES]
                sq = sq + xk * xk
            inv = lax.rsqrt(jnp.sum(sq, axis=-1, keepdims=True) * (1.0 / d) + EPS)
            inv_scr[rows, :] = jnp.broadcast_to(inv, (IN_STAT_RB, LANES))
            return c
        lax.fori_loop(0, IN_TM // IN_STAT_RB, stats, 0)

        def apply(rb, c):
            rows = pl.ds(pl.multiple_of(rb * IN_NORM_RB, IN_NORM_RB), IN_NORM_RB)
            inv = inv_scr[rows, :]
            for k in range(d // LANES):
                cols = slice(k * LANES, (k + 1) * LANES)
                y = (x_ref[rows, cols] * inv) * gain_scr[:, cols] + shift_scr[:, cols]
                h_scr[rows, cols] = y.astype(BF16)
            return c
        lax.fori_loop(0, IN_TM // IN_NORM_RB, apply, 0, unroll=2)

    def run(epilogue):
        def body(rb, c):
            rows = pl.ds(pl.multiple_of(rb * IN_RB, IN_RB), IN_RB)
            acc = jnp.dot(h_scr[rows, :], w_ref[...], preferred_element_type=F32)
            o_ref[rows, :] = epilogue(acc).astype(o_ref.dtype)
            return c
        lax.fori_loop(0, nrb, body, 0)

    @pl.when(j < GATE_TILES)
    def _():
        run(jax.nn.sigmoid)

    @pl.when((j == U_TILE) | (j == V_TILE))
    def _():
        run(_gelu)

    @pl.when(j == Q_TILE)
    def _():
        run(lambda a: a * np.float32(ATT_HEAD_DIM ** -0.5))

    @pl.when(j > Q_TILE)
    def _():
        run(lambda a: a)


def _in_proj(x2d, mod3, norm_g, w_in_bf16, seq):
    n, d = x2d.shape
    cols = w_in_bf16.shape[1]
    n_tiles = cols // COL_TILE
    tiles_per_seq = seq // IN_TM
    return pl.pallas_call(
        _in_proj_kernel,
        grid=(n // IN_TM, n_tiles),
        in_specs=[pl.BlockSpec((1, N_MOD, d), lambda i, j: (i // tiles_per_seq, 0, 0)),
                  pl.BlockSpec((1, d), lambda i, j: (0, 0)),
                  pl.BlockSpec((IN_TM, d), lambda i, j: (i, 0)),
                  pl.BlockSpec((d, COL_TILE), lambda i, j: (0, (j + n_tiles - GATE_TILES) % n_tiles))],
        out_specs=pl.BlockSpec((IN_TM, COL_TILE), lambda i, j: (i, j)),
        out_shape=jax.ShapeDtypeStruct((n, cols), BF16),
        scratch_shapes=[pltpu.VMEM((IN_TM, d), BF16), pltpu.VMEM((IN_NORM_RB, d), F32),
                        pltpu.VMEM((IN_NORM_RB, d), F32), pltpu.VMEM((IN_TM, LANES), F32)],
        compiler_params=_params(("arbitrary", "arbitrary")),
        name="in_proj",
    )(mod3, norm_g.reshape(1, d), x2d, w_in_bf16)


GM_TM = 256


def _gmlp_kernel(u_ref, v_ref, lng_ref, lnb_ref, ws_ref, bs_ref, o_ref):
    t = lax.broadcasted_iota(I32, (GMLP_BLOCK, GMLP_BLOCK), 0)
    s = lax.broadcasted_iota(I32, (GMLP_BLOCK, GMLP_BLOCK), 1)
    causal = (s // CHUNK) <= (t // CHUNK)
    lng = lng_ref[...]
    lnb = lnb_ref[...]
    for blk in range(GM_TM // GMLP_BLOCK):
        rows = slice(blk * GMLP_BLOCK, (blk + 1) * GMLP_BLOCK)
        v = v_ref[rows, :].astype(F32)
        mu = jnp.mean(v, axis=-1, keepdims=True)
        vc = v - mu
        var = jnp.mean(vc * vc, axis=-1, keepdims=True)
        vln = ((vc * lax.rsqrt(var + EPS)) * lng + lnb).astype(BF16)
        for g in range(GMLP_GROUPS):
            cols = slice(g * LANES, (g + 1) * LANES)
            w = jnp.where(causal, ws_ref[g], 0.0).astype(BF16)
            mixed = jnp.dot(w, vln[:, cols], preferred_element_type=F32) + bs_ref[g]
            o_ref[rows, cols] = (u_ref[rows, cols].astype(F32) * mixed).astype(o_ref.dtype)


def _gmlp(proj, ln_g, ln_b, w_s, b_s):
    n = proj.shape[0]
    return pl.pallas_call(
        _gmlp_kernel,
        grid=(n // GM_TM,),
        in_specs=[pl.BlockSpec((GM_TM, GMLP_WIDTH), lambda i: (i, U_TILE)),
                  pl.BlockSpec((GM_TM, GMLP_WIDTH), lambda i: (i, V_TILE)),
                  pl.BlockSpec((1, GMLP_WIDTH), lambda i: (0, 0)),
                  pl.BlockSpec((1, GMLP_WIDTH), lambda i: (0, 0)),
                  pl.BlockSpec((GMLP_GROUPS, GMLP_BLOCK, GMLP_BLOCK), lambda i: (0, 0, 0)),
                  pl.BlockSpec((GMLP_GROUPS, GMLP_BLOCK, 1), lambda i: (0, 0, 0))],
        out_specs=pl.BlockSpec((GM_TM, GMLP_WIDTH), lambda i: (i, 0)),
        out_shape=jax.ShapeDtypeStruct((n, GMLP_WIDTH), BF16),
        compiler_params=_params(("arbitrary",)),
        name="gmlp",
    )(proj, proj, ln_g.reshape(1, -1), ln_b.reshape(1, -1), w_s,
      b_s.reshape(GMLP_GROUPS, GMLP_BLOCK, 1))


def _band_bias(rel_table):
    heads = rel_table.shape[0]
    r = np.arange(ATT_SUB)[:, None]
    w = np.arange(ATT_WIN)[None, :]
    j = w // CHUNK - r // CHUNK
    in_band = (j >= 0) & (j <= LEFT_CHUNKS)
    a = np.arange(ATT_NSUB)[:, None, None]
    in_seq = np.broadcast_to(w[None] >= ATT_QBLK - a * ATT_SUB, (ATT_NSUB, ATT_SUB, ATT_WIN))
    visible = np.stack([np.broadcast_to(in_band, in_seq.shape), in_band[None] & in_seq])[:, None]
    far = LEFT_CHUNKS * CHUNK + ATT_SUB - 1
    n_clipped = far - MAX_REL + 1
    table = rel_table.astype(F32)
    lo = MAX_REL - (ATT_WIN - 1 - LEFT_CHUNKS * CHUNK)
    diag = jnp.concatenate([jnp.broadcast_to(table[:, 2 * MAX_REL:], (heads, n_clipped)),
                            jnp.flip(table[:, lo:2 * MAX_REL], axis=1),
                            jnp.zeros((heads, 1), F32)], axis=1)
    span = diag.shape[1] - 1
    shifted = jnp.tile(diag, (1, ATT_SUB))[:, :ATT_SUB * span].reshape(heads, ATT_SUB, span)
    bias = shifted[:, :, ATT_SUB - 1:ATT_SUB - 1 + ATT_WIN]
    return jnp.where(jnp.asarray(visible), bias[None, :, None], NEG_INF)


def _attn_kernel(q_ref, kp_ref, kc_ref, vp_ref, vc_ref, bias_ref, o_ref, k_scr, v_scr):
    k_scr[0:ATT_QBLK, :] = kp_ref[...]
    k_scr[ATT_QBLK:, :] = kc_ref[...]
    v_scr[0:ATT_QBLK, :] = vp_ref[...]
    v_scr[ATT_QBLK:, :] = vc_ref[...]
    lane = lax.broadcasted_iota(I32, (ATT_SUB, LANES), 1)
    low = lane < ATT_HEAD_DIM

    def window(a):
        return slice(a * ATT_SUB, a * ATT_SUB + ATT_WIN)

    def scores(a):
        q = q_ref[a * ATT_SUB:(a + 1) * ATT_SUB, :]
        zero = jnp.zeros_like(q)
        q2 = jnp.concatenate([jnp.where(low, q, zero), jnp.where(low, zero, q)], axis=0)
        return lax.dot_general(q2, k_scr[window(a), :], (((1,), (1,)), ((), ())), preferred_element_type=F32)

    s_next = scores(0)
    for a in range(ATT_NSUB):
        s = s_next + bias_ref[0, :, a].reshape(2 * ATT_SUB, ATT_WIN)
        if a + 1 < ATT_NSUB:
            s_next = scores(a + 1)
        m = jnp.max(s, axis=-1, keepdims=True)
        p = jnp.exp(s - m)
        l = jnp.sum(p, axis=-1, keepdims=True)
        pv = jnp.dot(p.astype(BF16), v_scr[window(a), :], preferred_element_type=F32) / l
        o_ref[a * ATT_SUB:(a + 1) * ATT_SUB, :] = jnp.where(low, pv[:ATT_SUB], pv[ATT_SUB:]).astype(o_ref.dtype)


def _attn(proj, rel_table, seq):
    n = proj.shape[0]
    bps = seq // ATT_QBLK
    pairs = ATT_WIDTH // LANES
    qc, kc, vc = (Q_TILE * COL_TILE // LANES, K_TILE * COL_TILE // LANES, VB_TILE * COL_TILE // LANES)
    bias = _band_bias(rel_table)

    def prev(i):
        return jnp.where(i % bps == 0, i, i - 1)

    def first(i):
        return jnp.where(i % bps == 0, 1, 0)

    blk = (ATT_QBLK, LANES)
    return pl.pallas_call(
        _attn_kernel,
        grid=(pairs, n // ATT_QBLK),
        in_specs=[pl.BlockSpec(blk, lambda h, i: (i, qc + h)),
                  pl.BlockSpec(blk, lambda h, i: (prev(i), kc + h)),
                  pl.BlockSpec(blk, lambda h, i: (i, kc + h)),
                  pl.BlockSpec(blk, lambda h, i: (prev(i), vc + h)),
                  pl.BlockSpec(blk, lambda h, i: (i, vc + h)),
                  pl.BlockSpec((1, 2, ATT_NSUB, ATT_SUB, ATT_WIN), lambda h, i: (first(i), h, 0, 0, 0))],
        out_specs=pl.BlockSpec(blk, lambda h, i: (i, h)),
        out_shape=jax.ShapeDtypeStruct((n, ATT_WIDTH), BF16),
        scratch_shapes=[pltpu.VMEM((2 * ATT_QBLK, LANES), BF16),
                        pltpu.VMEM((2 * ATT_QBLK, LANES), BF16)],
        compiler_params=_params(("arbitrary", "arbitrary")),
        name="attn",
    )(proj, proj, proj, proj, proj, bias)


MG_TM = 256


def _merge_kernel(mod_ref, g_ref, x_ref, ya_ref, yb_ref, ga_ref, gb_ref, wa_ref, wb_ref, wo_ref, wr_ref,
                  x1_ref, h2_ref, lt_ref):
    ya = jnp.dot(ya_ref[...], wa_ref[...], preferred_element_type=F32)
    yb = jnp.dot(yb_ref[...], wb_ref[...], preferred_element_type=F32)
    m = ga_ref[...].astype(F32) * ya + gb_ref[...].astype(F32) * yb
    mixed = jnp.dot(m.astype(BF16), wo_ref[...], preferred_element_type=F32)
    gate1 = mod_ref[0, 2:3, :]
    shift2 = mod_ref[0, 3:4, :]
    scale2 = mod_ref[0, 4:5, :]
    x1 = x_ref[...] + gate1 * mixed
    x1_ref[...] = x1
    h2 = _rms_mod(x1, g_ref[...], scale2, shift2).astype(BF16)
    h2_ref[...] = h2.reshape(h2_ref.shape)
    lt_ref[...] = lax.dot_general(wr_ref[...], h2, (((1,), (1,)), ((), ())), preferred_element_type=F32)


def _merge(x2d, mod3, norm_g, ya, yb, proj, wa, wb, wo, wr_t, seq):
    n, d = x2d.shape
    tps = seq // MG_TM
    const = lambda shape: pl.BlockSpec(shape, lambda i: (0,) * len(shape), pipeline_mode=pl.Buffered(1))
    return pl.pallas_call(
        _merge_kernel,
        grid=(n // MG_TM,),
        in_specs=[pl.BlockSpec((1, N_MOD, d), lambda i: (i // tps, 0, 0)),
                  pl.BlockSpec((1, d), lambda i: (0, 0)),
                  pl.BlockSpec((MG_TM, d), lambda i: (i, 0)),
                  pl.BlockSpec((MG_TM, GMLP_WIDTH), lambda i: (i, 0)),
                  pl.BlockSpec((MG_TM, ATT_WIDTH), lambda i: (i, 0)),
                  pl.BlockSpec((MG_TM, d), lambda i: (i, 0)),
                  pl.BlockSpec((MG_TM, d), lambda i: (i, 1)),
                  const((GMLP_WIDTH, d)), const((ATT_WIDTH, d)), const((d, d)),
                  const((ROUTER_ROWS, d))],
        out_specs=[pl.BlockSpec((MG_TM, d), lambda i: (i, 0)),
                   pl.BlockSpec((MG_TM, TOKEN_ROWS, LANES), lambda i: (i, 0, 0)),
                   pl.BlockSpec((ROUTER_ROWS, MG_TM), lambda i: (0, i))],
        out_shape=[jax.ShapeDtypeStruct((n, d), F32),
                   jax.ShapeDtypeStruct((n, TOKEN_ROWS, LANES), BF16),
                   jax.ShapeDtypeStruct((ROUTER_ROWS, n), F32)],
        compiler_params=_params(("arbitrary",)),
        name="merge",
    )(mod3, norm_g.reshape(1, d), x2d, ya, yb, proj, proj, wa, wb, wo, wr_t)


RT_TN = 512


def _first_argmax(vals, vmax, nrows):
    rows = lax.broadcasted_iota(I32, vals.shape, 0)
    return jnp.min(jnp.where(vals == vmax, rows, nrows), axis=0, keepdims=True)


def _route_kernel(lt_ref, e_ref, r_ref, w_ref, cnt_ref, carry_scr):
    @pl.when(pl.program_id(0) == 0)
    def _():
        carry_scr[...] = jnp.zeros_like(carry_scr)

    gl = lt_ref[0:N_GROUPS, :]
    gmax = jnp.max(gl, axis=0, keepdims=True)
    gidx = _first_argmax(gl, gmax, N_GROUPS)
    gw = 1.0 / jnp.sum(jnp.exp(gl - gmax), axis=0, keepdims=True)

    esel = lt_ref[SUBLANES:SUBLANES + EXPERTS_PER_GROUP, :]
    for g in range(1, N_GROUPS):
        lo = SUBLANES + g * EXPERTS_PER_GROUP
        esel = jnp.where(gidx == g, lt_ref[lo:lo + EXPERTS_PER_GROUP, :], esel)
    rows8 = lax.broadcasted_iota(I32, esel.shape, 0)
    m1 = jnp.max(esel, axis=0, keepdims=True)
    i1 = _first_argmax(esel, m1, EXPERTS_PER_GROUP)
    rest = jnp.where(rows8 == i1, -jnp.inf, esel)
    m2 = jnp.max(rest, axis=0, keepdims=True)
    i2 = _first_argmax(rest, m2, EXPERTS_PER_GROUP)
    z = jnp.exp(m2 - m1)
    w_top = 1.0 / (1.0 + z)
    e0 = gidx * EXPERTS_PER_GROUP + i1
    e1 = gidx * EXPERTS_PER_GROUP + i2
    e_ref[0:1, :] = e0
    e_ref[1:2, :] = e1
    w_ref[0:1, :] = gw * w_top
    w_ref[1:2, :] = gw * (z * w_top)

    rows_e = lax.broadcasted_iota(I32, (N_EXPERTS, RT_TN), 0)
    oh0 = rows_e == e0
    oh1 = rows_e == e1
    oh = jnp.where(oh0 | oh1, 1.0, 0.0)
    src = lax.broadcasted_iota(I32, (RT_TN, RT_TN), 0)
    dst = lax.broadcasted_iota(I32, (RT_TN, RT_TN), 1)
    before = jnp.where(src < dst, 1.0, 0.0).astype(BF16)
    carry = carry_scr[...]
    prefix = jnp.dot(oh.astype(BF16), before, preferred_element_type=F32) + carry[:, 0:1]
    r_ref[0:1, :] = jnp.sum(jnp.where(oh0, prefix, 0.0), axis=0, keepdims=True).astype(I32)
    r_ref[1:2, :] = jnp.sum(jnp.where(oh1, prefix, 0.0), axis=0, keepdims=True).astype(I32)
    carry = carry + jnp.sum(oh, axis=1, keepdims=True)
    carry_scr[...] = carry
    cnt_ref[...] = carry.astype(I32)


def _route(logits_t):
    n = logits_t.shape[1]
    slot = pl.BlockSpec((2, RT_TN), lambda i: (0, i))
    return pl.pallas_call(
        _route_kernel,
        grid=(n // RT_TN,),
        in_specs=[pl.BlockSpec((ROUTER_ROWS, RT_TN), lambda i: (0, i))],
        out_specs=[slot, slot, slot, pl.BlockSpec((N_EXPERTS, LANES), lambda i: (0, 0))],
        out_shape=[jax.ShapeDtypeStruct((2, n), I32),
                   jax.ShapeDtypeStruct((2, n), I32),
                   jax.ShapeDtypeStruct((2, n), F32),
                   jax.ShapeDtypeStruct((N_EXPERTS, LANES), I32)],
        scratch_shapes=[pltpu.VMEM((N_EXPERTS, LANES), F32)],
        compiler_params=_params(("arbitrary",)),
        name="route",
    )(logits_t)


DP_TM = 256


DMA_UNROLL = 8


def _token_copy(src_ref, src_row, dst_ref, dst_row, sem):
    return pltpu.make_async_copy(src_ref.at[src_row], dst_ref.at[dst_row], sem)


def _dispatch_kernel(pos0_ref, pos1_ref, pad_start_ref, pad_n_ref, nt_ref, h_ref, xs_ref, zero_scr, sem, zsem, tsem):
    i = pl.program_id(0)
    n_tiles_max = xs_ref.shape[0] // EXPERT_TILE

    def tail_copy(t):
        return pltpu.make_async_copy(zero_scr, xs_ref.at[pl.ds(t * EXPERT_TILE, EXPERT_TILE)], tsem)

    @pl.when(i == 0)
    def _():
        zero_scr[...] = jnp.zeros_like(zero_scr)
        for e in range(N_EXPERTS):
            start = pad_start_ref[e]

            def issue(r, c, start=start):
                _token_copy(zero_scr, 0, xs_ref, start + r, zsem).start()
                return c
            lax.fori_loop(0, pad_n_ref[e], issue, 0)

        def issue_tail(t, c):
            tail_copy(t).start()
            return c
        lax.fori_loop(nt_ref[0], n_tiles_max, issue_tail, 0)

        for e in range(N_EXPERTS):
            def drain(r, c):
                _token_copy(zero_scr, 0, xs_ref, 0, zsem).wait()
                return c
            lax.fori_loop(0, pad_n_ref[e], drain, 0)

        def drain_tail(t, c):
            tail_copy(t).wait()
            return c
        lax.fori_loop(nt_ref[0], n_tiles_max, drain_tail, 0)

    base = i * DP_TM

    def issue(r, c):
        _token_copy(h_ref, r, xs_ref, pos0_ref[base + r], sem).start()
        _token_copy(h_ref, r, xs_ref, pos1_ref[base + r], sem).start()
        return c
    lax.fori_loop(0, DP_TM, issue, 0, unroll=DMA_UNROLL)

    def drain(r, c):
        _token_copy(h_ref, 0, xs_ref, 0, sem).wait()
        _token_copy(h_ref, 0, xs_ref, 0, sem).wait()
        return c
    lax.fori_loop(0, DP_TM, drain, 0, unroll=DMA_UNROLL)


def _dispatch(h2, pos0, pos1, pad_start, pad_n, n_tiles, rows_out):
    n = h2.shape[0]
    grid_spec = pltpu.PrefetchScalarGridSpec(
        num_scalar_prefetch=5,
        grid=(n // DP_TM,),
        in_specs=[pl.BlockSpec((DP_TM, TOKEN_ROWS, LANES), lambda i, *_: (i, 0, 0))],
        out_specs=pl.BlockSpec(memory_space=pl.ANY),
        scratch_shapes=[pltpu.VMEM((EXPERT_TILE, TOKEN_ROWS, LANES), BF16),
                        pltpu.SemaphoreType.DMA(()), pltpu.SemaphoreType.DMA(()),
                        pltpu.SemaphoreType.DMA(())],
    )
    return pl.pallas_call(
        _dispatch_kernel,
        grid_spec=grid_spec,
        out_shape=jax.ShapeDtypeStruct((rows_out, TOKEN_ROWS, LANES), BF16),
        compiler_params=pltpu.CompilerParams(dimension_semantics=("arbitrary",),
                                             vmem_limit_bytes=VMEM_LIMIT, has_side_effects=True),
        name="dispatch",
    )(pos0, pos1, pad_start, pad_n, n_tiles, h2)


EXPERT_CAST_ROWS = 256


def _cast_rows(src_ref, dst_ref):
    rows_total = dst_ref.shape[0]

    def body(r, c):
        rows = pl.ds(pl.multiple_of(r * EXPERT_CAST_ROWS, EXPERT_CAST_ROWS), EXPERT_CAST_ROWS)
        dst_ref[rows, :] = src_ref[rows, :].astype(BF16)
        return c
    lax.fori_loop(0, rows_total // EXPERT_CAST_ROWS, body, 0)


def _experts_kernel(te_ref, tr_ref, nt_ref, first_ref, slot_ref, next_ref, xs_ref, w1_hbm, w3_hbm, w2_hbm, ys_ref,
                    w1_stage, w3_stage, w2_stage, w1_scr, w3_scr, w2_scr, sems):
    i = pl.program_id(0)

    def fetch(expert, slot):
        return (pltpu.make_async_copy(w1_hbm.at[expert], w1_stage.at[slot], sems.at[slot]),
                pltpu.make_async_copy(w3_hbm.at[expert], w3_stage.at[slot], sems.at[slot]),
                pltpu.make_async_copy(w2_hbm.at[expert], w2_stage.at[slot], sems.at[slot]))

    @pl.when(i == 0)
    def _():
        for copy in fetch(te_ref[0], 0):
            copy.start()

    @pl.when(first_ref[i] == 1)
    def _():
        slot = slot_ref[i]
        for copy in fetch(te_ref[i], slot):
            copy.wait()

        @pl.when(next_ref[i] >= 0)
        def _():
            for copy in fetch(next_ref[i], 1 - slot):
                copy.start()

        _cast_rows(w1_stage.at[slot], w1_scr)
        _cast_rows(w3_stage.at[slot], w3_scr)
        _cast_rows(w2_stage.at[slot], w2_scr)

    @pl.when(i < nt_ref[0])
    def _():
        x = xs_ref[...].reshape(EXPERT_TILE, D_MODEL)
        a = jnp.dot(x, w1_scr[...], preferred_element_type=F32)
        b = jnp.dot(x, w3_scr[...], preferred_element_type=F32)
        act = (a * jax.nn.sigmoid(a)) * b
        y = jnp.dot(act.astype(BF16), w2_scr[...], preferred_element_type=F32)
        ys_ref[...] = y.astype(BF16).reshape(ys_ref.shape)

    @pl.when(i >= nt_ref[0])
    def _():
        ys_ref[...] = jnp.zeros_like(ys_ref)


def _experts(xs, tile_expert, tile_row, n_tiles, tile_first, tile_slot, tile_next, w1, w3, w2):
    rows = xs.shape[0]
    _, d, f = w1.shape
    tile = (EXPERT_TILE, TOKEN_ROWS, LANES)
    hbm = pl.BlockSpec(memory_space=pl.ANY)
    grid_spec = pltpu.PrefetchScalarGridSpec(
        num_scalar_prefetch=6,
        grid=(rows // EXPERT_TILE,),
        in_specs=[pl.BlockSpec(tile, lambda i, te, tr, *_: (tr[i], 0, 0)), hbm, hbm, hbm],
        out_specs=pl.BlockSpec(tile, lambda i, *_: (i, 0, 0)),
        scratch_shapes=[pltpu.VMEM((2, d, f), F32), pltpu.VMEM((2, d, f), F32), pltpu.VMEM((2, f, d), F32),
                        pltpu.VMEM((d, f), BF16), pltpu.VMEM((d, f), BF16), pltpu.VMEM((f, d), BF16),
                        pltpu.SemaphoreType.DMA((2,))],
    )
    return pl.pallas_call(
        _experts_kernel,
        grid_spec=grid_spec,
        out_shape=jax.ShapeDtypeStruct(xs.shape, BF16),
        compiler_params=_params(("arbitrary",)),
        name="experts",
    )(tile_expert, tile_row, n_tiles, tile_first, tile_slot, tile_next, xs, w1, w3, w2)


CB_TM = 256


def _combine_kernel(pos0_ref, pos1_ref, mod_ref, fg_ref, x1_ref, w0_ref, w1_ref, ys_ref, o_ref,
                    y0_scr, y1_scr, sems):
    i = pl.program_id(0)
    slot = i % 2

    def gather(step, buf):
        base = step * CB_TM

        def issue(r, c):
            _token_copy(ys_ref, pos0_ref[base + r], y0_scr.at[buf], r, sems.at[buf]).start()
            _token_copy(ys_ref, pos1_ref[base + r], y1_scr.at[buf], r, sems.at[buf]).start()
            return c
        lax.fori_loop(0, CB_TM, issue, 0, unroll=DMA_UNROLL)

    @pl.when(i == 0)
    def _():
        gather(i, slot)

    @pl.when(i + 1 < pl.num_programs(0))
    def _():
        gather(i + 1, 1 - slot)

    def drain(r, c):
        _token_copy(ys_ref, 0, y0_scr.at[slot], 0, sems.at[slot]).wait()
        _token_copy(ys_ref, 0, y1_scr.at[slot], 0, sems.at[slot]).wait()
        return c
    lax.fori_loop(0, CB_TM, drain, 0, unroll=DMA_UNROLL)

    gate2 = mod_ref[0, 5:6, :]
    y0 = y0_scr[slot].reshape(CB_TM, D_MODEL).astype(F32)
    y1 = y1_scr[slot].reshape(CB_TM, D_MODEL).astype(F32)
    y = w0_ref[...] * y0 + w1_ref[...] * y1
    x2 = x1_ref[...] + gate2 * y
    o_ref[...] = (x2 * lax.rsqrt(jnp.mean(x2 * x2, axis=-1, keepdims=True) + EPS)) * fg_ref[...]


def _combine(x1, mod3, final_g, ys, pos0, pos1, cw0, cw1, seq):
    n, d = x1.shape
    tps = seq // CB_TM
    grid_spec = pltpu.PrefetchScalarGridSpec(
        num_scalar_prefetch=2,
        grid=(n // CB_TM,),
        in_specs=[pl.BlockSpec((1, N_MOD, d), lambda i, *_: (i // tps, 0, 0)),
                  pl.BlockSpec((1, d), lambda i, *_: (0, 0)),
                  pl.BlockSpec((CB_TM, d), lambda i, *_: (i, 0)),
                  pl.BlockSpec((CB_TM, 1), lambda i, *_: (i, 0)),
                  pl.BlockSpec((CB_TM, 1), lambda i, *_: (i, 0)),
                  pl.BlockSpec(memory_space=pl.ANY)],
        out_specs=pl.BlockSpec((CB_TM, d), lambda i, *_: (i, 0)),
        scratch_shapes=[pltpu.VMEM((2, CB_TM, TOKEN_ROWS, LANES), BF16),
                        pltpu.VMEM((2, CB_TM, TOKEN_ROWS, LANES), BF16),
                        pltpu.SemaphoreType.DMA((2,))],
    )
    return pl.pallas_call(
        _combine_kernel,
        grid_spec=grid_spec,
        out_shape=jax.ShapeDtypeStruct((n, d), F32),
        compiler_params=_params(("arbitrary",)),
        name="combine",
    )(pos0, pos1, mod3, final_g.reshape(1, d), x1, cw0, cw1, ys)


def _layer(x2d, c, seq, ada_w, ada_b, norm1_g, w_in, gmlp_ln_g, gmlp_ln_b, gmlp_w_s, gmlp_b_s, rel_bias,
           w_branch_a, w_branch_b, w_out, norm2_g, w_group, w_expert, w1, w3, w2, final_g):
    n, d = x2d.shape
    nb = c.shape[0]
    mod3 = _ada(c, ada_w, ada_b).reshape(nb, N_MOD, d)

    proj = _in_proj(x2d, mod3, norm1_g, w_in.astype(BF16), seq)
    ya = _gmlp(proj, gmlp_ln_g, gmlp_ln_b, gmlp_w_s, gmlp_b_s)
    yb = _attn(proj, rel_bias, seq)

    wr_t = jnp.concatenate([w_group.T, jnp.zeros((SUBLANES - N_GROUPS, d), F32),
                            w_expert.transpose(0, 2, 1).reshape(N_EXPERTS, d)], axis=0).astype(BF16)
    x1, h2, logits_t = _merge(x2d, mod3, norm2_g, ya, yb, proj, w_branch_a.astype(BF16),
                              w_branch_b.astype(BF16), w_out.astype(BF16), wr_t, seq)

    eidx, rank, cw, counts = _route(logits_t)
    counts = counts[:, 0]
    padded = ((counts + EXPERT_TILE - 1) // EXPERT_TILE) * EXPERT_TILE
    ends = jnp.cumsum(padded)
    offs = ends - padded
    experts = jnp.arange(N_EXPERTS, dtype=I32)
    pos = jnp.sum(jnp.where(eidx[:, :, None] == experts, offs, 0), axis=-1) + rank
    rows_out = 2 * n + N_EXPERTS * EXPERT_TILE
    n_tiles_max = rows_out // EXPERT_TILE
    n_tiles = (ends[-1] // EXPERT_TILE).astype(I32)
    tile_row = jnp.minimum(jnp.arange(n_tiles_max, dtype=I32), n_tiles - 1)
    tile_expert = jnp.minimum(jnp.sum(ends[None, :] <= (tile_row * EXPERT_TILE)[:, None], axis=-1),
                              N_EXPERTS - 1).astype(I32)

    tiles = jnp.arange(n_tiles_max, dtype=I32)
    prev_expert = jnp.concatenate([jnp.full((1,), -1, I32), tile_expert[:-1]])
    tile_first = ((tiles < n_tiles) & (tile_expert != prev_expert)).astype(I32)
    tile_slot = ((jnp.cumsum(tile_first) - 1) % 2).astype(I32)
    later_nonempty = (experts[None, :] > experts[:, None]) & (padded > 0)[None, :]
    next_nonempty = jnp.min(jnp.where(later_nonempty, experts[None, :], N_EXPERTS), axis=1)
    next_nonempty = jnp.where(next_nonempty == N_EXPERTS, -1, next_nonempty)
    tile_next = jnp.sum(jnp.where(tile_expert[:, None] == experts, next_nonempty, 0), axis=-1).astype(I32)

    n_tiles = n_tiles.reshape(1)
    xs = _dispatch(h2, pos[0], pos[1], (offs + counts).astype(I32), (padded - counts).astype(I32), n_tiles,
                   rows_out)
    ys = _experts(xs, tile_expert, tile_row, n_tiles, tile_first, tile_slot, tile_next,
                  w1.reshape(N_EXPERTS, d, D_EXPERT), w3.reshape(N_EXPERTS, d, D_EXPERT),
                  w2.reshape(N_EXPERTS, D_EXPERT, d))
    return _combine(x1, mod3, final_g, ys, pos[0], pos[1], cw[0].reshape(n, 1), cw[1].reshape(n, 1), seq)


def kernel(x, c, ada_w, ada_b, norm1_g, w_in, gmlp_ln_g, gmlp_ln_b, gmlp_w_s, gmlp_b_s, rel_bias, w_branch_a,
           w_branch_b, w_out, norm2_g, w_group, w_expert, w1, w3, w2, final_g):
    b, s, d = x.shape
    out = _layer(x.reshape(b * s, d), c, s, ada_w[0], ada_b[0], norm1_g[0], w_in[0], gmlp_ln_g[0], gmlp_ln_b[0],
                 gmlp_w_s[0], gmlp_b_s[0], rel_bias[0], w_branch_a[0], w_branch_b[0], w_out[0], norm2_g[0],
                 w_group[0], w_expert[0], w1[0], w3[0], w2[0], final_g)
    return out.reshape(b, s, d)
```

```python
import numpy as np
import jax
import jax.numpy as jnp
from jax import lax
from jax.experimental import pallas as pl
from jax.experimental.pallas import tpu as pltpu

F32 = jnp.float32
BF16 = jnp.bfloat16
I32 = jnp.int32

D_MODEL = 2048
CHUNK = 64
EPS = 1e-6
NEG_INF = -1e30
GMLP_BLOCK = 128
GMLP_GROUPS = 8
GMLP_WIDTH = 1024
ATT_HEADS = 16
ATT_HEAD_DIM = 64
ATT_WIDTH = ATT_HEADS * ATT_HEAD_DIM
LEFT_CHUNKS = 8
MAX_REL = 256
N_GROUPS = 4
EXPERTS_PER_GROUP = 8
N_EXPERTS = N_GROUPS * EXPERTS_PER_GROUP
D_EXPERT = 512
N_MOD = 6

LANES = 128
SUBLANES = 8
VMEM_LIMIT = 56 * 1024 * 1024

COL_TILE = 1024
PROJ_COLS = 2 * D_MODEL + 2 * GMLP_WIDTH + 3 * ATT_WIDTH
GATE_TILES = 2 * D_MODEL // COL_TILE
U_TILE = GATE_TILES
V_TILE = GATE_TILES + 1
Q_TILE = GATE_TILES + 2
K_TILE = GATE_TILES + 3
VB_TILE = GATE_TILES + 4

ROUTER_ROWS = SUBLANES + N_EXPERTS

ATT_QBLK = 512
ATT_SUB = 2 * CHUNK
ATT_NSUB = ATT_QBLK // ATT_SUB
ATT_WIN = ATT_SUB + LEFT_CHUNKS * CHUNK

EXPERT_TILE = 256
TOKEN_ROWS = D_MODEL // LANES


def _params(sem, vmem=VMEM_LIMIT):
    return pltpu.CompilerParams(dimension_semantics=sem, vmem_limit_bytes=vmem)


def _rms_mod(x, g, scale, shift):
    y = x * lax.rsqrt(jnp.mean(x * x, axis=-1, keepdims=True) + EPS)
    return (y * g) * (1.0 + scale) + shift


def _ada_kernel(cb_ref, w_ref, b_ref, o_ref, s_scr):
    nb, d, _ = cb_ref.shape
    tn = w_ref.shape[1]
    reps = tn // LANES

    @pl.when(pl.program_id(0) == 0)
    def _():
        cb = cb_ref[...]
        s_scr[...] = cb * jax.nn.sigmoid(cb)

    def body(i, accs):
        r = pl.ds(pl.multiple_of(i * SUBLANES, SUBLANES), SUBLANES)
        w8 = w_ref[r, :]
        return tuple(accs[b] + w8 * jnp.concatenate([s_scr[b, r, :]] * reps, axis=1) for b in range(nb))

    init = tuple(jnp.zeros((SUBLANES, tn), F32) for _ in range(nb))
    accs = lax.fori_loop(0, d // SUBLANES, body, init, unroll=4)
    for b in range(nb):
        o_ref[b:b + 1, :] = jnp.sum(accs[b], axis=0, keepdims=True) + b_ref[...]


def _ada(c, ada_w, ada_b):
    nb, d = c.shape
    n = ada_w.shape[1]
    tn = 1024
    cb = jnp.broadcast_to(c[:, :, None], (nb, d, LANES))
    return pl.pallas_call(
        _ada_kernel,
        grid=(n // tn,),
        in_specs=[pl.BlockSpec((nb, d, LANES), lambda j: (0, 0, 0)),
                  pl.BlockSpec((d, tn), lambda j: (0, j)),
                  pl.BlockSpec((1, tn), lambda j: (0, j))],
        out_specs=pl.BlockSpec((nb, tn), lambda j: (0, j)),
        out_shape=jax.ShapeDtypeStruct((nb, n), F32),
        scratch_shapes=[pltpu.VMEM((nb, d, LANES), F32)],
        compiler_params=_params(("arbitrary",)),
        name="ada",
    )(cb, ada_w, ada_b.reshape(1, n))


IN_TM = 1024
IN_RB = 512
IN_STAT_RB = 128
IN_NORM_RB = 16


def _gelu(a):
    return 0.5 * a * (1.0 + lax.erf(a * np.float32(np.sqrt(0.5))))


def _in_proj_kernel(mod_ref, g_ref, x_ref, w_ref, o_ref, h_scr, gain_scr, shift_scr, inv_scr):
    j = pl.program_id(1)
    nrb = IN_TM // IN_RB

    @pl.when(j == 0)
    def _():
        d = x_ref.shape[1]
        gain_scr[...] = jnp.broadcast_to(g_ref[...] * (1.0 + mod_ref[0, 1:2, :]), gain_scr.shape)
        shift_scr[...] = jnp.broadcast_to(mod_ref[0, 0:1, :], shift_scr.shape)

        def stats(rb, c):
            rows = pl.ds(pl.multiple_of(rb * IN_STAT_RB, IN_STAT_RB), IN_STAT_RB)
            sq = jnp.zeros((IN_STAT_RB, LANES), F32)
            for k in range(d // LANES):
                xk = x_ref[rows, k * LANES:(k + 1) * LANES]
                sq = sq + xk * xk
            inv = lax.rsqrt(jnp.sum(sq, axis=-1, keepdims=True) * (1.0 / d) + EPS)
            inv_scr[rows, :] = jnp.broadcast_to(inv, (IN_STAT_RB, LANES))
            return c
        lax.fori_loop(0, IN_TM // IN_STAT_RB, stats, 0)

        def apply(rb, c):
            rows = pl.ds(pl.multiple_of(rb * IN_NORM_RB, IN_NORM_RB), IN_NORM_RB)
            inv = inv_scr[rows, :]
            for k in range(d // LANES):
                cols = slice(k * LANES, (k + 1) * LANES)
                y = (x_ref[rows, cols] * inv) * gain_scr[:, cols] + shift_scr[:, cols]
                h_scr[rows, cols] = y.astype(BF16)
            return c
        lax.fori_loop(0, IN_TM // IN_NORM_RB, apply, 0, unroll=2)

    def run(epilogue):
        def body(rb, c):
            rows = pl.ds(pl.multiple_of(rb * IN_RB, IN_RB), IN_RB)
            acc = jnp.dot(h_scr[rows, :], w_ref[...], preferred_element_type=F32)
            o_ref[rows, :] = epilogue(acc).astype(o_ref.dtype)
            return c
        lax.fori_loop(0, nrb, body, 0)

    @pl.when(j < GATE_TILES)
    def _():
        run(jax.nn.sigmoid)

    @pl.when((j == U_TILE) | (j == V_TILE))
    def _():
        run(_gelu)

    @pl.when(j == Q_TILE)
    def _():
        run(lambda a: a * np.float32(ATT_HEAD_DIM ** -0.5))

    @pl.when(j > Q_TILE)
    def _():
        run(lambda a: a)


def _in_proj(x2d, mod3, norm_g, w_in_bf16, seq):
    n, d = x2d.shape
    cols = w_in_bf16.shape[1]
    n_tiles = cols // COL_TILE
    tiles_per_seq = seq // IN_TM
    return pl.pallas_call(
        _in_proj_kernel,
        grid=(n // IN_TM, n_tiles),
        in_specs=[pl.BlockSpec((1, N_MOD, d), lambda i, j: (i // tiles_per_seq, 0, 0)),
                  pl.BlockSpec((1, d), lambda i, j: (0, 0)),
                  pl.BlockSpec((IN_TM, d), lambda i, j: (i, 0)),
                  pl.BlockSpec((d, COL_TILE), lambda i, j: (0, (j + n_tiles - GATE_TILES) % n_tiles))],
        out_specs=pl.BlockSpec((IN_TM, COL_TILE), lambda i, j: (i, j)),
        out_shape=jax.ShapeDtypeStruct((n, cols), BF16),
        scratch_shapes=[pltpu.VMEM((IN_TM, d), BF16), pltpu.VMEM((IN_NORM_RB, d), F32),
                        pltpu.VMEM((IN_NORM_RB, d), F32), pltpu.VMEM((IN_TM, LANES), F32)],
        compiler_params=_params(("arbitrary", "arbitrary")),
        name="in_proj",
    )(mod3, norm_g.reshape(1, d), x2d, w_in_bf16)


GM_TM = 256


def _gmlp_kernel(u_ref, v_ref, lng_ref, lnb_ref, ws_ref, bs_ref, o_ref):
    t = lax.broadcasted_iota(I32, (GMLP_BLOCK, GMLP_BLOCK), 0)
    s = lax.broadcasted_iota(I32, (GMLP_BLOCK, GMLP_BLOCK), 1)
    causal = (s // CHUNK) <= (t // CHUNK)
    lng = lng_ref[...]
    lnb = lnb_ref[...]
    for blk in range(GM_TM // GMLP_BLOCK):
        rows = slice(blk * GMLP_BLOCK, (blk + 1) * GMLP_BLOCK)
        v = v_ref[rows, :].astype(F32)
        mu = jnp.mean(v, axis=-1, keepdims=True)
        vc = v - mu
        var = jnp.mean(vc * vc, axis=-1, keepdims=True)
        vln = ((vc * lax.rsqrt(var + EPS)) * lng + lnb).astype(BF16)
        for g in range(GMLP_GROUPS):
            cols = slice(g * LANES, (g + 1) * LANES)
            w = jnp.where(causal, ws_ref[g], 0.0).astype(BF16)
            mixed = jnp.dot(w, vln[:, cols], preferred_element_type=F32) + bs_ref[g]
            o_ref[rows, cols] = (u_ref[rows, cols].astype(F32) * mixed).astype(o_ref.dtype)


def _gmlp(proj, ln_g, ln_b, w_s, b_s):
    n = proj.shape[0]
    return pl.pallas_call(
        _gmlp_kernel,
        grid=(n // GM_TM,),
        in_specs=[pl.BlockSpec((GM_TM, GMLP_WIDTH), lambda i: (i, U_TILE)),
                  pl.BlockSpec((GM_TM, GMLP_WIDTH), lambda i: (i, V_TILE)),
                  pl.BlockSpec((1, GMLP_WIDTH), lambda i: (0, 0)),
                  pl.BlockSpec((1, GMLP_WIDTH), lambda i: (0, 0)),
                  pl.BlockSpec((GMLP_GROUPS, GMLP_BLOCK, GMLP_BLOCK), lambda i: (0, 0, 0)),
                  pl.BlockSpec((GMLP_GROUPS, GMLP_BLOCK, 1), lambda i: (0, 0, 0))],
        out_specs=pl.BlockSpec((GM_TM, GMLP_WIDTH), lambda i: (i, 0)),
        out_shape=jax.ShapeDtypeStruct((n, GMLP_WIDTH), BF16),
        compiler_params=_params(("arbitrary",)),
        name="gmlp",
    )(proj, proj, ln_g.reshape(1, -1), ln_b.reshape(1, -1), w_s,
      b_s.reshape(GMLP_GROUPS, GMLP_BLOCK, 1))


def _band_bias(rel_table):
    heads = rel_table.shape[0]
    r = np.arange(ATT_SUB)[:, None]
    w = np.arange(ATT_WIN)[None, :]
    j = w // CHUNK - r // CHUNK
    in_band = (j >= 0) & (j <= LEFT_CHUNKS)
    a = np.arange(ATT_NSUB)[:, None, None]
    in_seq = np.broadcast_to(w[None] >= ATT_QBLK - a * ATT_SUB, (ATT_NSUB, ATT_SUB, ATT_WIN))
    visible = np.stack([np.broadcast_to(in_band, in_seq.shape), in_band[None] & in_seq])[:, None]
    far = LEFT_CHUNKS * CHUNK + ATT_SUB - 1
    n_clipped = far - MAX_REL + 1
    table = rel_table.astype(F32)
    lo = MAX_REL - (ATT_WIN - 1 - LEFT_CHUNKS * CHUNK)
    diag = jnp.concatenate([jnp.broadcast_to(table[:, 2 * MAX_REL:], (heads, n_clipped)),
                            jnp.flip(table[:, lo:2 * MAX_REL], axis=1),
                            jnp.zeros((heads, 1), F32)], axis=1)
    span = diag.shape[1] - 1
    shifted = jnp.tile(diag, (1, ATT_SUB))[:, :ATT_SUB * span].reshape(heads, ATT_SUB, span)
    bias = shifted[:, :, ATT_SUB - 1:ATT_SUB - 1 + ATT_WIN]
    return jnp.where(jnp.asarray(visible), bias[None, :, None], NEG_INF)


def _attn_kernel(q_ref, kp_ref, kc_ref, vp_ref, vc_ref, bias_ref, o_ref, k_scr, v_scr):
    k_scr[0:ATT_QBLK, :] = kp_ref[...]
    k_scr[ATT_QBLK:, :] = kc_ref[...]
    v_scr[0:ATT_QBLK, :] = vp_ref[...]
    v_scr[ATT_QBLK:, :] = vc_ref[...]
    lane = lax.broadcasted_iota(I32, (ATT_SUB, LANES), 1)
    low = lane < ATT_HEAD_DIM

    def window(a):
        return slice(a * ATT_SUB, a * ATT_SUB + ATT_WIN)

    def scores(a):
        q = q_ref[a * ATT_SUB:(a + 1) * ATT_SUB, :]
        zero = jnp.zeros_like(q)
        q2 = jnp.concatenate([jnp.where(low, q, zero), jnp.where(low, zero, q)], axis=0)
        return lax.dot_general(q2, k_scr[window(a), :], (((1,), (1,)), ((), ())), preferred_element_type=F32)

    s_next = scores(0)
    for a in range(ATT_NSUB):
        s = s_next + bias_ref[0, :, a].reshape(2 * ATT_SUB, ATT_WIN)
        if a + 1 < ATT_NSUB:
            s_next = scores(a + 1)
        m = jnp.max(s, axis=-1, keepdims=True)
        p = jnp.exp(s - m)
        l = jnp.sum(p, axis=-1, keepdims=True)
        pv = jnp.dot(p.astype(BF16), v_scr[window(a), :], preferred_element_type=F32) / l
        o_ref[a * ATT_SUB:(a + 1) * ATT_SUB, :] = jnp.where(low, pv[:ATT_SUB], pv[ATT_SUB:]).astype(o_ref.dtype)


def _attn(proj, rel_table, seq):
    n = proj.shape[0]
    bps = seq // ATT_QBLK
    pairs = ATT_WIDTH // LANES
    qc, kc, vc = (Q_TILE * COL_TILE // LANES, K_TILE * COL_TILE // LANES, VB_TILE * COL_TILE // LANES)
    bias = _band_bias(rel_table)

    def prev(i):
        return jnp.where(i % bps == 0, i, i - 1)

    def first(i):
        return jnp.where(i % bps == 0, 1, 0)

    blk = (ATT_QBLK, LANES)
    return pl.pallas_call(
        _attn_kernel,
        grid=(pairs, n // ATT_QBLK),
        in_specs=[pl.BlockSpec(blk, lambda h, i: (i, qc + h)),
                  pl.BlockSpec(blk, lambda h, i: (prev(i), kc + h)),
                  pl.BlockSpec(blk, lambda h, i: (i, kc + h)),
                  pl.BlockSpec(blk, lambda h, i: (prev(i), vc + h)),
                  pl.BlockSpec(blk, lambda h, i: (i, vc + h)),
                  pl.BlockSpec((1, 2, ATT_NSUB, ATT_SUB, ATT_WIN), lambda h, i: (first(i), h, 0, 0, 0))],
        out_specs=pl.BlockSpec(blk, lambda h, i: (i, h)),
        out_shape=jax.ShapeDtypeStruct((n, ATT_WIDTH), BF16),
        scratch_shapes=[pltpu.VMEM((2 * ATT_QBLK, LANES), BF16),
                        pltpu.VMEM((2 * ATT_QBLK, LANES), BF16)],
        compiler_params=_params(("arbitrary", "arbitrary")),
        name="attn",
    )(proj, proj, proj, proj, proj, bias)


MG_TM = 256
MG_RB = 256


def _merge_kernel(mod_ref, g_ref, x_ref, ya_ref, yb_ref, ga_ref, gb_ref, wa_ref, wb_ref, wo_ref, wr_ref,
                  x1_ref, h2_ref, lt_ref):
    gate1 = mod_ref[0, 2:3, :]
    shift2 = mod_ref[0, 3:4, :]
    scale2 = mod_ref[0, 4:5, :]
    for part in range(MG_TM // MG_RB):
        rows = slice(part * MG_RB, (part + 1) * MG_RB)
        ya = jnp.dot(ya_ref[rows, :], wa_ref[...], preferred_element_type=F32)
        yb = jnp.dot(yb_ref[rows, :], wb_ref[...], preferred_element_type=F32)
        m = ga_ref[rows, :].astype(F32) * ya + gb_ref[rows, :].astype(F32) * yb
        mixed = jnp.dot(m.astype(BF16), wo_ref[...], preferred_element_type=F32)
        x1 = x_ref[rows, :] + gate1 * mixed
        x1_ref[rows, :] = x1
        h2 = _rms_mod(x1, g_ref[...], scale2, shift2).astype(BF16)
        h2_ref[rows] = h2.reshape(MG_RB, TOKEN_ROWS, LANES)
        lt_ref[:, rows] = lax.dot_general(wr_ref[...], h2, (((1,), (1,)), ((), ())),
                                          preferred_element_type=F32)


def _merge(x2d, mod3, norm_g, ya, yb, proj, wa, wb, wo, wr_t, seq):
    n, d = x2d.shape
    tps = seq // MG_TM
    const = lambda shape: pl.BlockSpec(shape, lambda i: (0,) * len(shape), pipeline_mode=pl.Buffered(1))
    return pl.pallas_call(
        _merge_kernel,
        grid=(n // MG_TM,),
        in_specs=[pl.BlockSpec((1, N_MOD, d), lambda i: (i // tps, 0, 0)),
                  pl.BlockSpec((1, d), lambda i: (0, 0)),
                  pl.BlockSpec((MG_TM, d), lambda i: (i, 0)),
                  pl.BlockSpec((MG_TM, GMLP_WIDTH), lambda i: (i, 0)),
                  pl.BlockSpec((MG_TM, ATT_WIDTH), lambda i: (i, 0)),
                  pl.BlockSpec((MG_TM, d), lambda i: (i, 0)),
                  pl.BlockSpec((MG_TM, d), lambda i: (i, 1)),
                  const((GMLP_WIDTH, d)), const((ATT_WIDTH, d)), const((d, d)),
                  const((ROUTER_ROWS, d))],
        out_specs=[pl.BlockSpec((MG_TM, d), lambda i: (i, 0)),
                   pl.BlockSpec((MG_TM, TOKEN_ROWS, LANES), lambda i: (i, 0, 0)),
                   pl.BlockSpec((ROUTER_ROWS, MG_TM), lambda i: (0, i))],
        out_shape=[jax.ShapeDtypeStruct((n, d), F32),
                   jax.ShapeDtypeStruct((n, TOKEN_ROWS, LANES), BF16),
                   jax.ShapeDtypeStruct((ROUTER_ROWS, n), F32)],
        compiler_params=_params(("arbitrary",)),
        name="merge",
    )(mod3, norm_g.reshape(1, d), x2d, ya, yb, proj, proj, wa, wb, wo, wr_t)


RT_TN = 512


def _first_argmax(vals, vmax, nrows):
    rows = lax.broadcasted_iota(I32, vals.shape, 0)
    return jnp.min(jnp.where(vals == vmax, rows, nrows), axis=0, keepdims=True)


def _route_kernel(lt_ref, e_ref, r_ref, w_ref, cnt_ref, carry_scr):
    @pl.when(pl.program_id(0) == 0)
    def _():
        carry_scr[...] = jnp.zeros_like(carry_scr)

    gl = lt_ref[0:N_GROUPS, :]
    gmax = jnp.max(gl, axis=0, keepdims=True)
    gidx = _first_argmax(gl, gmax, N_GROUPS)
    gw = 1.0 / jnp.sum(jnp.exp(gl - gmax), axis=0, keepdims=True)

    esel = lt_ref[SUBLANES:SUBLANES + EXPERTS_PER_GROUP, :]
    for g in range(1, N_GROUPS):
        lo = SUBLANES + g * EXPERTS_PER_GROUP
        esel = jnp.where(gidx == g, lt_ref[lo:lo + EXPERTS_PER_GROUP, :], esel)
    rows8 = lax.broadcasted_iota(I32, esel.shape, 0)
    m1 = jnp.max(esel, axis=0, keepdims=True)
    i1 = _first_argmax(esel, m1, EXPERTS_PER_GROUP)
    rest = jnp.where(rows8 == i1, -jnp.inf, esel)
    m2 = jnp.max(rest, axis=0, keepdims=True)
    i2 = _first_argmax(rest, m2, EXPERTS_PER_GROUP)
    z = jnp.exp(m2 - m1)
    w_top = 1.0 / (1.0 + z)
    e0 = gidx * EXPERTS_PER_GROUP + i1
    e1 = gidx * EXPERTS_PER_GROUP + i2
    e_ref[0:1, :] = e0
    e_ref[1:2, :] = e1
    w_ref[0:1, :] = gw * w_top
    w_ref[1:2, :] = gw * (z * w_top)

    rows_e = lax.broadcasted_iota(I32, (N_EXPERTS, RT_TN), 0)
    oh0 = rows_e == e0
    oh1 = rows_e == e1
    oh = jnp.where(oh0 | oh1, 1.0, 0.0)
    src = lax.broadcasted_iota(I32, (RT_TN, RT_TN), 0)
    dst = lax.broadcasted_iota(I32, (RT_TN, RT_TN), 1)
    before = jnp.where(src < dst, 1.0, 0.0).astype(BF16)
    carry = carry_scr[...]
    prefix = jnp.dot(oh.astype(BF16), before, preferred_element_type=F32) + carry[:, 0:1]
    r_ref[0:1, :] = jnp.sum(jnp.where(oh0, prefix, 0.0), axis=0, keepdims=True).astype(I32)
    r_ref[1:2, :] = jnp.sum(jnp.where(oh1, prefix, 0.0), axis=0, keepdims=True).astype(I32)
    carry = carry + jnp.sum(oh, axis=1, keepdims=True)
    carry_scr[...] = carry
    cnt_ref[...] = carry.astype(I32)


def _route(logits_t):
    n = logits_t.shape[1]
    slot = pl.BlockSpec((2, RT_TN), lambda i: (0, i))
    return pl.pallas_call(
        _route_kernel,
        grid=(n // RT_TN,),
        in_specs=[pl.BlockSpec((ROUTER_ROWS, RT_TN), lambda i: (0, i))],
        out_specs=[slot, slot, slot, pl.BlockSpec((N_EXPERTS, LANES), lambda i: (0, 0))],
        out_shape=[jax.ShapeDtypeStruct((2, n), I32),
                   jax.ShapeDtypeStruct((2, n), I32),
                   jax.ShapeDtypeStruct((2, n), F32),
                   jax.ShapeDtypeStruct((N_EXPERTS, LANES), I32)],
        scratch_shapes=[pltpu.VMEM((N_EXPERTS, LANES), F32)],
        compiler_params=_params(("arbitrary",)),
        name="route",
    )(logits_t)


DP_TM = 256


DMA_UNROLL = 8


def _token_copy(src_ref, src_row, dst_ref, dst_row, sem):
    return pltpu.make_async_copy(src_ref.at[src_row], dst_ref.at[dst_row], sem)


def _dispatch_kernel(pos0_ref, pos1_ref, pad_start_ref, pad_n_ref, nt_ref, h_ref, xs_ref, zero_scr, sem, zsem, tsem):
    i = pl.program_id(0)
    n_tiles_max = xs_ref.shape[0] // EXPERT_TILE

    def tail_copy(t):
        return pltpu.make_async_copy(zero_scr, xs_ref.at[pl.ds(t * EXPERT_TILE, EXPERT_TILE)], tsem)

    @pl.when(i == 0)
    def _():
        zero_scr[...] = jnp.zeros_like(zero_scr)
        for e in range(N_EXPERTS):
            start = pad_start_ref[e]

            def issue(r, c, start=start):
                _token_copy(zero_scr, 0, xs_ref, start + r, zsem).start()
                return c
            lax.fori_loop(0, pad_n_ref[e], issue, 0)

        def issue_tail(t, c):
            tail_copy(t).start()
            return c
        lax.fori_loop(nt_ref[0], n_tiles_max, issue_tail, 0)

        for e in range(N_EXPERTS):
            def drain(r, c):
                _token_copy(zero_scr, 0, xs_ref, 0, zsem).wait()
                return c
            lax.fori_loop(0, pad_n_ref[e], drain, 0)

        def drain_tail(t, c):
            tail_copy(t).wait()
            return c
        lax.fori_loop(nt_ref[0], n_tiles_max, drain_tail, 0)

    base = i * DP_TM

    def issue(r, c):
        _token_copy(h_ref, r, xs_ref, pos0_ref[base + r], sem).start(priority=0)
        _token_copy(h_ref, r, xs_ref, pos1_ref[base + r], sem).start(priority=1)
        return c
    lax.fori_loop(0, DP_TM, issue, 0, unroll=DMA_UNROLL)

    for _ in range(2):
        pltpu.make_async_copy(h_ref, xs_ref.at[pl.ds(0, DP_TM)], sem).wait()


def _dispatch(h2, pos0, pos1, pad_start, pad_n, n_tiles, rows_out):
    n = h2.shape[0]
    grid_spec = pltpu.PrefetchScalarGridSpec(
        num_scalar_prefetch=5,
        grid=(n // DP_TM,),
        in_specs=[pl.BlockSpec((DP_TM, TOKEN_ROWS, LANES), lambda i, *_: (i, 0, 0))],
        out_specs=pl.BlockSpec(memory_space=pl.ANY),
        scratch_shapes=[pltpu.VMEM((EXPERT_TILE, TOKEN_ROWS, LANES), BF16),
                        pltpu.SemaphoreType.DMA(()), pltpu.SemaphoreType.DMA(()),
                        pltpu.SemaphoreType.DMA(())],
    )
    return pl.pallas_call(
        _dispatch_kernel,
        grid_spec=grid_spec,
        out_shape=jax.ShapeDtypeStruct((rows_out, TOKEN_ROWS, LANES), BF16),
        compiler_params=pltpu.CompilerParams(dimension_semantics=("arbitrary",),
                                             vmem_limit_bytes=VMEM_LIMIT, has_side_effects=True),
        name="dispatch",
    )(pos0, pos1, pad_start, pad_n, n_tiles, h2)


EXPERT_CAST_ROWS = 256


def _cast_rows(src_ref, dst_ref):
    rows_total = dst_ref.shape[0]

    def body(r, c):
        rows = pl.ds(pl.multiple_of(r * EXPERT_CAST_ROWS, EXPERT_CAST_ROWS), EXPERT_CAST_ROWS)
        dst_ref[rows, :] = src_ref[rows, :].astype(BF16)
        return c
    lax.fori_loop(0, rows_total // EXPERT_CAST_ROWS, body, 0)


def _experts_kernel(te_ref, tr_ref, nt_ref, first_ref, slot_ref, next_ref, xs_ref, w1_hbm, w3_hbm, w2_hbm, ys_ref,
                    w1_stage, w3_stage, w2_stage, w1_scr, w3_scr, w2_scr, sems):
    i = pl.program_id(0)

    def fetch(expert, slot):
        return (pltpu.make_async_copy(w1_hbm.at[expert], w1_stage.at[slot], sems.at[slot]),
                pltpu.make_async_copy(w3_hbm.at[expert], w3_stage.at[slot], sems.at[slot]),
                pltpu.make_async_copy(w2_hbm.at[expert], w2_stage.at[slot], sems.at[slot]))

    @pl.when(i == 0)
    def _():
        for copy in fetch(te_ref[0], 0):
            copy.start()

    @pl.when(first_ref[i] == 1)
    def _():
        slot = slot_ref[i]
        for copy in fetch(te_ref[i], slot):
            copy.wait()

        @pl.when(next_ref[i] >= 0)
        def _():
            for copy in fetch(next_ref[i], 1 - slot):
                copy.start()

        _cast_rows(w1_stage.at[slot], w1_scr)
        _cast_rows(w3_stage.at[slot], w3_scr)
        _cast_rows(w2_stage.at[slot], w2_scr)

    @pl.when(i < nt_ref[0])
    def _():
        x = xs_ref[...].reshape(EXPERT_TILE, D_MODEL)
        a = jnp.dot(x, w1_scr[...], preferred_element_type=F32)
        b = jnp.dot(x, w3_scr[...], preferred_element_type=F32)
        act = (a * jax.nn.sigmoid(a)) * b
        y = jnp.dot(act.astype(BF16), w2_scr[...], preferred_element_type=F32)
        ys_ref[...] = y.astype(BF16).reshape(ys_ref.shape)

    @pl.when(i >= nt_ref[0])
    def _():
        ys_ref[...] = jnp.zeros_like(ys_ref)


def _experts(xs, tile_expert, tile_row, n_tiles, tile_first, tile_slot, tile_next, w1, w3, w2):
    rows = xs.shape[0]
    _, d, f = w1.shape
    tile = (EXPERT_TILE, TOKEN_ROWS, LANES)
    hbm = pl.BlockSpec(memory_space=pl.ANY)
    grid_spec = pltpu.PrefetchScalarGridSpec(
        num_scalar_prefetch=6,
        grid=(rows // EXPERT_TILE,),
        in_specs=[pl.BlockSpec(tile, lambda i, te, tr, *_: (tr[i], 0, 0)), hbm, hbm, hbm],
        out_specs=pl.BlockSpec(tile, lambda i, *_: (i, 0, 0)),
        scratch_shapes=[pltpu.VMEM((2, d, f), F32), pltpu.VMEM((2, d, f), F32), pltpu.VMEM((2, f, d), F32),
                        pltpu.VMEM((d, f), BF16), pltpu.VMEM((d, f), BF16), pltpu.VMEM((f, d), BF16),
                        pltpu.SemaphoreType.DMA((2,))],
    )
    return pl.pallas_call(
        _experts_kernel,
        grid_spec=grid_spec,
        out_shape=jax.ShapeDtypeStruct(xs.shape, BF16),
        compiler_params=_params(("arbitrary",)),
        name="experts",
    )(tile_expert, tile_row, n_tiles, tile_first, tile_slot, tile_next, xs, w1, w3, w2)


CB_TM = 256


def _combine_kernel(pos0_ref, pos1_ref, mod_ref, fg_ref, x1_ref, w0_ref, w1_ref, ys_ref, o_ref,
                    y0_scr, y1_scr, sems):
    i = pl.program_id(0)
    slot = i % 2

    def gather(step, buf):
        base = step * CB_TM

        def issue(r, c):
            _token_copy(ys_ref, pos0_ref[base + r], y0_scr.at[buf], r, sems.at[buf]).start(priority=0)
            _token_copy(ys_ref, pos1_ref[base + r], y1_scr.at[buf], r, sems.at[buf]).start(priority=1)
            return c
        lax.fori_loop(0, CB_TM, issue, 0, unroll=DMA_UNROLL)

    @pl.when(i == 0)
    def _():
        gather(i, slot)

    @pl.when(i + 1 < pl.num_programs(0))
    def _():
        gather(i + 1, 1 - slot)

    pltpu.make_async_copy(ys_ref.at[pl.ds(0, CB_TM)], y0_scr.at[slot], sems.at[slot]).wait()
    pltpu.make_async_copy(ys_ref.at[pl.ds(0, CB_TM)], y1_scr.at[slot], sems.at[slot]).wait()

    gate2 = mod_ref[0, 5:6, :]
    y0 = y0_scr[slot].reshape(CB_TM, D_MODEL).astype(F32)
    y1 = y1_scr[slot].reshape(CB_TM, D_MODEL).astype(F32)
    y = w0_ref[...] * y0 + w1_ref[...] * y1
    x2 = x1_ref[...] + gate2 * y
    o_ref[...] = (x2 * lax.rsqrt(jnp.mean(x2 * x2, axis=-1, keepdims=True) + EPS)) * fg_ref[...]


def _combine(x1, mod3, final_g, ys, pos0, pos1, cw0, cw1, seq):
    n, d = x1.shape
    tps = seq // CB_TM
    grid_spec = pltpu.PrefetchScalarGridSpec(
        num_scalar_prefetch=2,
        grid=(n // CB_TM,),
        in_specs=[pl.BlockSpec((1, N_MOD, d), lambda i, *_: (i // tps, 0, 0)),
                  pl.BlockSpec((1, d), lambda i, *_: (0, 0)),
                  pl.BlockSpec((CB_TM, d), lambda i, *_: (i, 0)),
                  pl.BlockSpec((CB_TM, 1), lambda i, *_: (i, 0)),
                  pl.BlockSpec((CB_TM, 1), lambda i, *_: (i, 0)),
                  pl.BlockSpec(memory_space=pl.ANY)],
        out_specs=pl.BlockSpec((CB_TM, d), lambda i, *_: (i, 0)),
        scratch_shapes=[pltpu.VMEM((2, CB_TM, TOKEN_ROWS, LANES), BF16),
                        pltpu.VMEM((2, CB_TM, TOKEN_ROWS, LANES), BF16),
                        pltpu.SemaphoreType.DMA((2,))],
    )
    return pl.pallas_call(
        _combine_kernel,
        grid_spec=grid_spec,
        out_shape=jax.ShapeDtypeStruct((n, d), F32),
        compiler_params=_params(("arbitrary",)),
        name="combine",
    )(pos0, pos1, mod3, final_g.reshape(1, d), x1, cw0, cw1, ys)


def _layer(x2d, c, seq, ada_w, ada_b, norm1_g, w_in, gmlp_ln_g, gmlp_ln_b, gmlp_w_s, gmlp_b_s, rel_bias,
           w_branch_a, w_branch_b, w_out, norm2_g, w_group, w_expert, w1, w3, w2, final_g):
    n, d = x2d.shape
    nb = c.shape[0]
    mod3 = _ada(c, ada_w, ada_b).reshape(nb, N_MOD, d)

    proj = _in_proj(x2d, mod3, norm1_g, w_in.astype(BF16), seq)
    ya = _gmlp(proj, gmlp_ln_g, gmlp_ln_b, gmlp_w_s, gmlp_b_s)
    yb = _attn(proj, rel_bias, seq)

    wr_t = jnp.concatenate([w_group.T, jnp.zeros((SUBLANES - N_GROUPS, d), F32),
                            w_expert.transpose(0, 2, 1).reshape(N_EXPERTS, d)], axis=0).astype(BF16)
    x1, h2, logits_t = _merge(x2d, mod3, norm2_g, ya, yb, proj, w_branch_a.astype(BF16),
                              w_branch_b.astype(BF16), w_out.astype(BF16), wr_t, seq)

    eidx, rank, cw, counts = _route(logits_t)
    counts = counts[:, 0]
    padded = ((counts + EXPERT_TILE - 1) // EXPERT_TILE) * EXPERT_TILE
    ends = jnp.cumsum(padded)
    offs = ends - padded
    experts = jnp.arange(N_EXPERTS, dtype=I32)
    pos = jnp.sum(jnp.where(eidx[:, :, None] == experts, offs, 0), axis=-1) + rank
    rows_out = 2 * n + N_EXPERTS * EXPERT_TILE
    n_tiles_max = rows_out // EXPERT_TILE
    n_tiles = (ends[-1] // EXPERT_TILE).astype(I32)
    tile_row = jnp.minimum(jnp.arange(n_tiles_max, dtype=I32), n_tiles - 1)
    tile_expert = jnp.minimum(jnp.sum(ends[None, :] <= (tile_row * EXPERT_TILE)[:, None], axis=-1),
                              N_EXPERTS - 1).astype(I32)

    tiles = jnp.arange(n_tiles_max, dtype=I32)
    prev_expert = jnp.concatenate([jnp.full((1,), -1, I32), tile_expert[:-1]])
    tile_first = ((tiles < n_tiles) & (tile_expert != prev_expert)).astype(I32)
    tile_slot = ((jnp.cumsum(tile_first) - 1) % 2).astype(I32)
    later_nonempty = (experts[None, :] > experts[:, None]) & (padded > 0)[None, :]
    next_nonempty = jnp.min(jnp.where(later_nonempty, experts[None, :], N_EXPERTS), axis=1)
    next_nonempty = jnp.where(next_nonempty == N_EXPERTS, -1, next_nonempty)
    tile_next = jnp.sum(jnp.where(tile_expert[:, None] == experts, next_nonempty, 0), axis=-1).astype(I32)

    n_tiles = n_tiles.reshape(1)
    xs = _dispatch(h2, pos[0], pos[1], (offs + counts).astype(I32), (padded - counts).astype(I32), n_tiles,
                   rows_out)
    ys = _experts(xs, tile_expert, tile_row, n_tiles, tile_first, tile_slot, tile_next,
                  w1.reshape(N_EXPERTS, d, D_EXPERT), w3.reshape(N_EXPERTS, d, D_EXPERT),
                  w2.reshape(N_EXPERTS, D_EXPERT, d))
    return _combine(x1, mod3, final_g, ys, pos[0], pos[1], cw[0].reshape(n, 1), cw[1].reshape(n, 1), seq)


def kernel(x, c, ada_w, ada_b, norm1_g, w_in, gmlp_ln_g, gmlp_ln_b, gmlp_w_s, gmlp_b_s, rel_bias, w_branch_a,
           w_branch_b, w_out, norm2_g, w_group, w_expert, w1, w3, w2, final_g):
    b, s, d = x.shape
    out = _layer(x.reshape(b * s, d), c, s, ada_w[0], ada_b[0], norm1_g[0], w_in[0], gmlp_ln_g[0], gmlp_ln_b[0],
                 gmlp_w_s[0], gmlp_b_s[0], rel_bias[0], w_branch_a[0], w_branch_b[0], w_out[0], norm2_g[0],
                 w_group[0], w_expert[0], w1[0], w3[0], w2[0], final_g)
    return out.reshape(b, s, d)
```

```python
import numpy as np
import jax
import jax.numpy as jnp
from jax import lax
from jax.experimental import pallas as pl
from jax.experimental.pallas import tpu as pltpu

F32 = jnp.float32
BF16 = jnp.bfloat16
I32 = jnp.int32

D_MODEL = 2048
CHUNK = 64
EPS = 1e-6
NEG_INF = -1e30
GMLP_BLOCK = 128
GMLP_GROUPS = 8
GMLP_WIDTH = 1024
ATT_HEADS = 16
ATT_HEAD_DIM = 64
ATT_WIDTH = ATT_HEADS * ATT_HEAD_DIM
LEFT_CHUNKS = 8
MAX_REL = 256
N_GROUPS = 4
EXPERTS_PER_GROUP = 8
N_EXPERTS = N_GROUPS * EXPERTS_PER_GROUP
D_EXPERT = 512
N_MOD = 6

LANES = 128
SUBLANES = 8
VMEM_LIMIT = 56 * 1024 * 1024

COL_TILE = 1024
PROJ_COLS = 2 * D_MODEL + 2 * GMLP_WIDTH + 3 * ATT_WIDTH
GATE_TILES = 2 * D_MODEL // COL_TILE
U_TILE = GATE_TILES
V_TILE = GATE_TILES + 1
Q_TILE = GATE_TILES + 2
K_TILE = GATE_TILES + 3
VB_TILE = GATE_TILES + 4

ROUTER_ROWS = SUBLANES + N_EXPERTS

ATT_QBLK = 512
ATT_SUB = 2 * CHUNK
ATT_NSUB = ATT_QBLK // ATT_SUB
ATT_PAIRS = 2
ATT_WIN = ATT_SUB + LEFT_CHUNKS * CHUNK

EXPERT_TILE = 256
TOKEN_ROWS = D_MODEL // LANES


def _params(sem, vmem=VMEM_LIMIT):
    return pltpu.CompilerParams(dimension_semantics=sem, vmem_limit_bytes=vmem)


def _rms_mod(x, g, scale, shift):
    y = x * lax.rsqrt(jnp.mean(x * x, axis=-1, keepdims=True) + EPS)
    return (y * g) * (1.0 + scale) + shift


def _ada_kernel(cb_ref, w_ref, b_ref, o_ref, s_scr):
    nb, d, _ = cb_ref.shape
    tn = w_ref.shape[1]
    reps = tn // LANES

    @pl.when(pl.program_id(0) == 0)
    def _():
        cb = cb_ref[...]
        s_scr[...] = cb * jax.nn.sigmoid(cb)

    def body(i, accs):
        r = pl.ds(pl.multiple_of(i * SUBLANES, SUBLANES), SUBLANES)
        w8 = w_ref[r, :]
        return tuple(accs[b] + w8 * jnp.concatenate([s_scr[b, r, :]] * reps, axis=1) for b in range(nb))

    init = tuple(jnp.zeros((SUBLANES, tn), F32) for _ in range(nb))
    accs = lax.fori_loop(0, d // SUBLANES, body, init, unroll=4)
    for b in range(nb):
        o_ref[b:b + 1, :] = jnp.sum(accs[b], axis=0, keepdims=True) + b_ref[...]


def _ada(c, ada_w, ada_b):
    nb, d = c.shape
    n = ada_w.shape[1]
    tn = 1024
    cb = jnp.broadcast_to(c[:, :, None], (nb, d, LANES))
    return pl.pallas_call(
        _ada_kernel,
        grid=(n // tn,),
        in_specs=[pl.BlockSpec((nb, d, LANES), lambda j: (0, 0, 0)),
                  pl.BlockSpec((d, tn), lambda j: (0, j)),
                  pl.BlockSpec((1, tn), lambda j: (0, j))],
        out_specs=pl.BlockSpec((nb, tn), lambda j: (0, j)),
        out_shape=jax.ShapeDtypeStruct((nb, n), F32),
        scratch_shapes=[pltpu.VMEM((nb, d, LANES), F32)],
        compiler_params=_params(("arbitrary",)),
        name="ada",
    )(cb, ada_w, ada_b.reshape(1, n))


IN_TM = 1024
IN_RB = 512
IN_STAT_RB = 128
IN_NORM_RB = 16


def _gelu(a):
    return 0.5 * a * (1.0 + lax.erf(a * np.float32(np.sqrt(0.5))))


def _in_proj_kernel(mod_ref, g_ref, x_ref, w_ref, o_ref, h_scr, gain_scr, shift_scr, inv_scr):
    j = pl.program_id(1)
    nrb = IN_TM // IN_RB

    @pl.when(j == 0)
    def _():
        d = x_ref.shape[1]
        gain_scr[...] = jnp.broadcast_to(g_ref[...] * (1.0 + mod_ref[0, 1:2, :]), gain_scr.shape)
        shift_scr[...] = jnp.broadcast_to(mod_ref[0, 0:1, :], shift_scr.shape)

        def stats(rb, c):
            rows = pl.ds(pl.multiple_of(rb * IN_STAT_RB, IN_STAT_RB), IN_STAT_RB)
            sq = jnp.zeros((IN_STAT_RB, LANES), F32)
            for k in range(d // LANES):
                xk = x_ref[rows, k * LANES:(k + 1) * LANES]
                sq = sq + xk * xk
            inv = lax.rsqrt(jnp.sum(sq, axis=-1, keepdims=True) * (1.0 / d) + EPS)
            inv_scr[rows, :] = jnp.broadcast_to(inv, (IN_STAT_RB, LANES))
            return c
        lax.fori_loop(0, IN_TM // IN_STAT_RB, stats, 0)

        def apply(rb, c):
            rows = pl.ds(pl.multiple_of(rb * IN_NORM_RB, IN_NORM_RB), IN_NORM_RB)
            inv = inv_scr[rows, :]
            for k in range(d // LANES):
                cols = slice(k * LANES, (k + 1) * LANES)
                y = (x_ref[rows, cols] * inv) * gain_scr[:, cols] + shift_scr[:, cols]
                h_scr[rows, cols] = y.astype(BF16)
            return c
        lax.fori_loop(0, IN_TM // IN_NORM_RB, apply, 0, unroll=2)

    def run(epilogue):
        def body(rb, c):
            rows = pl.ds(pl.multiple_of(rb * IN_RB, IN_RB), IN_RB)
            acc = jnp.dot(h_scr[rows, :], w_ref[...], preferred_element_type=F32)
            o_ref[rows, :] = epilogue(acc).astype(o_ref.dtype)
            return c
        lax.fori_loop(0, nrb, body, 0)

    @pl.when(j < GATE_TILES)
    def _():
        run(jax.nn.sigmoid)

    @pl.when((j == U_TILE) | (j == V_TILE))
    def _():
        run(_gelu)

    @pl.when(j == Q_TILE)
    def _():
        run(lambda a: a * np.float32(ATT_HEAD_DIM ** -0.5))

    @pl.when(j > Q_TILE)
    def _():
        run(lambda a: a)


def _in_proj(x2d, mod3, norm_g, w_in_bf16, seq):
    n, d = x2d.shape
    cols = w_in_bf16.shape[1]
    n_tiles = cols // COL_TILE
    tiles_per_seq = seq // IN_TM
    return pl.pallas_call(
        _in_proj_kernel,
        grid=(n // IN_TM, n_tiles),
        in_specs=[pl.BlockSpec((1, N_MOD, d), lambda i, j: (i // tiles_per_seq, 0, 0)),
                  pl.BlockSpec((1, d), lambda i, j: (0, 0)),
                  pl.BlockSpec((IN_TM, d), lambda i, j: (i, 0)),
                  pl.BlockSpec((d, COL_TILE), lambda i, j: (0, (j + n_tiles - GATE_TILES) % n_tiles))],
        out_specs=pl.BlockSpec((IN_TM, COL_TILE), lambda i, j: (i, j)),
        out_shape=jax.ShapeDtypeStruct((n, cols), BF16),
        scratch_shapes=[pltpu.VMEM((IN_TM, d), BF16), pltpu.VMEM((IN_NORM_RB, d), F32),
                        pltpu.VMEM((IN_NORM_RB, d), F32), pltpu.VMEM((IN_TM, LANES), F32)],
        compiler_params=_params(("arbitrary", "arbitrary")),
        name="in_proj",
    )(mod3, norm_g.reshape(1, d), x2d, w_in_bf16)


GM_TM = 256


def _gmlp_kernel(u_ref, v_ref, lng_ref, lnb_ref, ws_ref, bs_ref, o_ref):
    t = lax.broadcasted_iota(I32, (GMLP_BLOCK, GMLP_BLOCK), 0)
    s = lax.broadcasted_iota(I32, (GMLP_BLOCK, GMLP_BLOCK), 1)
    causal = (s // CHUNK) <= (t // CHUNK)
    lng = lng_ref[...]
    lnb = lnb_ref[...]
    for blk in range(GM_TM // GMLP_BLOCK):
        rows = slice(blk * GMLP_BLOCK, (blk + 1) * GMLP_BLOCK)
        v = v_ref[rows, :].astype(F32)
        mu = jnp.mean(v, axis=-1, keepdims=True)
        vc = v - mu
        var = jnp.mean(vc * vc, axis=-1, keepdims=True)
        vln = ((vc * lax.rsqrt(var + EPS)) * lng + lnb).astype(BF16)
        for g in range(GMLP_GROUPS):
            cols = slice(g * LANES, (g + 1) * LANES)
            w = jnp.where(causal, ws_ref[g], 0.0).astype(BF16)
            mixed = jnp.dot(w, vln[:, cols], preferred_element_type=F32) + bs_ref[g]
            o_ref[rows, cols] = (u_ref[rows, cols].astype(F32) * mixed).astype(o_ref.dtype)


def _gmlp(proj, ln_g, ln_b, w_s, b_s):
    n = proj.shape[0]
    return pl.pallas_call(
        _gmlp_kernel,
        grid=(n // GM_TM,),
        in_specs=[pl.BlockSpec((GM_TM, GMLP_WIDTH), lambda i: (i, U_TILE)),
                  pl.BlockSpec((GM_TM, GMLP_WIDTH), lambda i: (i, V_TILE)),
                  pl.BlockSpec((1, GMLP_WIDTH), lambda i: (0, 0)),
                  pl.BlockSpec((1, GMLP_WIDTH), lambda i: (0, 0)),
                  pl.BlockSpec((GMLP_GROUPS, GMLP_BLOCK, GMLP_BLOCK), lambda i: (0, 0, 0)),
                  pl.BlockSpec((GMLP_GROUPS, GMLP_BLOCK, 1), lambda i: (0, 0, 0))],
        out_specs=pl.BlockSpec((GM_TM, GMLP_WIDTH), lambda i: (i, 0)),
        out_shape=jax.ShapeDtypeStruct((n, GMLP_WIDTH), BF16),
        compiler_params=_params(("arbitrary",)),
        name="gmlp",
    )(proj, proj, ln_g.reshape(1, -1), ln_b.reshape(1, -1), w_s,
      b_s.reshape(GMLP_GROUPS, GMLP_BLOCK, 1))


def _band_bias(rel_table):
    heads = rel_table.shape[0]
    r = np.arange(ATT_SUB)[:, None]
    w = np.arange(ATT_WIN)[None, :]
    j = w // CHUNK - r // CHUNK
    in_band = (j >= 0) & (j <= LEFT_CHUNKS)
    a = np.arange(ATT_NSUB)[:, None, None]
    in_seq = np.broadcast_to(w[None] >= ATT_QBLK - a * ATT_SUB, (ATT_NSUB, ATT_SUB, ATT_WIN))
    visible = np.stack([np.broadcast_to(in_band, in_seq.shape), in_band[None] & in_seq])[:, None]
    far = LEFT_CHUNKS * CHUNK + ATT_SUB - 1
    n_clipped = far - MAX_REL + 1
    table = rel_table.astype(F32)
    lo = MAX_REL - (ATT_WIN - 1 - LEFT_CHUNKS * CHUNK)
    diag = jnp.concatenate([jnp.broadcast_to(table[:, 2 * MAX_REL:], (heads, n_clipped)),
                            jnp.flip(table[:, lo:2 * MAX_REL], axis=1),
                            jnp.zeros((heads, 1), F32)], axis=1)
    span = diag.shape[1] - 1
    shifted = jnp.tile(diag, (1, ATT_SUB))[:, :ATT_SUB * span].reshape(heads, ATT_SUB, span)
    bias = shifted[:, :, ATT_SUB - 1:ATT_SUB - 1 + ATT_WIN]
    return jnp.where(jnp.asarray(visible), bias[None, :, None], NEG_INF)


def _attn_kernel(q_ref, kp_ref, kc_ref, vp_ref, vc_ref, bias_ref, o_ref, k_scr, v_scr):
    k_scr[0:ATT_QBLK, :] = kp_ref[...]
    k_scr[ATT_QBLK:, :] = kc_ref[...]
    v_scr[0:ATT_QBLK, :] = vp_ref[...]
    v_scr[ATT_QBLK:, :] = vc_ref[...]
    lane = lax.broadcasted_iota(I32, (ATT_SUB, LANES), 1)
    low = lane < ATT_HEAD_DIM

    for pair in range(ATT_PAIRS):
        lanes = slice(pair * LANES, (pair + 1) * LANES)
        for a in range(ATT_NSUB):
            rows = slice(a * ATT_SUB, (a + 1) * ATT_SUB)
            win = slice(a * ATT_SUB, a * ATT_SUB + ATT_WIN)
            q = q_ref[rows, lanes]
            zero = jnp.zeros_like(q)
            q2 = jnp.concatenate([jnp.where(low, q, zero), jnp.where(low, zero, q)], axis=0)
            s = lax.dot_general(q2, k_scr[win, lanes], (((1,), (1,)), ((), ())), preferred_element_type=F32)
            s = s + bias_ref[0, 2 * pair:2 * pair + 2, a].reshape(2 * ATT_SUB, ATT_WIN)
            m = jnp.max(s, axis=-1, keepdims=True)
            p = jnp.exp(s - m)
            l = jnp.sum(p, axis=-1, keepdims=True)
            pv = jnp.dot(p.astype(BF16), v_scr[win, lanes], preferred_element_type=F32) / l
            o_ref[rows, lanes] = jnp.where(low, pv[:ATT_SUB], pv[ATT_SUB:]).astype(o_ref.dtype)


def _attn(proj, rel_table, seq):
    n = proj.shape[0]
    bps = seq // ATT_QBLK
    pairs = ATT_WIDTH // LANES
    width = ATT_PAIRS * LANES
    qc, kc, vc = (Q_TILE * COL_TILE // width, K_TILE * COL_TILE // width, VB_TILE * COL_TILE // width)
    bias = _band_bias(rel_table)

    def prev(i):
        return jnp.where(i % bps == 0, i, i - 1)

    def first(i):
        return jnp.where(i % bps == 0, 1, 0)

    blk = (ATT_QBLK, width)
    return pl.pallas_call(
        _attn_kernel,
        grid=(pairs // ATT_PAIRS, n // ATT_QBLK),
        in_specs=[pl.BlockSpec(blk, lambda h, i: (i, qc + h)),
                  pl.BlockSpec(blk, lambda h, i: (prev(i), kc + h)),
                  pl.BlockSpec(blk, lambda h, i: (i, kc + h)),
                  pl.BlockSpec(blk, lambda h, i: (prev(i), vc + h)),
                  pl.BlockSpec(blk, lambda h, i: (i, vc + h)),
                  pl.BlockSpec((1, 2 * ATT_PAIRS, ATT_NSUB, ATT_SUB, ATT_WIN),
                               lambda h, i: (first(i), h, 0, 0, 0))],
        out_specs=pl.BlockSpec(blk, lambda h, i: (i, h)),
        out_shape=jax.ShapeDtypeStruct((n, ATT_WIDTH), BF16),
        scratch_shapes=[pltpu.VMEM((2 * ATT_QBLK, width), BF16),
                        pltpu.VMEM((2 * ATT_QBLK, width), BF16)],
        compiler_params=_params(("arbitrary", "arbitrary")),
        name="attn",
    )(proj, proj, proj, proj, proj, bias)


MG_TM = 256
MG_RB = 256


def _merge_kernel(mod_ref, g_ref, x_ref, ya_ref, yb_ref, ga_ref, gb_ref, wa_ref, wb_ref, wo_ref, wr_ref,
                  x1_ref, h2_ref, lt_ref):
    gate1 = mod_ref[0, 2:3, :]
    shift2 = mod_ref[0, 3:4, :]
    scale2 = mod_ref[0, 4:5, :]
    for part in range(MG_TM // MG_RB):
        rows = slice(part * MG_RB, (part + 1) * MG_RB)
        ya = jnp.dot(ya_ref[rows, :], wa_ref[...], preferred_element_type=F32)
        yb = jnp.dot(yb_ref[rows, :], wb_ref[...], preferred_element_type=F32)
        m = ga_ref[rows, :].astype(F32) * ya + gb_ref[rows, :].astype(F32) * yb
        mixed = jnp.dot(m.astype(BF16), wo_ref[...], preferred_element_type=F32)
        x1 = x_ref[rows, :] + gate1 * mixed
        x1_ref[rows, :] = x1
        h2 = _rms_mod(x1, g_ref[...], scale2, shift2).astype(BF16)
        h2_ref[rows] = h2.reshape(MG_RB, TOKEN_ROWS, LANES)
        lt_ref[:, rows] = lax.dot_general(wr_ref[...], h2, (((1,), (1,)), ((), ())),
                                          preferred_element_type=F32)


def _merge(x2d, mod3, norm_g, ya, yb, proj, wa, wb, wo, wr_t, seq):
    n, d = x2d.shape
    tps = seq // MG_TM
    const = lambda shape: pl.BlockSpec(shape, lambda i: (0,) * len(shape), pipeline_mode=pl.Buffered(1))
    return pl.pallas_call(
        _merge_kernel,
        grid=(n // MG_TM,),
        in_specs=[pl.BlockSpec((1, N_MOD, d), lambda i: (i // tps, 0, 0)),
                  pl.BlockSpec((1, d), lambda i: (0, 0)),
                  pl.BlockSpec((MG_TM, d), lambda i: (i, 0)),
                  pl.BlockSpec((MG_TM, GMLP_WIDTH), lambda i: (i, 0)),
                  pl.BlockSpec((MG_TM, ATT_WIDTH), lambda i: (i, 0)),
                  pl.BlockSpec((MG_TM, d), lambda i: (i, 0)),
                  pl.BlockSpec((MG_TM, d), lambda i: (i, 1)),
                  const((GMLP_WIDTH, d)), const((ATT_WIDTH, d)), const((d, d)),
                  const((ROUTER_ROWS, d))],
        out_specs=[pl.BlockSpec((MG_TM, d), lambda i: (i, 0)),
                   pl.BlockSpec((MG_TM, TOKEN_ROWS, LANES), lambda i: (i, 0, 0)),
                   pl.BlockSpec((ROUTER_ROWS, MG_TM), lambda i: (0, i))],
        out_shape=[jax.ShapeDtypeStruct((n, d), F32),
                   jax.ShapeDtypeStruct((n, TOKEN_ROWS, LANES), BF16),
                   jax.ShapeDtypeStruct((ROUTER_ROWS, n), F32)],
        compiler_params=_params(("arbitrary",)),
        name="merge",
    )(mod3, norm_g.reshape(1, d), x2d, ya, yb, proj, proj, wa, wb, wo, wr_t)


RT_TN = 512


def _first_argmax(vals, vmax, nrows):
    rows = lax.broadcasted_iota(I32, vals.shape, 0)
    return jnp.min(jnp.where(vals == vmax, rows, nrows), axis=0, keepdims=True)


def _route_kernel(lt_ref, e_ref, r_ref, w_ref, cnt_ref, carry_scr):
    @pl.when(pl.program_id(0) == 0)
    def _():
        carry_scr[...] = jnp.zeros_like(carry_scr)

    gl = lt_ref[0:N_GROUPS, :]
    gmax = jnp.max(gl, axis=0, keepdims=True)
    gidx = _first_argmax(gl, gmax, N_GROUPS)
    gw = 1.0 / jnp.sum(jnp.exp(gl - gmax), axis=0, keepdims=True)

    esel = lt_ref[SUBLANES:SUBLANES + EXPERTS_PER_GROUP, :]
    for g in range(1, N_GROUPS):
        lo = SUBLANES + g * EXPERTS_PER_GROUP
        esel = jnp.where(gidx == g, lt_ref[lo:lo + EXPERTS_PER_GROUP, :], esel)
    rows8 = lax.broadcasted_iota(I32, esel.shape, 0)
    m1 = jnp.max(esel, axis=0, keepdims=True)
    i1 = _first_argmax(esel, m1, EXPERTS_PER_GROUP)
    rest = jnp.where(rows8 == i1, -jnp.inf, esel)
    m2 = jnp.max(rest, axis=0, keepdims=True)
    i2 = _first_argmax(rest, m2, EXPERTS_PER_GROUP)
    z = jnp.exp(m2 - m1)
    w_top = 1.0 / (1.0 + z)
    e0 = gidx * EXPERTS_PER_GROUP + i1
    e1 = gidx * EXPERTS_PER_GROUP + i2
    e_ref[0:1, :] = e0
    e_ref[1:2, :] = e1
    w_ref[0:1, :] = gw * w_top
    w_ref[1:2, :] = gw * (z * w_top)

    rows_e = lax.broadcasted_iota(I32, (N_EXPERTS, RT_TN), 0)
    oh0 = rows_e == e0
    oh1 = rows_e == e1
    oh = jnp.where(oh0 | oh1, 1.0, 0.0)
    src = lax.broadcasted_iota(I32, (RT_TN, RT_TN), 0)
    dst = lax.broadcasted_iota(I32, (RT_TN, RT_TN), 1)
    before = jnp.where(src < dst, 1.0, 0.0).astype(BF16)
    carry = carry_scr[...]
    prefix = jnp.dot(oh.astype(BF16), before, preferred_element_type=F32) + carry[:, 0:1]
    r_ref[0:1, :] = jnp.sum(jnp.where(oh0, prefix, 0.0), axis=0, keepdims=True).astype(I32)
    r_ref[1:2, :] = jnp.sum(jnp.where(oh1, prefix, 0.0), axis=0, keepdims=True).astype(I32)
    carry = carry + jnp.sum(oh, axis=1, keepdims=True)
    carry_scr[...] = carry
    cnt_ref[...] = carry.astype(I32)


def _route(logits_t):
    n = logits_t.shape[1]
    slot = pl.BlockSpec((2, RT_TN), lambda i: (0, i))
    return pl.pallas_call(
        _route_kernel,
        grid=(n // RT_TN,),
        in_specs=[pl.BlockSpec((ROUTER_ROWS, RT_TN), lambda i: (0, i))],
        out_specs=[slot, slot, slot, pl.BlockSpec((N_EXPERTS, LANES), lambda i: (0, 0))],
        out_shape=[jax.ShapeDtypeStruct((2, n), I32),
                   jax.ShapeDtypeStruct((2, n), I32),
                   jax.ShapeDtypeStruct((2, n), F32),
                   jax.ShapeDtypeStruct((N_EXPERTS, LANES), I32)],
        scratch_shapes=[pltpu.VMEM((N_EXPERTS, LANES), F32)],
        compiler_params=_params(("arbitrary",)),
        name="route",
    )(logits_t)


DMA_UNROLL = 8


def _token_copy(src_ref, src_row, dst_ref, dst_row, sem):
    return pltpu.make_async_copy(src_ref.at[src_row], dst_ref.at[dst_row], sem)


def _invert_kernel(pos0_ref, pos1_ref, pad_start_ref, pad_n_ref, end_ref, inv_ref):
    n = pos0_ref.shape[0]
    for e in range(N_EXPERTS):
        start = pad_start_ref[e]

        def fill(r, c, start=start):
            inv_ref[start + r] = 0
            return c
        lax.fori_loop(0, pad_n_ref[e], fill, 0)

    def fill_tail(r, c):
        inv_ref[r] = 0
        return c
    lax.fori_loop(end_ref[0], inv_ref.shape[0], fill_tail, 0)

    def place(t, c):
        inv_ref[pos0_ref[t]] = t
        inv_ref[pos1_ref[t]] = t
        return c
    lax.fori_loop(0, n, place, 0, unroll=DMA_UNROLL)


def _invert(pos0, pos1, pad_start, pad_n, rows_used, rows_out):
    grid_spec = pltpu.PrefetchScalarGridSpec(
        num_scalar_prefetch=5,
        grid=(1,),
        in_specs=[],
        out_specs=pl.BlockSpec(memory_space=pltpu.SMEM),
    )
    return pl.pallas_call(
        _invert_kernel,
        grid_spec=grid_spec,
        out_shape=jax.ShapeDtypeStruct((rows_out,), I32),
        compiler_params=_params(("arbitrary",)),
        name="invert",
    )(pos0, pos1, pad_start, pad_n, rows_used)


EXPERT_CAST_ROWS = 256


def _cast_rows(src_ref, dst_ref):
    rows_total = dst_ref.shape[0]

    def body(r, c):
        rows = pl.ds(pl.multiple_of(r * EXPERT_CAST_ROWS, EXPERT_CAST_ROWS), EXPERT_CAST_ROWS)
        dst_ref[rows, :] = src_ref[rows, :].astype(BF16)
        return c
    lax.fori_loop(0, rows_total // EXPERT_CAST_ROWS, body, 0)


def _experts_kernel(te_ref, nt_ref, first_ref, slot_ref, next_ref, inv_ref, h_hbm, w1_hbm, w3_hbm, w2_hbm, ys_ref,
                    w1_stage, w3_stage, w2_stage, w1_scr, w3_scr, w2_scr, x_even, x_odd, sems, gsems):
    i = pl.program_id(0)
    n_valid = nt_ref[0]

    def fetch(expert, slot):
        return (pltpu.make_async_copy(w1_hbm.at[expert], w1_stage.at[slot], sems.at[slot]),
                pltpu.make_async_copy(w3_hbm.at[expert], w3_stage.at[slot], sems.at[slot]),
                pltpu.make_async_copy(w2_hbm.at[expert], w2_stage.at[slot], sems.at[slot]))

    def gather_wait(buf, sem):
        pltpu.make_async_copy(h_hbm.at[pl.ds(0, EXPERT_TILE)], buf, sem).wait()

    @pl.when(i == 0)
    def _():
        for copy in fetch(te_ref[0], 0):
            copy.start()

        def issue(r, c):
            _token_copy(h_hbm, inv_ref[r], x_even, r, gsems.at[0]).start()
            return c
        lax.fori_loop(0, EXPERT_TILE, issue, 0, unroll=DMA_UNROLL)

    @pl.when(first_ref[i] == 1)
    def _():
        slot = slot_ref[i]
        for copy in fetch(te_ref[i], slot):
            copy.wait()

        @pl.when(next_ref[i] >= 0)
        def _():
            for copy in fetch(next_ref[i], 1 - slot):
                copy.start()

        _cast_rows(w1_stage.at[slot], w1_scr)
        _cast_rows(w3_stage.at[slot], w3_scr)
        _cast_rows(w2_stage.at[slot], w2_scr)

    def run(cur, cur_sem, nxt, nxt_sem):
        gather_wait(cur, cur_sem)
        base = jnp.minimum(i + 1, n_valid - 1) * EXPERT_TILE
        for r in range(EXPERT_TILE):
            _token_copy(h_hbm, inv_ref[base + r], nxt, r, nxt_sem).start(priority=r % 2)
        x = cur[...].reshape(EXPERT_TILE, D_MODEL)
        a = jnp.dot(x, w1_scr[...], preferred_element_type=F32)
        b = jnp.dot(x, w3_scr[...], preferred_element_type=F32)
        act = (a * jax.nn.sigmoid(a)) * b
        y = jnp.dot(act.astype(BF16), w2_scr[...], preferred_element_type=F32)
        ys_ref[...] = y.astype(BF16).reshape(ys_ref.shape)

        @pl.when(i == n_valid - 1)
        def _():
            gather_wait(nxt, nxt_sem)

    @pl.when((i < n_valid) & (i % 2 == 0))
    def _():
        run(x_even, gsems.at[0], x_odd, gsems.at[1])

    @pl.when((i < n_valid) & (i % 2 == 1))
    def _():
        run(x_odd, gsems.at[1], x_even, gsems.at[0])

    @pl.when(i >= n_valid)
    def _():
        ys_ref[...] = jnp.zeros_like(ys_ref)


def _experts(h2, inv, tile_expert, n_tiles, tile_first, tile_slot, tile_next, w1, w3, w2):
    rows = inv.shape[0]
    _, d, f = w1.shape
    tile = (EXPERT_TILE, TOKEN_ROWS, LANES)
    hbm = pl.BlockSpec(memory_space=pl.ANY)
    grid_spec = pltpu.PrefetchScalarGridSpec(
        num_scalar_prefetch=6,
        grid=(rows // EXPERT_TILE,),
        in_specs=[hbm, hbm, hbm, hbm],
        out_specs=pl.BlockSpec(tile, lambda i, *_: (i, 0, 0)),
        scratch_shapes=[pltpu.VMEM((2, d, f), F32), pltpu.VMEM((2, d, f), F32), pltpu.VMEM((2, f, d), F32),
                        pltpu.VMEM((d, f), BF16), pltpu.VMEM((d, f), BF16), pltpu.VMEM((f, d), BF16),
                        pltpu.VMEM(tile, BF16), pltpu.VMEM(tile, BF16),
                        pltpu.SemaphoreType.DMA((2,)), pltpu.SemaphoreType.DMA((2,))],
    )
    return pl.pallas_call(
        _experts_kernel,
        grid_spec=grid_spec,
        out_shape=jax.ShapeDtypeStruct((rows,) + tile[1:], BF16),
        compiler_params=_params(("arbitrary",)),
        name="experts",
    )(tile_expert, n_tiles, tile_first, tile_slot, tile_next, inv, h2, w1, w3, w2)


CB_TM = 256


def _combine_kernel(pos0_ref, pos1_ref, mod_ref, fg_ref, x1_ref, w0_ref, w1_ref, ys_ref, o_ref,
                    y0_scr, y1_scr, sems):
    i = pl.program_id(0)
    slot = i % 2

    def gather(step, buf):
        base = step * CB_TM

        def issue(r, c):
            _token_copy(ys_ref, pos0_ref[base + r], y0_scr.at[buf], r, sems.at[buf]).start(priority=0)
            _token_copy(ys_ref, pos1_ref[base + r], y1_scr.at[buf], r, sems.at[buf]).start(priority=1)
            return c
        lax.fori_loop(0, CB_TM, issue, 0, unroll=DMA_UNROLL)

    @pl.when(i == 0)
    def _():
        gather(i, slot)

    @pl.when(i + 1 < pl.num_programs(0))
    def _():
        gather(i + 1, 1 - slot)

    pltpu.make_async_copy(ys_ref.at[pl.ds(0, CB_TM)], y0_scr.at[slot], sems.at[slot]).wait()
    pltpu.make_async_copy(ys_ref.at[pl.ds(0, CB_TM)], y1_scr.at[slot], sems.at[slot]).wait()

    gate2 = mod_ref[0, 5:6, :]
    y0 = y0_scr[slot].reshape(CB_TM, D_MODEL).astype(F32)
    y1 = y1_scr[slot].reshape(CB_TM, D_MODEL).astype(F32)
    y = w0_ref[...] * y0 + w1_ref[...] * y1
    x2 = x1_ref[...] + gate2 * y
    o_ref[...] = (x2 * lax.rsqrt(jnp.mean(x2 * x2, axis=-1, keepdims=True) + EPS)) * fg_ref[...]


def _combine(x1, mod3, final_g, ys, pos0, pos1, cw0, cw1, seq):
    n, d = x1.shape
    tps = seq // CB_TM
    grid_spec = pltpu.PrefetchScalarGridSpec(
        num_scalar_prefetch=2,
        grid=(n // CB_TM,),
        in_specs=[pl.BlockSpec((1, N_MOD, d), lambda i, *_: (i // tps, 0, 0)),
                  pl.BlockSpec((1, d), lambda i, *_: (0, 0)),
                  pl.BlockSpec((CB_TM, d), lambda i, *_: (i, 0)),
                  pl.BlockSpec((CB_TM, 1), lambda i, *_: (i, 0)),
                  pl.BlockSpec((CB_TM, 1), lambda i, *_: (i, 0)),
                  pl.BlockSpec(memory_space=pl.ANY)],
        out_specs=pl.BlockSpec((CB_TM, d), lambda i, *_: (i, 0)),
        scratch_shapes=[pltpu.VMEM((2, CB_TM, TOKEN_ROWS, LANES), BF16),
                        pltpu.VMEM((2, CB_TM, TOKEN_ROWS, LANES), BF16),
                        pltpu.SemaphoreType.DMA((2,))],
    )
    return pl.pallas_call(
        _combine_kernel,
        grid_spec=grid_spec,
        out_shape=jax.ShapeDtypeStruct((n, d), F32),
        compiler_params=_params(("arbitrary",)),
        name="combine",
    )(pos0, pos1, mod3, final_g.reshape(1, d), x1, cw0, cw1, ys)


def _layer(x2d, c, seq, ada_w, ada_b, norm1_g, w_in, gmlp_ln_g, gmlp_ln_b, gmlp_w_s, gmlp_b_s, rel_bias,
           w_branch_a, w_branch_b, w_out, norm2_g, w_group, w_expert, w1, w3, w2, final_g):
    n, d = x2d.shape
    nb = c.shape[0]
    mod3 = _ada(c, ada_w, ada_b).reshape(nb, N_MOD, d)

    proj = _in_proj(x2d, mod3, norm1_g, w_in.astype(BF16), seq)
    ya = _gmlp(proj, gmlp_ln_g, gmlp_ln_b, gmlp_w_s, gmlp_b_s)
    yb = _attn(proj, rel_bias, seq)

    wr_t = jnp.concatenate([w_group.T, jnp.zeros((SUBLANES - N_GROUPS, d), F32),
                            w_expert.transpose(0, 2, 1).reshape(N_EXPERTS, d)], axis=0).astype(BF16)
    x1, h2, logits_t = _merge(x2d, mod3, norm2_g, ya, yb, proj, w_branch_a.astype(BF16),
                              w_branch_b.astype(BF16), w_out.astype(BF16), wr_t, seq)

    eidx, rank, cw, counts = _route(logits_t)
    counts = counts[:, 0]
    padded = ((counts + EXPERT_TILE - 1) // EXPERT_TILE) * EXPERT_TILE
    ends = jnp.cumsum(padded)
    offs = ends - padded
    experts = jnp.arange(N_EXPERTS, dtype=I32)
    pos = jnp.sum(jnp.where(eidx[:, :, None] == experts, offs, 0), axis=-1) + rank
    rows_out = 2 * n + N_EXPERTS * EXPERT_TILE
    n_tiles_max = rows_out // EXPERT_TILE
    n_tiles = (ends[-1] // EXPERT_TILE).astype(I32)
    tile_row = jnp.minimum(jnp.arange(n_tiles_max, dtype=I32), n_tiles - 1)
    tile_expert = jnp.minimum(jnp.sum(ends[None, :] <= (tile_row * EXPERT_TILE)[:, None], axis=-1),
                              N_EXPERTS - 1).astype(I32)

    tiles = jnp.arange(n_tiles_max, dtype=I32)
    prev_expert = jnp.concatenate([jnp.full((1,), -1, I32), tile_expert[:-1]])
    tile_first = ((tiles < n_tiles) & (tile_expert != prev_expert)).astype(I32)
    tile_slot = ((jnp.cumsum(tile_first) - 1) % 2).astype(I32)
    later_nonempty = (experts[None, :] > experts[:, None]) & (padded > 0)[None, :]
    next_nonempty = jnp.min(jnp.where(later_nonempty, experts[None, :], N_EXPERTS), axis=1)
    next_nonempty = jnp.where(next_nonempty == N_EXPERTS, -1, next_nonempty)
    tile_next = jnp.sum(jnp.where(tile_expert[:, None] == experts, next_nonempty, 0), axis=-1).astype(I32)

    inv = _invert(pos[0], pos[1], (offs + counts).astype(I32), (padded - counts).astype(I32),
                  ends[-1:].astype(I32), rows_out)
    ys = _experts(h2, inv, tile_expert, n_tiles.reshape(1), tile_first, tile_slot, tile_next,
                  w1.reshape(N_EXPERTS, d, D_EXPERT), w3.reshape(N_EXPERTS, d, D_EXPERT),
                  w2.reshape(N_EXPERTS, D_EXPERT, d))
    return _combine(x1, mod3, final_g, ys, pos[0], pos[1], cw[0].reshape(n, 1), cw[1].reshape(n, 1), seq)


def kernel(x, c, ada_w, ada_b, norm1_g, w_in, gmlp_ln_g, gmlp_ln_b, gmlp_w_s, gmlp_b_s, rel_bias, w_branch_a,
           w_branch_b, w_out, norm2_g, w_group, w_expert, w1, w3, w2, final_g):
    b, s, d = x.shape
    out = _layer(x.reshape(b * s, d), c, s, ada_w[0], ada_b[0], norm1_g[0], w_in[0], gmlp_ln_g[0], gmlp_ln_b[0],
                 gmlp_w_s[0], gmlp_b_s[0], rel_bias[0], w_branch_a[0], w_branch_b[0], w_out[0], norm2_g[0],
                 w_group[0], w_expert[0], w1[0], w3[0], w2[0], final_g)
    return out.reshape(b, s, d)
```

```python
import numpy as np
import jax
import jax.numpy as jnp
from jax import lax
from jax.experimental import pallas as pl
from jax.experimental.pallas import tpu as pltpu

F32 = jnp.float32
BF16 = jnp.bfloat16
I32 = jnp.int32

D_MODEL = 2048
CHUNK = 64
EPS = 1e-6
NEG_INF = -1e30
GMLP_BLOCK = 128
GMLP_GROUPS = 8
GMLP_WIDTH = 1024
ATT_HEADS = 16
ATT_HEAD_DIM = 64
ATT_WIDTH = ATT_HEADS * ATT_HEAD_DIM
LEFT_CHUNKS = 8
MAX_REL = 256
N_GROUPS = 4
EXPERTS_PER_GROUP = 8
N_EXPERTS = N_GROUPS * EXPERTS_PER_GROUP
D_EXPERT = 512
N_MOD = 6

LANES = 128
SUBLANES = 8
VMEM_LIMIT = 56 * 1024 * 1024

COL_TILE = 1024
PROJ_COLS = 2 * D_MODEL + 2 * GMLP_WIDTH + 3 * ATT_WIDTH
GATE_TILES = 2 * D_MODEL // COL_TILE
U_TILE = GATE_TILES
V_TILE = GATE_TILES + 1
Q_TILE = GATE_TILES + 2
K_TILE = GATE_TILES + 3
VB_TILE = GATE_TILES + 4

ROUTER_ROWS = SUBLANES + N_EXPERTS

ATT_QBLK = 512
ATT_SUB = 2 * CHUNK
ATT_NSUB = ATT_QBLK // ATT_SUB
ATT_PAIRS = 2
ATT_WIN = ATT_SUB + LEFT_CHUNKS * CHUNK

EXPERT_TILE = 256
TOKEN_ROWS = D_MODEL // LANES


def _params(sem, vmem=VMEM_LIMIT):
    return pltpu.CompilerParams(dimension_semantics=sem, vmem_limit_bytes=vmem)


def _rms_mod(x, g, scale, shift):
    y = x * lax.rsqrt(jnp.mean(x * x, axis=-1, keepdims=True) + EPS)
    return (y * g) * (1.0 + scale) + shift


def _ada_kernel(cb_ref, w_ref, b_ref, o_ref, s_scr):
    nb, d, _ = cb_ref.shape
    tn = w_ref.shape[1]
    reps = tn // LANES

    @pl.when(pl.program_id(0) == 0)
    def _():
        cb = cb_ref[...]
        s_scr[...] = cb * jax.nn.sigmoid(cb)

    def body(i, accs):
        r = pl.ds(pl.multiple_of(i * SUBLANES, SUBLANES), SUBLANES)
        w8 = w_ref[r, :]
        return tuple(accs[b] + w8 * jnp.concatenate([s_scr[b, r, :]] * reps, axis=1) for b in range(nb))

    init = tuple(jnp.zeros((SUBLANES, tn), F32) for _ in range(nb))
    accs = lax.fori_loop(0, d // SUBLANES, body, init, unroll=4)
    for b in range(nb):
        o_ref[b:b + 1, :] = jnp.sum(accs[b], axis=0, keepdims=True) + b_ref[...]


def _ada(c, ada_w, ada_b):
    nb, d = c.shape
    n = ada_w.shape[1]
    tn = 1024
    cb = jnp.broadcast_to(c[:, :, None], (nb, d, LANES))
    return pl.pallas_call(
        _ada_kernel,
        grid=(n // tn,),
        in_specs=[pl.BlockSpec((nb, d, LANES), lambda j: (0, 0, 0)),
                  pl.BlockSpec((d, tn), lambda j: (0, j)),
                  pl.BlockSpec((1, tn), lambda j: (0, j))],
        out_specs=pl.BlockSpec((nb, tn), lambda j: (0, j)),
        out_shape=jax.ShapeDtypeStruct((nb, n), F32),
        scratch_shapes=[pltpu.VMEM((nb, d, LANES), F32)],
        compiler_params=_params(("arbitrary",)),
        name="ada",
    )(cb, ada_w, ada_b.reshape(1, n))


IN_TM = 1024
IN_RB = 1024
IN_STAT_RB = 128
IN_NORM_RB = 16


def _gelu(a):
    return 0.5 * a * (1.0 + lax.erf(a * np.float32(np.sqrt(0.5))))


def _in_proj_kernel(mod_ref, g_ref, x_ref, w_ref, o_ref, h_scr, gain_scr, shift_scr, inv_scr):
    j = pl.program_id(1)
    nrb = IN_TM // IN_RB

    @pl.when(j == 0)
    def _():
        d = x_ref.shape[1]
        gain_scr[...] = jnp.broadcast_to(g_ref[...] * (1.0 + mod_ref[0, 1:2, :]), gain_scr.shape)
        shift_scr[...] = jnp.broadcast_to(mod_ref[0, 0:1, :], shift_scr.shape)

        def stats(rb, c):
            rows = pl.ds(pl.multiple_of(rb * IN_STAT_RB, IN_STAT_RB), IN_STAT_RB)
            sq = jnp.zeros((IN_STAT_RB, LANES), F32)
            for k in range(d // LANES):
                xk = x_ref[rows, k * LANES:(k + 1) * LANES]
                sq = sq + xk * xk
            inv = lax.rsqrt(jnp.sum(sq, axis=-1, keepdims=True) * (1.0 / d) + EPS)
            inv_scr[rows, :] = jnp.broadcast_to(inv, (IN_STAT_RB, LANES))
            return c
        lax.fori_loop(0, IN_TM // IN_STAT_RB, stats, 0)

        def apply(rb, c):
            rows = pl.ds(pl.multiple_of(rb * IN_NORM_RB, IN_NORM_RB), IN_NORM_RB)
            inv = inv_scr[rows, :]
            for k in range(d // LANES):
                cols = slice(k * LANES, (k + 1) * LANES)
                y = (x_ref[rows, cols] * inv) * gain_scr[:, cols] + shift_scr[:, cols]
                h_scr[rows, cols] = y.astype(BF16)
            return c
        lax.fori_loop(0, IN_TM // IN_NORM_RB, apply, 0, unroll=2)

    def run(epilogue):
        def body(rb, c):
            rows = pl.ds(pl.multiple_of(rb * IN_RB, IN_RB), IN_RB)
            acc = jnp.dot(h_scr[rows, :], w_ref[...], preferred_element_type=F32)
            o_ref[rows, :] = epilogue(acc).astype(o_ref.dtype)
            return c
        lax.fori_loop(0, nrb, body, 0)

    @pl.when(j < GATE_TILES)
    def _():
        run(jax.nn.sigmoid)

    @pl.when((j == U_TILE) | (j == V_TILE))
    def _():
        run(_gelu)

    @pl.when(j == Q_TILE)
    def _():
        run(lambda a: a * np.float32(ATT_HEAD_DIM ** -0.5))

    @pl.when(j > Q_TILE)
    def _():
        run(lambda a: a)


def _in_proj(x2d, mod3, norm_g, w_in_bf16, seq):
    n, d = x2d.shape
    cols = w_in_bf16.shape[1]
    n_tiles = cols // COL_TILE
    tiles_per_seq = seq // IN_TM
    return pl.pallas_call(
        _in_proj_kernel,
        grid=(n // IN_TM, n_tiles),
        in_specs=[pl.BlockSpec((1, N_MOD, d), lambda i, j: (i // tiles_per_seq, 0, 0)),
                  pl.BlockSpec((1, d), lambda i, j: (0, 0)),
                  pl.BlockSpec((IN_TM, d), lambda i, j: (i, 0)),
                  pl.BlockSpec((d, COL_TILE), lambda i, j: (0, (j + n_tiles - GATE_TILES) % n_tiles))],
        out_specs=pl.BlockSpec((IN_TM, COL_TILE), lambda i, j: (i, j)),
        out_shape=jax.ShapeDtypeStruct((n, cols), BF16),
        scratch_shapes=[pltpu.VMEM((IN_TM, d), BF16), pltpu.VMEM((IN_NORM_RB, d), F32),
                        pltpu.VMEM((IN_NORM_RB, d), F32), pltpu.VMEM((IN_TM, LANES), F32)],
        compiler_params=_params(("arbitrary", "arbitrary")),
        name="in_proj",
    )(mod3, norm_g.reshape(1, d), x2d, w_in_bf16)


GM_TM = 256


def _gmlp_kernel(u_ref, v_ref, lng_ref, lnb_ref, ws_ref, bs_ref, o_ref):
    t = lax.broadcasted_iota(I32, (GMLP_BLOCK, GMLP_BLOCK), 0)
    s = lax.broadcasted_iota(I32, (GMLP_BLOCK, GMLP_BLOCK), 1)
    causal = (s // CHUNK) <= (t // CHUNK)
    lng = lng_ref[...]
    lnb = lnb_ref[...]
    for blk in range(GM_TM // GMLP_BLOCK):
        rows = slice(blk * GMLP_BLOCK, (blk + 1) * GMLP_BLOCK)
        v = v_ref[rows, :].astype(F32)
        mu = jnp.mean(v, axis=-1, keepdims=True)
        vc = v - mu
        var = jnp.mean(vc * vc, axis=-1, keepdims=True)
        vln = ((vc * lax.rsqrt(var + EPS)) * lng + lnb).astype(BF16)
        for g in range(GMLP_GROUPS):
            cols = slice(g * LANES, (g + 1) * LANES)
            w = jnp.where(causal, ws_ref[g], 0.0).astype(BF16)
            mixed = jnp.dot(w, vln[:, cols], preferred_element_type=F32) + bs_ref[g]
            o_ref[rows, cols] = (u_ref[rows, cols].astype(F32) * mixed).astype(o_ref.dtype)


def _gmlp(proj, ln_g, ln_b, w_s, b_s):
    n = proj.shape[0]
    return pl.pallas_call(
        _gmlp_kernel,
        grid=(n // GM_TM,),
        in_specs=[pl.BlockSpec((GM_TM, GMLP_WIDTH), lambda i: (i, U_TILE)),
                  pl.BlockSpec((GM_TM, GMLP_WIDTH), lambda i: (i, V_TILE)),
                  pl.BlockSpec((1, GMLP_WIDTH), lambda i: (0, 0)),
                  pl.BlockSpec((1, GMLP_WIDTH), lambda i: (0, 0)),
                  pl.BlockSpec((GMLP_GROUPS, GMLP_BLOCK, GMLP_BLOCK), lambda i: (0, 0, 0)),
                  pl.BlockSpec((GMLP_GROUPS, GMLP_BLOCK, 1), lambda i: (0, 0, 0))],
        out_specs=pl.BlockSpec((GM_TM, GMLP_WIDTH), lambda i: (i, 0)),
        out_shape=jax.ShapeDtypeStruct((n, GMLP_WIDTH), BF16),
        compiler_params=_params(("arbitrary",)),
        name="gmlp",
    )(proj, proj, ln_g.reshape(1, -1), ln_b.reshape(1, -1), w_s,
      b_s.reshape(GMLP_GROUPS, GMLP_BLOCK, 1))


def _band_bias(rel_table):
    heads = rel_table.shape[0]
    r = np.arange(ATT_SUB)[:, None]
    w = np.arange(ATT_WIN)[None, :]
    j = w // CHUNK - r // CHUNK
    in_band = (j >= 0) & (j <= LEFT_CHUNKS)
    a = np.arange(ATT_NSUB)[:, None, None]
    in_seq = np.broadcast_to(w[None] >= ATT_QBLK - a * ATT_SUB, (ATT_NSUB, ATT_SUB, ATT_WIN))
    visible = np.stack([np.broadcast_to(in_band, in_seq.shape), in_band[None] & in_seq])[:, None]
    far = LEFT_CHUNKS * CHUNK + ATT_SUB - 1
    n_clipped = far - MAX_REL + 1
    table = rel_table.astype(F32)
    lo = MAX_REL - (ATT_WIN - 1 - LEFT_CHUNKS * CHUNK)
    diag = jnp.concatenate([jnp.broadcast_to(table[:, 2 * MAX_REL:], (heads, n_clipped)),
                            jnp.flip(table[:, lo:2 * MAX_REL], axis=1),
                            jnp.zeros((heads, 1), F32)], axis=1)
    span = diag.shape[1] - 1
    shifted = jnp.tile(diag, (1, ATT_SUB))[:, :ATT_SUB * span].reshape(heads, ATT_SUB, span)
    bias = shifted[:, :, ATT_SUB - 1:ATT_SUB - 1 + ATT_WIN]
    return jnp.where(jnp.asarray(visible), bias[None, :, None], NEG_INF)


def _attn_kernel(q_ref, kp_ref, kc_ref, vp_ref, vc_ref, bias_ref, o_ref, k_scr, v_scr):
    k_scr[0:ATT_QBLK, :] = kp_ref[...]
    k_scr[ATT_QBLK:, :] = kc_ref[...]
    v_scr[0:ATT_QBLK, :] = vp_ref[...]
    v_scr[ATT_QBLK:, :] = vc_ref[...]
    lane = lax.broadcasted_iota(I32, (ATT_SUB, LANES), 1)
    low = lane < ATT_HEAD_DIM

    for pair in range(ATT_PAIRS):
        lanes = slice(pair * LANES, (pair + 1) * LANES)
        for a in range(ATT_NSUB):
            rows = slice(a * ATT_SUB, (a + 1) * ATT_SUB)
            win = slice(a * ATT_SUB, a * ATT_SUB + ATT_WIN)
            q = q_ref[rows, lanes]
            zero = jnp.zeros_like(q)
            q2 = jnp.concatenate([jnp.where(low, q, zero), jnp.where(low, zero, q)], axis=0)
            s = lax.dot_general(q2, k_scr[win, lanes], (((1,), (1,)), ((), ())), preferred_element_type=F32)
            s = s + bias_ref[0, 2 * pair:2 * pair + 2, a].reshape(2 * ATT_SUB, ATT_WIN)
            m = jnp.max(s, axis=-1, keepdims=True)
            p = jnp.exp(s - m)
            l = jnp.sum(p, axis=-1, keepdims=True)
            pv = jnp.dot(p.astype(BF16), v_scr[win, lanes], preferred_element_type=F32) / l
            o_ref[rows, lanes] = jnp.where(low, pv[:ATT_SUB], pv[ATT_SUB:]).astype(o_ref.dtype)


def _attn(proj, rel_table, seq):
    n = proj.shape[0]
    bps = seq // ATT_QBLK
    pairs = ATT_WIDTH // LANES
    width = ATT_PAIRS * LANES
    qc, kc, vc = (Q_TILE * COL_TILE // width, K_TILE * COL_TILE // width, VB_TILE * COL_TILE // width)
    bias = _band_bias(rel_table)

    def prev(i):
        return jnp.where(i % bps == 0, i, i - 1)

    def first(i):
        return jnp.where(i % bps == 0, 1, 0)

    blk = (ATT_QBLK, width)
    return pl.pallas_call(
        _attn_kernel,
        grid=(pairs // ATT_PAIRS, n // ATT_QBLK),
        in_specs=[pl.BlockSpec(blk, lambda h, i: (i, qc + h)),
                  pl.BlockSpec(blk, lambda h, i: (prev(i), kc + h)),
                  pl.BlockSpec(blk, lambda h, i: (i, kc + h)),
                  pl.BlockSpec(blk, lambda h, i: (prev(i), vc + h)),
                  pl.BlockSpec(blk, lambda h, i: (i, vc + h)),
                  pl.BlockSpec((1, 2 * ATT_PAIRS, ATT_NSUB, ATT_SUB, ATT_WIN),
                               lambda h, i: (first(i), h, 0, 0, 0))],
        out_specs=pl.BlockSpec(blk, lambda h, i: (i, h)),
        out_shape=jax.ShapeDtypeStruct((n, ATT_WIDTH), BF16),
        scratch_shapes=[pltpu.VMEM((2 * ATT_QBLK, width), BF16),
                        pltpu.VMEM((2 * ATT_QBLK, width), BF16)],
        compiler_params=_params(("arbitrary", "arbitrary")),
        name="attn",
    )(proj, proj, proj, proj, proj, bias)


MG_TM = 256
MG_RB = 256


def _merge_kernel(mod_ref, g_ref, x_ref, ya_ref, yb_ref, ga_ref, gb_ref, wa_ref, wb_ref, wo_ref, wr_ref,
                  x1_ref, h2_ref, lt_ref):
    gate1 = mod_ref[0, 2:3, :]
    shift2 = mod_ref[0, 3:4, :]
    scale2 = mod_ref[0, 4:5, :]
    for part in range(MG_TM // MG_RB):
        rows = slice(part * MG_RB, (part + 1) * MG_RB)
        ya = jnp.dot(ya_ref[rows, :], wa_ref[...], preferred_element_type=F32)
        yb = jnp.dot(yb_ref[rows, :], wb_ref[...], preferred_element_type=F32)
        m = ga_ref[rows, :].astype(F32) * ya + gb_ref[rows, :].astype(F32) * yb
        mixed = jnp.dot(m.astype(BF16), wo_ref[...], preferred_element_type=F32)
        x1 = x_ref[rows, :] + gate1 * mixed
        x1_ref[rows, :] = x1
        h2 = _rms_mod(x1, g_ref[...], scale2, shift2).astype(BF16)
        h2_ref[rows] = h2.reshape(MG_RB, TOKEN_ROWS, LANES)
        lt_ref[:, rows] = lax.dot_general(wr_ref[...], h2, (((1,), (1,)), ((), ())),
                                          preferred_element_type=F32)


def _merge(x2d, mod3, norm_g, ya, yb, proj, wa, wb, wo, wr_t, seq):
    n, d = x2d.shape
    tps = seq // MG_TM
    const = lambda shape: pl.BlockSpec(shape, lambda i: (0,) * len(shape), pipeline_mode=pl.Buffered(1))
    return pl.pallas_call(
        _merge_kernel,
        grid=(n // MG_TM,),
        in_specs=[pl.BlockSpec((1, N_MOD, d), lambda i: (i // tps, 0, 0)),
                  pl.BlockSpec((1, d), lambda i: (0, 0)),
                  pl.BlockSpec((MG_TM, d), lambda i: (i, 0)),
                  pl.BlockSpec((MG_TM, GMLP_WIDTH), lambda i: (i, 0)),
                  pl.BlockSpec((MG_TM, ATT_WIDTH), lambda i: (i, 0)),
                  pl.BlockSpec((MG_TM, d), lambda i: (i, 0)),
                  pl.BlockSpec((MG_TM, d), lambda i: (i, 1)),
                  const((GMLP_WIDTH, d)), const((ATT_WIDTH, d)), const((d, d)),
                  const((ROUTER_ROWS, d))],
        out_specs=[pl.BlockSpec((MG_TM, d), lambda i: (i, 0)),
                   pl.BlockSpec((MG_TM, TOKEN_ROWS, LANES), lambda i: (i, 0, 0)),
                   pl.BlockSpec((ROUTER_ROWS, MG_TM), lambda i: (0, i))],
        out_shape=[jax.ShapeDtypeStruct((n, d), F32),
                   jax.ShapeDtypeStruct((n, TOKEN_ROWS, LANES), BF16),
                   jax.ShapeDtypeStruct((ROUTER_ROWS, n), F32)],
        compiler_params=_params(("arbitrary",)),
        name="merge",
    )(mod3, norm_g.reshape(1, d), x2d, ya, yb, proj, proj, wa, wb, wo, wr_t)


RT_TN = 512


def _first_argmax(vals, vmax, nrows):
    rows = lax.broadcasted_iota(I32, vals.shape, 0)
    return jnp.min(jnp.where(vals == vmax, rows, nrows), axis=0, keepdims=True)


def _route_kernel(lt_ref, e_ref, r_ref, w_ref, cnt_ref, carry_scr):
    @pl.when(pl.program_id(0) == 0)
    def _():
        carry_scr[...] = jnp.zeros_like(carry_scr)

    gl = lt_ref[0:N_GROUPS, :]
    gmax = jnp.max(gl, axis=0, keepdims=True)
    gidx = _first_argmax(gl, gmax, N_GROUPS)
    gw = 1.0 / jnp.sum(jnp.exp(gl - gmax), axis=0, keepdims=True)

    esel = lt_ref[SUBLANES:SUBLANES + EXPERTS_PER_GROUP, :]
    for g in range(1, N_GROUPS):
        lo = SUBLANES + g * EXPERTS_PER_GROUP
        esel = jnp.where(gidx == g, lt_ref[lo:lo + EXPERTS_PER_GROUP, :], esel)
    rows8 = lax.broadcasted_iota(I32, esel.shape, 0)
    m1 = jnp.max(esel, axis=0, keepdims=True)
    i1 = _first_argmax(esel, m1, EXPERTS_PER_GROUP)
    rest = jnp.where(rows8 == i1, -jnp.inf, esel)
    m2 = jnp.max(rest, axis=0, keepdims=True)
    i2 = _first_argmax(rest, m2, EXPERTS_PER_GROUP)
    z = jnp.exp(m2 - m1)
    w_top = 1.0 / (1.0 + z)
    e0 = gidx * EXPERTS_PER_GROUP + i1
    e1 = gidx * EXPERTS_PER_GROUP + i2
    e_ref[0:1, :] = e0
    e_ref[1:2, :] = e1
    w_ref[0:1, :] = gw * w_top
    w_ref[1:2, :] = gw * (z * w_top)

    rows_e = lax.broadcasted_iota(I32, (N_EXPERTS, RT_TN), 0)
    oh0 = rows_e == e0
    oh1 = rows_e == e1
    oh = jnp.where(oh0 | oh1, 1.0, 0.0)
    src = lax.broadcasted_iota(I32, (RT_TN, RT_TN), 0)
    dst = lax.broadcasted_iota(I32, (RT_TN, RT_TN), 1)
    before = jnp.where(src < dst, 1.0, 0.0).astype(BF16)
    carry = carry_scr[...]
    prefix = jnp.dot(oh.astype(BF16), before, preferred_element_type=F32) + carry[:, 0:1]
    r_ref[0:1, :] = jnp.sum(jnp.where(oh0, prefix, 0.0), axis=0, keepdims=True).astype(I32)
    r_ref[1:2, :] = jnp.sum(jnp.where(oh1, prefix, 0.0), axis=0, keepdims=True).astype(I32)
    carry = carry + jnp.sum(oh, axis=1, keepdims=True)
    carry_scr[...] = carry
    cnt_ref[...] = carry.astype(I32)


def _route(logits_t):
    n = logits_t.shape[1]
    slot = pl.BlockSpec((2, RT_TN), lambda i: (0, i))
    return pl.pallas_call(
        _route_kernel,
        grid=(n // RT_TN,),
        in_specs=[pl.BlockSpec((ROUTER_ROWS, RT_TN), lambda i: (0, i))],
        out_specs=[slot, slot, slot, pl.BlockSpec((N_EXPERTS, LANES), lambda i: (0, 0))],
        out_shape=[jax.ShapeDtypeStruct((2, n), I32),
                   jax.ShapeDtypeStruct((2, n), I32),
                   jax.ShapeDtypeStruct((2, n), F32),
                   jax.ShapeDtypeStruct((N_EXPERTS, LANES), I32)],
        scratch_shapes=[pltpu.VMEM((N_EXPERTS, LANES), F32)],
        compiler_params=_params(("arbitrary",)),
        name="route",
    )(logits_t)


DP_TM = 1024
DMA_UNROLL = 8


def _token_copy(src_ref, src_row, dst_ref, dst_row, sem):
    return pltpu.make_async_copy(src_ref.at[src_row], dst_ref.at[dst_row], sem)


def _dispatch_kernel(pos0_ref, pos1_ref, pad_start_ref, pad_n_ref, nt_ref, h_ref, xs_ref, zero_scr, sem, zsem, tsem):
    i = pl.program_id(0)
    n_tiles_max = xs_ref.shape[0] // EXPERT_TILE

    def tail_copy(t):
        return pltpu.make_async_copy(zero_scr, xs_ref.at[pl.ds(t * EXPERT_TILE, EXPERT_TILE)], tsem)

    @pl.when(i == 0)
    def _():
        zero_scr[...] = jnp.zeros_like(zero_scr)
        for e in range(N_EXPERTS):
            start = pad_start_ref[e]

            def issue(r, c, start=start):
                _token_copy(zero_scr, 0, xs_ref, start + r, zsem).start()
                return c
            lax.fori_loop(0, pad_n_ref[e], issue, 0)

        def issue_tail(t, c):
            tail_copy(t).start()
            return c
        lax.fori_loop(nt_ref[0], n_tiles_max, issue_tail, 0)

        for e in range(N_EXPERTS):
            def drain(r, c):
                _token_copy(zero_scr, 0, xs_ref, 0, zsem).wait()
                return c
            lax.fori_loop(0, pad_n_ref[e], drain, 0)

        def drain_tail(t, c):
            tail_copy(t).wait()
            return c
        lax.fori_loop(nt_ref[0], n_tiles_max, drain_tail, 0)

    base = i * DP_TM

    def issue(r, c):
        _token_copy(h_ref, r, xs_ref, pos0_ref[base + r], sem).start(priority=0)
        _token_copy(h_ref, r, xs_ref, pos1_ref[base + r], sem).start(priority=1)
        return c
    lax.fori_loop(0, DP_TM, issue, 0, unroll=DMA_UNROLL)

    for _ in range(2):
        pltpu.make_async_copy(h_ref, xs_ref.at[pl.ds(0, DP_TM)], sem).wait()


def _dispatch(h2, pos0, pos1, pad_start, pad_n, n_tiles, rows_out):
    n = h2.shape[0]
    grid_spec = pltpu.PrefetchScalarGridSpec(
        num_scalar_prefetch=5,
        grid=(n // DP_TM,),
        in_specs=[pl.BlockSpec((DP_TM, TOKEN_ROWS, LANES), lambda i, *_: (i, 0, 0))],
        out_specs=pl.BlockSpec(memory_space=pl.ANY),
        scratch_shapes=[pltpu.VMEM((EXPERT_TILE, TOKEN_ROWS, LANES), BF16),
                        pltpu.SemaphoreType.DMA(()), pltpu.SemaphoreType.DMA(()),
                        pltpu.SemaphoreType.DMA(())],
    )
    return pl.pallas_call(
        _dispatch_kernel,
        grid_spec=grid_spec,
        out_shape=jax.ShapeDtypeStruct((rows_out, TOKEN_ROWS, LANES), BF16),
        compiler_params=pltpu.CompilerParams(dimension_semantics=("arbitrary",),
                                             vmem_limit_bytes=VMEM_LIMIT, has_side_effects=True),
        name="dispatch",
    )(pos0, pos1, pad_start, pad_n, n_tiles, h2)


EXPERT_CAST_ROWS = 256


def _cast_rows(src_ref, dst_ref):
    rows_total = dst_ref.shape[0]

    def body(r, c):
        rows = pl.ds(pl.multiple_of(r * EXPERT_CAST_ROWS, EXPERT_CAST_ROWS), EXPERT_CAST_ROWS)
        dst_ref[rows, :] = src_ref[rows, :].astype(BF16)
        return c
    lax.fori_loop(0, rows_total // EXPERT_CAST_ROWS, body, 0)


def _experts_kernel(te_ref, tr_ref, nt_ref, first_ref, slot_ref, next_ref, xs_ref, w1_hbm, w3_hbm, w2_hbm, ys_ref,
                    w1_stage, w3_stage, w2_stage, w1_scr, w3_scr, w2_scr, sems):
    i = pl.program_id(0)

    def fetch(expert, slot):
        return (pltpu.make_async_copy(w1_hbm.at[expert], w1_stage.at[slot], sems.at[slot]),
                pltpu.make_async_copy(w3_hbm.at[expert], w3_stage.at[slot], sems.at[slot]),
                pltpu.make_async_copy(w2_hbm.at[expert], w2_stage.at[slot], sems.at[slot]))

    @pl.when(i == 0)
    def _():
        for copy in fetch(te_ref[0], 0):
            copy.start()

    @pl.when(first_ref[i] == 1)
    def _():
        slot = slot_ref[i]
        for copy in fetch(te_ref[i], slot):
            copy.wait()

        @pl.when(next_ref[i] >= 0)
        def _():
            for copy in fetch(next_ref[i], 1 - slot):
                copy.start()

        _cast_rows(w1_stage.at[slot], w1_scr)
        _cast_rows(w3_stage.at[slot], w3_scr)
        _cast_rows(w2_stage.at[slot], w2_scr)

    @pl.when(i < nt_ref[0])
    def _():
        x = xs_ref[...].reshape(EXPERT_TILE, D_MODEL)
        a = jnp.dot(x, w1_scr[...], preferred_element_type=F32)
        b = jnp.dot(x, w3_scr[...], preferred_element_type=F32)
        act = (a * jax.nn.sigmoid(a)) * b
        y = jnp.dot(act.astype(BF16), w2_scr[...], preferred_element_type=F32)
        ys_ref[...] = y.astype(BF16).reshape(ys_ref.shape)

    @pl.when(i >= nt_ref[0])
    def _():
        ys_ref[...] = jnp.zeros_like(ys_ref)


def _experts(xs, tile_expert, tile_row, n_tiles, tile_first, tile_slot, tile_next, w1, w3, w2):
    rows = xs.shape[0]
    _, d, f = w1.shape
    tile = (EXPERT_TILE, TOKEN_ROWS, LANES)
    hbm = pl.BlockSpec(memory_space=pl.ANY)
    grid_spec = pltpu.PrefetchScalarGridSpec(
        num_scalar_prefetch=6,
        grid=(rows // EXPERT_TILE,),
        in_specs=[pl.BlockSpec(tile, lambda i, te, tr, *_: (tr[i], 0, 0)), hbm, hbm, hbm],
        out_specs=pl.BlockSpec(tile, lambda i, *_: (i, 0, 0)),
        scratch_shapes=[pltpu.VMEM((2, d, f), F32), pltpu.VMEM((2, d, f), F32), pltpu.VMEM((2, f, d), F32),
                        pltpu.VMEM((d, f), BF16), pltpu.VMEM((d, f), BF16), pltpu.VMEM((f, d), BF16),
                        pltpu.SemaphoreType.DMA((2,))],
    )
    return pl.pallas_call(
        _experts_kernel,
        grid_spec=grid_spec,
        out_shape=jax.ShapeDtypeStruct(xs.shape, BF16),
        compiler_params=_params(("arbitrary",)),
        name="experts",
    )(tile_expert, tile_row, n_tiles, tile_first, tile_slot, tile_next, xs, w1, w3, w2)


CB_TM = 512


def _combine_kernel(pos0_ref, pos1_ref, mod_ref, fg_ref, x1_ref, w0_ref, w1_ref, ys_ref, o_ref,
                    y0_scr, y1_scr, sems):
    i = pl.program_id(0)
    slot = i % 2

    def gather(step, buf):
        base = step * CB_TM

        def issue(r, c):
            _token_copy(ys_ref, pos0_ref[base + r], y0_scr.at[buf], r, sems.at[buf]).start(priority=0)
            _token_copy(ys_ref, pos1_ref[base + r], y1_scr.at[buf], r, sems.at[buf]).start(priority=1)
            return c
        lax.fori_loop(0, CB_TM, issue, 0, unroll=DMA_UNROLL)

    @pl.when(i == 0)
    def _():
        gather(i, slot)

    @pl.when(i + 1 < pl.num_programs(0))
    def _():
        gather(i + 1, 1 - slot)

    pltpu.make_async_copy(ys_ref.at[pl.ds(0, CB_TM)], y0_scr.at[slot], sems.at[slot]).wait()
    pltpu.make_async_copy(ys_ref.at[pl.ds(0, CB_TM)], y1_scr.at[slot], sems.at[slot]).wait()

    gate2 = mod_ref[0, 5:6, :]
    y0 = y0_scr[slot].reshape(CB_TM, D_MODEL).astype(F32)
    y1 = y1_scr[slot].reshape(CB_TM, D_MODEL).astype(F32)
    y = w0_ref[...] * y0 + w1_ref[...] * y1
    x2 = x1_ref[...] + gate2 * y
    o_ref[...] = (x2 * lax.rsqrt(jnp.mean(x2 * x2, axis=-1, keepdims=True) + EPS)) * fg_ref[...]


def _combine(x1, mod3, final_g, ys, pos0, pos1, cw0, cw1, seq):
    n, d = x1.shape
    tps = seq // CB_TM
    grid_spec = pltpu.PrefetchScalarGridSpec(
        num_scalar_prefetch=2,
        grid=(n // CB_TM,),
        in_specs=[pl.BlockSpec((1, N_MOD, d), lambda i, *_: (i // tps, 0, 0)),
                  pl.BlockSpec((1, d), lambda i, *_: (0, 0)),
                  pl.BlockSpec((CB_TM, d), lambda i, *_: (i, 0)),
                  pl.BlockSpec((CB_TM, 1), lambda i, *_: (i, 0)),
                  pl.BlockSpec((CB_TM, 1), lambda i, *_: (i, 0)),
                  pl.BlockSpec(memory_space=pl.ANY)],
        out_specs=pl.BlockSpec((CB_TM, d), lambda i, *_: (i, 0)),
        scratch_shapes=[pltpu.VMEM((2, CB_TM, TOKEN_ROWS, LANES), BF16),
                        pltpu.VMEM((2, CB_TM, TOKEN_ROWS, LANES), BF16),
                        pltpu.SemaphoreType.DMA((2,))],
    )
    return pl.pallas_call(
        _combine_kernel,
        grid_spec=grid_spec,
        out_shape=jax.ShapeDtypeStruct((n, d), F32),
        compiler_params=_params(("arbitrary",)),
        name="combine",
    )(pos0, pos1, mod3, final_g.reshape(1, d), x1, cw0, cw1, ys)


def _layer(x2d, c, seq, ada_w, ada_b, norm1_g, w_in, gmlp_ln_g, gmlp_ln_b, gmlp_w_s, gmlp_b_s, rel_bias,
           w_branch_a, w_branch_b, w_out, norm2_g, w_group, w_expert, w1, w3, w2, final_g):
    n, d = x2d.shape
    nb = c.shape[0]
    mod3 = _ada(c, ada_w, ada_b).reshape(nb, N_MOD, d)

    proj = _in_proj(x2d, mod3, norm1_g, w_in.astype(BF16), seq)
    ya = _gmlp(proj, gmlp_ln_g, gmlp_ln_b, gmlp_w_s, gmlp_b_s)
    yb = _attn(proj, rel_bias, seq)

    wr_t = jnp.concatenate([w_group.T, jnp.zeros((SUBLANES - N_GROUPS, d), F32),
                            w_expert.transpose(0, 2, 1).reshape(N_EXPERTS, d)], axis=0).astype(BF16)
    x1, h2, logits_t = _merge(x2d, mod3, norm2_g, ya, yb, proj, w_branch_a.astype(BF16),
                              w_branch_b.astype(BF16), w_out.astype(BF16), wr_t, seq)

    eidx, rank, cw, counts = _route(logits_t)
    counts = counts[:, 0]
    padded = ((counts + EXPERT_TILE - 1) // EXPERT_TILE) * EXPERT_TILE
    ends = jnp.cumsum(padded)
    offs = ends - padded
    experts = jnp.arange(N_EXPERTS, dtype=I32)
    pos = jnp.sum(jnp.where(eidx[:, :, None] == experts, offs, 0), axis=-1) + rank
    rows_out = 2 * n + N_EXPERTS * EXPERT_TILE
    n_tiles_max = rows_out // EXPERT_TILE
    n_tiles = (ends[-1] // EXPERT_TILE).astype(I32)
    tile_row = jnp.minimum(jnp.arange(n_tiles_max, dtype=I32), n_tiles - 1)
    tile_expert = jnp.minimum(jnp.sum(ends[None, :] <= (tile_row * EXPERT_TILE)[:, None], axis=-1),
                              N_EXPERTS - 1).astype(I32)

    tiles = jnp.arange(n_tiles_max, dtype=I32)
    prev_expert = jnp.concatenate([jnp.full((1,), -1, I32), tile_expert[:-1]])
    tile_first = ((tiles < n_tiles) & (tile_expert != prev_expert)).astype(I32)
    tile_slot = ((jnp.cumsum(tile_first) - 1) % 2).astype(I32)
    later_nonempty = (experts[None, :] > experts[:, None]) & (padded > 0)[None, :]
    next_nonempty = jnp.min(jnp.where(later_nonempty, experts[None, :], N_EXPERTS), axis=1)
    next_nonempty = jnp.where(next_nonempty == N_EXPERTS, -1, next_nonempty)
    tile_next = jnp.sum(jnp.where(tile_expert[:, None] == experts, next_nonempty, 0), axis=-1).astype(I32)

    n_tiles = n_tiles.reshape(1)
    xs = _dispatch(h2, pos[0], pos[1], (offs + counts).astype(I32), (padded - counts).astype(I32), n_tiles,
                   rows_out)
    ys = _experts(xs, tile_expert, tile_row, n_tiles, tile_first, tile_slot, tile_next,
                  w1.reshape(N_EXPERTS, d, D_EXPERT), w3.reshape(N_EXPERTS, d, D_EXPERT),
                  w2.reshape(N_EXPERTS, D_EXPERT, d))
    return _combine(x1, mod3, final_g, ys, pos[0], pos[1], cw[0].reshape(n, 1), cw[1].reshape(n, 1), seq)


def kernel(x, c, ada_w, ada_b, norm1_g, w_in, gmlp_ln_g, gmlp_ln_b, gmlp_w_s, gmlp_b_s, rel_bias, w_branch_a,
           w_branch_b, w_out, norm2_g, w_group, w_expert, w1, w3, w2, final_g):
    b, s, d = x.shape
    out = _layer(x.reshape(b * s, d), c, s, ada_w[0], ada_b[0], norm1_g[0], w_in[0], gmlp_ln_g[0], gmlp_ln_b[0],
                 gmlp_w_s[0], gmlp_b_s[0], rel_bias[0], w_branch_a[0], w_branch_b[0], w_out[0], norm2_g[0],
                 w_group[0], w_expert[0], w1[0], w3[0], w2[0], final_g)
    return out.reshape(b, s, d)
```

```python
import numpy as np
import jax
import jax.numpy as jnp
from jax import lax
from jax.experimental import pallas as pl
from jax.experimental.pallas import tpu as pltpu

F32 = jnp.float32
BF16 = jnp.bfloat16
I32 = jnp.int32

D_MODEL = 2048
CHUNK = 64
EPS = 1e-6
NEG_INF = -1e30
LOG2E = float(np.log2(np.e))
GMLP_BLOCK = 128
GMLP_GROUPS = 8
GMLP_WIDTH = 1024
ATT_HEADS = 16
ATT_HEAD_DIM = 64
ATT_WIDTH = ATT_HEADS * ATT_HEAD_DIM
LEFT_CHUNKS = 8
MAX_REL = 256
N_GROUPS = 4
EXPERTS_PER_GROUP = 8
N_EXPERTS = N_GROUPS * EXPERTS_PER_GROUP
D_EXPERT = 512
N_MOD = 6

LANES = 128
SUBLANES = 8
VMEM_LIMIT = 56 * 1024 * 1024

COL_TILE = 1024
PROJ_COLS = 2 * D_MODEL + 2 * GMLP_WIDTH + 3 * ATT_WIDTH
GATE_TILES = 2 * D_MODEL // COL_TILE
U_TILE = GATE_TILES
V_TILE = GATE_TILES + 1
Q_TILE = GATE_TILES + 2
K_TILE = GATE_TILES + 3
VB_TILE = GATE_TILES + 4

ROUTER_ROWS = SUBLANES + N_EXPERTS

ATT_QBLK = 512
ATT_SUB = 2 * CHUNK
ATT_NSUB = ATT_QBLK // ATT_SUB
ATT_PAIRS = 2
ATT_WIN = ATT_SUB + LEFT_CHUNKS * CHUNK

EXPERT_TILE = 256
TOKEN_ROWS = D_MODEL // LANES


def _params(sem, vmem=VMEM_LIMIT):
    return pltpu.CompilerParams(dimension_semantics=sem, vmem_limit_bytes=vmem)


def _rms_mod(x, g, scale, shift):
    y = x * lax.rsqrt(jnp.mean(x * x, axis=-1, keepdims=True) + EPS)
    return (y * g) * (1.0 + scale) + shift


def _ada_kernel(cb_ref, w_ref, b_ref, o_ref, s_scr):
    nb, d, _ = cb_ref.shape
    tn = w_ref.shape[1]
    reps = tn // LANES

    @pl.when(pl.program_id(0) == 0)
    def _():
        cb = cb_ref[...]
        s_scr[...] = cb * jax.nn.sigmoid(cb)

    def body(i, accs):
        r = pl.ds(pl.multiple_of(i * SUBLANES, SUBLANES), SUBLANES)
        w8 = w_ref[r, :]
        return tuple(accs[b] + w8 * jnp.concatenate([s_scr[b, r, :]] * reps, axis=1) for b in range(nb))

    init = tuple(jnp.zeros((SUBLANES, tn), F32) for _ in range(nb))
    accs = lax.fori_loop(0, d // SUBLANES, body, init, unroll=4)
    for b in range(nb):
        o_ref[b:b + 1, :] = jnp.sum(accs[b], axis=0, keepdims=True) + b_ref[...]


def _ada(c, ada_w, ada_b):
    nb, d = c.shape
    n = ada_w.shape[1]
    tn = 1024
    cb = jnp.broadcast_to(c[:, :, None], (nb, d, LANES))
    return pl.pallas_call(
        _ada_kernel,
        grid=(n // tn,),
        in_specs=[pl.BlockSpec((nb, d, LANES), lambda j: (0, 0, 0)),
                  pl.BlockSpec((d, tn), lambda j: (0, j)),
                  pl.BlockSpec((1, tn), lambda j: (0, j))],
        out_specs=pl.BlockSpec((nb, tn), lambda j: (0, j)),
        out_shape=jax.ShapeDtypeStruct((nb, n), F32),
        scratch_shapes=[pltpu.VMEM((nb, d, LANES), F32)],
        compiler_params=_params(("arbitrary",)),
        name="ada",
    )(cb, ada_w, ada_b.reshape(1, n))


IN_TM = 1024
IN_RB = 1024
IN_STAT_RB = 128
IN_NORM_RB = 16


def _gelu(a):
    return 0.5 * a * (1.0 + lax.erf(a * np.float32(np.sqrt(0.5))))


def _in_proj_kernel(mod_ref, g_ref, x_ref, w_ref, o_ref, h_scr, gain_scr, shift_scr, inv_scr):
    j = pl.program_id(1)
    nrb = IN_TM // IN_RB

    @pl.when(j == 0)
    def _():
        d = x_ref.shape[1]
        gain_scr[...] = jnp.broadcast_to(g_ref[...] * (1.0 + mod_ref[0, 1:2, :]), gain_scr.shape)
        shift_scr[...] = jnp.broadcast_to(mod_ref[0, 0:1, :], shift_scr.shape)

        def stats(rb, c):
            rows = pl.ds(pl.multiple_of(rb * IN_STAT_RB, IN_STAT_RB), IN_STAT_RB)
            sq = jnp.zeros((IN_STAT_RB, LANES), F32)
            for k in range(d // LANES):
                xk = x_ref[rows, k * LANES:(k + 1) * LANES]
                sq = sq + xk * xk
            inv = lax.rsqrt(jnp.sum(sq, axis=-1, keepdims=True) * (1.0 / d) + EPS)
            inv_scr[rows, :] = jnp.broadcast_to(inv, (IN_STAT_RB, LANES))
            return c
        lax.fori_loop(0, IN_TM // IN_STAT_RB, stats, 0)

        def apply(rb, c):
            rows = pl.ds(pl.multiple_of(rb * IN_NORM_RB, IN_NORM_RB), IN_NORM_RB)
            inv = inv_scr[rows, :]
            for k in range(d // LANES):
                cols = slice(k * LANES, (k + 1) * LANES)
                y = (x_ref[rows, cols] * inv) * gain_scr[:, cols] + shift_scr[:, cols]
                h_scr[rows, cols] = y.astype(BF16)
            return c
        lax.fori_loop(0, IN_TM // IN_NORM_RB, apply, 0, unroll=2)

    def run(epilogue):
        def body(rb, c):
            rows = pl.ds(pl.multiple_of(rb * IN_RB, IN_RB), IN_RB)
            acc = jnp.dot(h_scr[rows, :], w_ref[...], preferred_element_type=F32)
            o_ref[rows, :] = epilogue(acc).astype(o_ref.dtype)
            return c
        lax.fori_loop(0, nrb, body, 0)

    @pl.when(j < GATE_TILES)
    def _():
        run(jax.nn.sigmoid)

    @pl.when((j == U_TILE) | (j == V_TILE))
    def _():
        run(_gelu)

    @pl.when(j == Q_TILE)
    def _():
        run(lambda a: a * np.float32(ATT_HEAD_DIM ** -0.5 * LOG2E))

    @pl.when(j > Q_TILE)
    def _():
        run(lambda a: a)


def _in_proj(x2d, mod3, norm_g, w_in_bf16, seq):
    n, d = x2d.shape
    cols = w_in_bf16.shape[1]
    n_tiles = cols // COL_TILE
    tiles_per_seq = seq // IN_TM
    return pl.pallas_call(
        _in_proj_kernel,
        grid=(n // IN_TM, n_tiles),
        in_specs=[pl.BlockSpec((1, N_MOD, d), lambda i, j: (i // tiles_per_seq, 0, 0)),
                  pl.BlockSpec((1, d), lambda i, j: (0, 0)),
                  pl.BlockSpec((IN_TM, d), lambda i, j: (i, 0)),
                  pl.BlockSpec((d, COL_TILE), lambda i, j: (0, (j + n_tiles - GATE_TILES) % n_tiles))],
        out_specs=pl.BlockSpec((IN_TM, COL_TILE), lambda i, j: (i, j)),
        out_shape=jax.ShapeDtypeStruct((n, cols), BF16),
        scratch_shapes=[pltpu.VMEM((IN_TM, d), BF16), pltpu.VMEM((IN_NORM_RB, d), F32),
                        pltpu.VMEM((IN_NORM_RB, d), F32), pltpu.VMEM((IN_TM, LANES), F32)],
        compiler_params=_params(("arbitrary", "arbitrary")),
        name="in_proj",
    )(mod3, norm_g.reshape(1, d), x2d, w_in_bf16)


GM_TM = 256


def _gmlp_kernel(u_ref, v_ref, lng_ref, lnb_ref, ws_ref, bs_ref, o_ref):
    t = lax.broadcasted_iota(I32, (GMLP_BLOCK, GMLP_BLOCK), 0)
    s = lax.broadcasted_iota(I32, (GMLP_BLOCK, GMLP_BLOCK), 1)
    causal = (s // CHUNK) <= (t // CHUNK)
    lng = lng_ref[...]
    lnb = lnb_ref[...]
    for blk in range(GM_TM // GMLP_BLOCK):
        rows = slice(blk * GMLP_BLOCK, (blk + 1) * GMLP_BLOCK)
        v = v_ref[rows, :].astype(F32)
        mu = jnp.mean(v, axis=-1, keepdims=True)
        vc = v - mu
        var = jnp.mean(vc * vc, axis=-1, keepdims=True)
        vln = ((vc * lax.rsqrt(var + EPS)) * lng + lnb).astype(BF16)
        for g in range(GMLP_GROUPS):
            cols = slice(g * LANES, (g + 1) * LANES)
            w = jnp.where(causal, ws_ref[g], 0.0).astype(BF16)
            mixed = jnp.dot(w, vln[:, cols], preferred_element_type=F32) + bs_ref[g]
            o_ref[rows, cols] = (u_ref[rows, cols].astype(F32) * mixed).astype(o_ref.dtype)


def _gmlp(proj, ln_g, ln_b, w_s, b_s):
    n = proj.shape[0]
    return pl.pallas_call(
        _gmlp_kernel,
        grid=(n // GM_TM,),
        in_specs=[pl.BlockSpec((GM_TM, GMLP_WIDTH), lambda i: (i, U_TILE)),
                  pl.BlockSpec((GM_TM, GMLP_WIDTH), lambda i: (i, V_TILE)),
                  pl.BlockSpec((1, GMLP_WIDTH), lambda i: (0, 0)),
                  pl.BlockSpec((1, GMLP_WIDTH), lambda i: (0, 0)),
                  pl.BlockSpec((GMLP_GROUPS, GMLP_BLOCK, GMLP_BLOCK), lambda i: (0, 0, 0)),
                  pl.BlockSpec((GMLP_GROUPS, GMLP_BLOCK, 1), lambda i: (0, 0, 0))],
        out_specs=pl.BlockSpec((GM_TM, GMLP_WIDTH), lambda i: (i, 0)),
        out_shape=jax.ShapeDtypeStruct((n, GMLP_WIDTH), BF16),
        compiler_params=_params(("arbitrary",)),
        name="gmlp",
    )(proj, proj, ln_g.reshape(1, -1), ln_b.reshape(1, -1), w_s,
      b_s.reshape(GMLP_GROUPS, GMLP_BLOCK, 1))


def _band_bias(rel_table):
    heads = rel_table.shape[0]
    r = np.arange(ATT_SUB)[:, None]
    w = np.arange(ATT_WIN)[None, :]
    j = w // CHUNK - r // CHUNK
    in_band = (j >= 0) & (j <= LEFT_CHUNKS)
    a = np.arange(ATT_NSUB)[:, None, None]
    in_seq = np.broadcast_to(w[None] >= ATT_QBLK - a * ATT_SUB, (ATT_NSUB, ATT_SUB, ATT_WIN))
    visible = np.stack([np.broadcast_to(in_band, in_seq.shape), in_band[None] & in_seq])[:, None]
    far = LEFT_CHUNKS * CHUNK + ATT_SUB - 1
    n_clipped = far - MAX_REL + 1
    table = rel_table.astype(F32) * np.float32(LOG2E)
    lo = MAX_REL - (ATT_WIN - 1 - LEFT_CHUNKS * CHUNK)
    diag = jnp.concatenate([jnp.broadcast_to(table[:, 2 * MAX_REL:], (heads, n_clipped)),
                            jnp.flip(table[:, lo:2 * MAX_REL], axis=1),
                            jnp.zeros((heads, 1), F32)], axis=1)
    span = diag.shape[1] - 1
    shifted = jnp.tile(diag, (1, ATT_SUB))[:, :ATT_SUB * span].reshape(heads, ATT_SUB, span)
    bias = shifted[:, :, ATT_SUB - 1:ATT_SUB - 1 + ATT_WIN]
    return jnp.where(jnp.asarray(visible), bias[None, :, None], NEG_INF)


def _attn_kernel(q_ref, kp_ref, kc_ref, vp_ref, vc_ref, bias_ref, o_ref, k_scr, v_scr, s_scr):
    k_scr[0:ATT_QBLK, :] = kp_ref[...]
    k_scr[ATT_QBLK:, :] = kc_ref[...]
    v_scr[0:ATT_QBLK, :] = vp_ref[...]
    v_scr[ATT_QBLK:, :] = vc_ref[...]
    lane = lax.broadcasted_iota(I32, (ATT_SUB, LANES), 1)
    low = lane < ATT_HEAD_DIM

    units = [(pair, a) for pair in range(ATT_PAIRS) for a in range(ATT_NSUB)]
    for u, (pair, a) in enumerate(units):
        lanes = slice(pair * LANES, (pair + 1) * LANES)
        q = q_ref[a * ATT_SUB:(a + 1) * ATT_SUB, lanes]
        zero = jnp.zeros_like(q)
        q2 = jnp.concatenate([jnp.where(low, q, zero), jnp.where(low, zero, q)], axis=0)
        win = slice(a * ATT_SUB, a * ATT_SUB + ATT_WIN)
        s_scr[u] = lax.dot_general(q2, k_scr[win, lanes], (((1,), (1,)), ((), ())), preferred_element_type=F32)

    for u, (pair, a) in enumerate(units):
        lanes = slice(pair * LANES, (pair + 1) * LANES)
        rows = slice(a * ATT_SUB, (a + 1) * ATT_SUB)
        win = slice(a * ATT_SUB, a * ATT_SUB + ATT_WIN)
        s = s_scr[u] + bias_ref[0, 2 * pair:2 * pair + 2, a].reshape(2 * ATT_SUB, ATT_WIN)
        m = jnp.max(s, axis=-1, keepdims=True)
        p = jnp.exp2(s - m)
        l = jnp.sum(p, axis=-1, keepdims=True)
        pv = jnp.dot(p.astype(BF16), v_scr[win, lanes], preferred_element_type=F32) / l
        o_ref[rows, lanes] = jnp.where(low, pv[:ATT_SUB], pv[ATT_SUB:]).astype(o_ref.dtype)


def _attn(proj, rel_table, seq):
    n = proj.shape[0]
    bps = seq // ATT_QBLK
    pairs = ATT_WIDTH // LANES
    width = ATT_PAIRS * LANES
    qc, kc, vc = (Q_TILE * COL_TILE // width, K_TILE * COL_TILE // width, VB_TILE * COL_TILE // width)
    bias = _band_bias(rel_table)

    def prev(i):
        return jnp.where(i % bps == 0, i, i - 1)

    def first(i):
        return jnp.where(i % bps == 0, 1, 0)

    blk = (ATT_QBLK, width)
    return pl.pallas_call(
        _attn_kernel,
        grid=(pairs // ATT_PAIRS, n // ATT_QBLK),
        in_specs=[pl.BlockSpec(blk, lambda h, i: (i, qc + h)),
                  pl.BlockSpec(blk, lambda h, i: (prev(i), kc + h)),
                  pl.BlockSpec(blk, lambda h, i: (i, kc + h)),
                  pl.BlockSpec(blk, lambda h, i: (prev(i), vc + h)),
                  pl.BlockSpec(blk, lambda h, i: (i, vc + h)),
                  pl.BlockSpec((1, 2 * ATT_PAIRS, ATT_NSUB, ATT_SUB, ATT_WIN),
                               lambda h, i: (first(i), h, 0, 0, 0))],
        out_specs=pl.BlockSpec(blk, lambda h, i: (i, h)),
        out_shape=jax.ShapeDtypeStruct((n, ATT_WIDTH), BF16),
        scratch_shapes=[pltpu.VMEM((2 * ATT_QBLK, width), BF16),
                        pltpu.VMEM((2 * ATT_QBLK, width), BF16),
                        pltpu.VMEM((ATT_PAIRS * ATT_NSUB, 2 * ATT_SUB, ATT_WIN), F32)],
        compiler_params=_params(("arbitrary", "arbitrary")),
        name="attn",
    )(proj, proj, proj, proj, proj, bias)


MG_TM = 256
MG_RB = 128


def _merge_kernel(mod_ref, g_ref, x_ref, ya_ref, yb_ref, ga_ref, gb_ref, wa_ref, wb_ref, wo_ref, wr_ref,
                  x1_ref, h2_ref, lt_ref, ya_scr, yb_scr, mix_scr):
    gate1 = mod_ref[0, 2:3, :]
    shift2 = mod_ref[0, 3:4, :]
    scale2 = mod_ref[0, 4:5, :]
    groups = [slice(part * MG_RB, (part + 1) * MG_RB) for part in range(MG_TM // MG_RB)]
    for rows in groups:
        ya_scr[rows, :] = jnp.dot(ya_ref[rows, :], wa_ref[...], preferred_element_type=F32)
        yb_scr[rows, :] = jnp.dot(yb_ref[rows, :], wb_ref[...], preferred_element_type=F32)
    for rows in groups:
        m = ga_ref[rows, :].astype(F32) * ya_scr[rows, :] + gb_ref[rows, :].astype(F32) * yb_scr[rows, :]
        mix_scr[rows, :] = jnp.dot(m.astype(BF16), wo_ref[...], preferred_element_type=F32)
    for rows in groups:
        x1 = x_ref[rows, :] + gate1 * mix_scr[rows, :]
        x1_ref[rows, :] = x1
        h2 = _rms_mod(x1, g_ref[...], scale2, shift2).astype(BF16)
        h2_ref[rows] = h2.reshape(MG_RB, TOKEN_ROWS, LANES)
        lt_ref[:, rows] = lax.dot_general(wr_ref[...], h2, (((1,), (1,)), ((), ())),
                                          preferred_element_type=F32)


def _merge(x2d, mod3, norm_g, ya, yb, proj, wa, wb, wo, wr_t, seq):
    n, d = x2d.shape
    tps = seq // MG_TM
    const = lambda shape: pl.BlockSpec(shape, lambda i: (0,) * len(shape), pipeline_mode=pl.Buffered(1))
    return pl.pallas_call(
        _merge_kernel,
        grid=(n // MG_TM,),
        in_specs=[pl.BlockSpec((1, N_MOD, d), lambda i: (i // tps, 0, 0)),
                  pl.BlockSpec((1, d), lambda i: (0, 0)),
                  pl.BlockSpec((MG_TM, d), lambda i: (i, 0)),
                  pl.BlockSpec((MG_TM, GMLP_WIDTH), lambda i: (i, 0)),
                  pl.BlockSpec((MG_TM, ATT_WIDTH), lambda i: (i, 0)),
                  pl.BlockSpec((MG_TM, d), lambda i: (i, 0)),
                  pl.BlockSpec((MG_TM, d), lambda i: (i, 1)),
                  const((GMLP_WIDTH, d)), const((ATT_WIDTH, d)), const((d, d)),
                  const((ROUTER_ROWS, d))],
        out_specs=[pl.BlockSpec((MG_TM, d), lambda i: (i, 0)),
                   pl.BlockSpec((MG_TM, TOKEN_ROWS, LANES), lambda i: (i, 0, 0)),
                   pl.BlockSpec((ROUTER_ROWS, MG_TM), lambda i: (0, i))],
        out_shape=[jax.ShapeDtypeStruct((n, d), F32),
                   jax.ShapeDtypeStruct((n, TOKEN_ROWS, LANES), BF16),
                   jax.ShapeDtypeStruct((ROUTER_ROWS, n), F32)],
        scratch_shapes=[pltpu.VMEM((MG_TM, d), F32), pltpu.VMEM((MG_TM, d), F32), pltpu.VMEM((MG_TM, d), F32)],
        compiler_params=_params(("arbitrary",)),
        name="merge",
    )(mod3, norm_g.reshape(1, d), x2d, ya, yb, proj, proj, wa, wb, wo, wr_t)


RT_TN = 512


def _first_argmax(vals, vmax, nrows):
    rows = lax.broadcasted_iota(I32, vals.shape, 0)
    return jnp.min(jnp.where(vals == vmax, rows, nrows), axis=0, keepdims=True)


def _route_kernel(lt_ref, e_ref, r_ref, w_ref, cnt_ref, carry_scr):
    @pl.when(pl.program_id(0) == 0)
    def _():
        carry_scr[...] = jnp.zeros_like(carry_scr)

    gl = lt_ref[0:N_GROUPS, :]
    gmax = jnp.max(gl, axis=0, keepdims=True)
    gidx = _first_argmax(gl, gmax, N_GROUPS)
    gw = 1.0 / jnp.sum(jnp.exp(gl - gmax), axis=0, keepdims=True)

    esel = lt_ref[SUBLANES:SUBLANES + EXPERTS_PER_GROUP, :]
    for g in range(1, N_GROUPS):
        lo = SUBLANES + g * EXPERTS_PER_GROUP
        esel = jnp.where(gidx == g, lt_ref[lo:lo + EXPERTS_PER_GROUP, :], esel)
    rows8 = lax.broadcasted_iota(I32, esel.shape, 0)
    m1 = jnp.max(esel, axis=0, keepdims=True)
    i1 = _first_argmax(esel, m1, EXPERTS_PER_GROUP)
    rest = jnp.where(rows8 == i1, -jnp.inf, esel)
    m2 = jnp.max(rest, axis=0, keepdims=True)
    i2 = _first_argmax(rest, m2, EXPERTS_PER_GROUP)
    z = jnp.exp(m2 - m1)
    w_top = 1.0 / (1.0 + z)
    e0 = gidx * EXPERTS_PER_GROUP + i1
    e1 = gidx * EXPERTS_PER_GROUP + i2
    e_ref[0:1, :] = e0
    e_ref[1:2, :] = e1
    w_ref[0:1, :] = gw * w_top
    w_ref[1:2, :] = gw * (z * w_top)

    rows_e = lax.broadcasted_iota(I32, (N_EXPERTS, RT_TN), 0)
    oh0 = rows_e == e0
    oh1 = rows_e == e1
    oh = jnp.where(oh0 | oh1, 1.0, 0.0)
    src = lax.broadcasted_iota(I32, (RT_TN, RT_TN), 0)
    dst = lax.broadcasted_iota(I32, (RT_TN, RT_TN), 1)
    before = jnp.where(src < dst, 1.0, 0.0).astype(BF16)
    carry = carry_scr[...]
    prefix = jnp.dot(oh.astype(BF16), before, preferred_element_type=F32) + carry[:, 0:1]
    r_ref[0:1, :] = jnp.sum(jnp.where(oh0, prefix, 0.0), axis=0, keepdims=True).astype(I32)
    r_ref[1:2, :] = jnp.sum(jnp.where(oh1, prefix, 0.0), axis=0, keepdims=True).astype(I32)
    carry = carry + jnp.sum(oh, axis=1, keepdims=True)
    carry_scr[...] = carry
    cnt_ref[...] = carry.astype(I32)


def _route(logits_t):
    n = logits_t.shape[1]
    slot = pl.BlockSpec((2, RT_TN), lambda i: (0, i))
    return pl.pallas_call(
        _route_kernel,
        grid=(n // RT_TN,),
        in_specs=[pl.BlockSpec((ROUTER_ROWS, RT_TN), lambda i: (0, i))],
        out_specs=[slot, slot, slot, pl.BlockSpec((N_EXPERTS, LANES), lambda i: (0, 0))],
        out_shape=[jax.ShapeDtypeStruct((2, n), I32),
                   jax.ShapeDtypeStruct((2, n), I32),
                   jax.ShapeDtypeStruct((2, n), F32),
                   jax.ShapeDtypeStruct((N_EXPERTS, LANES), I32)],
        scratch_shapes=[pltpu.VMEM((N_EXPERTS, LANES), F32)],
        compiler_params=_params(("arbitrary",)),
        name="route",
    )(logits_t)


DP_TM = 1024
DMA_UNROLL = 8


def _token_copy(src_ref, src_row, dst_ref, dst_row, sem):
    return pltpu.make_async_copy(src_ref.at[src_row], dst_ref.at[dst_row], sem)


def _dispatch_kernel(pos0_ref, pos1_ref, pad_start_ref, pad_n_ref, nt_ref, h_ref, xs_ref, zero_scr, sem, zsem, tsem):
    i = pl.program_id(0)
    n_tiles_max = xs_ref.shape[0] // EXPERT_TILE

    def tail_copy(t):
        return pltpu.make_async_copy(zero_scr, xs_ref.at[pl.ds(t * EXPERT_TILE, EXPERT_TILE)], tsem)

    @pl.when(i == 0)
    def _():
        zero_scr[...] = jnp.zeros_like(zero_scr)
        for e in range(N_EXPERTS):
            start = pad_start_ref[e]

            def issue(r, c, start=start):
                _token_copy(zero_scr, 0, xs_ref, start + r, zsem).start()
                return c
            lax.fori_loop(0, pad_n_ref[e], issue, 0)

        def issue_tail(t, c):
            tail_copy(t).start()
            return c
        lax.fori_loop(nt_ref[0], n_tiles_max, issue_tail, 0)

        for e in range(N_EXPERTS):
            def drain(r, c):
                _token_copy(zero_scr, 0, xs_ref, 0, zsem).wait()
                return c
            lax.fori_loop(0, pad_n_ref[e], drain, 0)

        def drain_tail(t, c):
            tail_copy(t).wait()
            return c
        lax.fori_loop(nt_ref[0], n_tiles_max, drain_tail, 0)

    base = i * DP_TM

    def issue(r, c):
        _token_copy(h_ref, r, xs_ref, pos0_ref[base + r], sem).start(priority=0)
        _token_copy(h_ref, r, xs_ref, pos1_ref[base + r], sem).start(priority=1)
        return c
    lax.fori_loop(0, DP_TM, issue, 0, unroll=DMA_UNROLL)

    for _ in range(2):
        pltpu.make_async_copy(h_ref, xs_ref.at[pl.ds(0, DP_TM)], sem).wait()


def _dispatch(h2, pos0, pos1, pad_start, pad_n, n_tiles, rows_out):
    n = h2.shape[0]
    grid_spec = pltpu.PrefetchScalarGridSpec(
        num_scalar_prefetch=5,
        grid=(n // DP_TM,),
        in_specs=[pl.BlockSpec((DP_TM, TOKEN_ROWS, LANES), lambda i, *_: (i, 0, 0))],
        out_specs=pl.BlockSpec(memory_space=pl.ANY),
        scratch_shapes=[pltpu.VMEM((EXPERT_TILE, TOKEN_ROWS, LANES), BF16),
                        pltpu.SemaphoreType.DMA(()), pltpu.SemaphoreType.DMA(()),
                        pltpu.SemaphoreType.DMA(())],
    )
    return pl.pallas_call(
        _dispatch_kernel,
        grid_spec=grid_spec,
        out_shape=jax.ShapeDtypeStruct((rows_out, TOKEN_ROWS, LANES), BF16),
        compiler_params=pltpu.CompilerParams(dimension_semantics=("arbitrary",),
                                             vmem_limit_bytes=VMEM_LIMIT, has_side_effects=True),
        name="dispatch",
    )(pos0, pos1, pad_start, pad_n, n_tiles, h2)


EXPERT_CAST_ROWS = 256


def _cast_rows(src_ref, dst_ref):
    rows_total = dst_ref.shape[0]

    def body(r, c):
        rows = pl.ds(pl.multiple_of(r * EXPERT_CAST_ROWS, EXPERT_CAST_ROWS), EXPERT_CAST_ROWS)
        dst_ref[rows, :] = src_ref[rows, :].astype(BF16)
        return c
    lax.fori_loop(0, rows_total // EXPERT_CAST_ROWS, body, 0)


def _experts_kernel(te_ref, tr_ref, nt_ref, first_ref, slot_ref, next_ref, xs_ref, w1_hbm, w3_hbm, w2_hbm, ys_ref,
                    w1_stage, w3_stage, w2_stage, w1_scr, w3_scr, w2_scr, sems):
    i = pl.program_id(0)

    def fetch(expert, slot):
        return (pltpu.make_async_copy(w1_hbm.at[expert], w1_stage.at[slot], sems.at[slot]),
                pltpu.make_async_copy(w3_hbm.at[expert], w3_stage.at[slot], sems.at[slot]),
                pltpu.make_async_copy(w2_hbm.at[expert], w2_stage.at[slot], sems.at[slot]))

    @pl.when(i == 0)
    def _():
        for copy in fetch(te_ref[0], 0):
            copy.start()

    @pl.when(first_ref[i] == 1)
    def _():
        slot = slot_ref[i]
        for copy in fetch(te_ref[i], slot):
            copy.wait()

        @pl.when(next_ref[i] >= 0)
        def _():
            for copy in fetch(next_ref[i], 1 - slot):
                copy.start()

        _cast_rows(w1_stage.at[slot], w1_scr)
        _cast_rows(w3_stage.at[slot], w3_scr)
        _cast_rows(w2_stage.at[slot], w2_scr)

    @pl.when(i < nt_ref[0])
    def _():
        x = xs_ref[...].reshape(EXPERT_TILE, D_MODEL)
        a = jnp.dot(x, w1_scr[...], preferred_element_type=F32)
        b = jnp.dot(x, w3_scr[...], preferred_element_type=F32)
        act = (a * jax.nn.sigmoid(a)) * b
        y = jnp.dot(act.astype(BF16), w2_scr[...], preferred_element_type=F32)
        ys_ref[...] = y.astype(BF16).reshape(ys_ref.shape)

    @pl.when(i >= nt_ref[0])
    def _():
        ys_ref[...] = jnp.zeros_like(ys_ref)


def _experts(xs, tile_expert, tile_row, n_tiles, tile_first, tile_slot, tile_next, w1, w3, w2):
    rows = xs.shape[0]
    _, d, f = w1.shape
    tile = (EXPERT_TILE, TOKEN_ROWS, LANES)
    hbm = pl.BlockSpec(memory_space=pl.ANY)
    grid_spec = pltpu.PrefetchScalarGridSpec(
        num_scalar_prefetch=6,
        grid=(rows // EXPERT_TILE,),
        in_specs=[pl.BlockSpec(tile, lambda i, te, tr, *_: (tr[i], 0, 0)), hbm, hbm, hbm],
        out_specs=pl.BlockSpec(tile, lambda i, *_: (i, 0, 0)),
        scratch_shapes=[pltpu.VMEM((2, d, f), F32), pltpu.VMEM((2, d, f), F32), pltpu.VMEM((2, f, d), F32),
                        pltpu.VMEM((d, f), BF16), pltpu.VMEM((d, f), BF16), pltpu.VMEM((f, d), BF16),
                        pltpu.SemaphoreType.DMA((2,))],
    )
    return pl.pallas_call(
        _experts_kernel,
        grid_spec=grid_spec,
        out_shape=jax.ShapeDtypeStruct(xs.shape, BF16),
        compiler_params=_params(("arbitrary",)),
        name="experts",
    )(tile_expert, tile_row, n_tiles, tile_first, tile_slot, tile_next, xs, w1, w3, w2)


CB_TM = 256


def _combine_kernel(pos0_ref, pos1_ref, mod_ref, fg_ref, x1_ref, w0_ref, w1_ref, ys_ref, o_ref,
                    y0_scr, y1_scr, sems):
    i = pl.program_id(0)
    slot = i % 2

    def gather(step, buf):
        base = step * CB_TM

        def issue(r, c):
            _token_copy(ys_ref, pos0_ref[base + r], y0_scr.at[buf], r, sems.at[buf]).start(priority=0)
            _token_copy(ys_ref, pos1_ref[base + r], y1_scr.at[buf], r, sems.at[buf]).start(priority=1)
            return c
        lax.fori_loop(0, CB_TM, issue, 0, unroll=DMA_UNROLL)

    @pl.when(i == 0)
    def _():
        gather(i, slot)

    @pl.when(i + 1 < pl.num_programs(0))
    def _():
        gather(i + 1, 1 - slot)

    pltpu.make_async_copy(ys_ref.at[pl.ds(0, CB_TM)], y0_scr.at[slot], sems.at[slot]).wait()
    pltpu.make_async_copy(ys_ref.at[pl.ds(0, CB_TM)], y1_scr.at[slot], sems.at[slot]).wait()

    gate2 = mod_ref[0, 5:6, :]
    y0 = y0_scr[slot].reshape(CB_TM, D_MODEL).astype(F32)
    y1 = y1_scr[slot].reshape(CB_TM, D_MODEL).astype(F32)
    y = w0_ref[...] * y0 + w1_ref[...] * y1
    x2 = x1_ref[...] + gate2 * y
    o_ref[...] = (x2 * lax.rsqrt(jnp.mean(x2 * x2, axis=-1, keepdims=True) + EPS)) * fg_ref[...]


def _combine(x1, mod3, final_g, ys, pos0, pos1, cw0, cw1, seq):
    n, d = x1.shape
    tps = seq // CB_TM
    grid_spec = pltpu.PrefetchScalarGridSpec(
        num_scalar_prefetch=2,
        grid=(n // CB_TM,),
        in_specs=[pl.BlockSpec((1, N_MOD, d), lambda i, *_: (i // tps, 0, 0)),
                  pl.BlockSpec((1, d), lambda i, *_: (0, 0)),
                  pl.BlockSpec((CB_TM, d), lambda i, *_: (i, 0)),
                  pl.BlockSpec((CB_TM, 1), lambda i, *_: (i, 0)),
                  pl.BlockSpec((CB_TM, 1), lambda i, *_: (i, 0)),
                  pl.BlockSpec(memory_space=pl.ANY)],
        out_specs=pl.BlockSpec((CB_TM, d), lambda i, *_: (i, 0)),
        scratch_shapes=[pltpu.VMEM((2, CB_TM, TOKEN_ROWS, LANES), BF16),
                        pltpu.VMEM((2, CB_TM, TOKEN_ROWS, LANES), BF16),
                        pltpu.SemaphoreType.DMA((2,))],
    )
    return pl.pallas_call(
        _combine_kernel,
        grid_spec=grid_spec,
        out_shape=jax.ShapeDtypeStruct((n, d), F32),
        compiler_params=_params(("arbitrary",)),
        name="combine",
    )(pos0, pos1, mod3, final_g.reshape(1, d), x1, cw0, cw1, ys)


def _layer(x2d, c, seq, ada_w, ada_b, norm1_g, w_in, gmlp_ln_g, gmlp_ln_b, gmlp_w_s, gmlp_b_s, rel_bias,
           w_branch_a, w_branch_b, w_out, norm2_g, w_group, w_expert, w1, w3, w2, final_g):
    n, d = x2d.shape
    nb = c.shape[0]
    mod3 = _ada(c, ada_w, ada_b).reshape(nb, N_MOD, d)

    proj = _in_proj(x2d, mod3, norm1_g, w_in.astype(BF16), seq)
    ya = _gmlp(proj, gmlp_ln_g, gmlp_ln_b, gmlp_w_s, gmlp_b_s)
    yb = _attn(proj, rel_bias, seq)

    wr_t = jnp.concatenate([w_group.T, jnp.zeros((SUBLANES - N_GROUPS, d), F32),
                            w_expert.transpose(0, 2, 1).reshape(N_EXPERTS, d)], axis=0).astype(BF16)
    x1, h2, logits_t = _merge(x2d, mod3, norm2_g, ya, yb, proj, w_branch_a.astype(BF16),
                              w_branch_b.astype(BF16), w_out.astype(BF16), wr_t, seq)

    eidx, rank, cw, counts = _route(logits_t)
    counts = counts[:, 0]
    padded = ((counts + EXPERT_TILE - 1) // EXPERT_TILE) * EXPERT_TILE
    ends = jnp.cumsum(padded)
    offs = ends - padded
    experts = jnp.arange(N_EXPERTS, dtype=I32)
    pos = jnp.sum(jnp.where(eidx[:, :, None] == experts, offs, 0), axis=-1) + rank
    rows_out = 2 * n + N_EXPERTS * EXPERT_TILE
    n_tiles_max = rows_out // EXPERT_TILE
    n_tiles = (ends[-1] // EXPERT_TILE).astype(I32)
    tile_row = jnp.minimum(jnp.arange(n_tiles_max, dtype=I32), n_tiles - 1)
    tile_expert = jnp.minimum(jnp.sum(ends[None, :] <= (tile_row * EXPERT_TILE)[:, None], axis=-1),
                              N_EXPERTS - 1).astype(I32)

    tiles = jnp.arange(n_tiles_max, dtype=I32)
    prev_expert = jnp.concatenate([jnp.full((1,), -1, I32), tile_expert[:-1]])
    tile_first = ((tiles < n_tiles) & (tile_expert != prev_expert)).astype(I32)
    tile_slot = ((jnp.cumsum(tile_first) - 1) % 2).astype(I32)
    later_nonempty = (experts[None, :] > experts[:, None]) & (padded > 0)[None, :]
    next_nonempty = jnp.min(jnp.where(later_nonempty, experts[None, :], N_EXPERTS), axis=1)
    next_nonempty = jnp.where(next_nonempty == N_EXPERTS, -1, next_nonempty)
    tile_next = jnp.sum(jnp.where(tile_expert[:, None] == experts, next_nonempty, 0), axis=-1).astype(I32)

    n_tiles = n_tiles.reshape(1)
    xs = _dispatch(h2, pos[0], pos[1], (offs + counts).astype(I32), (padded - counts).astype(I32), n_tiles,
                   rows_out)
    ys = _experts(xs, tile_expert, tile_row, n_tiles, tile_first, tile_slot, tile_next,
                  w1.reshape(N_EXPERTS, d, D_EXPERT), w3.reshape(N_EXPERTS, d, D_EXPERT),
                  w2.reshape(N_EXPERTS, D_EXPERT, d))
    return _combine(x1, mod3, final_g, ys, pos[0], pos[1], cw[0].reshape(n, 1), cw[1].reshape(n, 1), seq)


def kernel(x, c, ada_w, ada_b, norm1_g, w_in, gmlp_ln_g, gmlp_ln_b, gmlp_w_s, gmlp_b_s, rel_bias, w_branch_a,
           w_branch_b, w_out, norm2_g, w_group, w_expert, w1, w3, w2, final_g):
    b, s, d = x.shape
    out = _layer(x.reshape(b * s, d), c, s, ada_w[0], ada_b[0], norm1_g[0], w_in[0], gmlp_ln_g[0], gmlp_ln_b[0],
                 gmlp_w_s[0], gmlp_b_s[0], rel_bias[0], w_branch_a[0], w_branch_b[0], w_out[0], norm2_g[0],
                 w_group[0], w_expert[0], w1[0], w3[0], w2[0], final_g)
    return out.reshape(b, s, d)
```

```python
import numpy as np
import jax
import jax.numpy as jnp
from jax import lax
from jax.experimental import pallas as pl
from jax.experimental.pallas import tpu as pltpu

F32 = jnp.float32
BF16 = jnp.bfloat16
I32 = jnp.int32

D_MODEL = 2048
CHUNK = 64
EPS = 1e-6
NEG_INF = -1e30
LOG2E = float(np.log2(np.e))
GMLP_BLOCK = 128
GMLP_GROUPS = 8
GMLP_WIDTH = 1024
ATT_HEADS = 16
ATT_HEAD_DIM = 64
ATT_WIDTH = ATT_HEADS * ATT_HEAD_DIM
LEFT_CHUNKS = 8
MAX_REL = 256
N_GROUPS = 4
EXPERTS_PER_GROUP = 8
N_EXPERTS = N_GROUPS * EXPERTS_PER_GROUP
D_EXPERT = 512
N_MOD = 6

LANES = 128
SUBLANES = 8
VMEM_LIMIT = 56 * 1024 * 1024

COL_TILE = 1024
PROJ_COLS = 2 * D_MODEL + 2 * GMLP_WIDTH + 3 * ATT_WIDTH
GATE_TILES = 2 * D_MODEL // COL_TILE
U_TILE = GATE_TILES
V_TILE = GATE_TILES + 1
Q_TILE = GATE_TILES + 2
K_TILE = GATE_TILES + 3
VB_TILE = GATE_TILES + 4

ROUTER_ROWS = SUBLANES + N_EXPERTS

ATT_QBLK = 512
ATT_SUB = 2 * CHUNK
ATT_NSUB = ATT_QBLK // ATT_SUB
ATT_PAIRS = 2
ATT_WIN = ATT_SUB + LEFT_CHUNKS * CHUNK

EXPERT_TILE = 256
TOKEN_ROWS = D_MODEL // LANES


def _params(sem, vmem=VMEM_LIMIT):
    return pltpu.CompilerParams(dimension_semantics=sem, vmem_limit_bytes=vmem)


def _rms_mod(x, g, scale, shift):
    y = x * lax.rsqrt(jnp.mean(x * x, axis=-1, keepdims=True) + EPS)
    return (y * g) * (1.0 + scale) + shift


def _ada_kernel(cb_ref, w_ref, b_ref, o_ref, s_scr):
    nb, d, _ = cb_ref.shape
    tn = w_ref.shape[1]
    reps = tn // LANES

    @pl.when(pl.program_id(0) == 0)
    def _():
        cb = cb_ref[...]
        s_scr[...] = cb * jax.nn.sigmoid(cb)

    def body(i, accs):
        r = pl.ds(pl.multiple_of(i * SUBLANES, SUBLANES), SUBLANES)
        w8 = w_ref[r, :]
        return tuple(accs[b] + w8 * jnp.concatenate([s_scr[b, r, :]] * reps, axis=1) for b in range(nb))

    init = tuple(jnp.zeros((SUBLANES, tn), F32) for _ in range(nb))
    accs = lax.fori_loop(0, d // SUBLANES, body, init, unroll=4)
    for b in range(nb):
        o_ref[b:b + 1, :] = jnp.sum(accs[b], axis=0, keepdims=True) + b_ref[...]


def _ada(c, ada_w, ada_b):
    nb, d = c.shape
    n = ada_w.shape[1]
    tn = 1024
    cb = jnp.broadcast_to(c[:, :, None], (nb, d, LANES))
    return pl.pallas_call(
        _ada_kernel,
        grid=(n // tn,),
        in_specs=[pl.BlockSpec((nb, d, LANES), lambda j: (0, 0, 0)),
                  pl.BlockSpec((d, tn), lambda j: (0, j)),
                  pl.BlockSpec((1, tn), lambda j: (0, j))],
        out_specs=pl.BlockSpec((nb, tn), lambda j: (0, j)),
        out_shape=jax.ShapeDtypeStruct((nb, n), F32),
        scratch_shapes=[pltpu.VMEM((nb, d, LANES), F32)],
        compiler_params=_params(("arbitrary",)),
        name="ada",
    )(cb, ada_w, ada_b.reshape(1, n))


IN_TM = 1024
IN_RB = 1024
IN_STAT_RB = 128
IN_NORM_RB = 16


def _gelu(a):
    return 0.5 * a * (1.0 + lax.erf(a * np.float32(np.sqrt(0.5))))


def _sigmoid(a):
    return 0.5 * jnp.tanh(0.5 * a) + 0.5


def _in_proj_kernel(mod_ref, g_ref, x_ref, w_ref, o_ref, h_scr, gain_scr, shift_scr, inv_scr):
    j = pl.program_id(1)
    nrb = IN_TM // IN_RB

    @pl.when(j == 0)
    def _():
        d = x_ref.shape[1]
        gain_scr[...] = jnp.broadcast_to(g_ref[...] * (1.0 + mod_ref[0, 1:2, :]), gain_scr.shape)
        shift_scr[...] = jnp.broadcast_to(mod_ref[0, 0:1, :], shift_scr.shape)

        def stats(rb, c):
            rows = pl.ds(pl.multiple_of(rb * IN_STAT_RB, IN_STAT_RB), IN_STAT_RB)
            sq = jnp.zeros((IN_STAT_RB, LANES), F32)
            for k in range(d // LANES):
                xk = x_ref[rows, k * LANES:(k + 1) * LANES]
                sq = sq + xk * xk
            inv = lax.rsqrt(jnp.sum(sq, axis=-1, keepdims=True) * (1.0 / d) + EPS)
            inv_scr[rows, :] = jnp.broadcast_to(inv, (IN_STAT_RB, LANES))
            return c
        lax.fori_loop(0, IN_TM // IN_STAT_RB, stats, 0)

        def apply(rb, c):
            rows = pl.ds(pl.multiple_of(rb * IN_NORM_RB, IN_NORM_RB), IN_NORM_RB)
            inv = inv_scr[rows, :]
            for k in range(d // LANES):
                cols = slice(k * LANES, (k + 1) * LANES)
                y = (x_ref[rows, cols] * inv) * gain_scr[:, cols] + shift_scr[:, cols]
                h_scr[rows, cols] = y.astype(BF16)
            return c
        lax.fori_loop(0, IN_TM // IN_NORM_RB, apply, 0, unroll=2)

    def run(epilogue):
        def body(rb, c):
            rows = pl.ds(pl.multiple_of(rb * IN_RB, IN_RB), IN_RB)
            acc = jnp.dot(h_scr[rows, :], w_ref[...], preferred_element_type=F32)
            o_ref[rows, :] = epilogue(acc).astype(o_ref.dtype)
            return c
        lax.fori_loop(0, nrb, body, 0)

    @pl.when(j < GATE_TILES)
    def _():
        run(_sigmoid)

    @pl.when((j == U_TILE) | (j == V_TILE))
    def _():
        run(_gelu)

    @pl.when(j == Q_TILE)
    def _():
        run(lambda a: a * np.float32(ATT_HEAD_DIM ** -0.5 * LOG2E))

    @pl.when(j > Q_TILE)
    def _():
        run(lambda a: a)


def _in_proj(x2d, mod3, norm_g, w_in_bf16, seq):
    n, d = x2d.shape
    cols = w_in_bf16.shape[1]
    n_tiles = cols // COL_TILE
    tiles_per_seq = seq // IN_TM
    return pl.pallas_call(
        _in_proj_kernel,
        grid=(n // IN_TM, n_tiles),
        in_specs=[pl.BlockSpec((1, N_MOD, d), lambda i, j: (i // tiles_per_seq, 0, 0)),
                  pl.BlockSpec((1, d), lambda i, j: (0, 0)),
                  pl.BlockSpec((IN_TM, d), lambda i, j: (i, 0)),
                  pl.BlockSpec((d, COL_TILE), lambda i, j: (0, (j + n_tiles - GATE_TILES) % n_tiles))],
        out_specs=pl.BlockSpec((IN_TM, COL_TILE), lambda i, j: (i, j)),
        out_shape=jax.ShapeDtypeStruct((n, cols), BF16),
        scratch_shapes=[pltpu.VMEM((IN_TM, d), BF16), pltpu.VMEM((IN_NORM_RB, d), F32),
                        pltpu.VMEM((IN_NORM_RB, d), F32), pltpu.VMEM((IN_TM, LANES), F32)],
        compiler_params=_params(("arbitrary", "arbitrary")),
        name="in_proj",
    )(mod3, norm_g.reshape(1, d), x2d, w_in_bf16)


GM_TM = 512


def _gmlp_kernel(u_ref, v_ref, lng_ref, lnb_ref, ws_ref, bs_ref, o_ref):
    t = lax.broadcasted_iota(I32, (GMLP_BLOCK, GMLP_BLOCK), 0)
    s = lax.broadcasted_iota(I32, (GMLP_BLOCK, GMLP_BLOCK), 1)
    causal = (s // CHUNK) <= (t // CHUNK)
    lng = lng_ref[...]
    lnb = lnb_ref[...]
    for blk in range(GM_TM // GMLP_BLOCK):
        rows = slice(blk * GMLP_BLOCK, (blk + 1) * GMLP_BLOCK)
        v = v_ref[rows, :].astype(F32)
        mu = jnp.mean(v, axis=-1, keepdims=True)
        vc = v - mu
        var = jnp.mean(vc * vc, axis=-1, keepdims=True)
        vln = ((vc * lax.rsqrt(var + EPS)) * lng + lnb).astype(BF16)
        for g in range(GMLP_GROUPS):
            cols = slice(g * LANES, (g + 1) * LANES)
            w = jnp.where(causal, ws_ref[g], 0.0).astype(BF16)
            mixed = jnp.dot(w, vln[:, cols], preferred_element_type=F32) + bs_ref[g]
            o_ref[rows, cols] = (u_ref[rows, cols].astype(F32) * mixed).astype(o_ref.dtype)


def _gmlp(proj, ln_g, ln_b, w_s, b_s):
    n = proj.shape[0]
    return pl.pallas_call(
        _gmlp_kernel,
        grid=(n // GM_TM,),
        in_specs=[pl.BlockSpec((GM_TM, GMLP_WIDTH), lambda i: (i, U_TILE)),
                  pl.BlockSpec((GM_TM, GMLP_WIDTH), lambda i: (i, V_TILE)),
                  pl.BlockSpec((1, GMLP_WIDTH), lambda i: (0, 0)),
                  pl.BlockSpec((1, GMLP_WIDTH), lambda i: (0, 0)),
                  pl.BlockSpec((GMLP_GROUPS, GMLP_BLOCK, GMLP_BLOCK), lambda i: (0, 0, 0)),
                  pl.BlockSpec((GMLP_GROUPS, GMLP_BLOCK, 1), lambda i: (0, 0, 0))],
        out_specs=pl.BlockSpec((GM_TM, GMLP_WIDTH), lambda i: (i, 0)),
        out_shape=jax.ShapeDtypeStruct((n, GMLP_WIDTH), BF16),
        compiler_params=_params(("arbitrary",)),
        name="gmlp",
    )(proj, proj, ln_g.reshape(1, -1), ln_b.reshape(1, -1), w_s,
      b_s.reshape(GMLP_GROUPS, GMLP_BLOCK, 1))


def _band_bias(rel_table):
    heads = rel_table.shape[0]
    r = np.arange(ATT_SUB)[:, None]
    w = np.arange(ATT_WIN)[None, :]
    j = w // CHUNK - r // CHUNK
    in_band = (j >= 0) & (j <= LEFT_CHUNKS)
    a = np.arange(ATT_NSUB)[:, None, None]
    in_seq = np.broadcast_to(w[None] >= ATT_QBLK - a * ATT_SUB, (ATT_NSUB, ATT_SUB, ATT_WIN))
    visible = np.stack([np.broadcast_to(in_band, in_seq.shape), in_band[None] & in_seq])[:, None]
    far = LEFT_CHUNKS * CHUNK + ATT_SUB - 1
    n_clipped = far - MAX_REL + 1
    table = rel_table.astype(F32) * np.float32(LOG2E)
    lo = MAX_REL - (ATT_WIN - 1 - LEFT_CHUNKS * CHUNK)
    diag = jnp.concatenate([jnp.broadcast_to(table[:, 2 * MAX_REL:], (heads, n_clipped)),
                            jnp.flip(table[:, lo:2 * MAX_REL], axis=1),
                            jnp.zeros((heads, 1), F32)], axis=1)
    span = diag.shape[1] - 1
    shifted = jnp.tile(diag, (1, ATT_SUB))[:, :ATT_SUB * span].reshape(heads, ATT_SUB, span)
    bias = shifted[:, :, ATT_SUB - 1:ATT_SUB - 1 + ATT_WIN]
    return jnp.where(jnp.asarray(visible), bias[None, :, None], NEG_INF)


def _attn_kernel(q_ref, kp_ref, kc_ref, vp_ref, vc_ref, bias_ref, o_ref, k_scr, v_scr, s_scr):
    k_scr[0:ATT_QBLK, :] = kp_ref[...]
    k_scr[ATT_QBLK:, :] = kc_ref[...]
    v_scr[0:ATT_QBLK, :] = vp_ref[...]
    v_scr[ATT_QBLK:, :] = vc_ref[...]
    lane = lax.broadcasted_iota(I32, (ATT_SUB, LANES), 1)
    low = lane < ATT_HEAD_DIM

    units = [(pair, a) for pair in range(ATT_PAIRS) for a in range(ATT_NSUB)]
    for u, (pair, a) in enumerate(units):
        lanes = slice(pair * LANES, (pair + 1) * LANES)
        q = q_ref[a * ATT_SUB:(a + 1) * ATT_SUB, lanes]
        zero = jnp.zeros_like(q)
        q2 = jnp.concatenate([jnp.where(low, q, zero), jnp.where(low, zero, q)], axis=0)
        win = slice(a * ATT_SUB, a * ATT_SUB + ATT_WIN)
        s = lax.dot_general(q2, k_scr[win, lanes], (((1,), (1,)), ((), ())), preferred_element_type=F32)
        s_scr[u] = s + bias_ref[0, 2 * pair:2 * pair + 2, a].reshape(2 * ATT_SUB, ATT_WIN)

    for u, (pair, a) in enumerate(units):
        lanes = slice(pair * LANES, (pair + 1) * LANES)
        rows = slice(a * ATT_SUB, (a + 1) * ATT_SUB)
        win = slice(a * ATT_SUB, a * ATT_SUB + ATT_WIN)
        s = s_scr[u]
        m = jnp.max(s, axis=-1, keepdims=True)
        p = jnp.exp2(s - m)
        l = jnp.sum(p, axis=-1, keepdims=True)
        pv = jnp.dot(p.astype(BF16), v_scr[win, lanes], preferred_element_type=F32) / l
        o_ref[rows, lanes] = jnp.where(low, pv[:ATT_SUB], pv[ATT_SUB:]).astype(o_ref.dtype)


def _attn(proj, rel_table, seq):
    n = proj.shape[0]
    bps = seq // ATT_QBLK
    pairs = ATT_WIDTH // LANES
    width = ATT_PAIRS * LANES
    qc, kc, vc = (Q_TILE * COL_TILE // width, K_TILE * COL_TILE // width, VB_TILE * COL_TILE // width)
    bias = _band_bias(rel_table)

    def prev(i):
        return jnp.where(i % bps == 0, i, i - 1)

    def first(i):
        return jnp.where(i % bps == 0, 1, 0)

    blk = (ATT_QBLK, width)
    return pl.pallas_call(
        _attn_kernel,
        grid=(pairs // ATT_PAIRS, n // ATT_QBLK),
        in_specs=[pl.BlockSpec(blk, lambda h, i: (i, qc + h)),
                  pl.BlockSpec(blk, lambda h, i: (prev(i), kc + h)),
                  pl.BlockSpec(blk, lambda h, i: (i, kc + h)),
                  pl.BlockSpec(blk, lambda h, i: (prev(i), vc + h)),
                  pl.BlockSpec(blk, lambda h, i: (i, vc + h)),
                  pl.BlockSpec((1, 2 * ATT_PAIRS, ATT_NSUB, ATT_SUB, ATT_WIN),
                               lambda h, i: (first(i), h, 0, 0, 0))],
        out_specs=pl.BlockSpec(blk, lambda h, i: (i, h)),
        out_shape=jax.ShapeDtypeStruct((n, ATT_WIDTH), BF16),
        scratch_shapes=[pltpu.VMEM((2 * ATT_QBLK, width), BF16),
                        pltpu.VMEM((2 * ATT_QBLK, width), BF16),
                        pltpu.VMEM((ATT_PAIRS * ATT_NSUB, 2 * ATT_SUB, ATT_WIN), F32)],
        compiler_params=_params(("arbitrary", "arbitrary")),
        name="attn",
    )(proj, proj, proj, proj, proj, bias)


MG_TM = 256


def _merge_kernel(mod_ref, g_ref, x_ref, ya_ref, yb_ref, ga_ref, gb_ref, wa_ref, wb_ref, wo_ref, wr_ref,
                  x1_ref, h2_ref, lt_ref):
    gate1 = mod_ref[0, 2:3, :]
    shift2 = mod_ref[0, 3:4, :]
    scale2 = mod_ref[0, 4:5, :]
    ya = jnp.dot(ya_ref[...], wa_ref[...], preferred_element_type=F32)
    yb = jnp.dot(yb_ref[...], wb_ref[...], preferred_element_type=F32)
    m = ga_ref[...].astype(F32) * ya + gb_ref[...].astype(F32) * yb
    mixed = jnp.dot(m.astype(BF16), wo_ref[...], preferred_element_type=F32)
    x1 = x_ref[...] + gate1 * mixed
    x1_ref[...] = x1
    h2 = _rms_mod(x1, g_ref[...], scale2, shift2).astype(BF16)
    h2_ref[...] = h2.reshape(h2_ref.shape)
    lt_ref[...] = lax.dot_general(wr_ref[...], h2, (((1,), (1,)), ((), ())), preferred_element_type=F32)


def _merge(x2d, mod3, norm_g, ya, yb, proj, wa, wb, wo, wr_t, seq):
    n, d = x2d.shape
    tps = seq // MG_TM
    const = lambda shape: pl.BlockSpec(shape, lambda i: (0,) * len(shape), pipeline_mode=pl.Buffered(1))
    return pl.pallas_call(
        _merge_kernel,
        grid=(n // MG_TM,),
        in_specs=[pl.BlockSpec((1, N_MOD, d), lambda i: (i // tps, 0, 0)),
                  pl.BlockSpec((1, d), lambda i: (0, 0)),
                  pl.BlockSpec((MG_TM, d), lambda i: (i, 0)),
                  pl.BlockSpec((MG_TM, GMLP_WIDTH), lambda i: (i, 0)),
                  pl.BlockSpec((MG_TM, ATT_WIDTH), lambda i: (i, 0)),
                  pl.BlockSpec((MG_TM, d), lambda i: (i, 0)),
                  pl.BlockSpec((MG_TM, d), lambda i: (i, 1)),
                  const((GMLP_WIDTH, d)), const((ATT_WIDTH, d)), const((d, d)),
                  const((ROUTER_ROWS, d))],
        out_specs=[pl.BlockSpec((MG_TM, d), lambda i: (i, 0)),
                   pl.BlockSpec((MG_TM, TOKEN_ROWS, LANES), lambda i: (i, 0, 0)),
                   pl.BlockSpec((ROUTER_ROWS, MG_TM), lambda i: (0, i))],
        out_shape=[jax.ShapeDtypeStruct((n, d), F32),
                   jax.ShapeDtypeStruct((n, TOKEN_ROWS, LANES), BF16),
                   jax.ShapeDtypeStruct((ROUTER_ROWS, n), F32)],
        compiler_params=_params(("arbitrary",)),
        name="merge",
    )(mod3, norm_g.reshape(1, d), x2d, ya, yb, proj, proj, wa, wb, wo, wr_t)


RT_TN = 512


def _first_argmax(vals, vmax, nrows):
    rows = lax.broadcasted_iota(I32, vals.shape, 0)
    return jnp.min(jnp.where(vals == vmax, rows, nrows), axis=0, keepdims=True)


def _route_kernel(lt_ref, e_ref, r_ref, w_ref, cnt_ref, carry_scr):
    @pl.when(pl.program_id(0) == 0)
    def _():
        carry_scr[...] = jnp.zeros_like(carry_scr)

    gl = lt_ref[0:N_GROUPS, :]
    gmax = jnp.max(gl, axis=0, keepdims=True)
    gidx = _first_argmax(gl, gmax, N_GROUPS)
    gw = 1.0 / jnp.sum(jnp.exp(gl - gmax), axis=0, keepdims=True)

    esel = lt_ref[SUBLANES:SUBLANES + EXPERTS_PER_GROUP, :]
    for g in range(1, N_GROUPS):
        lo = SUBLANES + g * EXPERTS_PER_GROUP
        esel = jnp.where(gidx == g, lt_ref[lo:lo + EXPERTS_PER_GROUP, :], esel)
    rows8 = lax.broadcasted_iota(I32, esel.shape, 0)
    m1 = jnp.max(esel, axis=0, keepdims=True)
    i1 = _first_argmax(esel, m1, EXPERTS_PER_GROUP)
    rest = jnp.where(rows8 == i1, -jnp.inf, esel)
    m2 = jnp.max(rest, axis=0, keepdims=True)
    i2 = _first_argmax(rest, m2, EXPERTS_PER_GROUP)
    z = jnp.exp(m2 - m1)
    w_top = 1.0 / (1.0 + z)
    e0 = gidx * EXPERTS_PER_GROUP + i1
    e1 = gidx * EXPERTS_PER_GROUP + i2
    e_ref[0:1, :] = e0
    e_ref[1:2, :] = e1
    w_ref[0:1, :] = gw * w_top
    w_ref[1:2, :] = gw * (z * w_top)

    rows_e = lax.broadcasted_iota(I32, (N_EXPERTS, RT_TN), 0)
    oh0 = rows_e == e0
    oh1 = rows_e == e1
    oh = jnp.where(oh0 | oh1, 1.0, 0.0)
    src = lax.broadcasted_iota(I32, (RT_TN, RT_TN), 0)
    dst = lax.broadcasted_iota(I32, (RT_TN, RT_TN), 1)
    before = jnp.where(src < dst, 1.0, 0.0).astype(BF16)
    carry = carry_scr[...]
    prefix = jnp.dot(oh.astype(BF16), before, preferred_element_type=F32) + carry[:, 0:1]
    r_ref[0:1, :] = jnp.sum(jnp.where(oh0, prefix, 0.0), axis=0, keepdims=True).astype(I32)
    r_ref[1:2, :] = jnp.sum(jnp.where(oh1, prefix, 0.0), axis=0, keepdims=True).astype(I32)
    carry = carry + jnp.sum(oh, axis=1, keepdims=True)
    carry_scr[...] = carry
    cnt_ref[...] = carry.astype(I32)


def _route(logits_t):
    n = logits_t.shape[1]
    slot = pl.BlockSpec((2, RT_TN), lambda i: (0, i))
    return pl.pallas_call(
        _route_kernel,
        grid=(n // RT_TN,),
        in_specs=[pl.BlockSpec((ROUTER_ROWS, RT_TN), lambda i: (0, i))],
        out_specs=[slot, slot, slot, pl.BlockSpec((N_EXPERTS, LANES), lambda i: (0, 0))],
        out_shape=[jax.ShapeDtypeStruct((2, n), I32),
                   jax.ShapeDtypeStruct((2, n), I32),
                   jax.ShapeDtypeStruct((2, n), F32),
                   jax.ShapeDtypeStruct((N_EXPERTS, LANES), I32)],
        scratch_shapes=[pltpu.VMEM((N_EXPERTS, LANES), F32)],
        compiler_params=_params(("arbitrary",)),
        name="route",
    )(logits_t)


DP_TM = 1024
DMA_UNROLL = 8


def _token_copy(src_ref, src_row, dst_ref, dst_row, sem):
    return pltpu.make_async_copy(src_ref.at[src_row], dst_ref.at[dst_row], sem)


def _dispatch_kernel(pos0_ref, pos1_ref, pad_start_ref, pad_n_ref, nt_ref, h_ref, xs_ref, zero_scr, sem, zsem, tsem):
    i = pl.program_id(0)
    n_tiles_max = xs_ref.shape[0] // EXPERT_TILE

    def tail_copy(t):
        return pltpu.make_async_copy(zero_scr, xs_ref.at[pl.ds(t * EXPERT_TILE, EXPERT_TILE)], tsem)

    @pl.when(i == 0)
    def _():
        zero_scr[...] = jnp.zeros_like(zero_scr)
        for e in range(N_EXPERTS):
            start = pad_start_ref[e]

            def issue(r, c, start=start):
                _token_copy(zero_scr, 0, xs_ref, start + r, zsem).start()
                return c
            lax.fori_loop(0, pad_n_ref[e], issue, 0)

        def issue_tail(t, c):
            tail_copy(t).start()
            return c
        lax.fori_loop(nt_ref[0], n_tiles_max, issue_tail, 0)

        for e in range(N_EXPERTS):
            def drain(r, c):
                _token_copy(zero_scr, 0, xs_ref, 0, zsem).wait()
                return c
            lax.fori_loop(0, pad_n_ref[e], drain, 0)

        def drain_tail(t, c):
            tail_copy(t).wait()
            return c
        lax.fori_loop(nt_ref[0], n_tiles_max, drain_tail, 0)

    base = i * DP_TM

    def issue(r, c):
        _token_copy(h_ref, r, xs_ref, pos0_ref[base + r], sem).start(priority=0)
        _token_copy(h_ref, r, xs_ref, pos1_ref[base + r], sem).start(priority=1)
        return c
    lax.fori_loop(0, DP_TM, issue, 0, unroll=DMA_UNROLL)

    for _ in range(2):
        pltpu.make_async_copy(h_ref, xs_ref.at[pl.ds(0, DP_TM)], sem).wait()


def _dispatch(h2, pos0, pos1, pad_start, pad_n, n_tiles, rows_out):
    n = h2.shape[0]
    grid_spec = pltpu.PrefetchScalarGridSpec(
        num_scalar_prefetch=5,
        grid=(n // DP_TM,),
        in_specs=[pl.BlockSpec((DP_TM, TOKEN_ROWS, LANES), lambda i, *_: (i, 0, 0))],
        out_specs=pl.BlockSpec(memory_space=pl.ANY),
        scratch_shapes=[pltpu.VMEM((EXPERT_TILE, TOKEN_ROWS, LANES), BF16),
                        pltpu.SemaphoreType.DMA(()), pltpu.SemaphoreType.DMA(()),
                        pltpu.SemaphoreType.DMA(())],
    )
    return pl.pallas_call(
        _dispatch_kernel,
        grid_spec=grid_spec,
        out_shape=jax.ShapeDtypeStruct((rows_out, TOKEN_ROWS, LANES), BF16),
        compiler_params=pltpu.CompilerParams(dimension_semantics=("arbitrary",),
                                             vmem_limit_bytes=VMEM_LIMIT, has_side_effects=True),
        name="dispatch",
    )(pos0, pos1, pad_start, pad_n, n_tiles, h2)


EXPERT_CAST_ROWS = 256


def _cast_rows(src_ref, dst_ref):
    rows_total = dst_ref.shape[0]

    def body(r, c):
        rows = pl.ds(pl.multiple_of(r * EXPERT_CAST_ROWS, EXPERT_CAST_ROWS), EXPERT_CAST_ROWS)
        dst_ref[rows, :] = src_ref[rows, :].astype(BF16)
        return c
    lax.fori_loop(0, rows_total // EXPERT_CAST_ROWS, body, 0)


def _experts_kernel(te_ref, tr_ref, nt_ref, first_ref, slot_ref, next_ref, xs_ref, w1_hbm, w3_hbm, w2_hbm, ys_ref,
                    w1_stage, w3_stage, w2_stage, w1_scr, w3_scr, w2_scr, sems):
    i = pl.program_id(0)

    def fetch(expert, slot):
        return (pltpu.make_async_copy(w1_hbm.at[expert], w1_stage.at[slot], sems.at[slot]),
                pltpu.make_async_copy(w3_hbm.at[expert], w3_stage.at[slot], sems.at[slot]),
                pltpu.make_async_copy(w2_hbm.at[expert], w2_stage.at[slot], sems.at[slot]))

    @pl.when(i == 0)
    def _():
        for copy in fetch(te_ref[0], 0):
            copy.start()

    @pl.when(first_ref[i] == 1)
    def _():
        slot = slot_ref[i]
        for copy in fetch(te_ref[i], slot):
            copy.wait()

        @pl.when(next_ref[i] >= 0)
        def _():
            for copy in fetch(next_ref[i], 1 - slot):
                copy.start()

        _cast_rows(w1_stage.at[slot], w1_scr)
        _cast_rows(w3_stage.at[slot], w3_scr)
        _cast_rows(w2_stage.at[slot], w2_scr)

    @pl.when(i < nt_ref[0])
    def _():
        x = xs_ref[...].reshape(EXPERT_TILE, D_MODEL)
        a = jnp.dot(x, w1_scr[...], preferred_element_type=F32)
        b = jnp.dot(x, w3_scr[...], preferred_element_type=F32)
        act = (a * jax.nn.sigmoid(a)) * b
        y = jnp.dot(act.astype(BF16), w2_scr[...], preferred_element_type=F32)
        ys_ref[...] = y.astype(BF16).reshape(ys_ref.shape)

    @pl.when(i >= nt_ref[0])
    def _():
        ys_ref[...] = jnp.zeros_like(ys_ref)


def _experts(xs, tile_expert, tile_row, n_tiles, tile_first, tile_slot, tile_next, w1, w3, w2):
    rows = xs.shape[0]
    _, d, f = w1.shape
    tile = (EXPERT_TILE, TOKEN_ROWS, LANES)
    hbm = pl.BlockSpec(memory_space=pl.ANY)
    grid_spec = pltpu.PrefetchScalarGridSpec(
        num_scalar_prefetch=6,
        grid=(rows // EXPERT_TILE,),
        in_specs=[pl.BlockSpec(tile, lambda i, te, tr, *_: (tr[i], 0, 0)), hbm, hbm, hbm],
        out_specs=pl.BlockSpec(tile, lambda i, *_: (i, 0, 0)),
        scratch_shapes=[pltpu.VMEM((2, d, f), F32), pltpu.VMEM((2, d, f), F32), pltpu.VMEM((2, f, d), F32),
                        pltpu.VMEM((d, f), BF16), pltpu.VMEM((d, f), BF16), pltpu.VMEM((f, d), BF16),
                        pltpu.SemaphoreType.DMA((2,))],
    )
    return pl.pallas_call(
        _experts_kernel,
        grid_spec=grid_spec,
        out_shape=jax.ShapeDtypeStruct(xs.shape, BF16),
        compiler_params=_params(("arbitrary",)),
        name="experts",
    )(tile_expert, tile_row, n_tiles, tile_first, tile_slot, tile_next, xs, w1, w3, w2)


CB_TM = 256


def _combine_kernel(pos0_ref, pos1_ref, mod_ref, fg_ref, x1_ref, w0_ref, w1_ref, ys_ref, o_ref,
                    y0_scr, y1_scr, sems):
    i = pl.program_id(0)
    slot = i % 2

    def gather(step, buf):
        base = step * CB_TM

        def issue(r, c):
            _token_copy(ys_ref, pos0_ref[base + r], y0_scr.at[buf], r, sems.at[buf]).start(priority=0)
            _token_copy(ys_ref, pos1_ref[base + r], y1_scr.at[buf], r, sems.at[buf]).start(priority=1)
            return c
        lax.fori_loop(0, CB_TM, issue, 0, unroll=DMA_UNROLL)

    @pl.when(i == 0)
    def _():
        gather(i, slot)

    @pl.when(i + 1 < pl.num_programs(0))
    def _():
        gather(i + 1, 1 - slot)

    pltpu.make_async_copy(ys_ref.at[pl.ds(0, CB_TM)], y0_scr.at[slot], sems.at[slot]).wait()
    pltpu.make_async_copy(ys_ref.at[pl.ds(0, CB_TM)], y1_scr.at[slot], sems.at[slot]).wait()

    gate2 = mod_ref[0, 5:6, :]
    y0 = y0_scr[slot].reshape(CB_TM, D_MODEL).astype(F32)
    y1 = y1_scr[slot].reshape(CB_TM, D_MODEL).astype(F32)
    y = w0_ref[...] * y0 + w1_ref[...] * y1
    x2 = x1_ref[...] + gate2 * y
    o_ref[...] = (x2 * lax.rsqrt(jnp.mean(x2 * x2, axis=-1, keepdims=True) + EPS)) * fg_ref[...]


def _combine(x1, mod3, final_g, ys, pos0, pos1, cw0, cw1, seq):
    n, d = x1.shape
    tps = seq // CB_TM
    grid_spec = pltpu.PrefetchScalarGridSpec(
        num_scalar_prefetch=2,
        grid=(n // CB_TM,),
        in_specs=[pl.BlockSpec((1, N_MOD, d), lambda i, *_: (i // tps, 0, 0)),
                  pl.BlockSpec((1, d), lambda i, *_: (0, 0)),
                  pl.BlockSpec((CB_TM, d), lambda i, *_: (i, 0)),
                  pl.BlockSpec((CB_TM, 1), lambda i, *_: (i, 0)),
                  pl.BlockSpec((CB_TM, 1), lambda i, *_: (i, 0)),
                  pl.BlockSpec(memory_space=pl.ANY)],
        out_specs=pl.BlockSpec((CB_TM, d), lambda i, *_: (i, 0)),
        scratch_shapes=[pltpu.VMEM((2, CB_TM, TOKEN_ROWS, LANES), BF16),
                        pltpu.VMEM((2, CB_TM, TOKEN_ROWS, LANES), BF16),
                        pltpu.SemaphoreType.DMA((2,))],
    )
    return pl.pallas_call(
        _combine_kernel,
        grid_spec=grid_spec,
        out_shape=jax.ShapeDtypeStruct((n, d), F32),
        compiler_params=_params(("arbitrary",)),
        name="combine",
    )(pos0, pos1, mod3, final_g.reshape(1, d), x1, cw0, cw1, ys)


def _layer(x2d, c, seq, ada_w, ada_b, norm1_g, w_in, gmlp_ln_g, gmlp_ln_b, gmlp_w_s, gmlp_b_s, rel_bias,
           w_branch_a, w_branch_b, w_out, norm2_g, w_group, w_expert, w1, w3, w2, final_g):
    n, d = x2d.shape
    nb = c.shape[0]
    mod3 = _ada(c, ada_w, ada_b).reshape(nb, N_MOD, d)

    proj = _in_proj(x2d, mod3, norm1_g, w_in.astype(BF16), seq)
    ya = _gmlp(proj, gmlp_ln_g, gmlp_ln_b, gmlp_w_s, gmlp_b_s)
    yb = _attn(proj, rel_bias, seq)

    wr_t = jnp.concatenate([w_group.T, jnp.zeros((SUBLANES - N_GROUPS, d), F32),
                            w_expert.transpose(0, 2, 1).reshape(N_EXPERTS, d)], axis=0).astype(BF16)
    x1, h2, logits_t = _merge(x2d, mod3, norm2_g, ya, yb, proj, w_branch_a.astype(BF16),
                              w_branch_b.astype(BF16), w_out.astype(BF16), wr_t, seq)

    eidx, rank, cw, counts = _route(logits_t)
    counts = counts[:, 0]
    padded = ((counts + EXPERT_TILE - 1) // EXPERT_TILE) * EXPERT_TILE
    ends = jnp.cumsum(padded)
    offs = ends - padded
    experts = jnp.arange(N_EXPERTS, dtype=I32)
    pos = jnp.sum(jnp.where(eidx[:, :, None] == experts, offs, 0), axis=-1) + rank
    rows_out = 2 * n + N_EXPERTS * EXPERT_TILE
    n_tiles_max = rows_out // EXPERT_TILE
    n_tiles = (ends[-1] // EXPERT_TILE).astype(I32)
    tile_row = jnp.minimum(jnp.arange(n_tiles_max, dtype=I32), n_tiles - 1)
    tile_expert = jnp.minimum(jnp.sum(ends[None, :] <= (tile_row * EXPERT_TILE)[:, None], axis=-1),
                              N_EXPERTS - 1).astype(I32)

    tiles = jnp.arange(n_tiles_max, dtype=I32)
    prev_expert = jnp.concatenate([jnp.full((1,), -1, I32), tile_expert[:-1]])
    tile_first = ((tiles < n_tiles) & (tile_expert != prev_expert)).astype(I32)
    tile_slot = ((jnp.cumsum(tile_first) - 1) % 2).astype(I32)
    later_nonempty = (experts[None, :] > experts[:, None]) & (padded > 0)[None, :]
    next_nonempty = jnp.min(jnp.where(later_nonempty, experts[None, :], N_EXPERTS), axis=1)
    next_nonempty = jnp.where(next_nonempty == N_EXPERTS, -1, next_nonempty)
    tile_next = jnp.sum(jnp.where(tile_expert[:, None] == experts, next_nonempty, 0), axis=-1).astype(I32)

    n_tiles = n_tiles.reshape(1)
    xs = _dispatch(h2, pos[0], pos[1], (offs + counts).astype(I32), (padded - counts).astype(I32), n_tiles,
                   rows_out)
    ys = _experts(xs, tile_expert, tile_row, n_tiles, tile_first, tile_slot, tile_next,
                  w1.reshape(N_EXPERTS, d, D_EXPERT), w3.reshape(N_EXPERTS, d, D_EXPERT),
                  w2.reshape(N_EXPERTS, D_EXPERT, d))
    return _combine(x1, mod3, final_g, ys, pos[0], pos[1], cw[0].reshape(n, 1), cw[1].reshape(n, 1), seq)


def kernel(x, c, ada_w, ada_b, norm1_g, w_in, gmlp_ln_g, gmlp_ln_b, gmlp_w_s, gmlp_b_s, rel_bias, w_branch_a,
           w_branch_b, w_out, norm2_g, w_group, w_expert, w1, w3, w2, final_g):
    b, s, d = x.shape
    out = _layer(x.reshape(b * s, d), c, s, ada_w[0], ada_b[0], norm1_g[0], w_in[0], gmlp_ln_g[0], gmlp_ln_b[0],
                 gmlp_w_s[0], gmlp_b_s[0], rel_bias[0], w_branch_a[0], w_branch_b[0], w_out[0], norm2_g[0],
                 w_group[0], w_expert[0], w1[0], w3[0], w2[0], final_g)
    return out.reshape(b, s, d)
```

```python
import functools

import numpy as np
import jax
import jax.numpy as jnp
from jax import lax
from jax.experimental import pallas as pl
from jax.experimental.pallas import tpu as pltpu

F32 = jnp.float32
BF16 = jnp.bfloat16
I32 = jnp.int32

D_MODEL = 2048
CHUNK = 64
EPS = 1e-6
NEG_INF = -1e30
LOG2E = float(np.log2(np.e))
GMLP_BLOCK = 128
GMLP_GROUPS = 8
GMLP_WIDTH = 1024
ATT_HEADS = 16
ATT_HEAD_DIM = 64
ATT_WIDTH = ATT_HEADS * ATT_HEAD_DIM
LEFT_CHUNKS = 8
MAX_REL = 256
N_GROUPS = 4
EXPERTS_PER_GROUP = 8
N_EXPERTS = N_GROUPS * EXPERTS_PER_GROUP
D_EXPERT = 512
N_MOD = 6

LANES = 128
SUBLANES = 8
VMEM_LIMIT = 56 * 1024 * 1024

COL_TILE = 1024
PROJ_COLS = 2 * D_MODEL + 2 * GMLP_WIDTH + 3 * ATT_WIDTH
GATE_TILES = 2 * D_MODEL // COL_TILE
U_TILE = GATE_TILES
V_TILE = GATE_TILES + 1
Q_TILE = GATE_TILES + 2
K_TILE = GATE_TILES + 3
VB_TILE = GATE_TILES + 4

ROUTER_ROWS = SUBLANES + N_EXPERTS

ATT_QBLK = 512
ATT_SUB = 2 * CHUNK
ATT_NSUB = ATT_QBLK // ATT_SUB
ATT_PAIRS = 2
ATT_WIN = ATT_SUB + LEFT_CHUNKS * CHUNK

EXPERT_TILE = 256
TOKEN_ROWS = D_MODEL // LANES


def _params(sem, vmem=VMEM_LIMIT):
    return pltpu.CompilerParams(dimension_semantics=sem, vmem_limit_bytes=vmem)


def _rms_mod(x, g, scale, shift):
    y = x * lax.rsqrt(jnp.mean(x * x, axis=-1, keepdims=True) + EPS)
    return (y * g) * (1.0 + scale) + shift


def _ada_kernel(cb_ref, w_ref, b_ref, o_ref, s_scr):
    nb, d, _ = cb_ref.shape
    tn = w_ref.shape[1]
    reps = tn // LANES

    @pl.when(pl.program_id(0) == 0)
    def _():
        cb = cb_ref[...]
        s_scr[...] = cb * jax.nn.sigmoid(cb)

    def body(i, accs):
        r = pl.ds(pl.multiple_of(i * SUBLANES, SUBLANES), SUBLANES)
        w8 = w_ref[r, :]
        return tuple(accs[b] + w8 * jnp.concatenate([s_scr[b, r, :]] * reps, axis=1) for b in range(nb))

    init = tuple(jnp.zeros((SUBLANES, tn), F32) for _ in range(nb))
    accs = lax.fori_loop(0, d // SUBLANES, body, init, unroll=4)
    for b in range(nb):
        o_ref[b:b + 1, :] = jnp.sum(accs[b], axis=0, keepdims=True) + b_ref[...]


def _ada(c, ada_w, ada_b):
    nb, d = c.shape
    n = ada_w.shape[1]
    tn = 1024
    cb = jnp.broadcast_to(c[:, :, None], (nb, d, LANES))
    return pl.pallas_call(
        _ada_kernel,
        grid=(n // tn,),
        in_specs=[pl.BlockSpec((nb, d, LANES), lambda j: (0, 0, 0)),
                  pl.BlockSpec((d, tn), lambda j: (0, j)),
                  pl.BlockSpec((1, tn), lambda j: (0, j))],
        out_specs=pl.BlockSpec((nb, tn), lambda j: (0, j)),
        out_shape=jax.ShapeDtypeStruct((nb, n), F32),
        scratch_shapes=[pltpu.VMEM((nb, d, LANES), F32)],
        compiler_params=_params(("arbitrary",)),
        name="ada",
    )(cb, ada_w, ada_b.reshape(1, n))


IN_TM = 1024
IN_RB = 1024
IN_STAT_RB = 128
IN_NORM_RB = 16


def _gelu(a):
    return 0.5 * a * (1.0 + lax.erf(a * np.float32(np.sqrt(0.5))))


def _sigmoid(a):
    return 0.5 * jnp.tanh(0.5 * a) + 0.5


def _in_proj_kernel(mod_ref, g_ref, x_ref, w_ref, o_ref, h_scr, gain_scr, shift_scr, inv_scr):
    j = pl.program_id(1)
    nrb = IN_TM // IN_RB

    @pl.when(j == 0)
    def _():
        d = x_ref.shape[1]
        gain_scr[...] = jnp.broadcast_to(g_ref[...] * (1.0 + mod_ref[0, 1:2, :]), gain_scr.shape)
        shift_scr[...] = jnp.broadcast_to(mod_ref[0, 0:1, :], shift_scr.shape)

        def stats(rb, c):
            rows = pl.ds(pl.multiple_of(rb * IN_STAT_RB, IN_STAT_RB), IN_STAT_RB)
            sq = jnp.zeros((IN_STAT_RB, LANES), F32)
            for k in range(d // LANES):
                xk = x_ref[rows, k * LANES:(k + 1) * LANES]
                sq = sq + xk * xk
            inv = lax.rsqrt(jnp.sum(sq, axis=-1, keepdims=True) * (1.0 / d) + EPS)
            inv_scr[rows, :] = jnp.broadcast_to(inv, (IN_STAT_RB, LANES))
            return c
        lax.fori_loop(0, IN_TM // IN_STAT_RB, stats, 0)

        def apply(rb, c):
            rows = pl.ds(pl.multiple_of(rb * IN_NORM_RB, IN_NORM_RB), IN_NORM_RB)
            inv = inv_scr[rows, :]
            for k in range(d // LANES):
                cols = slice(k * LANES, (k + 1) * LANES)
                y = (x_ref[rows, cols] * inv) * gain_scr[:, cols] + shift_scr[:, cols]
                h_scr[rows, cols] = y.astype(BF16)
            return c
        lax.fori_loop(0, IN_TM // IN_NORM_RB, apply, 0, unroll=2)

    def run(epilogue):
        def body(rb, c):
            rows = pl.ds(pl.multiple_of(rb * IN_RB, IN_RB), IN_RB)
            acc = jnp.dot(h_scr[rows, :], w_ref[...], preferred_element_type=F32)
            o_ref[rows, :] = epilogue(acc).astype(o_ref.dtype)
            return c
        lax.fori_loop(0, nrb, body, 0)

    @pl.when(j < GATE_TILES)
    def _():
        run(_sigmoid)

    @pl.when((j == U_TILE) | (j == V_TILE))
    def _():
        run(_gelu)

    @pl.when(j == Q_TILE)
    def _():
        run(lambda a: a * np.float32(ATT_HEAD_DIM ** -0.5 * LOG2E))

    @pl.when(j > Q_TILE)
    def _():
        run(lambda a: a)


def _in_proj(x2d, mod3, norm_g, w_in_bf16, seq):
    n, d = x2d.shape
    cols = w_in_bf16.shape[1]
    n_tiles = cols // COL_TILE
    tiles_per_seq = seq // IN_TM
    return pl.pallas_call(
        _in_proj_kernel,
        grid=(n // IN_TM, n_tiles),
        in_specs=[pl.BlockSpec((1, N_MOD, d), lambda i, j: (i // tiles_per_seq, 0, 0)),
                  pl.BlockSpec((1, d), lambda i, j: (0, 0)),
                  pl.BlockSpec((IN_TM, d), lambda i, j: (i, 0)),
                  pl.BlockSpec((d, COL_TILE), lambda i, j: (0, (j + n_tiles - GATE_TILES) % n_tiles))],
        out_specs=pl.BlockSpec((IN_TM, COL_TILE), lambda i, j: (i, j)),
        out_shape=jax.ShapeDtypeStruct((n, cols), BF16),
        scratch_shapes=[pltpu.VMEM((IN_TM, d), BF16), pltpu.VMEM((IN_NORM_RB, d), F32),
                        pltpu.VMEM((IN_NORM_RB, d), F32), pltpu.VMEM((IN_TM, LANES), F32)],
        compiler_params=_params(("arbitrary", "arbitrary")),
        name="in_proj",
    )(mod3, norm_g.reshape(1, d), x2d, w_in_bf16)


GM_TM = 512


def _gmlp_kernel(u_ref, v_ref, lng_ref, lnb_ref, ws_ref, bs_ref, o_ref):
    t = lax.broadcasted_iota(I32, (GMLP_BLOCK, GMLP_BLOCK), 0)
    s = lax.broadcasted_iota(I32, (GMLP_BLOCK, GMLP_BLOCK), 1)
    causal = (s // CHUNK) <= (t // CHUNK)
    lng = lng_ref[...]
    lnb = lnb_ref[...]
    for blk in range(GM_TM // GMLP_BLOCK):
        rows = slice(blk * GMLP_BLOCK, (blk + 1) * GMLP_BLOCK)
        v = v_ref[rows, :].astype(F32)
        mu = jnp.mean(v, axis=-1, keepdims=True)
        vc = v - mu
        var = jnp.mean(vc * vc, axis=-1, keepdims=True)
        vln = ((vc * lax.rsqrt(var + EPS)) * lng + lnb).astype(BF16)
        for g in range(GMLP_GROUPS):
            cols = slice(g * LANES, (g + 1) * LANES)
            w = jnp.where(causal, ws_ref[g], 0.0).astype(BF16)
            mixed = jnp.dot(w, vln[:, cols], preferred_element_type=F32) + bs_ref[g]
            o_ref[rows, cols] = (u_ref[rows, cols].astype(F32) * mixed).astype(o_ref.dtype)


def _gmlp(proj, ln_g, ln_b, w_s, b_s):
    n = proj.shape[0]
    return pl.pallas_call(
        _gmlp_kernel,
        grid=(n // GM_TM,),
        in_specs=[pl.BlockSpec((GM_TM, GMLP_WIDTH), lambda i: (i, U_TILE)),
                  pl.BlockSpec((GM_TM, GMLP_WIDTH), lambda i: (i, V_TILE)),
                  pl.BlockSpec((1, GMLP_WIDTH), lambda i: (0, 0)),
                  pl.BlockSpec((1, GMLP_WIDTH), lambda i: (0, 0)),
                  pl.BlockSpec((GMLP_GROUPS, GMLP_BLOCK, GMLP_BLOCK), lambda i: (0, 0, 0)),
                  pl.BlockSpec((GMLP_GROUPS, GMLP_BLOCK, 1), lambda i: (0, 0, 0))],
        out_specs=pl.BlockSpec((GM_TM, GMLP_WIDTH), lambda i: (i, 0)),
        out_shape=jax.ShapeDtypeStruct((n, GMLP_WIDTH), BF16),
        compiler_params=_params(("arbitrary",)),
        name="gmlp",
    )(proj, proj, ln_g.reshape(1, -1), ln_b.reshape(1, -1), w_s,
      b_s.reshape(GMLP_GROUPS, GMLP_BLOCK, 1))


def _band_bias(rel_table):
    heads = rel_table.shape[0]
    r = np.arange(ATT_SUB)[:, None]
    w = np.arange(ATT_WIN)[None, :]
    j = w // CHUNK - r // CHUNK
    in_band = (j >= 0) & (j <= LEFT_CHUNKS)
    a = np.arange(ATT_NSUB)[:, None, None]
    in_seq = np.broadcast_to(w[None] >= ATT_QBLK - a * ATT_SUB, (ATT_NSUB, ATT_SUB, ATT_WIN))
    visible = np.concatenate([in_band[None], in_band[None] & in_seq])
    far = LEFT_CHUNKS * CHUNK + ATT_SUB - 1
    n_clipped = far - MAX_REL + 1
    table = rel_table.astype(F32) * np.float32(LOG2E)
    lo = MAX_REL - (ATT_WIN - 1 - LEFT_CHUNKS * CHUNK)
    diag = jnp.concatenate([jnp.broadcast_to(table[:, 2 * MAX_REL:], (heads, n_clipped)),
                            jnp.flip(table[:, lo:2 * MAX_REL], axis=1),
                            jnp.zeros((heads, 1), F32)], axis=1)
    span = diag.shape[1] - 1
    shifted = jnp.tile(diag, (1, ATT_SUB))[:, :ATT_SUB * span].reshape(heads, ATT_SUB, span)
    bias = shifted[:, :, ATT_SUB - 1:ATT_SUB - 1 + ATT_WIN]
    return jnp.where(jnp.asarray(visible)[None], bias[:, None], NEG_INF)


def _attn_kernel(q_ref, kp_ref, kc_ref, vp_ref, vc_ref, bias_ref, o_ref, k_scr, v_scr, s_scr, *, blocks_per_seq):
    first = pl.program_id(1) % blocks_per_seq == 0
    k_scr[0:ATT_QBLK, :] = kp_ref[...]
    k_scr[ATT_QBLK:, :] = kc_ref[...]
    v_scr[0:ATT_QBLK, :] = vp_ref[...]
    v_scr[ATT_QBLK:, :] = vc_ref[...]
    lane = lax.broadcasted_iota(I32, (ATT_SUB, LANES), 1)
    low = lane < ATT_HEAD_DIM

    units = [(pair, a) for pair in range(ATT_PAIRS) for a in range(ATT_NSUB)]
    for u, (pair, a) in enumerate(units):
        lanes = slice(pair * LANES, (pair + 1) * LANES)
        q = q_ref[a * ATT_SUB:(a + 1) * ATT_SUB, lanes]
        zero = jnp.zeros_like(q)
        q2 = jnp.concatenate([jnp.where(low, q, zero), jnp.where(low, zero, q)], axis=0)
        win = slice(a * ATT_SUB, a * ATT_SUB + ATT_WIN)
        s = lax.dot_general(q2, k_scr[win, lanes], (((1,), (1,)), ((), ())), preferred_element_type=F32)
        slab = jnp.where(first, a + 1, 0)
        bias = bias_ref[2 * pair:2 * pair + 2, pl.ds(slab, 1)]
        s_scr[u] = s + bias.reshape(2 * ATT_SUB, ATT_WIN)

    for u, (pair, a) in enumerate(units):
        lanes = slice(pair * LANES, (pair + 1) * LANES)
        rows = slice(a * ATT_SUB, (a + 1) * ATT_SUB)
        win = slice(a * ATT_SUB, a * ATT_SUB + ATT_WIN)
        s = s_scr[u]
        m = jnp.max(s, axis=-1, keepdims=True)
        p = jnp.exp2(s - m)
        l = jnp.sum(p, axis=-1, keepdims=True)
        pv = jnp.dot(p.astype(BF16), v_scr[win, lanes], preferred_element_type=F32) / l
        o_ref[rows, lanes] = jnp.where(low, pv[:ATT_SUB], pv[ATT_SUB:]).astype(o_ref.dtype)


def _attn(proj, rel_table, seq):
    n = proj.shape[0]
    bps = seq // ATT_QBLK
    pairs = ATT_WIDTH // LANES
    width = ATT_PAIRS * LANES
    qc, kc, vc = (Q_TILE * COL_TILE // width, K_TILE * COL_TILE // width, VB_TILE * COL_TILE // width)
    bias = _band_bias(rel_table)

    def prev(i):
        return jnp.where(i % bps == 0, i, i - 1)

    blk = (ATT_QBLK, width)
    return pl.pallas_call(
        functools.partial(_attn_kernel, blocks_per_seq=bps),
        grid=(pairs // ATT_PAIRS, n // ATT_QBLK),
        in_specs=[pl.BlockSpec(blk, lambda h, i: (i, qc + h)),
                  pl.BlockSpec(blk, lambda h, i: (prev(i), kc + h)),
                  pl.BlockSpec(blk, lambda h, i: (i, kc + h)),
                  pl.BlockSpec(blk, lambda h, i: (prev(i), vc + h)),
                  pl.BlockSpec(blk, lambda h, i: (i, vc + h)),
                  pl.BlockSpec((2 * ATT_PAIRS, 1 + ATT_NSUB, ATT_SUB, ATT_WIN), lambda h, i: (h, 0, 0, 0))],
        out_specs=pl.BlockSpec(blk, lambda h, i: (i, h)),
        out_shape=jax.ShapeDtypeStruct((n, ATT_WIDTH), BF16),
        scratch_shapes=[pltpu.VMEM((2 * ATT_QBLK, width), BF16),
                        pltpu.VMEM((2 * ATT_QBLK, width), BF16),
                        pltpu.VMEM((ATT_PAIRS * ATT_NSUB, 2 * ATT_SUB, ATT_WIN), F32)],
        compiler_params=_params(("arbitrary", "arbitrary")),
        name="attn",
    )(proj, proj, proj, proj, proj, bias)


MG_TM = 256


def _merge_kernel(mod_ref, g_ref, x_ref, ya_ref, yb_ref, ga_ref, gb_ref, wa_ref, wb_ref, wo_ref, wr_ref,
                  x1_ref, h2_ref, lt_ref):
    gate1 = mod_ref[0, 2:3, :]
    shift2 = mod_ref[0, 3:4, :]
    scale2 = mod_ref[0, 4:5, :]
    ya = jnp.dot(ya_ref[...], wa_ref[...], preferred_element_type=F32)
    yb = jnp.dot(yb_ref[...], wb_ref[...], preferred_element_type=F32)
    m = ga_ref[...].astype(F32) * ya + gb_ref[...].astype(F32) * yb
    mixed = jnp.dot(m.astype(BF16), wo_ref[...], preferred_element_type=F32)
    x1 = x_ref[...] + gate1 * mixed
    x1_ref[...] = x1
    h2 = _rms_mod(x1, g_ref[...], scale2, shift2).astype(BF16)
    h2_ref[...] = h2.reshape(h2_ref.shape)
    lt_ref[...] = lax.dot_general(wr_ref[...], h2, (((1,), (1,)), ((), ())), preferred_element_type=F32)


def _merge(x2d, mod3, norm_g, ya, yb, proj, wa, wb, wo, wr_t, seq):
    n, d = x2d.shape
    tps = seq // MG_TM
    const = lambda shape: pl.BlockSpec(shape, lambda i: (0,) * len(shape), pipeline_mode=pl.Buffered(1))
    return pl.pallas_call(
        _merge_kernel,
        grid=(n // MG_TM,),
        in_specs=[pl.BlockSpec((1, N_MOD, d), lambda i: (i // tps, 0, 0)),
                  pl.BlockSpec((1, d), lambda i: (0, 0)),
                  pl.BlockSpec((MG_TM, d), lambda i: (i, 0)),
                  pl.BlockSpec((MG_TM, GMLP_WIDTH), lambda i: (i, 0)),
                  pl.BlockSpec((MG_TM, ATT_WIDTH), lambda i: (i, 0)),
                  pl.BlockSpec((MG_TM, d), lambda i: (i, 0)),
                  pl.BlockSpec((MG_TM, d), lambda i: (i, 1)),
                  const((GMLP_WIDTH, d)), const((ATT_WIDTH, d)), const((d, d)),
                  const((ROUTER_ROWS, d))],
        out_specs=[pl.BlockSpec((MG_TM, d), lambda i: (i, 0)),
                   pl.BlockSpec((MG_TM, TOKEN_ROWS, LANES), lambda i: (i, 0, 0)),
                   pl.BlockSpec((ROUTER_ROWS, MG_TM), lambda i: (0, i))],
        out_shape=[jax.ShapeDtypeStruct((n, d), F32),
                   jax.ShapeDtypeStruct((n, TOKEN_ROWS, LANES), BF16),
                   jax.ShapeDtypeStruct((ROUTER_ROWS, n), F32)],
        compiler_params=_params(("arbitrary",)),
        name="merge",
    )(mod3, norm_g.reshape(1, d), x2d, ya, yb, proj, proj, wa, wb, wo, wr_t)


RT_TN = 512


def _first_argmax(vals, vmax, nrows):
    rows = lax.broadcasted_iota(I32, vals.shape, 0)
    return jnp.min(jnp.where(vals == vmax, rows, nrows), axis=0, keepdims=True)


def _route_kernel(lt_ref, e_ref, r_ref, w_ref, cnt_ref, carry_scr):
    @pl.when(pl.program_id(0) == 0)
    def _():
        carry_scr[...] = jnp.zeros_like(carry_scr)

    gl = lt_ref[0:N_GROUPS, :]
    gmax = jnp.max(gl, axis=0, keepdims=True)
    gidx = _first_argmax(gl, gmax, N_GROUPS)
    gw = 1.0 / jnp.sum(jnp.exp(gl - gmax), axis=0, keepdims=True)

    esel = lt_ref[SUBLANES:SUBLANES + EXPERTS_PER_GROUP, :]
    for g in range(1, N_GROUPS):
        lo = SUBLANES + g * EXPERTS_PER_GROUP
        esel = jnp.where(gidx == g, lt_ref[lo:lo + EXPERTS_PER_GROUP, :], esel)
    rows8 = lax.broadcasted_iota(I32, esel.shape, 0)
    m1 = jnp.max(esel, axis=0, keepdims=True)
    i1 = _first_argmax(esel, m1, EXPERTS_PER_GROUP)
    rest = jnp.where(rows8 == i1, -jnp.inf, esel)
    m2 = jnp.max(rest, axis=0, keepdims=True)
    i2 = _first_argmax(rest, m2, EXPERTS_PER_GROUP)
    z = jnp.exp(m2 - m1)
    w_top = 1.0 / (1.0 + z)
    e0 = gidx * EXPERTS_PER_GROUP + i1
    e1 = gidx * EXPERTS_PER_GROUP + i2
    e_ref[0:1, :] = e0
    e_ref[1:2, :] = e1
    w_ref[0:1, :] = gw * w_top
    w_ref[1:2, :] = gw * (z * w_top)

    rows_e = lax.broadcasted_iota(I32, (N_EXPERTS, RT_TN), 0)
    oh0 = rows_e == e0
    oh1 = rows_e == e1
    oh = jnp.where(oh0 | oh1, 1.0, 0.0)
    src = lax.broadcasted_iota(I32, (RT_TN, RT_TN), 0)
    dst = lax.broadcasted_iota(I32, (RT_TN, RT_TN), 1)
    before = jnp.where(src < dst, 1.0, 0.0).astype(BF16)
    carry = carry_scr[...]
    prefix = jnp.dot(oh.astype(BF16), before, preferred_element_type=F32) + carry[:, 0:1]
    r_ref[0:1, :] = jnp.sum(jnp.where(oh0, prefix, 0.0), axis=0, keepdims=True).astype(I32)
    r_ref[1:2, :] = jnp.sum(jnp.where(oh1, prefix, 0.0), axis=0, keepdims=True).astype(I32)
    carry = carry + jnp.sum(oh, axis=1, keepdims=True)
    carry_scr[...] = carry
    cnt_ref[...] = carry.astype(I32)


def _route(logits_t):
    n = logits_t.shape[1]
    slot = pl.BlockSpec((2, RT_TN), lambda i: (0, i))
    return pl.pallas_call(
        _route_kernel,
        grid=(n // RT_TN,),
        in_specs=[pl.BlockSpec((ROUTER_ROWS, RT_TN), lambda i: (0, i))],
        out_specs=[slot, slot, slot, pl.BlockSpec((N_EXPERTS, LANES), lambda i: (0, 0))],
        out_shape=[jax.ShapeDtypeStruct((2, n), I32),
                   jax.ShapeDtypeStruct((2, n), I32),
                   jax.ShapeDtypeStruct((2, n), F32),
                   jax.ShapeDtypeStruct((N_EXPERTS, LANES), I32)],
        scratch_shapes=[pltpu.VMEM((N_EXPERTS, LANES), F32)],
        compiler_params=_params(("arbitrary",)),
        name="route",
    )(logits_t)


DP_TM = 2048
DMA_UNROLL = 8
PAD_CHUNK = 32


def _token_copy(src_ref, src_row, dst_ref, dst_row, sem):
    return pltpu.make_async_copy(src_ref.at[src_row], dst_ref.at[dst_row], sem)


def _dispatch_kernel(pos0_ref, pos1_ref, pad_start_ref, pad_n_ref, nt_ref, h_ref, xs_ref, zero_scr, sem, zsem, tsem,
                     csem):
    i = pl.program_id(0)
    n_tiles_max = xs_ref.shape[0] // EXPERT_TILE

    def tail_copy(t):
        return pltpu.make_async_copy(zero_scr, xs_ref.at[pl.ds(t * EXPERT_TILE, EXPERT_TILE)], tsem)

    @pl.when(i == 0)
    def _():
        zero_scr[...] = jnp.zeros_like(zero_scr)

        def chunk_copy(row):
            return pltpu.make_async_copy(zero_scr.at[pl.ds(0, PAD_CHUNK)], xs_ref.at[pl.ds(row, PAD_CHUNK)], csem)

        for e in range(N_EXPERTS):
            start = pad_start_ref[e]
            n_chunks = pad_n_ref[e] // PAD_CHUNK

            def issue_chunk(k, c, start=start):
                chunk_copy(start + k * PAD_CHUNK).start()
                return c
            lax.fori_loop(0, n_chunks, issue_chunk, 0)

            def issue(r, c, start=start):
                _token_copy(zero_scr, 0, xs_ref, start + r, zsem).start()
                return c
            lax.fori_loop(n_chunks * PAD_CHUNK, pad_n_ref[e], issue, 0)

        def issue_tail(t, c):
            tail_copy(t).start()
            return c
        lax.fori_loop(nt_ref[0], n_tiles_max, issue_tail, 0)

        for e in range(N_EXPERTS):
            n_chunks = pad_n_ref[e] // PAD_CHUNK

            def drain_chunk(k, c):
                chunk_copy(0).wait()
                return c
            lax.fori_loop(0, n_chunks, drain_chunk, 0)

            def drain(r, c):
                _token_copy(zero_scr, 0, xs_ref, 0, zsem).wait()
                return c
            lax.fori_loop(n_chunks * PAD_CHUNK, pad_n_ref[e], drain, 0)

        def drain_tail(t, c):
            tail_copy(t).wait()
            return c
        lax.fori_loop(nt_ref[0], n_tiles_max, drain_tail, 0)

    base = i * DP_TM

    def issue(r, c):
        _token_copy(h_ref, r, xs_ref, pos0_ref[base + r], sem).start(priority=0)
        _token_copy(h_ref, r, xs_ref, pos1_ref[base + r], sem).start(priority=1)
        return c
    lax.fori_loop(0, DP_TM, issue, 0, unroll=DMA_UNROLL)

    for _ in range(2):
        pltpu.make_async_copy(h_ref, xs_ref.at[pl.ds(0, DP_TM)], sem).wait()


def _dispatch(h2, pos0, pos1, pad_start, pad_n, n_tiles, rows_out):
    n = h2.shape[0]
    grid_spec = pltpu.PrefetchScalarGridSpec(
        num_scalar_prefetch=5,
        grid=(n // DP_TM,),
        in_specs=[pl.BlockSpec((DP_TM, TOKEN_ROWS, LANES), lambda i, *_: (i, 0, 0))],
        out_specs=pl.BlockSpec(memory_space=pl.ANY),
        scratch_shapes=[pltpu.VMEM((EXPERT_TILE, TOKEN_ROWS, LANES), BF16),
                        pltpu.SemaphoreType.DMA(()), pltpu.SemaphoreType.DMA(()),
                        pltpu.SemaphoreType.DMA(()), pltpu.SemaphoreType.DMA(())],
    )
    return pl.pallas_call(
        _dispatch_kernel,
        grid_spec=grid_spec,
        out_shape=jax.ShapeDtypeStruct((rows_out, TOKEN_ROWS, LANES), BF16),
        compiler_params=pltpu.CompilerParams(dimension_semantics=("arbitrary",),
                                             vmem_limit_bytes=VMEM_LIMIT, has_side_effects=True),
        name="dispatch",
    )(pos0, pos1, pad_start, pad_n, n_tiles, h2)


EXPERT_CAST_ROWS = 256


def _cast_rows(src_ref, dst_ref):
    rows_total = dst_ref.shape[0]

    def body(r, c):
        rows = pl.ds(pl.multiple_of(r * EXPERT_CAST_ROWS, EXPERT_CAST_ROWS), EXPERT_CAST_ROWS)
        dst_ref[rows, :] = src_ref[rows, :].astype(BF16)
        return c
    lax.fori_loop(0, rows_total // EXPERT_CAST_ROWS, body, 0)


def _experts_kernel(te_ref, tr_ref, nt_ref, first_ref, slot_ref, next_ref, xs_ref, w1_hbm, w3_hbm, w2_hbm, ys_ref,
                    w1_stage, w3_stage, w2_stage, w1_scr, w3_scr, w2_scr, sems):
    i = pl.program_id(0)

    def fetch(expert, slot):
        return (pltpu.make_async_copy(w1_hbm.at[expert], w1_stage.at[slot], sems.at[slot]),
                pltpu.make_async_copy(w3_hbm.at[expert], w3_stage.at[slot], sems.at[slot]),
                pltpu.make_async_copy(w2_hbm.at[expert], w2_stage.at[slot], sems.at[slot]))

    @pl.when(i == 0)
    def _():
        for copy in fetch(te_ref[0], 0):
            copy.start()

    @pl.when(first_ref[i] == 1)
    def _():
        slot = slot_ref[i]
        for copy in fetch(te_ref[i], slot):
            copy.wait()

        @pl.when(next_ref[i] >= 0)
        def _():
            for copy in fetch(next_ref[i], 1 - slot):
                copy.start()

        _cast_rows(w1_stage.at[slot], w1_scr)
        _cast_rows(w3_stage.at[slot], w3_scr)
        _cast_rows(w2_stage.at[slot], w2_scr)

    @pl.when(i < nt_ref[0])
    def _():
        x = xs_ref[...].reshape(EXPERT_TILE, D_MODEL)
        a = jnp.dot(x, w1_scr[...], preferred_element_type=F32)
        b = jnp.dot(x, w3_scr[...], preferred_element_type=F32)
        act = (a * jax.nn.sigmoid(a)) * b
        y = jnp.dot(act.astype(BF16), w2_scr[...], preferred_element_type=F32)
        ys_ref[...] = y.astype(BF16).reshape(ys_ref.shape)

    @pl.when(i >= nt_ref[0])
    def _():
        ys_ref[...] = jnp.zeros_like(ys_ref)


def _experts(xs, tile_expert, tile_row, n_tiles, tile_first, tile_slot, tile_next, w1, w3, w2):
    rows = xs.shape[0]
    _, d, f = w1.shape
    tile = (EXPERT_TILE, TOKEN_ROWS, LANES)
    hbm = pl.BlockSpec(memory_space=pl.ANY)
    grid_spec = pltpu.PrefetchScalarGridSpec(
        num_scalar_prefetch=6,
        grid=(rows // EXPERT_TILE,),
        in_specs=[pl.BlockSpec(tile, lambda i, te, tr, *_: (tr[i], 0, 0)), hbm, hbm, hbm],
        out_specs=pl.BlockSpec(tile, lambda i, *_: (i, 0, 0)),
        scratch_shapes=[pltpu.VMEM((2, d, f), F32), pltpu.VMEM((2, d, f), F32), pltpu.VMEM((2, f, d), F32),
                        pltpu.VMEM((d, f), BF16), pltpu.VMEM((d, f), BF16), pltpu.VMEM((f, d), BF16),
                        pltpu.SemaphoreType.DMA((2,))],
    )
    return pl.pallas_call(
        _experts_kernel,
        grid_spec=grid_spec,
        out_shape=jax.ShapeDtypeStruct(xs.shape, BF16),
        compiler_params=_params(("arbitrary",)),
        name="experts",
    )(tile_expert, tile_row, n_tiles, tile_first, tile_slot, tile_next, xs, w1, w3, w2)


CB_TM = 256


def _combine_kernel(pos0_ref, pos1_ref, mod_ref, fg_ref, x1_ref, w0_ref, w1_ref, ys_ref, o_ref,
                    y0_scr, y1_scr, sems):
    i = pl.program_id(0)
    slot = i % 2

    def gather(step, buf):
        base = step * CB_TM

        def issue(r, c):
            _token_copy(ys_ref, pos0_ref[base + r], y0_scr.at[buf], r, sems.at[buf]).start(priority=0)
            _token_copy(ys_ref, pos1_ref[base + r], y1_scr.at[buf], r, sems.at[buf]).start(priority=1)
            return c
        lax.fori_loop(0, CB_TM, issue, 0, unroll=DMA_UNROLL)

    @pl.when(i == 0)
    def _():
        gather(i, slot)

    @pl.when(i + 1 < pl.num_programs(0))
    def _():
        gather(i + 1, 1 - slot)

    pltpu.make_async_copy(ys_ref.at[pl.ds(0, CB_TM)], y0_scr.at[slot], sems.at[slot]).wait()
    pltpu.make_async_copy(ys_ref.at[pl.ds(0, CB_TM)], y1_scr.at[slot], sems.at[slot]).wait()

    gate2 = mod_ref[0, 5:6, :]
    y0 = y0_scr[slot].reshape(CB_TM, D_MODEL).astype(F32)
    y1 = y1_scr[slot].reshape(CB_TM, D_MODEL).astype(F32)
    y = w0_ref[...] * y0 + w1_ref[...] * y1
    x2 = x1_ref[...] + gate2 * y
    o_ref[...] = (x2 * lax.rsqrt(jnp.mean(x2 * x2, axis=-1, keepdims=True) + EPS)) * fg_ref[...]


def _combine(x1, mod3, final_g, ys, pos0, pos1, cw0, cw1, seq):
    n, d = x1.shape
    tps = seq // CB_TM
    grid_spec = pltpu.PrefetchScalarGridSpec(
        num_scalar_prefetch=2,
        grid=(n // CB_TM,),
        in_specs=[pl.BlockSpec((1, N_MOD, d), lambda i, *_: (i // tps, 0, 0)),
                  pl.BlockSpec((1, d), lambda i, *_: (0, 0)),
                  pl.BlockSpec((CB_TM, d), lambda i, *_: (i, 0)),
                  pl.BlockSpec((CB_TM, 1), lambda i, *_: (i, 0)),
                  pl.BlockSpec((CB_TM, 1), lambda i, *_: (i, 0)),
                  pl.BlockSpec(memory_space=pl.ANY)],
        out_specs=pl.BlockSpec((CB_TM, d), lambda i, *_: (i, 0)),
        scratch_shapes=[pltpu.VMEM((2, CB_TM, TOKEN_ROWS, LANES), BF16),
                        pltpu.VMEM((2, CB_TM, TOKEN_ROWS, LANES), BF16),
                        pltpu.SemaphoreType.DMA((2,))],
    )
    return pl.pallas_call(
        _combine_kernel,
        grid_spec=grid_spec,
        out_shape=jax.ShapeDtypeStruct((n, d), F32),
        compiler_params=_params(("arbitrary",)),
        name="combine",
    )(pos0, pos1, mod3, final_g.reshape(1, d), x1, cw0, cw1, ys)


def _layer(x2d, c, seq, ada_w, ada_b, norm1_g, w_in, gmlp_ln_g, gmlp_ln_b, gmlp_w_s, gmlp_b_s, rel_bias,
           w_branch_a, w_branch_b, w_out, norm2_g, w_group, w_expert, w1, w3, w2, final_g):
    n, d = x2d.shape
    nb = c.shape[0]
    mod3 = _ada(c, ada_w, ada_b).reshape(nb, N_MOD, d)

    proj = _in_proj(x2d, mod3, norm1_g, w_in.astype(BF16), seq)
    ya = _gmlp(proj, gmlp_ln_g, gmlp_ln_b, gmlp_w_s, gmlp_b_s)
    yb = _attn(proj, rel_bias, seq)

    wr_t = jnp.concatenate([w_group.T, jnp.zeros((SUBLANES - N_GROUPS, d), F32),
                            w_expert.transpose(0, 2, 1).reshape(N_EXPERTS, d)], axis=0).astype(BF16)
    x1, h2, logits_t = _merge(x2d, mod3, norm2_g, ya, yb, proj, w_branch_a.astype(BF16),
                              w_branch_b.astype(BF16), w_out.astype(BF16), wr_t, seq)

    eidx, rank, cw, counts = _route(logits_t)
    counts = counts[:, 0]
    padded = ((counts + EXPERT_TILE - 1) // EXPERT_TILE) * EXPERT_TILE
    ends = jnp.cumsum(padded)
    offs = ends - padded
    experts = jnp.arange(N_EXPERTS, dtype=I32)
    pos = jnp.sum(jnp.where(eidx[:, :, None] == experts, offs, 0), axis=-1) + rank
    rows_out = 2 * n + N_EXPERTS * EXPERT_TILE
    n_tiles_max = rows_out // EXPERT_TILE
    n_tiles = (ends[-1] // EXPERT_TILE).astype(I32)
    tile_row = jnp.minimum(jnp.arange(n_tiles_max, dtype=I32), n_tiles - 1)
    tile_expert = jnp.minimum(jnp.sum(ends[None, :] <= (tile_row * EXPERT_TILE)[:, None], axis=-1),
                              N_EXPERTS - 1).astype(I32)

    tiles = jnp.arange(n_tiles_max, dtype=I32)
    prev_expert = jnp.concatenate([jnp.full((1,), -1, I32), tile_expert[:-1]])
    tile_first = ((tiles < n_tiles) & (tile_expert != prev_expert)).astype(I32)
    tile_slot = ((jnp.cumsum(tile_first) - 1) % 2).astype(I32)
    later_nonempty = (experts[None, :] > experts[:, None]) & (padded > 0)[None, :]
    next_nonempty = jnp.min(jnp.where(later_nonempty, experts[None, :], N_EXPERTS), axis=1)
    next_nonempty = jnp.where(next_nonempty == N_EXPERTS, -1, next_nonempty)
    tile_next = jnp.sum(jnp.where(tile_expert[:, None] == experts, next_nonempty, 0), axis=-1).astype(I32)

    n_tiles = n_tiles.reshape(1)
    xs = _dispatch(h2, pos[0], pos[1], (offs + counts).astype(I32), (padded - counts).astype(I32), n_tiles,
                   rows_out)
    ys = _experts(xs, tile_expert, tile_row, n_tiles, tile_first, tile_slot, tile_next,
                  w1.reshape(N_EXPERTS, d, D_EXPERT), w3.reshape(N_EXPERTS, d, D_EXPERT),
                  w2.reshape(N_EXPERTS, D_EXPERT, d))
    return _combine(x1, mod3, final_g, ys, pos[0], pos[1], cw[0].reshape(n, 1), cw[1].reshape(n, 1), seq)


def kernel(x, c, ada_w, ada_b, norm1_g, w_in, gmlp_ln_g, gmlp_ln_b, gmlp_w_s, gmlp_b_s, rel_bias, w_branch_a,
           w_branch_b, w_out, norm2_g, w_group, w_expert, w1, w3, w2, final_g):
    b, s, d = x.shape
    out = _layer(x.reshape(b * s, d), c, s, ada_w[0], ada_b[0], norm1_g[0], w_in[0], gmlp_ln_g[0], gmlp_ln_b[0],
                 gmlp_w_s[0], gmlp_b_s[0], rel_bias[0], w_branch_a[0], w_branch_b[0], w_out[0], norm2_g[0],
                 w_group[0], w_expert[0], w1[0], w3[0], w2[0], final_g)
    return out.reshape(b, s, d)
```

```python
import functools

import numpy as np
import jax
import jax.numpy as jnp
from jax import lax
from jax.experimental import pallas as pl
from jax.experimental.pallas import tpu as pltpu

F32 = jnp.float32
BF16 = jnp.bfloat16
I32 = jnp.int32

D_MODEL = 2048
CHUNK = 64
EPS = 1e-6
NEG_INF = -1e30
LOG2E = float(np.log2(np.e))
GMLP_BLOCK = 128
GMLP_GROUPS = 8
GMLP_WIDTH = 1024
ATT_HEADS = 16
ATT_HEAD_DIM = 64
ATT_WIDTH = ATT_HEADS * ATT_HEAD_DIM
LEFT_CHUNKS = 8
MAX_REL = 256
N_GROUPS = 4
EXPERTS_PER_GROUP = 8
N_EXPERTS = N_GROUPS * EXPERTS_PER_GROUP
D_EXPERT = 512
N_MOD = 6

LANES = 128
SUBLANES = 8
VMEM_LIMIT = 56 * 1024 * 1024

COL_TILE = 1024
PROJ_COLS = 2 * D_MODEL + 2 * GMLP_WIDTH + 3 * ATT_WIDTH
GATE_TILES = 2 * D_MODEL // COL_TILE
U_TILE = GATE_TILES
V_TILE = GATE_TILES + 1
Q_TILE = GATE_TILES + 2
K_TILE = GATE_TILES + 3
VB_TILE = GATE_TILES + 4

ROUTER_ROWS = SUBLANES + N_EXPERTS

ATT_QBLK = 512
ATT_SUB = 2 * CHUNK
ATT_NSUB = ATT_QBLK // ATT_SUB
ATT_PAIRS = 2
ATT_WIN = ATT_SUB + LEFT_CHUNKS * CHUNK

EXPERT_TILE = 256
TOKEN_ROWS = D_MODEL // LANES


def _params(sem, vmem=VMEM_LIMIT):
    return pltpu.CompilerParams(dimension_semantics=sem, vmem_limit_bytes=vmem)


def _rms_mod(x, g, scale, shift):
    y = x * lax.rsqrt(jnp.mean(x * x, axis=-1, keepdims=True) + EPS)
    return (y * g) * (1.0 + scale) + shift


def _ada_kernel(cb_ref, w_ref, b_ref, o_ref, s_scr):
    nb, d, _ = cb_ref.shape
    tn = w_ref.shape[1]
    reps = tn // LANES

    @pl.when(pl.program_id(0) == 0)
    def _():
        cb = cb_ref[...]
        s_scr[...] = cb * jax.nn.sigmoid(cb)

    def body(i, accs):
        r = pl.ds(pl.multiple_of(i * SUBLANES, SUBLANES), SUBLANES)
        w8 = w_ref[r, :]
        return tuple(accs[b] + w8 * jnp.concatenate([s_scr[b, r, :]] * reps, axis=1) for b in range(nb))

    init = tuple(jnp.zeros((SUBLANES, tn), F32) for _ in range(nb))
    accs = lax.fori_loop(0, d // SUBLANES, body, init, unroll=4)
    for b in range(nb):
        o_ref[b:b + 1, :] = jnp.sum(accs[b], axis=0, keepdims=True) + b_ref[...]


def _ada(c, ada_w, ada_b):
    nb, d = c.shape
    n = ada_w.shape[1]
    tn = 1024
    cb = jnp.broadcast_to(c[:, :, None], (nb, d, LANES))
    return pl.pallas_call(
        _ada_kernel,
        grid=(n // tn,),
        in_specs=[pl.BlockSpec((nb, d, LANES), lambda j: (0, 0, 0)),
                  pl.BlockSpec((d, tn), lambda j: (0, j)),
                  pl.BlockSpec((1, tn), lambda j: (0, j))],
        out_specs=pl.BlockSpec((nb, tn), lambda j: (0, j)),
        out_shape=jax.ShapeDtypeStruct((nb, n), F32),
        scratch_shapes=[pltpu.VMEM((nb, d, LANES), F32)],
        compiler_params=_params(("arbitrary",)),
        name="ada",
    )(cb, ada_w, ada_b.reshape(1, n))


IN_TM = 1024
IN_RB = 1024
IN_STAT_RB = 128
IN_NORM_RB = 16


def _gelu(a):
    return 0.5 * a * (1.0 + lax.erf(a * np.float32(np.sqrt(0.5))))


def _sigmoid(a):
    return 0.5 * jnp.tanh(0.5 * a) + 0.5


def _in_proj_kernel(mod_ref, g_ref, x_ref, w_ref, o_ref, h_scr, gain_scr, shift_scr, inv_scr):
    j = pl.program_id(1)
    nrb = IN_TM // IN_RB

    @pl.when(j == 0)
    def _():
        d = x_ref.shape[1]
        gain_scr[...] = jnp.broadcast_to(g_ref[...] * (1.0 + mod_ref[0, 1:2, :]), gain_scr.shape)
        shift_scr[...] = jnp.broadcast_to(mod_ref[0, 0:1, :], shift_scr.shape)

        def stats(rb, c):
            rows = pl.ds(pl.multiple_of(rb * IN_STAT_RB, IN_STAT_RB), IN_STAT_RB)
            sq = jnp.zeros((IN_STAT_RB, LANES), F32)
            for k in range(d // LANES):
                xk = x_ref[rows, k * LANES:(k + 1) * LANES]
                sq = sq + xk * xk
            inv = lax.rsqrt(jnp.sum(sq, axis=-1, keepdims=True) * (1.0 / d) + EPS)
            inv_scr[rows, :] = jnp.broadcast_to(inv, (IN_STAT_RB, LANES))
            return c
        lax.fori_loop(0, IN_TM // IN_STAT_RB, stats, 0)

        def apply(rb, c):
            rows = pl.ds(pl.multiple_of(rb * IN_NORM_RB, IN_NORM_RB), IN_NORM_RB)
            inv = inv_scr[rows, :]
            for k in range(d // LANES):
                cols = slice(k * LANES, (k + 1) * LANES)
                y = (x_ref[rows, cols] * inv) * gain_scr[:, cols] + shift_scr[:, cols]
                h_scr[rows, cols] = y.astype(BF16)
            return c
        lax.fori_loop(0, IN_TM // IN_NORM_RB, apply, 0, unroll=2)

    def run(epilogue):
        def body(rb, c):
            rows = pl.ds(pl.multiple_of(rb * IN_RB, IN_RB), IN_RB)
            acc = jnp.dot(h_scr[rows, :], w_ref[...], preferred_element_type=F32)
            o_ref[rows, :] = epilogue(acc).astype(o_ref.dtype)
            return c
        lax.fori_loop(0, nrb, body, 0)

    @pl.when(j < GATE_TILES)
    def _():
        run(_sigmoid)

    @pl.when((j == U_TILE) | (j == V_TILE))
    def _():
        run(_gelu)

    @pl.when(j == Q_TILE)
    def _():
        run(lambda a: a * np.float32(ATT_HEAD_DIM ** -0.5 * LOG2E))

    @pl.when(j > Q_TILE)
    def _():
        run(lambda a: a)


def _in_proj(x2d, mod3, norm_g, w_in_bf16, seq):
    n, d = x2d.shape
    cols = w_in_bf16.shape[1]
    n_tiles = cols // COL_TILE
    tiles_per_seq = seq // IN_TM
    return pl.pallas_call(
        _in_proj_kernel,
        grid=(n // IN_TM, n_tiles),
        in_specs=[pl.BlockSpec((1, N_MOD, d), lambda i, j: (i // tiles_per_seq, 0, 0)),
                  pl.BlockSpec((1, d), lambda i, j: (0, 0)),
                  pl.BlockSpec((IN_TM, d), lambda i, j: (i, 0)),
                  pl.BlockSpec((d, COL_TILE), lambda i, j: (0, (j + n_tiles - GATE_TILES) % n_tiles))],
        out_specs=pl.BlockSpec((IN_TM, COL_TILE), lambda i, j: (i, j)),
        out_shape=jax.ShapeDtypeStruct((n, cols), BF16),
        scratch_shapes=[pltpu.VMEM((IN_TM, d), BF16), pltpu.VMEM((IN_NORM_RB, d), F32),
                        pltpu.VMEM((IN_NORM_RB, d), F32), pltpu.VMEM((IN_TM, LANES), F32)],
        compiler_params=_params(("arbitrary", "arbitrary")),
        name="in_proj",
    )(mod3, norm_g.reshape(1, d), x2d, w_in_bf16)


GM_TM = 512


def _gmlp_kernel(u_ref, v_ref, lng_ref, lnb_ref, ws_ref, bs_ref, o_ref):
    t = lax.broadcasted_iota(I32, (GMLP_BLOCK, GMLP_BLOCK), 0)
    s = lax.broadcasted_iota(I32, (GMLP_BLOCK, GMLP_BLOCK), 1)
    causal = (s // CHUNK) <= (t // CHUNK)
    lng = lng_ref[...]
    lnb = lnb_ref[...]
    for blk in range(GM_TM // GMLP_BLOCK):
        rows = slice(blk * GMLP_BLOCK, (blk + 1) * GMLP_BLOCK)
        v = v_ref[rows, :].astype(F32)
        mu = jnp.mean(v, axis=-1, keepdims=True)
        vc = v - mu
        var = jnp.mean(vc * vc, axis=-1, keepdims=True)
        vln = ((vc * lax.rsqrt(var + EPS)) * lng + lnb).astype(BF16)
        for g in range(GMLP_GROUPS):
            cols = slice(g * LANES, (g + 1) * LANES)
            w = jnp.where(causal, ws_ref[g], 0.0).astype(BF16)
            mixed = jnp.dot(w, vln[:, cols], preferred_element_type=F32) + bs_ref[g]
            o_ref[rows, cols] = (u_ref[rows, cols].astype(F32) * mixed).astype(o_ref.dtype)


def _gmlp(proj, ln_g, ln_b, w_s, b_s):
    n = proj.shape[0]
    return pl.pallas_call(
        _gmlp_kernel,
        grid=(n // GM_TM,),
        in_specs=[pl.BlockSpec((GM_TM, GMLP_WIDTH), lambda i: (i, U_TILE)),
                  pl.BlockSpec((GM_TM, GMLP_WIDTH), lambda i: (i, V_TILE)),
                  pl.BlockSpec((1, GMLP_WIDTH), lambda i: (0, 0)),
                  pl.BlockSpec((1, GMLP_WIDTH), lambda i: (0, 0)),
                  pl.BlockSpec((GMLP_GROUPS, GMLP_BLOCK, GMLP_BLOCK), lambda i: (0, 0, 0)),
                  pl.BlockSpec((GMLP_GROUPS, GMLP_BLOCK, 1), lambda i: (0, 0, 0))],
        out_specs=pl.BlockSpec((GM_TM, GMLP_WIDTH), lambda i: (i, 0)),
        out_shape=jax.ShapeDtypeStruct((n, GMLP_WIDTH), BF16),
        compiler_params=_params(("arbitrary",)),
        name="gmlp",
    )(proj, proj, ln_g.reshape(1, -1), ln_b.reshape(1, -1), w_s,
      b_s.reshape(GMLP_GROUPS, GMLP_BLOCK, 1))


def _band_bias(rel_table):
    heads = rel_table.shape[0]
    r = np.arange(ATT_SUB)[:, None]
    w = np.arange(ATT_WIN)[None, :]
    j = w // CHUNK - r // CHUNK
    in_band = (j >= 0) & (j <= LEFT_CHUNKS)
    a = np.arange(ATT_NSUB)[:, None, None]
    in_seq = np.broadcast_to(w[None] >= ATT_QBLK - a * ATT_SUB, (ATT_NSUB, ATT_SUB, ATT_WIN))
    visible = np.concatenate([in_band[None], in_band[None] & in_seq])
    far = LEFT_CHUNKS * CHUNK + ATT_SUB - 1
    n_clipped = far - MAX_REL + 1
    table = rel_table.astype(F32) * np.float32(LOG2E)
    lo = MAX_REL - (ATT_WIN - 1 - LEFT_CHUNKS * CHUNK)
    diag = jnp.concatenate([jnp.broadcast_to(table[:, 2 * MAX_REL:], (heads, n_clipped)),
                            jnp.flip(table[:, lo:2 * MAX_REL], axis=1),
                            jnp.zeros((heads, 1), F32)], axis=1)
    span = diag.shape[1] - 1
    shifted = jnp.tile(diag, (1, ATT_SUB))[:, :ATT_SUB * span].reshape(heads, ATT_SUB, span)
    bias = shifted[:, :, ATT_SUB - 1:ATT_SUB - 1 + ATT_WIN]
    return jnp.where(jnp.asarray(visible)[None], bias[:, None], NEG_INF)


def _attn_kernel(q_ref, kp_ref, kc_ref, vp_ref, vc_ref, bias_ref, o_ref, k_scr, v_scr, s_scr, *, blocks_per_seq):
    first = pl.program_id(1) % blocks_per_seq == 0
    k_scr[0:ATT_QBLK, :] = kp_ref[...]
    k_scr[ATT_QBLK:, :] = kc_ref[...]
    v_scr[0:ATT_QBLK, :] = vp_ref[...]
    v_scr[ATT_QBLK:, :] = vc_ref[...]
    lane = lax.broadcasted_iota(I32, (ATT_SUB, LANES), 1)
    low = lane < ATT_HEAD_DIM

    units = [(pair, a) for pair in range(ATT_PAIRS) for a in range(ATT_NSUB)]
    for u, (pair, a) in enumerate(units):
        lanes = slice(pair * LANES, (pair + 1) * LANES)
        q = q_ref[a * ATT_SUB:(a + 1) * ATT_SUB, lanes]
        zero = jnp.zeros_like(q)
        q2 = jnp.concatenate([jnp.where(low, q, zero), jnp.where(low, zero, q)], axis=0)
        win = slice(a * ATT_SUB, a * ATT_SUB + ATT_WIN)
        s = lax.dot_general(q2, k_scr[win, lanes], (((1,), (1,)), ((), ())), preferred_element_type=F32)
        slab = jnp.where(first, a + 1, 0)
        bias = bias_ref[2 * pair:2 * pair + 2, pl.ds(slab, 1)]
        s_scr[u] = s + bias.reshape(2 * ATT_SUB, ATT_WIN)

    for u, (pair, a) in enumerate(units):
        lanes = slice(pair * LANES, (pair + 1) * LANES)
        rows = slice(a * ATT_SUB, (a + 1) * ATT_SUB)
        win = slice(a * ATT_SUB, a * ATT_SUB + ATT_WIN)
        s = s_scr[u]
        m = jnp.max(s, axis=-1, keepdims=True)
        p = jnp.exp2(s - m)
        l = jnp.sum(p, axis=-1, keepdims=True)
        pv = jnp.dot(p.astype(BF16), v_scr[win, lanes], preferred_element_type=F32) / l
        o_ref[rows, lanes] = jnp.where(low, pv[:ATT_SUB], pv[ATT_SUB:]).astype(o_ref.dtype)


def _attn(proj, rel_table, seq):
    n = proj.shape[0]
    bps = seq // ATT_QBLK
    pairs = ATT_WIDTH // LANES
    width = ATT_PAIRS * LANES
    qc, kc, vc = (Q_TILE * COL_TILE // width, K_TILE * COL_TILE // width, VB_TILE * COL_TILE // width)
    bias = _band_bias(rel_table)

    def prev(i):
        return jnp.where(i % bps == 0, i, i - 1)

    blk = (ATT_QBLK, width)
    return pl.pallas_call(
        functools.partial(_attn_kernel, blocks_per_seq=bps),
        grid=(pairs // ATT_PAIRS, n // ATT_QBLK),
        in_specs=[pl.BlockSpec(blk, lambda h, i: (i, qc + h)),
                  pl.BlockSpec(blk, lambda h, i: (prev(i), kc + h)),
                  pl.BlockSpec(blk, lambda h, i: (i, kc + h)),
                  pl.BlockSpec(blk, lambda h, i: (prev(i), vc + h)),
                  pl.BlockSpec(blk, lambda h, i: (i, vc + h)),
                  pl.BlockSpec((2 * ATT_PAIRS, 1 + ATT_NSUB, ATT_SUB, ATT_WIN), lambda h, i: (h, 0, 0, 0))],
        out_specs=pl.BlockSpec(blk, lambda h, i: (i, h)),
        out_shape=jax.ShapeDtypeStruct((n, ATT_WIDTH), BF16),
        scratch_shapes=[pltpu.VMEM((2 * ATT_QBLK, width), BF16),
                        pltpu.VMEM((2 * ATT_QBLK, width), BF16),
                        pltpu.VMEM((ATT_PAIRS * ATT_NSUB, 2 * ATT_SUB, ATT_WIN), F32)],
        compiler_params=_params(("arbitrary", "arbitrary")),
        name="attn",
    )(proj, proj, proj, proj, proj, bias)


MG_TM = 256


def _merge_kernel(mod_ref, g_ref, x_ref, ya_ref, yb_ref, ga_ref, gb_ref, wa_ref, wb_ref, wo_ref, wr_ref,
                  x1_ref, h2_ref, lt_ref):
    gate1 = mod_ref[0, 2:3, :]
    shift2 = mod_ref[0, 3:4, :]
    scale2 = mod_ref[0, 4:5, :]
    ya = jnp.dot(ya_ref[...], wa_ref[...], preferred_element_type=F32)
    yb = jnp.dot(yb_ref[...], wb_ref[...], preferred_element_type=F32)
    m = ga_ref[...].astype(F32) * ya + gb_ref[...].astype(F32) * yb
    mixed = jnp.dot(m.astype(BF16), wo_ref[...], preferred_element_type=F32)
    x1 = x_ref[...] + gate1 * mixed
    x1_ref[...] = x1
    h2 = _rms_mod(x1, g_ref[...], scale2, shift2).astype(BF16)
    h2_ref[...] = h2.reshape(h2_ref.shape)
    lt_ref[...] = lax.dot_general(wr_ref[...], h2, (((1,), (1,)), ((), ())), preferred_element_type=F32)


def _merge(x2d, mod3, norm_g, ya, yb, proj, wa, wb, wo, wr_t, seq):
    n, d = x2d.shape
    tps = seq // MG_TM
    const = lambda shape: pl.BlockSpec(shape, lambda i: (0,) * len(shape), pipeline_mode=pl.Buffered(1))
    return pl.pallas_call(
        _merge_kernel,
        grid=(n // MG_TM,),
        in_specs=[pl.BlockSpec((1, N_MOD, d), lambda i: (i // tps, 0, 0)),
                  pl.BlockSpec((1, d), lambda i: (0, 0)),
                  pl.BlockSpec((MG_TM, d), lambda i: (i, 0)),
                  pl.BlockSpec((MG_TM, GMLP_WIDTH), lambda i: (i, 0)),
                  pl.BlockSpec((MG_TM, ATT_WIDTH), lambda i: (i, 0)),
                  pl.BlockSpec((MG_TM, d), lambda i: (i, 0)),
                  pl.BlockSpec((MG_TM, d), lambda i: (i, 1)),
                  const((GMLP_WIDTH, d)), const((ATT_WIDTH, d)), const((d, d)),
                  const((ROUTER_ROWS, d))],
        out_specs=[pl.BlockSpec((MG_TM, d), lambda i: (i, 0)),
                   pl.BlockSpec((MG_TM, TOKEN_ROWS, LANES), lambda i: (i, 0, 0)),
                   pl.BlockSpec((ROUTER_ROWS, MG_TM), lambda i: (0, i))],
        out_shape=[jax.ShapeDtypeStruct((n, d), F32),
                   jax.ShapeDtypeStruct((n, TOKEN_ROWS, LANES), BF16),
                   jax.ShapeDtypeStruct((ROUTER_ROWS, n), F32)],
        compiler_params=_params(("arbitrary",)),
        name="merge",
    )(mod3, norm_g.reshape(1, d), x2d, ya, yb, proj, proj, wa, wb, wo, wr_t)


RT_TN = 512


def _first_argmax(vals, vmax, nrows):
    rows = lax.broadcasted_iota(I32, vals.shape, 0)
    return jnp.min(jnp.where(vals == vmax, rows, nrows), axis=0, keepdims=True)


def _route_kernel(lt_ref, e_ref, r_ref, w_ref, cnt_ref, carry_scr):
    @pl.when(pl.program_id(0) == 0)
    def _():
        carry_scr[...] = jnp.zeros_like(carry_scr)

    gl = lt_ref[0:N_GROUPS, :]
    gmax = jnp.max(gl, axis=0, keepdims=True)
    gidx = _first_argmax(gl, gmax, N_GROUPS)
    gw = 1.0 / jnp.sum(jnp.exp(gl - gmax), axis=0, keepdims=True)

    esel = lt_ref[SUBLANES:SUBLANES + EXPERTS_PER_GROUP, :]
    for g in range(1, N_GROUPS):
        lo = SUBLANES + g * EXPERTS_PER_GROUP
        esel = jnp.where(gidx == g, lt_ref[lo:lo + EXPERTS_PER_GROUP, :], esel)
    rows8 = lax.broadcasted_iota(I32, esel.shape, 0)
    m1 = jnp.max(esel, axis=0, keepdims=True)
    i1 = _first_argmax(esel, m1, EXPERTS_PER_GROUP)
    rest = jnp.where(rows8 == i1, -jnp.inf, esel)
    m2 = jnp.max(rest, axis=0, keepdims=True)
    i2 = _first_argmax(rest, m2, EXPERTS_PER_GROUP)
    z = jnp.exp(m2 - m1)
    w_top = 1.0 / (1.0 + z)
    e0 = gidx * EXPERTS_PER_GROUP + i1
    e1 = gidx * EXPERTS_PER_GROUP + i2
    e_ref[0:1, :] = e0
    e_ref[1:2, :] = e1
    w_ref[0:1, :] = gw * w_top
    w_ref[1:2, :] = gw * (z * w_top)

    rows_e = lax.broadcasted_iota(I32, (N_EXPERTS, RT_TN), 0)
    oh0 = rows_e == e0
    oh1 = rows_e == e1
    oh = jnp.where(oh0 | oh1, 1.0, 0.0)
    src = lax.broadcasted_iota(I32, (RT_TN, RT_TN), 0)
    dst = lax.broadcasted_iota(I32, (RT_TN, RT_TN), 1)
    before = jnp.where(src < dst, 1.0, 0.0).astype(BF16)
    carry = carry_scr[...]
    prefix = jnp.dot(oh.astype(BF16), before, preferred_element_type=F32) + carry[:, 0:1]
    r_ref[0:1, :] = jnp.sum(jnp.where(oh0, prefix, 0.0), axis=0, keepdims=True).astype(I32)
    r_ref[1:2, :] = jnp.sum(jnp.where(oh1, prefix, 0.0), axis=0, keepdims=True).astype(I32)
    carry = carry + jnp.sum(oh, axis=1, keepdims=True)
    carry_scr[...] = carry
    cnt_ref[...] = carry.astype(I32)


def _route(logits_t):
    n = logits_t.shape[1]
    slot = pl.BlockSpec((2, RT_TN), lambda i: (0, i))
    return pl.pallas_call(
        _route_kernel,
        grid=(n // RT_TN,),
        in_specs=[pl.BlockSpec((ROUTER_ROWS, RT_TN), lambda i: (0, i))],
        out_specs=[slot, slot, slot, pl.BlockSpec((N_EXPERTS, LANES), lambda i: (0, 0))],
        out_shape=[jax.ShapeDtypeStruct((2, n), I32),
                   jax.ShapeDtypeStruct((2, n), I32),
                   jax.ShapeDtypeStruct((2, n), F32),
                   jax.ShapeDtypeStruct((N_EXPERTS, LANES), I32)],
        scratch_shapes=[pltpu.VMEM((N_EXPERTS, LANES), F32)],
        compiler_params=_params(("arbitrary",)),
        name="route",
    )(logits_t)


DP_TM = 2048
DMA_UNROLL = 8
PAD_CHUNK = 32


def _token_copy(src_ref, src_row, dst_ref, dst_row, sem):
    return pltpu.make_async_copy(src_ref.at[src_row], dst_ref.at[dst_row], sem)


def _dispatch_kernel(pos0_ref, pos1_ref, pad_start_ref, pad_n_ref, nt_ref, h_ref, xs_ref, zero_scr, sem, zsem, tsem,
                     csem):
    i = pl.program_id(0)
    n_tiles_max = xs_ref.shape[0] // EXPERT_TILE

    def tail_copy(t):
        return pltpu.make_async_copy(zero_scr, xs_ref.at[pl.ds(t * EXPERT_TILE, EXPERT_TILE)], tsem)

    @pl.when(i == 0)
    def _():
        zero_scr[...] = jnp.zeros_like(zero_scr)

        def chunk_copy(row):
            return pltpu.make_async_copy(zero_scr.at[pl.ds(0, PAD_CHUNK)], xs_ref.at[pl.ds(row, PAD_CHUNK)], csem)

        for e in range(N_EXPERTS):
            start = pad_start_ref[e]
            n_chunks = pad_n_ref[e] // PAD_CHUNK

            def issue_chunk(k, c, start=start):
                chunk_copy(start + k * PAD_CHUNK).start()
                return c
            lax.fori_loop(0, n_chunks, issue_chunk, 0)

            def issue(r, c, start=start):
                _token_copy(zero_scr, 0, xs_ref, start + r, zsem).start()
                return c
            lax.fori_loop(n_chunks * PAD_CHUNK, pad_n_ref[e], issue, 0)

        def issue_tail(t, c):
            tail_copy(t).start()
            return c
        lax.fori_loop(nt_ref[0], n_tiles_max, issue_tail, 0)

        for e in range(N_EXPERTS):
            n_chunks = pad_n_ref[e] // PAD_CHUNK

            def drain_chunk(k, c):
                chunk_copy(0).wait()
                return c
            lax.fori_loop(0, n_chunks, drain_chunk, 0)

            def drain(r, c):
                _token_copy(zero_scr, 0, xs_ref, 0, zsem).wait()
                return c
            lax.fori_loop(n_chunks * PAD_CHUNK, pad_n_ref[e], drain, 0)

        def drain_tail(t, c):
            tail_copy(t).wait()
            return c
        lax.fori_loop(nt_ref[0], n_tiles_max, drain_tail, 0)

    base = i * DP_TM

    def issue(r, c):
        _token_copy(h_ref, r, xs_ref, pos0_ref[base + r], sem).start(priority=0)
        _token_copy(h_ref, r, xs_ref, pos1_ref[base + r], sem).start(priority=1)
        return c
    lax.fori_loop(0, DP_TM, issue, 0, unroll=DMA_UNROLL)

    for _ in range(2):
        pltpu.make_async_copy(h_ref, xs_ref.at[pl.ds(0, DP_TM)], sem).wait()


def _dispatch(h2, pos0, pos1, pad_start, pad_n, n_tiles, rows_out):
    n = h2.shape[0]
    grid_spec = pltpu.PrefetchScalarGridSpec(
        num_scalar_prefetch=5,
        grid=(n // DP_TM,),
        in_specs=[pl.BlockSpec((DP_TM, TOKEN_ROWS, LANES), lambda i, *_: (i, 0, 0))],
        out_specs=pl.BlockSpec(memory_space=pl.ANY),
        scratch_shapes=[pltpu.VMEM((EXPERT_TILE, TOKEN_ROWS, LANES), BF16),
                        pltpu.SemaphoreType.DMA(()), pltpu.SemaphoreType.DMA(()),
                        pltpu.SemaphoreType.DMA(()), pltpu.SemaphoreType.DMA(())],
    )
    return pl.pallas_call(
        _dispatch_kernel,
        grid_spec=grid_spec,
        out_shape=jax.ShapeDtypeStruct((rows_out, TOKEN_ROWS, LANES), BF16),
        compiler_params=pltpu.CompilerParams(dimension_semantics=("arbitrary",),
                                             vmem_limit_bytes=VMEM_LIMIT, has_side_effects=True),
        name="dispatch",
    )(pos0, pos1, pad_start, pad_n, n_tiles, h2)


EXPERT_CAST_ROWS = 256


def _cast_rows(src_ref, dst_ref):
    rows_total = dst_ref.shape[0]

    def body(r, c):
        rows = pl.ds(pl.multiple_of(r * EXPERT_CAST_ROWS, EXPERT_CAST_ROWS), EXPERT_CAST_ROWS)
        dst_ref[rows, :] = src_ref[rows, :].astype(BF16)
        return c
    lax.fori_loop(0, rows_total // EXPERT_CAST_ROWS, body, 0)


def _experts_kernel(te_ref, tr_ref, nt_ref, first_ref, slot_ref, next_ref, xs_ref, w1_hbm, w3_hbm, w2_hbm, ys_ref,
                    w1_stage, w3_stage, w2_stage, w1_scr, w3_scr, w2_scr, sems):
    i = pl.program_id(0)

    def fetch(expert, slot):
        return (pltpu.make_async_copy(w1_hbm.at[expert], w1_stage.at[slot], sems.at[slot]),
                pltpu.make_async_copy(w3_hbm.at[expert], w3_stage.at[slot], sems.at[slot]),
                pltpu.make_async_copy(w2_hbm.at[expert], w2_stage.at[slot], sems.at[slot]))

    @pl.when(i == 0)
    def _():
        for copy in fetch(te_ref[0], 0):
            copy.start()

    @pl.when(first_ref[i] == 1)
    def _():
        slot = slot_ref[i]
        for copy in fetch(te_ref[i], slot):
            copy.wait()

        @pl.when(next_ref[i] >= 0)
        def _():
            for copy in fetch(next_ref[i], 1 - slot):
                copy.start()

        _cast_rows(w1_stage.at[slot], w1_scr)
        _cast_rows(w3_stage.at[slot], w3_scr)
        _cast_rows(w2_stage.at[slot], w2_scr)

    @pl.when(i < nt_ref[0])
    def _():
        x = xs_ref[...].reshape(EXPERT_TILE, D_MODEL)
        a = jnp.dot(x, w1_scr[...], preferred_element_type=F32)
        b = jnp.dot(x, w3_scr[...], preferred_element_type=F32)
        act = (a * jax.nn.sigmoid(a)) * b
        y = jnp.dot(act.astype(BF16), w2_scr[...], preferred_element_type=F32)
        ys_ref[...] = y.astype(BF16).reshape(ys_ref.shape)

    @pl.when(i >= nt_ref[0])
    def _():
        ys_ref[...] = jnp.zeros_like(ys_ref)


def _experts(xs, tile_expert, tile_row, n_tiles, tile_first, tile_slot, tile_next, w1, w3, w2):
    rows = xs.shape[0]
    _, d, f = w1.shape
    tile = (EXPERT_TILE, TOKEN_ROWS, LANES)
    hbm = pl.BlockSpec(memory_space=pl.ANY)
    grid_spec = pltpu.PrefetchScalarGridSpec(
        num_scalar_prefetch=6,
        grid=(rows // EXPERT_TILE,),
        in_specs=[pl.BlockSpec(tile, lambda i, te, tr, *_: (tr[i], 0, 0)), hbm, hbm, hbm],
        out_specs=pl.BlockSpec(tile, lambda i, *_: (i, 0, 0)),
        scratch_shapes=[pltpu.VMEM((2, d, f), F32), pltpu.VMEM((2, d, f), F32), pltpu.VMEM((2, f, d), F32),
                        pltpu.VMEM((d, f), BF16), pltpu.VMEM((d, f), BF16), pltpu.VMEM((f, d), BF16),
                        pltpu.SemaphoreType.DMA((2,))],
    )
    return pl.pallas_call(
        _experts_kernel,
        grid_spec=grid_spec,
        out_shape=jax.ShapeDtypeStruct(xs.shape, BF16),
        compiler_params=_params(("arbitrary",)),
        name="experts",
    )(tile_expert, tile_row, n_tiles, tile_first, tile_slot, tile_next, xs, w1, w3, w2)


CB_TM = 256
CB_CHUNK = 32


def _combine_kernel(pos0_ref, pos1_ref, mod_ref, fg_ref, x1_ref, w0_ref, w1_ref, ys_ref, o_ref,
                    y0_even, y1_even, y0_odd, y1_odd, sems):
    i = pl.program_id(0)
    last = pl.num_programs(0) - 1

    def issue(step, bufs, sem, r):
        t = step * CB_TM + r
        _token_copy(ys_ref, pos0_ref[t], bufs[0], r, sem).start(priority=0)
        _token_copy(ys_ref, pos1_ref[t], bufs[1], r, sem).start(priority=1)

    def wait_all(bufs, sem):
        for buf in bufs:
            pltpu.make_async_copy(ys_ref.at[pl.ds(0, CB_TM)], buf, sem).wait()

    even = ((y0_even, y1_even), sems.at[0])
    odd = ((y0_odd, y1_odd), sems.at[1])

    @pl.when(i == 0)
    def _():
        def first(r, c):
            issue(i, *even, r)
            return c
        lax.fori_loop(0, CB_TM, first, 0, unroll=DMA_UNROLL)

    gate2 = mod_ref[0, 5:6, :]
    fg = fg_ref[...]

    def run(cur, nxt):
        wait_all(*cur)
        nxt_step = jnp.minimum(i + 1, last)

        def chunk(c, carry):
            for r in range(CB_CHUNK):
                issue(nxt_step, *nxt, c * CB_CHUNK + r)
            rows = pl.ds(pl.multiple_of(c * CB_CHUNK, CB_CHUNK), CB_CHUNK)
            y0 = cur[0][0][rows].reshape(CB_CHUNK, D_MODEL).astype(F32)
            y1 = cur[0][1][rows].reshape(CB_CHUNK, D_MODEL).astype(F32)
            y = w0_ref[rows, :] * y0 + w1_ref[rows, :] * y1
            x2 = x1_ref[rows, :] + gate2 * y
            o_ref[rows, :] = (x2 * lax.rsqrt(jnp.mean(x2 * x2, axis=-1, keepdims=True) + EPS)) * fg
            return carry
        lax.fori_loop(0, CB_TM // CB_CHUNK, chunk, 0)

        @pl.when(i == last)
        def _():
            wait_all(*nxt)

    @pl.when(i % 2 == 0)
    def _():
        run(even, odd)

    @pl.when(i % 2 == 1)
    def _():
        run(odd, even)


def _combine(x1, mod3, final_g, ys, pos0, pos1, cw0, cw1, seq):
    n, d = x1.shape
    tps = seq // CB_TM
    grid_spec = pltpu.PrefetchScalarGridSpec(
        num_scalar_prefetch=2,
        grid=(n // CB_TM,),
        in_specs=[pl.BlockSpec((1, N_MOD, d), lambda i, *_: (i // tps, 0, 0)),
                  pl.BlockSpec((1, d), lambda i, *_: (0, 0)),
                  pl.BlockSpec((CB_TM, d), lambda i, *_: (i, 0)),
                  pl.BlockSpec((CB_TM, 1), lambda i, *_: (i, 0)),
                  pl.BlockSpec((CB_TM, 1), lambda i, *_: (i, 0)),
                  pl.BlockSpec(memory_space=pl.ANY)],
        out_specs=pl.BlockSpec((CB_TM, d), lambda i, *_: (i, 0)),
        scratch_shapes=[pltpu.VMEM((CB_TM, TOKEN_ROWS, LANES), BF16) for _ in range(4)]
        + [pltpu.SemaphoreType.DMA((2,))],
    )
    return pl.pallas_call(
        _combine_kernel,
        grid_spec=grid_spec,
        out_shape=jax.ShapeDtypeStruct((n, d), F32),
        compiler_params=_params(("arbitrary",)),
        name="combine",
    )(pos0, pos1, mod3, final_g.reshape(1, d), x1, cw0, cw1, ys)


def _layer(x2d, c, seq, ada_w, ada_b, norm1_g, w_in, gmlp_ln_g, gmlp_ln_b, gmlp_w_s, gmlp_b_s, rel_bias,
           w_branch_a, w_branch_b, w_out, norm2_g, w_group, w_expert, w1, w3, w2, final_g):
    n, d = x2d.shape
    nb = c.shape[0]
    mod3 = _ada(c, ada_w, ada_b).reshape(nb, N_MOD, d)

    proj = _in_proj(x2d, mod3, norm1_g, w_in.astype(BF16), seq)
    ya = _gmlp(proj, gmlp_ln_g, gmlp_ln_b, gmlp_w_s, gmlp_b_s)
    yb = _attn(proj, rel_bias, seq)

    wr_t = jnp.concatenate([w_group.T, jnp.zeros((SUBLANES - N_GROUPS, d), F32),
                            w_expert.transpose(0, 2, 1).reshape(N_EXPERTS, d)], axis=0).astype(BF16)
    x1, h2, logits_t = _merge(x2d, mod3, norm2_g, ya, yb, proj, w_branch_a.astype(BF16),
                              w_branch_b.astype(BF16), w_out.astype(BF16), wr_t, seq)

    eidx, rank, cw, counts = _route(logits_t)
    counts = counts[:, 0]
    padded = ((counts + EXPERT_TILE - 1) // EXPERT_TILE) * EXPERT_TILE
    ends = jnp.cumsum(padded)
    offs = ends - padded
    experts = jnp.arange(N_EXPERTS, dtype=I32)
    pos = jnp.sum(jnp.where(eidx[:, :, None] == experts, offs, 0), axis=-1) + rank
    rows_out = 2 * n + N_EXPERTS * EXPERT_TILE
    n_tiles_max = rows_out // EXPERT_TILE
    n_tiles = (ends[-1] // EXPERT_TILE).astype(I32)
    tile_row = jnp.minimum(jnp.arange(n_tiles_max, dtype=I32), n_tiles - 1)
    tile_expert = jnp.minimum(jnp.sum(ends[None, :] <= (tile_row * EXPERT_TILE)[:, None], axis=-1),
                              N_EXPERTS - 1).astype(I32)

    tiles = jnp.arange(n_tiles_max, dtype=I32)
    prev_expert = jnp.concatenate([jnp.full((1,), -1, I32), tile_expert[:-1]])
    tile_first = ((tiles < n_tiles) & (tile_expert != prev_expert)).astype(I32)
    tile_slot = ((jnp.cumsum(tile_first) - 1) % 2).astype(I32)
    later_nonempty = (experts[None, :] > experts[:, None]) & (padded > 0)[None, :]
    next_nonempty = jnp.min(jnp.where(later_nonempty, experts[None, :], N_EXPERTS), axis=1)
    next_nonempty = jnp.where(next_nonempty == N_EXPERTS, -1, next_nonempty)
    tile_next = jnp.sum(jnp.where(tile_expert[:, None] == experts, next_nonempty, 0), axis=-1).astype(I32)

    n_tiles = n_tiles.reshape(1)
    xs = _dispatch(h2, pos[0], pos[1], (offs + counts).astype(I32), (padded - counts).astype(I32), n_tiles,
                   rows_out)
    ys = _experts(xs, tile_expert, tile_row, n_tiles, tile_first, tile_slot, tile_next,
                  w1.reshape(N_EXPERTS, d, D_EXPERT), w3.reshape(N_EXPERTS, d, D_EXPERT),
                  w2.reshape(N_EXPERTS, D_EXPERT, d))
    return _combine(x1, mod3, final_g, ys, pos[0], pos[1], cw[0].reshape(n, 1), cw[1].reshape(n, 1), seq)


def kernel(x, c, ada_w, ada_b, norm1_g, w_in, gmlp_ln_g, gmlp_ln_b, gmlp_w_s, gmlp_b_s, rel_bias, w_branch_a,
           w_branch_b, w_out, norm2_g, w_group, w_expert, w1, w3, w2, final_g):
    b, s, d = x.shape
    out = _layer(x.reshape(b * s, d), c, s, ada_w[0], ada_b[0], norm1_g[0], w_in[0], gmlp_ln_g[0], gmlp_ln_b[0],
                 gmlp_w_s[0], gmlp_b_s[0], rel_bias[0], w_branch_a[0], w_branch_b[0], w_out[0], norm2_g[0],
                 w_group[0], w_expert[0], w1[0], w3[0], w2[0], final_g)
    return out.reshape(b, s, d)
```

```python
import functools

import numpy as np
import jax
import jax.numpy as jnp
from jax import lax
from jax.experimental import pallas as pl
from jax.experimental.pallas import tpu as pltpu

F32 = jnp.float32
BF16 = jnp.bfloat16
I32 = jnp.int32

D_MODEL = 2048
CHUNK = 64
EPS = 1e-6
NEG_INF = -1e30
LOG2E = float(np.log2(np.e))
GMLP_BLOCK = 128
GMLP_GROUPS = 8
GMLP_WIDTH = 1024
ATT_HEADS = 16
ATT_HEAD_DIM = 64
ATT_WIDTH = ATT_HEADS * ATT_HEAD_DIM
LEFT_CHUNKS = 8
MAX_REL = 256
N_GROUPS = 4
EXPERTS_PER_GROUP = 8
N_EXPERTS = N_GROUPS * EXPERTS_PER_GROUP
D_EXPERT = 512
N_MOD = 6

LANES = 128
SUBLANES = 8
VMEM_LIMIT = 56 * 1024 * 1024

COL_TILE = 1024
PROJ_COLS = 2 * D_MODEL + 2 * GMLP_WIDTH + 3 * ATT_WIDTH
GATE_TILES = 2 * D_MODEL // COL_TILE
U_TILE = GATE_TILES
V_TILE = GATE_TILES + 1
Q_TILE = GATE_TILES + 2
K_TILE = GATE_TILES + 3
VB_TILE = GATE_TILES + 4

ROUTER_ROWS = SUBLANES + N_EXPERTS

ATT_QBLK = 1024
ATT_PREV = LEFT_CHUNKS * CHUNK
ATT_SUB = 2 * CHUNK
ATT_NSUB = ATT_QBLK // ATT_SUB
ATT_NFIRST = ATT_PREV // ATT_SUB
ATT_PAIRS = 2
ATT_WIN = ATT_SUB + LEFT_CHUNKS * CHUNK

EXPERT_TILE = 256
TOKEN_ROWS = D_MODEL // LANES


def _params(sem, vmem=VMEM_LIMIT):
    return pltpu.CompilerParams(dimension_semantics=sem, vmem_limit_bytes=vmem)


def _rms_mod(x, g, scale, shift):
    y = x * lax.rsqrt(jnp.mean(x * x, axis=-1, keepdims=True) + EPS)
    return (y * g) * (1.0 + scale) + shift


def _ada_kernel(cb_ref, w_ref, b_ref, o_ref, s_scr):
    nb, d, _ = cb_ref.shape
    tn = w_ref.shape[1]
    reps = tn // LANES

    @pl.when(pl.program_id(0) == 0)
    def _():
        cb = cb_ref[...]
        s_scr[...] = cb * jax.nn.sigmoid(cb)

    def body(i, accs):
        r = pl.ds(pl.multiple_of(i * SUBLANES, SUBLANES), SUBLANES)
        w8 = w_ref[r, :]
        return tuple(accs[b] + w8 * jnp.concatenate([s_scr[b, r, :]] * reps, axis=1) for b in range(nb))

    init = tuple(jnp.zeros((SUBLANES, tn), F32) for _ in range(nb))
    accs = lax.fori_loop(0, d // SUBLANES, body, init, unroll=4)
    for b in range(nb):
        o_ref[b:b + 1, :] = jnp.sum(accs[b], axis=0, keepdims=True) + b_ref[...]


def _ada(c, ada_w, ada_b):
    nb, d = c.shape
    n = ada_w.shape[1]
    tn = 1024
    cb = jnp.broadcast_to(c[:, :, None], (nb, d, LANES))
    return pl.pallas_call(
        _ada_kernel,
        grid=(n // tn,),
        in_specs=[pl.BlockSpec((nb, d, LANES), lambda j: (0, 0, 0)),
                  pl.BlockSpec((d, tn), lambda j: (0, j)),
                  pl.BlockSpec((1, tn), lambda j: (0, j))],
        out_specs=pl.BlockSpec((nb, tn), lambda j: (0, j)),
        out_shape=jax.ShapeDtypeStruct((nb, n), F32),
        scratch_shapes=[pltpu.VMEM((nb, d, LANES), F32)],
        compiler_params=_params(("arbitrary",)),
        name="ada",
    )(cb, ada_w, ada_b.reshape(1, n))


IN_TM = 1024
IN_RB = 1024
IN_STAT_RB = 128
IN_NORM_RB = 16


def _gelu(a):
    return 0.5 * a * (1.0 + lax.erf(a * np.float32(np.sqrt(0.5))))


def _sigmoid(a):
    return 0.5 * jnp.tanh(0.5 * a) + 0.5


def _in_proj_kernel(mod_ref, g_ref, x_ref, w_ref, o_ref, h_scr, gain_scr, shift_scr, inv_scr):
    j = pl.program_id(1)
    nrb = IN_TM // IN_RB

    @pl.when(j == 0)
    def _():
        d = x_ref.shape[1]
        gain_scr[...] = jnp.broadcast_to(g_ref[...] * (1.0 + mod_ref[0, 1:2, :]), gain_scr.shape)
        shift_scr[...] = jnp.broadcast_to(mod_ref[0, 0:1, :], shift_scr.shape)

        def stats(rb, c):
            rows = pl.ds(pl.multiple_of(rb * IN_STAT_RB, IN_STAT_RB), IN_STAT_RB)
            sq = jnp.zeros((IN_STAT_RB, LANES), F32)
            for k in range(d // LANES):
                xk = x_ref[rows, k * LANES:(k + 1) * LANES]
                sq = sq + xk * xk
            inv = lax.rsqrt(jnp.sum(sq, axis=-1, keepdims=True) * (1.0 / d) + EPS)
            inv_scr[rows, :] = jnp.broadcast_to(inv, (IN_STAT_RB, LANES))
            return c
        lax.fori_loop(0, IN_TM // IN_STAT_RB, stats, 0)

        def apply(rb, c):
            rows = pl.ds(pl.multiple_of(rb * IN_NORM_RB, IN_NORM_RB), IN_NORM_RB)
            inv = inv_scr[rows, :]
            for k in range(d // LANES):
                cols = slice(k * LANES, (k + 1) * LANES)
                y = (x_ref[rows, cols] * inv) * gain_scr[:, cols] + shift_scr[:, cols]
                h_scr[rows, cols] = y.astype(BF16)
            return c
        lax.fori_loop(0, IN_TM // IN_NORM_RB, apply, 0, unroll=2)

    def run(epilogue):
        def body(rb, c):
            rows = pl.ds(pl.multiple_of(rb * IN_RB, IN_RB), IN_RB)
            acc = jnp.dot(h_scr[rows, :], w_ref[...], preferred_element_type=F32)
            o_ref[rows, :] = epilogue(acc).astype(o_ref.dtype)
            return c
        lax.fori_loop(0, nrb, body, 0)

    @pl.when(j < GATE_TILES)
    def _():
        run(_sigmoid)

    @pl.when((j == U_TILE) | (j == V_TILE))
    def _():
        run(_gelu)

    @pl.when(j == Q_TILE)
    def _():
        run(lambda a: a * np.float32(ATT_HEAD_DIM ** -0.5 * LOG2E))

    @pl.when(j > Q_TILE)
    def _():
        run(lambda a: a)


def _in_proj(x2d, mod3, norm_g, w_in_bf16, seq):
    n, d = x2d.shape
    cols = w_in_bf16.shape[1]
    n_tiles = cols // COL_TILE
    tiles_per_seq = seq // IN_TM
    return pl.pallas_call(
        _in_proj_kernel,
        grid=(n // IN_TM, n_tiles),
        in_specs=[pl.BlockSpec((1, N_MOD, d), lambda i, j: (i // tiles_per_seq, 0, 0)),
                  pl.BlockSpec((1, d), lambda i, j: (0, 0)),
                  pl.BlockSpec((IN_TM, d), lambda i, j: (i, 0)),
                  pl.BlockSpec((d, COL_TILE), lambda i, j: (0, (j + n_tiles - GATE_TILES) % n_tiles))],
        out_specs=pl.BlockSpec((IN_TM, COL_TILE), lambda i, j: (i, j)),
        out_shape=jax.ShapeDtypeStruct((n, cols), BF16),
        scratch_shapes=[pltpu.VMEM((IN_TM, d), BF16), pltpu.VMEM((IN_NORM_RB, d), F32),
                        pltpu.VMEM((IN_NORM_RB, d), F32), pltpu.VMEM((IN_TM, LANES), F32)],
        compiler_params=_params(("arbitrary", "arbitrary")),
        name="in_proj",
    )(mod3, norm_g.reshape(1, d), x2d, w_in_bf16)


GM_TM = 512


def _gmlp_kernel(u_ref, v_ref, lng_ref, lnb_ref, ws_ref, bs_ref, o_ref):
    t = lax.broadcasted_iota(I32, (GMLP_BLOCK, GMLP_BLOCK), 0)
    s = lax.broadcasted_iota(I32, (GMLP_BLOCK, GMLP_BLOCK), 1)
    causal = (s // CHUNK) <= (t // CHUNK)
    lng = lng_ref[...]
    lnb = lnb_ref[...]
    for blk in range(GM_TM // GMLP_BLOCK):
        rows = slice(blk * GMLP_BLOCK, (blk + 1) * GMLP_BLOCK)
        v = v_ref[rows, :].astype(F32)
        mu = jnp.mean(v, axis=-1, keepdims=True)
        vc = v - mu
        var = jnp.mean(vc * vc, axis=-1, keepdims=True)
        vln = ((vc * lax.rsqrt(var + EPS)) * lng + lnb).astype(BF16)
        for g in range(GMLP_GROUPS):
            cols = slice(g * LANES, (g + 1) * LANES)
            w = jnp.where(causal, ws_ref[g], 0.0).astype(BF16)
            mixed = jnp.dot(w, vln[:, cols], preferred_element_type=F32) + bs_ref[g]
            o_ref[rows, cols] = (u_ref[rows, cols].astype(F32) * mixed).astype(o_ref.dtype)


def _gmlp(proj, ln_g, ln_b, w_s, b_s):
    n = proj.shape[0]
    return pl.pallas_call(
        _gmlp_kernel,
        grid=(n // GM_TM,),
        in_specs=[pl.BlockSpec((GM_TM, GMLP_WIDTH), lambda i: (i, U_TILE)),
                  pl.BlockSpec((GM_TM, GMLP_WIDTH), lambda i: (i, V_TILE)),
                  pl.BlockSpec((1, GMLP_WIDTH), lambda i: (0, 0)),
                  pl.BlockSpec((1, GMLP_WIDTH), lambda i: (0, 0)),
                  pl.BlockSpec((GMLP_GROUPS, GMLP_BLOCK, GMLP_BLOCK), lambda i: (0, 0, 0)),
                  pl.BlockSpec((GMLP_GROUPS, GMLP_BLOCK, 1), lambda i: (0, 0, 0))],
        out_specs=pl.BlockSpec((GM_TM, GMLP_WIDTH), lambda i: (i, 0)),
        out_shape=jax.ShapeDtypeStruct((n, GMLP_WIDTH), BF16),
        compiler_params=_params(("arbitrary",)),
        name="gmlp",
    )(proj, proj, ln_g.reshape(1, -1), ln_b.reshape(1, -1), w_s,
      b_s.reshape(GMLP_GROUPS, GMLP_BLOCK, 1))


def _band_bias(rel_table):
    heads = rel_table.shape[0]
    r = np.arange(ATT_SUB)[:, None]
    w = np.arange(ATT_WIN)[None, :]
    j = w // CHUNK - r // CHUNK
    in_band = (j >= 0) & (j <= LEFT_CHUNKS)
    a = np.arange(ATT_NFIRST)[:, None, None]
    in_seq = np.broadcast_to(w[None] >= ATT_PREV - a * ATT_SUB, (ATT_NFIRST, ATT_SUB, ATT_WIN))
    visible = np.concatenate([in_band[None], in_band[None] & in_seq])
    far = LEFT_CHUNKS * CHUNK + ATT_SUB - 1
    n_clipped = far - MAX_REL + 1
    table = rel_table.astype(F32) * np.float32(LOG2E)
    lo = MAX_REL - (ATT_WIN - 1 - LEFT_CHUNKS * CHUNK)
    diag = jnp.concatenate([jnp.broadcast_to(table[:, 2 * MAX_REL:], (heads, n_clipped)),
                            jnp.flip(table[:, lo:2 * MAX_REL], axis=1),
                            jnp.zeros((heads, 1), F32)], axis=1)
    span = diag.shape[1] - 1
    shifted = jnp.tile(diag, (1, ATT_SUB))[:, :ATT_SUB * span].reshape(heads, ATT_SUB, span)
    bias = shifted[:, :, ATT_SUB - 1:ATT_SUB - 1 + ATT_WIN]
    return jnp.where(jnp.asarray(visible)[None], bias[:, None], NEG_INF)


def _attn_kernel(q_ref, kp_ref, kc_ref, vp_ref, vc_ref, bias_ref, o_ref, k_scr, v_scr, s_scr, *, blocks_per_seq):
    first = pl.program_id(1) % blocks_per_seq == 0
    k_scr[0:ATT_PREV, :] = kp_ref[...]
    k_scr[ATT_PREV:, :] = kc_ref[...]
    v_scr[0:ATT_PREV, :] = vp_ref[...]
    v_scr[ATT_PREV:, :] = vc_ref[...]
    lane = lax.broadcasted_iota(I32, (ATT_SUB, LANES), 1)
    low = lane < ATT_HEAD_DIM

    units = [(pair, a) for pair in range(ATT_PAIRS) for a in range(ATT_NSUB)]
    for u, (pair, a) in enumerate(units):
        lanes = slice(pair * LANES, (pair + 1) * LANES)
        q = q_ref[a * ATT_SUB:(a + 1) * ATT_SUB, lanes]
        zero = jnp.zeros_like(q)
        q2 = jnp.concatenate([jnp.where(low, q, zero), jnp.where(low, zero, q)], axis=0)
        win = slice(a * ATT_SUB, a * ATT_SUB + ATT_WIN)
        s = lax.dot_general(q2, k_scr[win, lanes], (((1,), (1,)), ((), ())), preferred_element_type=F32)
        slab = jnp.where(first, a + 1, 0) if a < ATT_NFIRST else 0
        bias = bias_ref[2 * pair:2 * pair + 2, pl.ds(slab, 1)]
        s_scr[u] = s + bias.reshape(2 * ATT_SUB, ATT_WIN)

    for u, (pair, a) in enumerate(units):
        lanes = slice(pair * LANES, (pair + 1) * LANES)
        rows = slice(a * ATT_SUB, (a + 1) * ATT_SUB)
        win = slice(a * ATT_SUB, a * ATT_SUB + ATT_WIN)
        s = s_scr[u]
        m = jnp.max(s, axis=-1, keepdims=True)
        p = jnp.exp2(s - m)
        l = jnp.sum(p, axis=-1, keepdims=True)
        pv = jnp.dot(p.astype(BF16), v_scr[win, lanes], preferred_element_type=F32) / l
        o_ref[rows, lanes] = jnp.where(low, pv[:ATT_SUB], pv[ATT_SUB:]).astype(o_ref.dtype)


def _attn(proj, rel_table, seq):
    n = proj.shape[0]
    bps = seq // ATT_QBLK
    pairs = ATT_WIDTH // LANES
    width = ATT_PAIRS * LANES
    qc, kc, vc = (Q_TILE * COL_TILE // width, K_TILE * COL_TILE // width, VB_TILE * COL_TILE // width)
    bias = _band_bias(rel_table)

    ratio = ATT_QBLK // ATT_PREV

    def prev(i):
        return jnp.where(i % bps == 0, i * ratio, i * ratio - 1)

    blk = (ATT_QBLK, width)
    pblk = (ATT_PREV, width)
    return pl.pallas_call(
        functools.partial(_attn_kernel, blocks_per_seq=bps),
        grid=(pairs // ATT_PAIRS, n // ATT_QBLK),
        in_specs=[pl.BlockSpec(blk, lambda h, i: (i, qc + h)),
                  pl.BlockSpec(pblk, lambda h, i: (prev(i), kc + h)),
                  pl.BlockSpec(blk, lambda h, i: (i, kc + h)),
                  pl.BlockSpec(pblk, lambda h, i: (prev(i), vc + h)),
                  pl.BlockSpec(blk, lambda h, i: (i, vc + h)),
                  pl.BlockSpec((2 * ATT_PAIRS, 1 + ATT_NFIRST, ATT_SUB, ATT_WIN), lambda h, i: (h, 0, 0, 0))],
        out_specs=pl.BlockSpec(blk, lambda h, i: (i, h)),
        out_shape=jax.ShapeDtypeStruct((n, ATT_WIDTH), BF16),
        scratch_shapes=[pltpu.VMEM((ATT_PREV + ATT_QBLK, width), BF16),
                        pltpu.VMEM((ATT_PREV + ATT_QBLK, width), BF16),
                        pltpu.VMEM((ATT_PAIRS * ATT_NSUB, 2 * ATT_SUB, ATT_WIN), F32)],
        compiler_params=_params(("arbitrary", "arbitrary")),
        name="attn",
    )(proj, proj, proj, proj, proj, bias)


MG_TM = 256


def _merge_kernel(mod_ref, g_ref, x_ref, ya_ref, yb_ref, ga_ref, gb_ref, wa_ref, wb_ref, wo_ref, wr_ref,
                  x1_ref, h2_ref, lt_ref):
    gate1 = mod_ref[0, 2:3, :]
    shift2 = mod_ref[0, 3:4, :]
    scale2 = mod_ref[0, 4:5, :]
    ya = jnp.dot(ya_ref[...], wa_ref[...], preferred_element_type=F32)
    yb = jnp.dot(yb_ref[...], wb_ref[...], preferred_element_type=F32)
    m = ga_ref[...].astype(F32) * ya + gb_ref[...].astype(F32) * yb
    mixed = jnp.dot(m.astype(BF16), wo_ref[...], preferred_element_type=F32)
    x1 = x_ref[...] + gate1 * mixed
    x1_ref[...] = x1
    h2 = _rms_mod(x1, g_ref[...], scale2, shift2).astype(BF16)
    h2_ref[...] = h2.reshape(h2_ref.shape)
    lt_ref[...] = lax.dot_general(wr_ref[...], h2, (((1,), (1,)), ((), ())), preferred_element_type=F32)


def _merge(x2d, mod3, norm_g, ya, yb, proj, wa, wb, wo, wr_t, seq):
    n, d = x2d.shape
    tps = seq // MG_TM
    const = lambda shape: pl.BlockSpec(shape, lambda i: (0,) * len(shape), pipeline_mode=pl.Buffered(1))
    return pl.pallas_call(
        _merge_kernel,
        grid=(n // MG_TM,),
        in_specs=[pl.BlockSpec((1, N_MOD, d), lambda i: (i // tps, 0, 0)),
                  pl.BlockSpec((1, d), lambda i: (0, 0)),
                  pl.BlockSpec((MG_TM, d), lambda i: (i, 0)),
                  pl.BlockSpec((MG_TM, GMLP_WIDTH), lambda i: (i, 0)),
                  pl.BlockSpec((MG_TM, ATT_WIDTH), lambda i: (i, 0)),
                  pl.BlockSpec((MG_TM, d), lambda i: (i, 0)),
                  pl.BlockSpec((MG_TM, d), lambda i: (i, 1)),
                  const((GMLP_WIDTH, d)), const((ATT_WIDTH, d)), const((d, d)),
                  const((ROUTER_ROWS, d))],
        out_specs=[pl.BlockSpec((MG_TM, d), lambda i: (i, 0)),
                   pl.BlockSpec((MG_TM, TOKEN_ROWS, LANES), lambda i: (i, 0, 0)),
                   pl.BlockSpec((ROUTER_ROWS, MG_TM), lambda i: (0, i))],
        out_shape=[jax.ShapeDtypeStruct((n, d), F32),
                   jax.ShapeDtypeStruct((n, TOKEN_ROWS, LANES), BF16),
                   jax.ShapeDtypeStruct((ROUTER_ROWS, n), F32)],
        compiler_params=_params(("arbitrary",)),
        name="merge",
    )(mod3, norm_g.reshape(1, d), x2d, ya, yb, proj, proj, wa, wb, wo, wr_t)


RT_TN = 512


def _first_argmax(vals, vmax, nrows):
    rows = lax.broadcasted_iota(I32, vals.shape, 0)
    return jnp.min(jnp.where(vals == vmax, rows, nrows), axis=0, keepdims=True)


def _route_kernel(lt_ref, e_ref, r_ref, w_ref, cnt_ref, carry_scr):
    @pl.when(pl.program_id(0) == 0)
    def _():
        carry_scr[...] = jnp.zeros_like(carry_scr)

    gl = lt_ref[0:N_GROUPS, :]
    gmax = jnp.max(gl, axis=0, keepdims=True)
    gidx = _first_argmax(gl, gmax, N_GROUPS)
    gw = 1.0 / jnp.sum(jnp.exp(gl - gmax), axis=0, keepdims=True)

    esel = lt_ref[SUBLANES:SUBLANES + EXPERTS_PER_GROUP, :]
    for g in range(1, N_GROUPS):
        lo = SUBLANES + g * EXPERTS_PER_GROUP
        esel = jnp.where(gidx == g, lt_ref[lo:lo + EXPERTS_PER_GROUP, :], esel)
    rows8 = lax.broadcasted_iota(I32, esel.shape, 0)
    m1 = jnp.max(esel, axis=0, keepdims=True)
    i1 = _first_argmax(esel, m1, EXPERTS_PER_GROUP)
    rest = jnp.where(rows8 == i1, -jnp.inf, esel)
    m2 = jnp.max(rest, axis=0, keepdims=True)
    i2 = _first_argmax(rest, m2, EXPERTS_PER_GROUP)
    z = jnp.exp(m2 - m1)
    w_top = 1.0 / (1.0 + z)
    e0 = gidx * EXPERTS_PER_GROUP + i1
    e1 = gidx * EXPERTS_PER_GROUP + i2
    e_ref[0:1, :] = e0
    e_ref[1:2, :] = e1
    w_ref[0:1, :] = gw * w_top
    w_ref[1:2, :] = gw * (z * w_top)

    rows_e = lax.broadcasted_iota(I32, (N_EXPERTS, RT_TN), 0)
    oh0 = rows_e == e0
    oh1 = rows_e == e1
    oh = jnp.where(oh0 | oh1, 1.0, 0.0)
    src = lax.broadcasted_iota(I32, (RT_TN, RT_TN), 0)
    dst = lax.broadcasted_iota(I32, (RT_TN, RT_TN), 1)
    before = jnp.where(src < dst, 1.0, 0.0).astype(BF16)
    carry = carry_scr[...]
    prefix = jnp.dot(oh.astype(BF16), before, preferred_element_type=F32) + carry[:, 0:1]
    r_ref[0:1, :] = jnp.sum(jnp.where(oh0, prefix, 0.0), axis=0, keepdims=True).astype(I32)
    r_ref[1:2, :] = jnp.sum(jnp.where(oh1, prefix, 0.0), axis=0, keepdims=True).astype(I32)
    carry = carry + jnp.sum(oh, axis=1, keepdims=True)
    carry_scr[...] = carry
    cnt_ref[...] = carry.astype(I32)


def _route(logits_t):
    n = logits_t.shape[1]
    slot = pl.BlockSpec((2, RT_TN), lambda i: (0, i))
    return pl.pallas_call(
        _route_kernel,
        grid=(n // RT_TN,),
        in_specs=[pl.BlockSpec((ROUTER_ROWS, RT_TN), lambda i: (0, i))],
        out_specs=[slot, slot, slot, pl.BlockSpec((N_EXPERTS, LANES), lambda i: (0, 0))],
        out_shape=[jax.ShapeDtypeStruct((2, n), I32),
                   jax.ShapeDtypeStruct((2, n), I32),
                   jax.ShapeDtypeStruct((2, n), F32),
                   jax.ShapeDtypeStruct((N_EXPERTS, LANES), I32)],
        scratch_shapes=[pltpu.VMEM((N_EXPERTS, LANES), F32)],
        compiler_params=_params(("arbitrary",)),
        name="route",
    )(logits_t)


DP_TM = 2048
DMA_UNROLL = 8
PAD_CHUNK = 32


def _token_copy(src_ref, src_row, dst_ref, dst_row, sem):
    return pltpu.make_async_copy(src_ref.at[src_row], dst_ref.at[dst_row], sem)


def _dispatch_kernel(pos0_ref, pos1_ref, pad_start_ref, pad_n_ref, nt_ref, h_ref, xs_ref, zero_scr, sem, zsem, tsem,
                     csem):
    i = pl.program_id(0)
    n_tiles_max = xs_ref.shape[0] // EXPERT_TILE

    def tail_copy(t):
        return pltpu.make_async_copy(zero_scr, xs_ref.at[pl.ds(t * EXPERT_TILE, EXPERT_TILE)], tsem)

    @pl.when(i == 0)
    def _():
        zero_scr[...] = jnp.zeros_like(zero_scr)

        def chunk_copy(row):
            return pltpu.make_async_copy(zero_scr.at[pl.ds(0, PAD_CHUNK)], xs_ref.at[pl.ds(row, PAD_CHUNK)], csem)

        for e in range(N_EXPERTS):
            start = pad_start_ref[e]
            n_chunks = pad_n_ref[e] // PAD_CHUNK

            def issue_chunk(k, c, start=start):
                chunk_copy(start + k * PAD_CHUNK).start()
                return c
            lax.fori_loop(0, n_chunks, issue_chunk, 0)

            def issue(r, c, start=start):
                _token_copy(zero_scr, 0, xs_ref, start + r, zsem).start()
                return c
            lax.fori_loop(n_chunks * PAD_CHUNK, pad_n_ref[e], issue, 0)

        def issue_tail(t, c):
            tail_copy(t).start()
            return c
        lax.fori_loop(nt_ref[0], n_tiles_max, issue_tail, 0)

        for e in range(N_EXPERTS):
            n_chunks = pad_n_ref[e] // PAD_CHUNK

            def drain_chunk(k, c):
                chunk_copy(0).wait()
                return c
            lax.fori_loop(0, n_chunks, drain_chunk, 0)

            def drain(r, c):
                _token_copy(zero_scr, 0, xs_ref, 0, zsem).wait()
                return c
            lax.fori_loop(n_chunks * PAD_CHUNK, pad_n_ref[e], drain, 0)

        def drain_tail(t, c):
            tail_copy(t).wait()
            return c
        lax.fori_loop(nt_ref[0], n_tiles_max, drain_tail, 0)

    base = i * DP_TM

    def issue(r, c):
        _token_copy(h_ref, r, xs_ref, pos0_ref[base + r], sem).start(priority=0)
        _token_copy(h_ref, r, xs_ref, pos1_ref[base + r], sem).start(priority=1)
        return c
    lax.fori_loop(0, DP_TM, issue, 0, unroll=DMA_UNROLL)

    for _ in range(2):
        pltpu.make_async_copy(h_ref, xs_ref.at[pl.ds(0, DP_TM)], sem).wait()


def _dispatch(h2, pos0, pos1, pad_start, pad_n, n_tiles, rows_out):
    n = h2.shape[0]
    grid_spec = pltpu.PrefetchScalarGridSpec(
        num_scalar_prefetch=5,
        grid=(n // DP_TM,),
        in_specs=[pl.BlockSpec((DP_TM, TOKEN_ROWS, LANES), lambda i, *_: (i, 0, 0))],
        out_specs=pl.BlockSpec(memory_space=pl.ANY),
        scratch_shapes=[pltpu.VMEM((EXPERT_TILE, TOKEN_ROWS, LANES), BF16),
                        pltpu.SemaphoreType.DMA(()), pltpu.SemaphoreType.DMA(()),
                        pltpu.SemaphoreType.DMA(()), pltpu.SemaphoreType.DMA(())],
    )
    return pl.pallas_call(
        _dispatch_kernel,
        grid_spec=grid_spec,
        out_shape=jax.ShapeDtypeStruct((rows_out, TOKEN_ROWS, LANES), BF16),
        compiler_params=pltpu.CompilerParams(dimension_semantics=("arbitrary",),
                                             vmem_limit_bytes=VMEM_LIMIT, has_side_effects=True),
        name="dispatch",
    )(pos0, pos1, pad_start, pad_n, n_tiles, h2)


EXPERT_CAST_ROWS = 256


def _cast_rows(src_ref, dst_ref):
    rows_total = dst_ref.shape[0]

    def body(r, c):
        rows = pl.ds(pl.multiple_of(r * EXPERT_CAST_ROWS, EXPERT_CAST_ROWS), EXPERT_CAST_ROWS)
        dst_ref[rows, :] = src_ref[rows, :].astype(BF16)
        return c
    lax.fori_loop(0, rows_total // EXPERT_CAST_ROWS, body, 0)


def _experts_kernel(te_ref, tr_ref, nt_ref, first_ref, slot_ref, next_ref, xs_ref, w1_hbm, w3_hbm, w2_hbm, ys_ref,
                    w1_stage, w3_stage, w2_stage, w1_scr, w3_scr, w2_scr, sems):
    i = pl.program_id(0)

    def fetch(expert, slot):
        return (pltpu.make_async_copy(w1_hbm.at[expert], w1_stage.at[slot], sems.at[slot]),
                pltpu.make_async_copy(w3_hbm.at[expert], w3_stage.at[slot], sems.at[slot]),
                pltpu.make_async_copy(w2_hbm.at[expert], w2_stage.at[slot], sems.at[slot]))

    @pl.when(i == 0)
    def _():
        for copy in fetch(te_ref[0], 0):
            copy.start()

    @pl.when(first_ref[i] == 1)
    def _():
        slot = slot_ref[i]
        for copy in fetch(te_ref[i], slot):
            copy.wait()

        @pl.when(next_ref[i] >= 0)
        def _():
            for copy in fetch(next_ref[i], 1 - slot):
                copy.start()

        _cast_rows(w1_stage.at[slot], w1_scr)
        _cast_rows(w3_stage.at[slot], w3_scr)
        _cast_rows(w2_stage.at[slot], w2_scr)

    @pl.when(i < nt_ref[0])
    def _():
        x = xs_ref[...].reshape(EXPERT_TILE, D_MODEL)
        a = jnp.dot(x, w1_scr[...], preferred_element_type=F32)
        b = jnp.dot(x, w3_scr[...], preferred_element_type=F32)
        act = (a * jax.nn.sigmoid(a)) * b
        y = jnp.dot(act.astype(BF16), w2_scr[...], preferred_element_type=F32)
        ys_ref[...] = y.astype(BF16).reshape(ys_ref.shape)

    @pl.when(i >= nt_ref[0])
    def _():
        ys_ref[...] = jnp.zeros_like(ys_ref)


def _experts(xs, tile_expert, tile_row, n_tiles, tile_first, tile_slot, tile_next, w1, w3, w2):
    rows = xs.shape[0]
    _, d, f = w1.shape
    tile = (EXPERT_TILE, TOKEN_ROWS, LANES)
    hbm = pl.BlockSpec(memory_space=pl.ANY)
    grid_spec = pltpu.PrefetchScalarGridSpec(
        num_scalar_prefetch=6,
        grid=(rows // EXPERT_TILE,),
        in_specs=[pl.BlockSpec(tile, lambda i, te, tr, *_: (tr[i], 0, 0)), hbm, hbm, hbm],
        out_specs=pl.BlockSpec(tile, lambda i, *_: (i, 0, 0)),
        scratch_shapes=[pltpu.VMEM((2, d, f), F32), pltpu.VMEM((2, d, f), F32), pltpu.VMEM((2, f, d), F32),
                        pltpu.VMEM((d, f), BF16), pltpu.VMEM((d, f), BF16), pltpu.VMEM((f, d), BF16),
                        pltpu.SemaphoreType.DMA((2,))],
    )
    return pl.pallas_call(
        _experts_kernel,
        grid_spec=grid_spec,
        out_shape=jax.ShapeDtypeStruct(xs.shape, BF16),
        compiler_params=_params(("arbitrary",)),
        name="experts",
    )(tile_expert, tile_row, n_tiles, tile_first, tile_slot, tile_next, xs, w1, w3, w2)


CB_TM = 256


def _combine_kernel(pos0_ref, pos1_ref, mod_ref, fg_ref, x1_ref, w0_ref, w1_ref, ys_ref, o_ref,
                    y0_scr, y1_scr, sems):
    i = pl.program_id(0)
    slot = i % 2

    def gather(step, buf):
        base = step * CB_TM

        def issue(r, c):
            _token_copy(ys_ref, pos0_ref[base + r], y0_scr.at[buf], r, sems.at[buf]).start(priority=0)
            _token_copy(ys_ref, pos1_ref[base + r], y1_scr.at[buf], r, sems.at[buf]).start(priority=1)
            return c
        lax.fori_loop(0, CB_TM, issue, 0, unroll=DMA_UNROLL)

    @pl.when(i == 0)
    def _():
        gather(i, slot)

    @pl.when(i + 1 < pl.num_programs(0))
    def _():
        gather(i + 1, 1 - slot)

    pltpu.make_async_copy(ys_ref.at[pl.ds(0, CB_TM)], y0_scr.at[slot], sems.at[slot]).wait()
    pltpu.make_async_copy(ys_ref.at[pl.ds(0, CB_TM)], y1_scr.at[slot], sems.at[slot]).wait()

    gate2 = mod_ref[0, 5:6, :]
    y0 = y0_scr[slot].reshape(CB_TM, D_MODEL).astype(F32)
    y1 = y1_scr[slot].reshape(CB_TM, D_MODEL).astype(F32)
    y = w0_ref[...] * y0 + w1_ref[...] * y1
    x2 = x1_ref[...] + gate2 * y
    o_ref[...] = (x2 * lax.rsqrt(jnp.mean(x2 * x2, axis=-1, keepdims=True) + EPS)) * fg_ref[...]


def _combine(x1, mod3, final_g, ys, pos0, pos1, cw0, cw1, seq):
    n, d = x1.shape
    tps = seq // CB_TM
    grid_spec = pltpu.PrefetchScalarGridSpec(
        num_scalar_prefetch=2,
        grid=(n // CB_TM,),
        in_specs=[pl.BlockSpec((1, N_MOD, d), lambda i, *_: (i // tps, 0, 0)),
                  pl.BlockSpec((1, d), lambda i, *_: (0, 0)),
                  pl.BlockSpec((CB_TM, d), lambda i, *_: (i, 0)),
                  pl.BlockSpec((CB_TM, 1), lambda i, *_: (i, 0)),
                  pl.BlockSpec((CB_TM, 1), lambda i, *_: (i, 0)),
                  pl.BlockSpec(memory_space=pl.ANY)],
        out_specs=pl.BlockSpec((CB_TM, d), lambda i, *_: (i, 0)),
        scratch_shapes=[pltpu.VMEM((2, CB_TM, TOKEN_ROWS, LANES), BF16),
                        pltpu.VMEM((2, CB_TM, TOKEN_ROWS, LANES), BF16),
                        pltpu.SemaphoreType.DMA((2,))],
    )
    return pl.pallas_call(
        _combine_kernel,
        grid_spec=grid_spec,
        out_shape=jax.ShapeDtypeStruct((n, d), F32),
        compiler_params=_params(("arbitrary",)),
        name="combine",
    )(pos0, pos1, mod3, final_g.reshape(1, d), x1, cw0, cw1, ys)


def _layer(x2d, c, seq, ada_w, ada_b, norm1_g, w_in, gmlp_ln_g, gmlp_ln_b, gmlp_w_s, gmlp_b_s, rel_bias,
           w_branch_a, w_branch_b, w_out, norm2_g, w_group, w_expert, w1, w3, w2, final_g):
    n, d = x2d.shape
    nb = c.shape[0]
    mod3 = _ada(c, ada_w, ada_b).reshape(nb, N_MOD, d)

    proj = _in_proj(x2d, mod3, norm1_g, w_in.astype(BF16), seq)
    ya = _gmlp(proj, gmlp_ln_g, gmlp_ln_b, gmlp_w_s, gmlp_b_s)
    yb = _attn(proj, rel_bias, seq)

    wr_t = jnp.concatenate([w_group.T, jnp.zeros((SUBLANES - N_GROUPS, d), F32),
                            w_expert.transpose(0, 2, 1).reshape(N_EXPERTS, d)], axis=0).astype(BF16)
    x1, h2, logits_t = _merge(x2d, mod3, norm2_g, ya, yb, proj, w_branch_a.astype(BF16),
                              w_branch_b.astype(BF16), w_out.astype(BF16), wr_t, seq)

    eidx, rank, cw, counts = _route(logits_t)
    counts = counts[:, 0]
    padded = ((counts + EXPERT_TILE - 1) // EXPERT_TILE) * EXPERT_TILE
    ends = jnp.cumsum(padded)
    offs = ends - padded
    experts = jnp.arange(N_EXPERTS, dtype=I32)
    pos = jnp.sum(jnp.where(eidx[:, :, None] == experts, offs, 0), axis=-1) + rank
    rows_out = 2 * n + N_EXPERTS * EXPERT_TILE
    n_tiles_max = rows_out // EXPERT_TILE
    n_tiles = (ends[-1] // EXPERT_TILE).astype(I32)
    tile_row = jnp.minimum(jnp.arange(n_tiles_max, dtype=I32), n_tiles - 1)
    tile_expert = jnp.minimum(jnp.sum(ends[None, :] <= (tile_row * EXPERT_TILE)[:, None], axis=-1),
                              N_EXPERTS - 1).astype(I32)

    tiles = jnp.arange(n_tiles_max, dtype=I32)
    prev_expert = jnp.concatenate([jnp.full((1,), -1, I32), tile_expert[:-1]])
    tile_first = ((tiles < n_tiles) & (tile_expert != prev_expert)).astype(I32)
    tile_slot = ((jnp.cumsum(tile_first) - 1) % 2).astype(I32)
    later_nonempty = (experts[None, :] > experts[:, None]) & (padded > 0)[None, :]
    next_nonempty = jnp.min(jnp.where(later_nonempty, experts[None, :], N_EXPERTS), axis=1)
    next_nonempty = jnp.where(next_nonempty == N_EXPERTS, -1, next_nonempty)
    tile_next = jnp.sum(jnp.where(tile_expert[:, None] == experts, next_nonempty, 0), axis=-1).astype(I32)

    n_tiles = n_tiles.reshape(1)
    xs = _dispatch(h2, pos[0], pos[1], (offs + counts).astype(I32), (padded - counts).astype(I32), n_tiles,
                   rows_out)
    ys = _experts(xs, tile_expert, tile_row, n_tiles, tile_first, tile_slot, tile_next,
                  w1.reshape(N_EXPERTS, d, D_EXPERT), w3.reshape(N_EXPERTS, d, D_EXPERT),
                  w2.reshape(N_EXPERTS, D_EXPERT, d))
    return _combine(x1, mod3, final_g, ys, pos[0], pos[1], cw[0].reshape(n, 1), cw[1].reshape(n, 1), seq)


def kernel(x, c, ada_w, ada_b, norm1_g, w_in, gmlp_ln_g, gmlp_ln_b, gmlp_w_s, gmlp_b_s, rel_bias, w_branch_a,
           w_branch_b, w_out, norm2_g, w_group, w_expert, w1, w3, w2, final_g):
    b, s, d = x.shape
    out = _layer(x.reshape(b * s, d), c, s, ada_w[0], ada_b[0], norm1_g[0], w_in[0], gmlp_ln_g[0], gmlp_ln_b[0],
                 gmlp_w_s[0], gmlp_b_s[0], rel_bias[0], w_branch_a[0], w_branch_b[0], w_out[0], norm2_g[0],
                 w_group[0], w_expert[0], w1[0], w3[0], w2[0], final_g)
    return out.reshape(b, s, d)
```

```python
import functools

import numpy as np
import jax
import jax.numpy as jnp
from jax import lax
from jax.experimental import pallas as pl
from jax.experimental.pallas import tpu as pltpu

F32 = jnp.float32
BF16 = jnp.bfloat16
I32 = jnp.int32

D_MODEL = 2048
CHUNK = 64
EPS = 1e-6
NEG_INF = -1e30
LOG2E = float(np.log2(np.e))
GMLP_BLOCK = 128
GMLP_GROUPS = 8
GMLP_WIDTH = 1024
ATT_HEADS = 16
ATT_HEAD_DIM = 64
ATT_WIDTH = ATT_HEADS * ATT_HEAD_DIM
LEFT_CHUNKS = 8
MAX_REL = 256
N_GROUPS = 4
EXPERTS_PER_GROUP = 8
N_EXPERTS = N_GROUPS * EXPERTS_PER_GROUP
D_EXPERT = 512
N_MOD = 6

LANES = 128
SUBLANES = 8
VMEM_LIMIT = 56 * 1024 * 1024

COL_TILE = 1024
PROJ_COLS = 2 * D_MODEL + 2 * GMLP_WIDTH + 3 * ATT_WIDTH
GATE_TILES = 2 * D_MODEL // COL_TILE
U_TILE = GATE_TILES
V_TILE = GATE_TILES + 1
Q_TILE = GATE_TILES + 2
K_TILE = GATE_TILES + 3
VB_TILE = GATE_TILES + 4

ROUTER_ROWS = SUBLANES + N_EXPERTS

ATT_QBLK = 1024
ATT_PREV = LEFT_CHUNKS * CHUNK
ATT_SUB = 2 * CHUNK
ATT_NSUB = ATT_QBLK // ATT_SUB
ATT_NFIRST = ATT_PREV // ATT_SUB
ATT_PAIRS = 2
ATT_WIN = ATT_SUB + LEFT_CHUNKS * CHUNK

EXPERT_TILE = 256
TOKEN_ROWS = D_MODEL // LANES


def _params(sem, vmem=VMEM_LIMIT):
    return pltpu.CompilerParams(dimension_semantics=sem, vmem_limit_bytes=vmem)


def _rms_mod(x, g, scale, shift):
    y = x * lax.rsqrt(jnp.mean(x * x, axis=-1, keepdims=True) + EPS)
    return (y * g) * (1.0 + scale) + shift


def _ada_kernel(cb_ref, w_ref, b_ref, o_ref, s_scr):
    nb, d, _ = cb_ref.shape
    tn = w_ref.shape[1]
    reps = tn // LANES

    @pl.when(pl.program_id(0) == 0)
    def _():
        cb = cb_ref[...]
        s_scr[...] = cb * jax.nn.sigmoid(cb)

    def body(i, accs):
        r = pl.ds(pl.multiple_of(i * SUBLANES, SUBLANES), SUBLANES)
        w8 = w_ref[r, :]
        return tuple(accs[b] + w8 * jnp.concatenate([s_scr[b, r, :]] * reps, axis=1) for b in range(nb))

    init = tuple(jnp.zeros((SUBLANES, tn), F32) for _ in range(nb))
    accs = lax.fori_loop(0, d // SUBLANES, body, init, unroll=4)
    for b in range(nb):
        o_ref[b:b + 1, :] = jnp.sum(accs[b], axis=0, keepdims=True) + b_ref[...]


def _ada(c, ada_w, ada_b):
    nb, d = c.shape
    n = ada_w.shape[1]
    tn = 1024
    cb = jnp.broadcast_to(c[:, :, None], (nb, d, LANES))
    return pl.pallas_call(
        _ada_kernel,
        grid=(n // tn,),
        in_specs=[pl.BlockSpec((nb, d, LANES), lambda j: (0, 0, 0)),
                  pl.BlockSpec((d, tn), lambda j: (0, j)),
                  pl.BlockSpec((1, tn), lambda j: (0, j))],
        out_specs=pl.BlockSpec((nb, tn), lambda j: (0, j)),
        out_shape=jax.ShapeDtypeStruct((nb, n), F32),
        scratch_shapes=[pltpu.VMEM((nb, d, LANES), F32)],
        compiler_params=_params(("arbitrary",)),
        name="ada",
    )(cb, ada_w, ada_b.reshape(1, n))


IN_TM = 1024
IN_RB = 1024
IN_STAT_RB = 128
IN_NORM_RB = 16


def _gelu(a):
    return 0.5 * a * (1.0 + lax.erf(a * np.float32(np.sqrt(0.5))))


def _sigmoid(a):
    return 0.5 * jnp.tanh(0.5 * a) + 0.5


def _in_proj_kernel(mod_ref, g_ref, x_ref, w_ref, o_ref, h_scr, gain_scr, shift_scr, inv_scr):
    j = pl.program_id(1)
    nrb = IN_TM // IN_RB

    @pl.when(j == 0)
    def _():
        d = x_ref.shape[1]
        gain_scr[...] = jnp.broadcast_to(g_ref[...] * (1.0 + mod_ref[0, 1:2, :]), gain_scr.shape)
        shift_scr[...] = jnp.broadcast_to(mod_ref[0, 0:1, :], shift_scr.shape)

        def stats(rb, c):
            rows = pl.ds(pl.multiple_of(rb * IN_STAT_RB, IN_STAT_RB), IN_STAT_RB)
            sq = jnp.zeros((IN_STAT_RB, LANES), F32)
            for k in range(d // LANES):
                xk = x_ref[rows, k * LANES:(k + 1) * LANES]
                sq = sq + xk * xk
            inv = lax.rsqrt(jnp.sum(sq, axis=-1, keepdims=True) * (1.0 / d) + EPS)
            inv_scr[rows, :] = jnp.broadcast_to(inv, (IN_STAT_RB, LANES))
            return c
        lax.fori_loop(0, IN_TM // IN_STAT_RB, stats, 0)

        def apply(rb, c):
            rows = pl.ds(pl.multiple_of(rb * IN_NORM_RB, IN_NORM_RB), IN_NORM_RB)
            inv = inv_scr[rows, :]
            for k in range(d // LANES):
                cols = slice(k * LANES, (k + 1) * LANES)
                y = (x_ref[rows, cols] * inv) * gain_scr[:, cols] + shift_scr[:, cols]
                h_scr[rows, cols] = y.astype(BF16)
            return c
        lax.fori_loop(0, IN_TM // IN_NORM_RB, apply, 0, unroll=2)

    def run(epilogue):
        def body(rb, c):
            rows = pl.ds(pl.multiple_of(rb * IN_RB, IN_RB), IN_RB)
            acc = jnp.dot(h_scr[rows, :], w_ref[...], preferred_element_type=F32)
            o_ref[rows, :] = epilogue(acc).astype(o_ref.dtype)
            return c
        lax.fori_loop(0, nrb, body, 0)

    @pl.when(j < GATE_TILES)
    def _():
        run(_sigmoid)

    @pl.when((j == U_TILE) | (j == V_TILE))
    def _():
        run(_gelu)

    @pl.when(j == Q_TILE)
    def _():
        run(lambda a: a * np.float32(ATT_HEAD_DIM ** -0.5 * LOG2E))

    @pl.when(j > Q_TILE)
    def _():
        run(lambda a: a)


def _in_proj(x2d, mod3, norm_g, w_in_bf16, seq):
    n, d = x2d.shape
    cols = w_in_bf16.shape[1]
    n_tiles = cols // COL_TILE
    tiles_per_seq = seq // IN_TM
    return pl.pallas_call(
        _in_proj_kernel,
        grid=(n // IN_TM, n_tiles),
        in_specs=[pl.BlockSpec((1, N_MOD, d), lambda i, j: (i // tiles_per_seq, 0, 0)),
                  pl.BlockSpec((1, d), lambda i, j: (0, 0)),
                  pl.BlockSpec((IN_TM, d), lambda i, j: (i, 0)),
                  pl.BlockSpec((d, COL_TILE), lambda i, j: (0, (j + n_tiles - GATE_TILES) % n_tiles))],
        out_specs=pl.BlockSpec((IN_TM, COL_TILE), lambda i, j: (i, j)),
        out_shape=jax.ShapeDtypeStruct((n, cols), BF16),
        scratch_shapes=[pltpu.VMEM((IN_TM, d), BF16), pltpu.VMEM((IN_NORM_RB, d), F32),
                        pltpu.VMEM((IN_NORM_RB, d), F32), pltpu.VMEM((IN_TM, LANES), F32)],
        compiler_params=_params(("arbitrary", "arbitrary")),
        name="in_proj",
    )(mod3, norm_g.reshape(1, d), x2d, w_in_bf16)


GM_TM = 512


def _gmlp_kernel(u_ref, v_ref, lng_ref, lnb_ref, ws_ref, bs_ref, o_ref):
    t = lax.broadcasted_iota(I32, (GMLP_BLOCK, GMLP_BLOCK), 0)
    s = lax.broadcasted_iota(I32, (GMLP_BLOCK, GMLP_BLOCK), 1)
    causal = (s // CHUNK) <= (t // CHUNK)
    lng = lng_ref[...]
    lnb = lnb_ref[...]
    for blk in range(GM_TM // GMLP_BLOCK):
        rows = slice(blk * GMLP_BLOCK, (blk + 1) * GMLP_BLOCK)
        v = v_ref[rows, :].astype(F32)
        mu = jnp.mean(v, axis=-1, keepdims=True)
        vc = v - mu
        var = jnp.mean(vc * vc, axis=-1, keepdims=True)
        vln = ((vc * lax.rsqrt(var + EPS)) * lng + lnb).astype(BF16)
        for g in range(GMLP_GROUPS):
            cols = slice(g * LANES, (g + 1) * LANES)
            w = jnp.where(causal, ws_ref[g], 0.0).astype(BF16)
            mixed = jnp.dot(w, vln[:, cols], preferred_element_type=F32) + bs_ref[g]
            o_ref[rows, cols] = (u_ref[rows, cols].astype(F32) * mixed).astype(o_ref.dtype)


def _gmlp(proj, ln_g, ln_b, w_s, b_s):
    n = proj.shape[0]
    return pl.pallas_call(
        _gmlp_kernel,
        grid=(n // GM_TM,),
        in_specs=[pl.BlockSpec((GM_TM, GMLP_WIDTH), lambda i: (i, U_TILE)),
                  pl.BlockSpec((GM_TM, GMLP_WIDTH), lambda i: (i, V_TILE)),
                  pl.BlockSpec((1, GMLP_WIDTH), lambda i: (0, 0)),
                  pl.BlockSpec((1, GMLP_WIDTH), lambda i: (0, 0)),
                  pl.BlockSpec((GMLP_GROUPS, GMLP_BLOCK, GMLP_BLOCK), lambda i: (0, 0, 0)),
                  pl.BlockSpec((GMLP_GROUPS, GMLP_BLOCK, 1), lambda i: (0, 0, 0))],
        out_specs=pl.BlockSpec((GM_TM, GMLP_WIDTH), lambda i: (i, 0)),
        out_shape=jax.ShapeDtypeStruct((n, GMLP_WIDTH), BF16),
        compiler_params=_params(("arbitrary",)),
        name="gmlp",
    )(proj, proj, ln_g.reshape(1, -1), ln_b.reshape(1, -1), w_s,
      b_s.reshape(GMLP_GROUPS, GMLP_BLOCK, 1))


def _band_bias(rel_table):
    heads = rel_table.shape[0]
    r = np.arange(ATT_SUB)[:, None]
    w = np.arange(ATT_WIN)[None, :]
    j = w // CHUNK - r // CHUNK
    in_band = (j >= 0) & (j <= LEFT_CHUNKS)
    a = np.arange(ATT_NFIRST)[:, None, None]
    in_seq = np.broadcast_to(w[None] >= ATT_PREV - a * ATT_SUB, (ATT_NFIRST, ATT_SUB, ATT_WIN))
    visible = np.concatenate([in_band[None], in_band[None] & in_seq])
    far = LEFT_CHUNKS * CHUNK + ATT_SUB - 1
    n_clipped = far - MAX_REL + 1
    table = rel_table.astype(F32) * np.float32(LOG2E)
    lo = MAX_REL - (ATT_WIN - 1 - LEFT_CHUNKS * CHUNK)
    diag = jnp.concatenate([jnp.broadcast_to(table[:, 2 * MAX_REL:], (heads, n_clipped)),
                            jnp.flip(table[:, lo:2 * MAX_REL], axis=1),
                            jnp.zeros((heads, 1), F32)], axis=1)
    span = diag.shape[1] - 1
    shifted = jnp.tile(diag, (1, ATT_SUB))[:, :ATT_SUB * span].reshape(heads, ATT_SUB, span)
    bias = shifted[:, :, ATT_SUB - 1:ATT_SUB - 1 + ATT_WIN]
    return jnp.where(jnp.asarray(visible)[None], bias[:, None], NEG_INF)


def _attn_kernel(q_ref, kp_ref, kc_ref, vp_ref, vc_ref, bias_ref, o_ref, k_scr, v_scr, s_scr, *, blocks_per_seq):
    first = pl.program_id(1) % blocks_per_seq == 0
    k_scr[0:ATT_PREV, :] = kp_ref[...]
    k_scr[ATT_PREV:, :] = kc_ref[...]
    v_scr[0:ATT_PREV, :] = vp_ref[...]
    v_scr[ATT_PREV:, :] = vc_ref[...]
    lane = lax.broadcasted_iota(I32, (ATT_SUB, LANES), 1)
    low = lane < ATT_HEAD_DIM

    units = [(pair, a) for pair in range(ATT_PAIRS) for a in range(ATT_NSUB)]
    for u, (pair, a) in enumerate(units):
        lanes = slice(pair * LANES, (pair + 1) * LANES)
        q = q_ref[a * ATT_SUB:(a + 1) * ATT_SUB, lanes]
        zero = jnp.zeros_like(q)
        q2 = jnp.concatenate([jnp.where(low, q, zero), jnp.where(low, zero, q)], axis=0)
        win = slice(a * ATT_SUB, a * ATT_SUB + ATT_WIN)
        s = lax.dot_general(q2, k_scr[win, lanes], (((1,), (1,)), ((), ())), preferred_element_type=F32)
        slab = jnp.where(first, a + 1, 0) if a < ATT_NFIRST else 0
        bias = bias_ref[2 * pair:2 * pair + 2, pl.ds(slab, 1)]
        s_scr[u] = s + bias.reshape(2 * ATT_SUB, ATT_WIN)

    for u, (pair, a) in enumerate(units):
        lanes = slice(pair * LANES, (pair + 1) * LANES)
        rows = slice(a * ATT_SUB, (a + 1) * ATT_SUB)
        win = slice(a * ATT_SUB, a * ATT_SUB + ATT_WIN)
        s = s_scr[u]
        m = jnp.max(s, axis=-1, keepdims=True)
        p = jnp.exp2(s - m)
        l = jnp.sum(p, axis=-1, keepdims=True)
        pv = jnp.dot(p.astype(BF16), v_scr[win, lanes], preferred_element_type=F32) / l
        o_ref[rows, lanes] = jnp.where(low, pv[:ATT_SUB], pv[ATT_SUB:]).astype(o_ref.dtype)


def _attn(proj, rel_table, seq):
    n = proj.shape[0]
    bps = seq // ATT_QBLK
    pairs = ATT_WIDTH // LANES
    width = ATT_PAIRS * LANES
    qc, kc, vc = (Q_TILE * COL_TILE // width, K_TILE * COL_TILE // width, VB_TILE * COL_TILE // width)
    bias = _band_bias(rel_table)

    ratio = ATT_QBLK // ATT_PREV

    def prev(i):
        return jnp.where(i % bps == 0, i * ratio, i * ratio - 1)

    blk = (ATT_QBLK, width)
    pblk = (ATT_PREV, width)
    return pl.pallas_call(
        functools.partial(_attn_kernel, blocks_per_seq=bps),
        grid=(pairs // ATT_PAIRS, n // ATT_QBLK),
        in_specs=[pl.BlockSpec(blk, lambda h, i: (i, qc + h)),
                  pl.BlockSpec(pblk, lambda h, i: (prev(i), kc + h)),
                  pl.BlockSpec(blk, lambda h, i: (i, kc + h)),
                  pl.BlockSpec(pblk, lambda h, i: (prev(i), vc + h)),
                  pl.BlockSpec(blk, lambda h, i: (i, vc + h)),
                  pl.BlockSpec((2 * ATT_PAIRS, 1 + ATT_NFIRST, ATT_SUB, ATT_WIN), lambda h, i: (h, 0, 0, 0))],
        out_specs=pl.BlockSpec(blk, lambda h, i: (i, h)),
        out_shape=jax.ShapeDtypeStruct((n, ATT_WIDTH), BF16),
        scratch_shapes=[pltpu.VMEM((ATT_PREV + ATT_QBLK, width), BF16),
                        pltpu.VMEM((ATT_PREV + ATT_QBLK, width), BF16),
                        pltpu.VMEM((ATT_PAIRS * ATT_NSUB, 2 * ATT_SUB, ATT_WIN), F32)],
        compiler_params=_params(("arbitrary", "arbitrary")),
        name="attn",
    )(proj, proj, proj, proj, proj, bias)


MG_TM = 256


def _merge_kernel(mod_ref, g_ref, x_ref, ya_ref, yb_ref, ga_ref, gb_ref, wa_ref, wb_ref, wo_ref, wr_ref,
                  x1_ref, h2_ref, lt_ref, mixed_even, mixed_odd):
    i = pl.program_id(0)

    @pl.when(i == 0)
    def _():
        mixed_odd[...] = jnp.zeros_like(mixed_odd)

    def run(mixed_new, mixed_done):
        ya = jnp.dot(ya_ref[...], wa_ref[...], preferred_element_type=F32)
        yb = jnp.dot(yb_ref[...], wb_ref[...], preferred_element_type=F32)
        m = ga_ref[...].astype(F32) * ya + gb_ref[...].astype(F32) * yb
        mixed_new[...] = jnp.dot(m.astype(BF16), wo_ref[...], preferred_element_type=F32)

        gate1 = mod_ref[0, 2:3, :]
        shift2 = mod_ref[0, 3:4, :]
        scale2 = mod_ref[0, 4:5, :]
        x1 = x_ref[...] + gate1 * mixed_done[...]
        x1_ref[...] = x1
        h2 = _rms_mod(x1, g_ref[...], scale2, shift2).astype(BF16)
        h2_ref[...] = h2.reshape(h2_ref.shape)
        lt_ref[...] = lax.dot_general(wr_ref[...], h2, (((1,), (1,)), ((), ())), preferred_element_type=F32)

    @pl.when(i % 2 == 0)
    def _():
        run(mixed_even, mixed_odd)

    @pl.when(i % 2 == 1)
    def _():
        run(mixed_odd, mixed_even)


def _merge(x2d, mod3, norm_g, ya, yb, proj, wa, wb, wo, wr_t, seq):
    n, d = x2d.shape
    tps = seq // MG_TM
    tiles = n // MG_TM
    const = lambda shape: pl.BlockSpec(shape, lambda i: (0,) * len(shape), pipeline_mode=pl.Buffered(1))

    def cur(i):
        return jnp.minimum(i, tiles - 1)

    def done(i):
        return jnp.maximum(i - 1, 0)

    return pl.pallas_call(
        _merge_kernel,
        grid=(tiles + 1,),
        in_specs=[pl.BlockSpec((1, N_MOD, d), lambda i: (done(i) // tps, 0, 0)),
                  pl.BlockSpec((1, d), lambda i: (0, 0)),
                  pl.BlockSpec((MG_TM, d), lambda i: (done(i), 0)),
                  pl.BlockSpec((MG_TM, GMLP_WIDTH), lambda i: (cur(i), 0)),
                  pl.BlockSpec((MG_TM, ATT_WIDTH), lambda i: (cur(i), 0)),
                  pl.BlockSpec((MG_TM, d), lambda i: (cur(i), 0)),
                  pl.BlockSpec((MG_TM, d), lambda i: (cur(i), 1)),
                  const((GMLP_WIDTH, d)), const((ATT_WIDTH, d)), const((d, d)),
                  const((ROUTER_ROWS, d))],
        out_specs=[pl.BlockSpec((MG_TM, d), lambda i: (done(i), 0)),
                   pl.BlockSpec((MG_TM, TOKEN_ROWS, LANES), lambda i: (done(i), 0, 0)),
                   pl.BlockSpec((ROUTER_ROWS, MG_TM), lambda i: (0, done(i)))],
        out_shape=[jax.ShapeDtypeStruct((n, d), F32),
                   jax.ShapeDtypeStruct((n, TOKEN_ROWS, LANES), BF16),
                   jax.ShapeDtypeStruct((ROUTER_ROWS, n), F32)],
        scratch_shapes=[pltpu.VMEM((MG_TM, d), F32), pltpu.VMEM((MG_TM, d), F32)],
        compiler_params=_params(("arbitrary",)),
        name="merge",
    )(mod3, norm_g.reshape(1, d), x2d, ya, yb, proj, proj, wa, wb, wo, wr_t)


RT_TN = 512


def _first_argmax(vals, vmax, nrows):
    rows = lax.broadcasted_iota(I32, vals.shape, 0)
    return jnp.min(jnp.where(vals == vmax, rows, nrows), axis=0, keepdims=True)


def _route_kernel(lt_ref, e_ref, r_ref, w_ref, cnt_ref, carry_scr):
    @pl.when(pl.program_id(0) == 0)
    def _():
        carry_scr[...] = jnp.zeros_like(carry_scr)

    gl = lt_ref[0:N_GROUPS, :]
    gmax = jnp.max(gl, axis=0, keepdims=True)
    gidx = _first_argmax(gl, gmax, N_GROUPS)
    gw = 1.0 / jnp.sum(jnp.exp(gl - gmax), axis=0, keepdims=True)

    esel = lt_ref[SUBLANES:SUBLANES + EXPERTS_PER_GROUP, :]
    for g in range(1, N_GROUPS):
        lo = SUBLANES + g * EXPERTS_PER_GROUP
        esel = jnp.where(gidx == g, lt_ref[lo:lo + EXPERTS_PER_GROUP, :], esel)
    rows8 = lax.broadcasted_iota(I32, esel.shape, 0)
    m1 = jnp.max(esel, axis=0, keepdims=True)
    i1 = _first_argmax(esel, m1, EXPERTS_PER_GROUP)
    rest = jnp.where(rows8 == i1, -jnp.inf, esel)
    m2 = jnp.max(rest, axis=0, keepdims=True)
    i2 = _first_argmax(rest, m2, EXPERTS_PER_GROUP)
    z = jnp.exp(m2 - m1)
    w_top = 1.0 / (1.0 + z)
    e0 = gidx * EXPERTS_PER_GROUP + i1
    e1 = gidx * EXPERTS_PER_GROUP + i2
    e_ref[0:1, :] = e0
    e_ref[1:2, :] = e1
    w_ref[0:1, :] = gw * w_top
    w_ref[1:2, :] = gw * (z * w_top)

    rows_e = lax.broadcasted_iota(I32, (N_EXPERTS, RT_TN), 0)
    oh0 = rows_e == e0
    oh1 = rows_e == e1
    oh = jnp.where(oh0 | oh1, 1.0, 0.0)
    src = lax.broadcasted_iota(I32, (RT_TN, RT_TN), 0)
    dst = lax.broadcasted_iota(I32, (RT_TN, RT_TN), 1)
    before = jnp.where(src < dst, 1.0, 0.0).astype(BF16)
    carry = carry_scr[...]
    prefix = jnp.dot(oh.astype(BF16), before, preferred_element_type=F32) + carry[:, 0:1]
    r_ref[0:1, :] = jnp.sum(jnp.where(oh0, prefix, 0.0), axis=0, keepdims=True).astype(I32)
    r_ref[1:2, :] = jnp.sum(jnp.where(oh1, prefix, 0.0), axis=0, keepdims=True).astype(I32)
    carry = carry + jnp.sum(oh, axis=1, keepdims=True)
    carry_scr[...] = carry
    cnt_ref[...] = carry.astype(I32)


def _route(logits_t):
    n = logits_t.shape[1]
    slot = pl.BlockSpec((2, RT_TN), lambda i: (0, i))
    return pl.pallas_call(
        _route_kernel,
        grid=(n // RT_TN,),
        in_specs=[pl.BlockSpec((ROUTER_ROWS, RT_TN), lambda i: (0, i))],
        out_specs=[slot, slot, slot, pl.BlockSpec((N_EXPERTS, LANES), lambda i: (0, 0))],
        out_shape=[jax.ShapeDtypeStruct((2, n), I32),
                   jax.ShapeDtypeStruct((2, n), I32),
                   jax.ShapeDtypeStruct((2, n), F32),
                   jax.ShapeDtypeStruct((N_EXPERTS, LANES), I32)],
        scratch_shapes=[pltpu.VMEM((N_EXPERTS, LANES), F32)],
        compiler_params=_params(("arbitrary",)),
        name="route",
    )(logits_t)


DP_TM = 2048
DMA_UNROLL = 8
PAD_CHUNK = 32


def _token_copy(src_ref, src_row, dst_ref, dst_row, sem):
    return pltpu.make_async_copy(src_ref.at[src_row], dst_ref.at[dst_row], sem)


def _dispatch_kernel(pos0_ref, pos1_ref, pad_start_ref, pad_n_ref, nt_ref, h_ref, xs_ref, zero_scr, sem, zsem, tsem,
                     csem):
    i = pl.program_id(0)
    n_tiles_max = xs_ref.shape[0] // EXPERT_TILE

    def tail_copy(t):
        return pltpu.make_async_copy(zero_scr, xs_ref.at[pl.ds(t * EXPERT_TILE, EXPERT_TILE)], tsem)

    @pl.when(i == 0)
    def _():
        zero_scr[...] = jnp.zeros_like(zero_scr)

        def chunk_copy(row):
            return pltpu.make_async_copy(zero_scr.at[pl.ds(0, PAD_CHUNK)], xs_ref.at[pl.ds(row, PAD_CHUNK)], csem)

        for e in range(N_EXPERTS):
            start = pad_start_ref[e]
            n_chunks = pad_n_ref[e] // PAD_CHUNK

            def issue_chunk(k, c, start=start):
                chunk_copy(start + k * PAD_CHUNK).start()
                return c
            lax.fori_loop(0, n_chunks, issue_chunk, 0)

            def issue(r, c, start=start):
                _token_copy(zero_scr, 0, xs_ref, start + r, zsem).start()
                return c
            lax.fori_loop(n_chunks * PAD_CHUNK, pad_n_ref[e], issue, 0)

        def issue_tail(t, c):
            tail_copy(t).start()
            return c
        lax.fori_loop(nt_ref[0], n_tiles_max, issue_tail, 0)

        for e in range(N_EXPERTS):
            n_chunks = pad_n_ref[e] // PAD_CHUNK

            def drain_chunk(k, c):
                chunk_copy(0).wait()
                return c
            lax.fori_loop(0, n_chunks, drain_chunk, 0)

            def drain(r, c):
                _token_copy(zero_scr, 0, xs_ref, 0, zsem).wait()
                return c
            lax.fori_loop(n_chunks * PAD_CHUNK, pad_n_ref[e], drain, 0)

        def drain_tail(t, c):
            tail_copy(t).wait()
            return c
        lax.fori_loop(nt_ref[0], n_tiles_max, drain_tail, 0)

    base = i * DP_TM

    def issue(r, c):
        _token_copy(h_ref, r, xs_ref, pos0_ref[base + r], sem).start(priority=0)
        _token_copy(h_ref, r, xs_ref, pos1_ref[base + r], sem).start(priority=1)
        return c
    lax.fori_loop(0, DP_TM, issue, 0, unroll=DMA_UNROLL)

    for _ in range(2):
        pltpu.make_async_copy(h_ref, xs_ref.at[pl.ds(0, DP_TM)], sem).wait()


def _dispatch(h2, pos0, pos1, pad_start, pad_n, n_tiles, rows_out):
    n = h2.shape[0]
    grid_spec = pltpu.PrefetchScalarGridSpec(
        num_scalar_prefetch=5,
        grid=(n // DP_TM,),
        in_specs=[pl.BlockSpec((DP_TM, TOKEN_ROWS, LANES), lambda i, *_: (i, 0, 0))],
        out_specs=pl.BlockSpec(memory_space=pl.ANY),
        scratch_shapes=[pltpu.VMEM((EXPERT_TILE, TOKEN_ROWS, LANES), BF16),
                        pltpu.SemaphoreType.DMA(()), pltpu.SemaphoreType.DMA(()),
                        pltpu.SemaphoreType.DMA(()), pltpu.SemaphoreType.DMA(())],
    )
    return pl.pallas_call(
        _dispatch_kernel,
        grid_spec=grid_spec,
        out_shape=jax.ShapeDtypeStruct((rows_out, TOKEN_ROWS, LANES), BF16),
        compiler_params=pltpu.CompilerParams(dimension_semantics=("arbitrary",),
                                             vmem_limit_bytes=VMEM_LIMIT, has_side_effects=True),
        name="dispatch",
    )(pos0, pos1, pad_start, pad_n, n_tiles, h2)


EXPERT_CAST_ROWS = 256


def _cast_rows(src_ref, dst_ref):
    rows_total = dst_ref.shape[0]

    def body(r, c):
        rows = pl.ds(pl.multiple_of(r * EXPERT_CAST_ROWS, EXPERT_CAST_ROWS), EXPERT_CAST_ROWS)
        dst_ref[rows, :] = src_ref[rows, :].astype(BF16)
        return c
    lax.fori_loop(0, rows_total // EXPERT_CAST_ROWS, body, 0)


def _experts_kernel(te_ref, tr_ref, nt_ref, first_ref, slot_ref, next_ref, xs_ref, w1_hbm, w3_hbm, w2_hbm, ys_ref,
                    w1_stage, w3_stage, w2_stage, w1_scr, w3_scr, w2_scr, sems):
    i = pl.program_id(0)

    def fetch(expert, slot):
        return (pltpu.make_async_copy(w1_hbm.at[expert], w1_stage.at[slot], sems.at[slot]),
                pltpu.make_async_copy(w3_hbm.at[expert], w3_stage.at[slot], sems.at[slot]),
                pltpu.make_async_copy(w2_hbm.at[expert], w2_stage.at[slot], sems.at[slot]))

    @pl.when(i == 0)
    def _():
        for copy in fetch(te_ref[0], 0):
            copy.start()

    @pl.when(first_ref[i] == 1)
    def _():
        slot = slot_ref[i]
        for copy in fetch(te_ref[i], slot):
            copy.wait()

        @pl.when(next_ref[i] >= 0)
        def _():
            for copy in fetch(next_ref[i], 1 - slot):
                copy.start()

        _cast_rows(w1_stage.at[slot], w1_scr)
        _cast_rows(w3_stage.at[slot], w3_scr)
        _cast_rows(w2_stage.at[slot], w2_scr)

    @pl.when(i < nt_ref[0])
    def _():
        x = xs_ref[...].reshape(EXPERT_TILE, D_MODEL)
        a = jnp.dot(x, w1_scr[...], preferred_element_type=F32)
        b = jnp.dot(x, w3_scr[...], preferred_element_type=F32)
        act = (a * jax.nn.sigmoid(a)) * b
        y = jnp.dot(act.astype(BF16), w2_scr[...], preferred_element_type=F32)
        ys_ref[...] = y.astype(BF16).reshape(ys_ref.shape)

    @pl.when(i >= nt_ref[0])
    def _():
        ys_ref[...] = jnp.zeros_like(ys_ref)


def _experts(xs, tile_expert, tile_row, n_tiles, tile_first, tile_slot, tile_next, w1, w3, w2):
    rows = xs.shape[0]
    _, d, f = w1.shape
    tile = (EXPERT_TILE, TOKEN_ROWS, LANES)
    hbm = pl.BlockSpec(memory_space=pl.ANY)
    grid_spec = pltpu.PrefetchScalarGridSpec(
        num_scalar_prefetch=6,
        grid=(rows // EXPERT_TILE,),
        in_specs=[pl.BlockSpec(tile, lambda i, te, tr, *_: (tr[i], 0, 0)), hbm, hbm, hbm],
        out_specs=pl.BlockSpec(tile, lambda i, *_: (i, 0, 0)),
        scratch_shapes=[pltpu.VMEM((2, d, f), F32), pltpu.VMEM((2, d, f), F32), pltpu.VMEM((2, f, d), F32),
                        pltpu.VMEM((d, f), BF16), pltpu.VMEM((d, f), BF16), pltpu.VMEM((f, d), BF16),
                        pltpu.SemaphoreType.DMA((2,))],
    )
    return pl.pallas_call(
        _experts_kernel,
        grid_spec=grid_spec,
        out_shape=jax.ShapeDtypeStruct(xs.shape, BF16),
        compiler_params=_params(("arbitrary",)),
        name="experts",
    )(tile_expert, tile_row, n_tiles, tile_first, tile_slot, tile_next, xs, w1, w3, w2)


CB_TM = 256


def _combine_kernel(pos0_ref, pos1_ref, mod_ref, fg_ref, x1_ref, w0_ref, w1_ref, ys_ref, o_ref,
                    y0_scr, y1_scr, sems):
    i = pl.program_id(0)
    slot = i % 2

    def gather(step, buf):
        base = step * CB_TM

        def issue(r, c):
            _token_copy(ys_ref, pos0_ref[base + r], y0_scr.at[buf], r, sems.at[buf]).start(priority=0)
            _token_copy(ys_ref, pos1_ref[base + r], y1_scr.at[buf], r, sems.at[buf]).start(priority=1)
            return c
        lax.fori_loop(0, CB_TM, issue, 0, unroll=DMA_UNROLL)

    @pl.when(i == 0)
    def _():
        gather(i, slot)

    @pl.when(i + 1 < pl.num_programs(0))
    def _():
        gather(i + 1, 1 - slot)

    pltpu.make_async_copy(ys_ref.at[pl.ds(0, CB_TM)], y0_scr.at[slot], sems.at[slot]).wait()
    pltpu.make_async_copy(ys_ref.at[pl.ds(0, CB_TM)], y1_scr.at[slot], sems.at[slot]).wait()

    gate2 = mod_ref[0, 5:6, :]
    y0 = y0_scr[slot].reshape(CB_TM, D_MODEL).astype(F32)
    y1 = y1_scr[slot].reshape(CB_TM, D_MODEL).astype(F32)
    y = w0_ref[...] * y0 + w1_ref[...] * y1
    x2 = x1_ref[...] + gate2 * y
    o_ref[...] = (x2 * lax.rsqrt(jnp.mean(x2 * x2, axis=-1, keepdims=True) + EPS)) * fg_ref[...]


def _combine(x1, mod3, final_g, ys, pos0, pos1, cw0, cw1, seq):
    n, d = x1.shape
    tps = seq // CB_TM
    grid_spec = pltpu.PrefetchScalarGridSpec(
        num_scalar_prefetch=2,
        grid=(n // CB_TM,),
        in_specs=[pl.BlockSpec((1, N_MOD, d), lambda i, *_: (i // tps, 0, 0)),
                  pl.BlockSpec((1, d), lambda i, *_: (0, 0)),
                  pl.BlockSpec((CB_TM, d), lambda i, *_: (i, 0)),
                  pl.BlockSpec((CB_TM, 1), lambda i, *_: (i, 0)),
                  pl.BlockSpec((CB_TM, 1), lambda i, *_: (i, 0)),
                  pl.BlockSpec(memory_space=pl.ANY)],
        out_specs=pl.BlockSpec((CB_TM, d), lambda i, *_: (i, 0)),
        scratch_shapes=[pltpu.VMEM((2, CB_TM, TOKEN_ROWS, LANES), BF16),
                        pltpu.VMEM((2, CB_TM, TOKEN_ROWS, LANES), BF16),
                        pltpu.SemaphoreType.DMA((2,))],
    )
    return pl.pallas_call(
        _combine_kernel,
        grid_spec=grid_spec,
        out_shape=jax.ShapeDtypeStruct((n, d), F32),
        compiler_params=_params(("arbitrary",)),
        name="combine",
    )(pos0, pos1, mod3, final_g.reshape(1, d), x1, cw0, cw1, ys)


def _layer(x2d, c, seq, ada_w, ada_b, norm1_g, w_in, gmlp_ln_g, gmlp_ln_b, gmlp_w_s, gmlp_b_s, rel_bias,
           w_branch_a, w_branch_b, w_out, norm2_g, w_group, w_expert, w1, w3, w2, final_g):
    n, d = x2d.shape
    nb = c.shape[0]
    mod3 = _ada(c, ada_w, ada_b).reshape(nb, N_MOD, d)

    proj = _in_proj(x2d, mod3, norm1_g, w_in.astype(BF16), seq)
    ya = _gmlp(proj, gmlp_ln_g, gmlp_ln_b, gmlp_w_s, gmlp_b_s)
    yb = _attn(proj, rel_bias, seq)

    wr_t = jnp.concatenate([w_group.T, jnp.zeros((SUBLANES - N_GROUPS, d), F32),
                            w_expert.transpose(0, 2, 1).reshape(N_EXPERTS, d)], axis=0).astype(BF16)
    x1, h2, logits_t = _merge(x2d, mod3, norm2_g, ya, yb, proj, w_branch_a.astype(BF16),
                              w_branch_b.astype(BF16), w_out.astype(BF16), wr_t, seq)

    eidx, rank, cw, counts = _route(logits_t)
    counts = counts[:, 0]
    padded = ((counts + EXPERT_TILE - 1) // EXPERT_TILE) * EXPERT_TILE
    ends = jnp.cumsum(padded)
    offs = ends - padded
    experts = jnp.arange(N_EXPERTS, dtype=I32)
    pos = jnp.sum(jnp.where(eidx[:, :, None] == experts, offs, 0), axis=-1) + rank
    rows_out = 2 * n + N_EXPERTS * EXPERT_TILE
    n_tiles_max = rows_out // EXPERT_TILE
    n_tiles = (ends[-1] // EXPERT_TILE).astype(I32)
    tile_row = jnp.minimum(jnp.arange(n_tiles_max, dtype=I32), n_tiles - 1)
    tile_expert = jnp.minimum(jnp.sum(ends[None, :] <= (tile_row * EXPERT_TILE)[:, None], axis=-1),
                              N_EXPERTS - 1).astype(I32)

    tiles = jnp.arange(n_tiles_max, dtype=I32)
    prev_expert = jnp.concatenate([jnp.full((1,), -1, I32), tile_expert[:-1]])
    tile_first = ((tiles < n_tiles) & (tile_expert != prev_expert)).astype(I32)
    tile_slot = ((jnp.cumsum(tile_first) - 1) % 2).astype(I32)
    later_nonempty = (experts[None, :] > experts[:, None]) & (padded > 0)[None, :]
    next_nonempty = jnp.min(jnp.where(later_nonempty, experts[None, :], N_EXPERTS), axis=1)
    next_nonempty = jnp.where(next_nonempty == N_EXPERTS, -1, next_nonempty)
    tile_next = jnp.sum(jnp.where(tile_expert[:, None] == experts, next_nonempty, 0), axis=-1).astype(I32)

    n_tiles = n_tiles.reshape(1)
    xs = _dispatch(h2, pos[0], pos[1], (offs + counts).astype(I32), (padded - counts).astype(I32), n_tiles,
                   rows_out)
    ys = _experts(xs, tile_expert, tile_row, n_tiles, tile_first, tile_slot, tile_next,
                  w1.reshape(N_EXPERTS, d, D_EXPERT), w3.reshape(N_EXPERTS, d, D_EXPERT),
                  w2.reshape(N_EXPERTS, D_EXPERT, d))
    return _combine(x1, mod3, final_g, ys, pos[0], pos[1], cw[0].reshape(n, 1), cw[1].reshape(n, 1), seq)


def kernel(x, c, ada_w, ada_b, norm1_g, w_in, gmlp_ln_g, gmlp_ln_b, gmlp_w_s, gmlp_b_s, rel_bias, w_branch_a,
           w_branch_b, w_out, norm2_g, w_group, w_expert, w1, w3, w2, final_g):
    b, s, d = x.shape
    out = _layer(x.reshape(b * s, d), c, s, ada_w[0], ada_b[0], norm1_g[0], w_in[0], gmlp_ln_g[0], gmlp_ln_b[0],
                 gmlp_w_s[0], gmlp_b_s[0], rel_bias[0], w_branch_a[0], w_branch_b[0], w_out[0], norm2_g[0],
                 w_group[0], w_expert[0], w1[0], w3[0], w2[0], final_g)
    return out.reshape(b, s, d)
```

```python
import functools

import numpy as np
import jax
import jax.numpy as jnp
from jax import lax
from jax.experimental import pallas as pl
from jax.experimental.pallas import tpu as pltpu

F32 = jnp.float32
BF16 = jnp.bfloat16
I32 = jnp.int32

D_MODEL = 2048
CHUNK = 64
EPS = 1e-6
NEG_INF = -1e30
LOG2E = float(np.log2(np.e))
GMLP_BLOCK = 128
GMLP_GROUPS = 8
GMLP_WIDTH = 1024
ATT_HEADS = 16
ATT_HEAD_DIM = 64
ATT_WIDTH = ATT_HEADS * ATT_HEAD_DIM
LEFT_CHUNKS = 8
MAX_REL = 256
N_GROUPS = 4
EXPERTS_PER_GROUP = 8
N_EXPERTS = N_GROUPS * EXPERTS_PER_GROUP
D_EXPERT = 512
N_MOD = 6

LANES = 128
SUBLANES = 8
VMEM_LIMIT = 56 * 1024 * 1024

COL_TILE = 1024
PROJ_COLS = 2 * D_MODEL + 2 * GMLP_WIDTH + 3 * ATT_WIDTH
GATE_TILES = 2 * D_MODEL // COL_TILE
U_TILE = GATE_TILES
V_TILE = GATE_TILES + 1
Q_TILE = GATE_TILES + 2
K_TILE = GATE_TILES + 3
VB_TILE = GATE_TILES + 4

ROUTER_ROWS = SUBLANES + N_EXPERTS

ATT_QBLK = 1024
ATT_PREV = LEFT_CHUNKS * CHUNK
ATT_SUB = 2 * CHUNK
ATT_NSUB = ATT_QBLK // ATT_SUB
ATT_NFIRST = ATT_PREV // ATT_SUB
ATT_PAIRS = 2
ATT_WIN = ATT_SUB + LEFT_CHUNKS * CHUNK

EXPERT_TILE = 256
TOKEN_ROWS = D_MODEL // LANES


def _params(sem, vmem=VMEM_LIMIT):
    return pltpu.CompilerParams(dimension_semantics=sem, vmem_limit_bytes=vmem)


def _rms_mod(x, g, scale, shift):
    y = x * lax.rsqrt(jnp.mean(x * x, axis=-1, keepdims=True) + EPS)
    return (y * g) * (1.0 + scale) + shift


def _ada_kernel(cb_ref, w_ref, b_ref, o_ref, s_scr):
    nb, d, _ = cb_ref.shape
    tn = w_ref.shape[1]
    reps = tn // LANES

    @pl.when(pl.program_id(0) == 0)
    def _():
        cb = cb_ref[...]
        s_scr[...] = cb * jax.nn.sigmoid(cb)

    def body(i, accs):
        r = pl.ds(pl.multiple_of(i * SUBLANES, SUBLANES), SUBLANES)
        w8 = w_ref[r, :]
        return tuple(accs[b] + w8 * jnp.concatenate([s_scr[b, r, :]] * reps, axis=1) for b in range(nb))

    init = tuple(jnp.zeros((SUBLANES, tn), F32) for _ in range(nb))
    accs = lax.fori_loop(0, d // SUBLANES, body, init, unroll=4)
    for b in range(nb):
        o_ref[b:b + 1, :] = jnp.sum(accs[b], axis=0, keepdims=True) + b_ref[...]


ADA_TN = 1024


def _ada(c, ada_w, ada_b):
    nb, d = c.shape
    n = ada_w.shape[1]
    tn = ADA_TN
    cb = jnp.broadcast_to(c[:, :, None], (nb, d, LANES))
    return pl.pallas_call(
        _ada_kernel,
        grid=(n // tn,),
        in_specs=[pl.BlockSpec((nb, d, LANES), lambda j: (0, 0, 0)),
                  pl.BlockSpec((d, tn), lambda j: (0, j)),
                  pl.BlockSpec((1, tn), lambda j: (0, j))],
        out_specs=pl.BlockSpec((nb, tn), lambda j: (0, j)),
        out_shape=jax.ShapeDtypeStruct((nb, n), F32),
        scratch_shapes=[pltpu.VMEM((nb, d, LANES), F32)],
        compiler_params=_params(("arbitrary",)),
        name="ada",
    )(cb, ada_w, ada_b.reshape(1, n))


IN_TM = 1024
IN_STAT_RB = 128
IN_NORM_RB = 16


def _gelu(a):
    return 0.5 * a * (1.0 + lax.erf(a * np.float32(np.sqrt(0.5))))


def _sigmoid(a):
    return 0.5 * jnp.tanh(0.5 * a) + 0.5


def _in_proj_kernel(mod_ref, g_ref, x_ref, w_ref, o_ref, h_scr, gain_scr, shift_scr, inv_scr):
    j = pl.program_id(1)

    @pl.when(j == 0)
    def _():
        d = x_ref.shape[1]
        gain_scr[...] = jnp.broadcast_to(g_ref[...] * (1.0 + mod_ref[0, 1:2, :]), gain_scr.shape)
        shift_scr[...] = jnp.broadcast_to(mod_ref[0, 0:1, :], shift_scr.shape)

        def stats(rb, c):
            rows = pl.ds(pl.multiple_of(rb * IN_STAT_RB, IN_STAT_RB), IN_STAT_RB)
            sq = jnp.zeros((IN_STAT_RB, LANES), F32)
            for k in range(d // LANES):
                xk = x_ref[rows, k * LANES:(k + 1) * LANES]
                sq = sq + xk * xk
            inv = lax.rsqrt(jnp.sum(sq, axis=-1, keepdims=True) * (1.0 / d) + EPS)
            inv_scr[rows, :] = jnp.broadcast_to(inv, (IN_STAT_RB, LANES))
            return c
        lax.fori_loop(0, IN_TM // IN_STAT_RB, stats, 0)

        def apply(rb, c):
            rows = pl.ds(pl.multiple_of(rb * IN_NORM_RB, IN_NORM_RB), IN_NORM_RB)
            inv = inv_scr[rows, :]
            for k in range(d // LANES):
                cols = slice(k * LANES, (k + 1) * LANES)
                y = (x_ref[rows, cols] * inv) * gain_scr[:, cols] + shift_scr[:, cols]
                h_scr[rows, cols] = y.astype(BF16)
            return c
        lax.fori_loop(0, IN_TM // IN_NORM_RB, apply, 0, unroll=2)

    def run(epilogue):
        acc = jnp.dot(h_scr[...], w_ref[...], preferred_element_type=F32)
        o_ref[...] = epilogue(acc).astype(o_ref.dtype)

    @pl.when(j < GATE_TILES)
    def _():
        run(_sigmoid)

    @pl.when((j == U_TILE) | (j == V_TILE))
    def _():
        run(_gelu)

    @pl.when(j == Q_TILE)
    def _():
        run(lambda a: a * np.float32(ATT_HEAD_DIM ** -0.5 * LOG2E))

    @pl.when(j > Q_TILE)
    def _():
        run(lambda a: a)


def _in_proj(x2d, mod3, norm_g, w_in_bf16, seq):
    n, d = x2d.shape
    cols = w_in_bf16.shape[1]
    n_tiles = cols // COL_TILE
    tiles_per_seq = seq // IN_TM
    return pl.pallas_call(
        _in_proj_kernel,
        grid=(n // IN_TM, n_tiles),
        in_specs=[pl.BlockSpec((1, N_MOD, d), lambda i, j: (i // tiles_per_seq, 0, 0)),
                  pl.BlockSpec((1, d), lambda i, j: (0, 0)),
                  pl.BlockSpec((IN_TM, d), lambda i, j: (i, 0)),
                  pl.BlockSpec((d, COL_TILE), lambda i, j: (0, (j + n_tiles - GATE_TILES) % n_tiles))],
        out_specs=pl.BlockSpec((IN_TM, COL_TILE), lambda i, j: (i, j)),
        out_shape=jax.ShapeDtypeStruct((n, cols), BF16),
        scratch_shapes=[pltpu.VMEM((IN_TM, d), BF16), pltpu.VMEM((IN_NORM_RB, d), F32),
                        pltpu.VMEM((IN_NORM_RB, d), F32), pltpu.VMEM((IN_TM, LANES), F32)],
        compiler_params=_params(("arbitrary", "arbitrary")),
        name="in_proj",
    )(mod3, norm_g.reshape(1, d), x2d, w_in_bf16)


GM_TM = 512


def _gmlp_kernel(u_ref, v_ref, lng_ref, lnb_ref, ws_ref, bs_ref, o_ref):
    t = lax.broadcasted_iota(I32, (GMLP_BLOCK, GMLP_BLOCK), 0)
    s = lax.broadcasted_iota(I32, (GMLP_BLOCK, GMLP_BLOCK), 1)
    causal = (s // CHUNK) <= (t // CHUNK)
    lng = lng_ref[...]
    lnb = lnb_ref[...]
    for blk in range(GM_TM // GMLP_BLOCK):
        rows = slice(blk * GMLP_BLOCK, (blk + 1) * GMLP_BLOCK)
        v = v_ref[rows, :].astype(F32)
        mu = jnp.mean(v, axis=-1, keepdims=True)
        vc = v - mu
        var = jnp.mean(vc * vc, axis=-1, keepdims=True)
        vln = ((vc * lax.rsqrt(var + EPS)) * lng + lnb).astype(BF16)
        for g in range(GMLP_GROUPS):
            cols = slice(g * LANES, (g + 1) * LANES)
            w = jnp.where(causal, ws_ref[g], 0.0).astype(BF16)
            mixed = jnp.dot(w, vln[:, cols], preferred_element_type=F32) + bs_ref[g]
            o_ref[rows, cols] = (u_ref[rows, cols].astype(F32) * mixed).astype(o_ref.dtype)


def _gmlp(proj, ln_g, ln_b, w_s, b_s):
    n = proj.shape[0]
    return pl.pallas_call(
        _gmlp_kernel,
        grid=(n // GM_TM,),
        in_specs=[pl.BlockSpec((GM_TM, GMLP_WIDTH), lambda i: (i, U_TILE)),
                  pl.BlockSpec((GM_TM, GMLP_WIDTH), lambda i: (i, V_TILE)),
                  pl.BlockSpec((1, GMLP_WIDTH), lambda i: (0, 0)),
                  pl.BlockSpec((1, GMLP_WIDTH), lambda i: (0, 0)),
                  pl.BlockSpec((GMLP_GROUPS, GMLP_BLOCK, GMLP_BLOCK), lambda i: (0, 0, 0)),
                  pl.BlockSpec((GMLP_GROUPS, GMLP_BLOCK, 1), lambda i: (0, 0, 0))],
        out_specs=pl.BlockSpec((GM_TM, GMLP_WIDTH), lambda i: (i, 0)),
        out_shape=jax.ShapeDtypeStruct((n, GMLP_WIDTH), BF16),
        compiler_params=_params(("arbitrary",)),
        name="gmlp",
    )(proj, proj, ln_g.reshape(1, -1), ln_b.reshape(1, -1), w_s,
      b_s.reshape(GMLP_GROUPS, GMLP_BLOCK, 1))


def _band_bias(rel_table):
    heads = rel_table.shape[0]
    r = np.arange(ATT_SUB)[:, None]
    w = np.arange(ATT_WIN)[None, :]
    j = w // CHUNK - r // CHUNK
    in_band = (j >= 0) & (j <= LEFT_CHUNKS)
    a = np.arange(ATT_NFIRST)[:, None, None]
    in_seq = np.broadcast_to(w[None] >= ATT_PREV - a * ATT_SUB, (ATT_NFIRST, ATT_SUB, ATT_WIN))
    visible = np.concatenate([in_band[None], in_band[None] & in_seq])
    far = LEFT_CHUNKS * CHUNK + ATT_SUB - 1
    n_clipped = far - MAX_REL + 1
    table = rel_table.astype(F32) * np.float32(LOG2E)
    lo = MAX_REL - (ATT_WIN - 1 - LEFT_CHUNKS * CHUNK)
    diag = jnp.concatenate([jnp.broadcast_to(table[:, 2 * MAX_REL:], (heads, n_clipped)),
                            jnp.flip(table[:, lo:2 * MAX_REL], axis=1),
                            jnp.zeros((heads, 1), F32)], axis=1)
    span = diag.shape[1] - 1
    shifted = jnp.tile(diag, (1, ATT_SUB))[:, :ATT_SUB * span].reshape(heads, ATT_SUB, span)
    bias = shifted[:, :, ATT_SUB - 1:ATT_SUB - 1 + ATT_WIN]
    return jnp.where(jnp.asarray(visible)[None], bias[:, None], NEG_INF)


def _attn_kernel(q_ref, kp_ref, kc_ref, vp_ref, vc_ref, bias_ref, o_ref, k_scr, v_scr, s_scr, *, blocks_per_seq):
    first = pl.program_id(1) % blocks_per_seq == 0
    k_scr[0:ATT_PREV, :] = kp_ref[...]
    k_scr[ATT_PREV:, :] = kc_ref[...]
    v_scr[0:ATT_PREV, :] = vp_ref[...]
    v_scr[ATT_PREV:, :] = vc_ref[...]
    lane = lax.broadcasted_iota(I32, (ATT_SUB, LANES), 1)
    low = lane < ATT_HEAD_DIM

    units = [(pair, a) for pair in range(ATT_PAIRS) for a in range(ATT_NSUB)]
    for u, (pair, a) in enumerate(units):
        lanes = slice(pair * LANES, (pair + 1) * LANES)
        q = q_ref[a * ATT_SUB:(a + 1) * ATT_SUB, lanes]
        zero = jnp.zeros_like(q)
        q2 = jnp.concatenate([jnp.where(low, q, zero), jnp.where(low, zero, q)], axis=0)
        win = slice(a * ATT_SUB, a * ATT_SUB + ATT_WIN)
        s = lax.dot_general(q2, k_scr[win, lanes], (((1,), (1,)), ((), ())), preferred_element_type=F32)
        slab = jnp.where(first, a + 1, 0) if a < ATT_NFIRST else 0
        bias = bias_ref[2 * pair:2 * pair + 2, pl.ds(slab, 1)]
        s_scr[u] = s + bias.reshape(2 * ATT_SUB, ATT_WIN)

    for u, (pair, a) in enumerate(units):
        lanes = slice(pair * LANES, (pair + 1) * LANES)
        rows = slice(a * ATT_SUB, (a + 1) * ATT_SUB)
        win = slice(a * ATT_SUB, a * ATT_SUB + ATT_WIN)
        s = s_scr[u]
        m = jnp.max(s, axis=-1, keepdims=True)
        p = jnp.exp2(s - m)
        l = jnp.sum(p, axis=-1, keepdims=True)
        pv = jnp.dot(p.astype(BF16), v_scr[win, lanes], preferred_element_type=F32) / l
        o_ref[rows, lanes] = jnp.where(low, pv[:ATT_SUB], pv[ATT_SUB:]).astype(o_ref.dtype)


def _attn(proj, rel_table, seq):
    n = proj.shape[0]
    bps = seq // ATT_QBLK
    pairs = ATT_WIDTH // LANES
    width = ATT_PAIRS * LANES
    qc, kc, vc = (Q_TILE * COL_TILE // width, K_TILE * COL_TILE // width, VB_TILE * COL_TILE // width)
    bias = _band_bias(rel_table)

    ratio = ATT_QBLK // ATT_PREV

    def prev(i):
        return jnp.where(i % bps == 0, i * ratio, i * ratio - 1)

    blk = (ATT_QBLK, width)
    pblk = (ATT_PREV, width)
    return pl.pallas_call(
        functools.partial(_attn_kernel, blocks_per_seq=bps),
        grid=(pairs // ATT_PAIRS, n // ATT_QBLK),
        in_specs=[pl.BlockSpec(blk, lambda h, i: (i, qc + h)),
                  pl.BlockSpec(pblk, lambda h, i: (prev(i), kc + h)),
                  pl.BlockSpec(blk, lambda h, i: (i, kc + h)),
                  pl.BlockSpec(pblk, lambda h, i: (prev(i), vc + h)),
                  pl.BlockSpec(blk, lambda h, i: (i, vc + h)),
                  pl.BlockSpec((2 * ATT_PAIRS, 1 + ATT_NFIRST, ATT_SUB, ATT_WIN), lambda h, i: (h, 0, 0, 0))],
        out_specs=pl.BlockSpec(blk, lambda h, i: (i, h)),
        out_shape=jax.ShapeDtypeStruct((n, ATT_WIDTH), BF16),
        scratch_shapes=[pltpu.VMEM((ATT_PREV + ATT_QBLK, width), BF16),
                        pltpu.VMEM((ATT_PREV + ATT_QBLK, width), BF16),
                        pltpu.VMEM((ATT_PAIRS * ATT_NSUB, 2 * ATT_SUB, ATT_WIN), F32)],
        compiler_params=_params(("arbitrary", "arbitrary")),
        name="attn",
    )(proj, proj, proj, proj, proj, bias)


MG_TM = 256


def _merge_kernel(mod_ref, g_ref, x_ref, ya_ref, yb_ref, ga_ref, gb_ref, wa_ref, wb_ref, wo_ref, wr_ref,
                  x1_ref, h2_ref, lt_ref, mixed_even, mixed_odd):
    i = pl.program_id(0)

    @pl.when(i == 0)
    def _():
        mixed_odd[...] = jnp.zeros_like(mixed_odd)

    def run(mixed_new, mixed_done):
        ya = jnp.dot(ya_ref[...], wa_ref[...], preferred_element_type=F32)
        yb = jnp.dot(yb_ref[...], wb_ref[...], preferred_element_type=F32)
        m = ga_ref[...].astype(F32) * ya + gb_ref[...].astype(F32) * yb
        mixed_new[...] = jnp.dot(m.astype(BF16), wo_ref[...], preferred_element_type=F32)

        gate1 = mod_ref[0, 2:3, :]
        shift2 = mod_ref[0, 3:4, :]
        scale2 = mod_ref[0, 4:5, :]
        x1 = x_ref[...] + gate1 * mixed_done[...]
        x1_ref[...] = x1
        h2 = _rms_mod(x1, g_ref[...], scale2, shift2).astype(BF16)
        h2_ref[...] = h2.reshape(h2_ref.shape)
        lt_ref[...] = lax.dot_general(wr_ref[...], h2, (((1,), (1,)), ((), ())), preferred_element_type=F32)

    @pl.when(i % 2 == 0)
    def _():
        run(mixed_even, mixed_odd)

    @pl.when(i % 2 == 1)
    def _():
        run(mixed_odd, mixed_even)


def _merge(x2d, mod3, norm_g, ya, yb, proj, wa, wb, wo, wr_t, seq):
    n, d = x2d.shape
    tps = seq // MG_TM
    tiles = n // MG_TM
    const = lambda shape: pl.BlockSpec(shape, lambda i: (0,) * len(shape), pipeline_mode=pl.Buffered(1))

    def cur(i):
        return jnp.minimum(i, tiles - 1)

    def done(i):
        return jnp.maximum(i - 1, 0)

    return pl.pallas_call(
        _merge_kernel,
        grid=(tiles + 1,),
        in_specs=[pl.BlockSpec((1, N_MOD, d), lambda i: (done(i) // tps, 0, 0)),
                  pl.BlockSpec((1, d), lambda i: (0, 0)),
                  pl.BlockSpec((MG_TM, d), lambda i: (done(i), 0)),
                  pl.BlockSpec((MG_TM, GMLP_WIDTH), lambda i: (cur(i), 0)),
                  pl.BlockSpec((MG_TM, ATT_WIDTH), lambda i: (cur(i), 0)),
                  pl.BlockSpec((MG_TM, d), lambda i: (cur(i), 0)),
                  pl.BlockSpec((MG_TM, d), lambda i: (cur(i), 1)),
                  const((GMLP_WIDTH, d)), const((ATT_WIDTH, d)), const((d, d)),
                  const((ROUTER_ROWS, d))],
        out_specs=[pl.BlockSpec((MG_TM, d), lambda i: (done(i), 0)),
                   pl.BlockSpec((MG_TM, TOKEN_ROWS, LANES), lambda i: (done(i), 0, 0)),
                   pl.BlockSpec((ROUTER_ROWS, MG_TM), lambda i: (0, done(i)))],
        out_shape=[jax.ShapeDtypeStruct((n, d), F32),
                   jax.ShapeDtypeStruct((n, TOKEN_ROWS, LANES), BF16),
                   jax.ShapeDtypeStruct((ROUTER_ROWS, n), F32)],
        scratch_shapes=[pltpu.VMEM((MG_TM, d), F32), pltpu.VMEM((MG_TM, d), F32)],
        compiler_params=_params(("arbitrary",)),
        name="merge",
    )(mod3, norm_g.reshape(1, d), x2d, ya, yb, proj, proj, wa, wb, wo, wr_t)


RT_TN = 512


def _first_argmax(vals, vmax, nrows):
    rows = lax.broadcasted_iota(I32, vals.shape, 0)
    return jnp.min(jnp.where(vals == vmax, rows, nrows), axis=0, keepdims=True)


def _route_kernel(lt_ref, e_ref, r_ref, w_ref, cnt_ref, carry_scr):
    @pl.when(pl.program_id(0) == 0)
    def _():
        carry_scr[...] = jnp.zeros_like(carry_scr)

    gl = lt_ref[0:N_GROUPS, :]
    gmax = jnp.max(gl, axis=0, keepdims=True)
    gidx = _first_argmax(gl, gmax, N_GROUPS)
    gw = 1.0 / jnp.sum(jnp.exp(gl - gmax), axis=0, keepdims=True)

    esel = lt_ref[SUBLANES:SUBLANES + EXPERTS_PER_GROUP, :]
    for g in range(1, N_GROUPS):
        lo = SUBLANES + g * EXPERTS_PER_GROUP
        esel = jnp.where(gidx == g, lt_ref[lo:lo + EXPERTS_PER_GROUP, :], esel)
    rows8 = lax.broadcasted_iota(I32, esel.shape, 0)
    m1 = jnp.max(esel, axis=0, keepdims=True)
    i1 = _first_argmax(esel, m1, EXPERTS_PER_GROUP)
    rest = jnp.where(rows8 == i1, -jnp.inf, esel)
    m2 = jnp.max(rest, axis=0, keepdims=True)
    i2 = _first_argmax(rest, m2, EXPERTS_PER_GROUP)
    z = jnp.exp(m2 - m1)
    w_top = 1.0 / (1.0 + z)
    e0 = gidx * EXPERTS_PER_GROUP + i1
    e1 = gidx * EXPERTS_PER_GROUP + i2
    e_ref[0:1, :] = e0
    e_ref[1:2, :] = e1
    w_ref[0:1, :] = gw * w_top
    w_ref[1:2, :] = gw * (z * w_top)

    rows_e = lax.broadcasted_iota(I32, (N_EXPERTS, RT_TN), 0)
    oh0 = rows_e == e0
    oh1 = rows_e == e1
    oh = jnp.where(oh0 | oh1, 1.0, 0.0)
    src = lax.broadcasted_iota(I32, (RT_TN, RT_TN), 0)
    dst = lax.broadcasted_iota(I32, (RT_TN, RT_TN), 1)
    before = jnp.where(src < dst, 1.0, 0.0).astype(BF16)
    carry = carry_scr[...]
    prefix = jnp.dot(oh.astype(BF16), before, preferred_element_type=F32) + carry[:, 0:1]
    r_ref[0:1, :] = jnp.sum(jnp.where(oh0, prefix, 0.0), axis=0, keepdims=True).astype(I32)
    r_ref[1:2, :] = jnp.sum(jnp.where(oh1, prefix, 0.0), axis=0, keepdims=True).astype(I32)
    carry = carry + jnp.sum(oh, axis=1, keepdims=True)
    carry_scr[...] = carry
    cnt_ref[...] = carry.astype(I32)


def _route(logits_t):
    n = logits_t.shape[1]
    slot = pl.BlockSpec((2, RT_TN), lambda i: (0, i))
    return pl.pallas_call(
        _route_kernel,
        grid=(n // RT_TN,),
        in_specs=[pl.BlockSpec((ROUTER_ROWS, RT_TN), lambda i: (0, i))],
        out_specs=[slot, slot, slot, pl.BlockSpec((N_EXPERTS, LANES), lambda i: (0, 0))],
        out_shape=[jax.ShapeDtypeStruct((2, n), I32),
                   jax.ShapeDtypeStruct((2, n), I32),
                   jax.ShapeDtypeStruct((2, n), F32),
                   jax.ShapeDtypeStruct((N_EXPERTS, LANES), I32)],
        scratch_shapes=[pltpu.VMEM((N_EXPERTS, LANES), F32)],
        compiler_params=_params(("arbitrary",)),
        name="route",
    )(logits_t)


DP_TM = 2048
DMA_UNROLL = 8
PAD_CHUNK = 32


def _token_copy(src_ref, src_row, dst_ref, dst_row, sem):
    return pltpu.make_async_copy(src_ref.at[src_row], dst_ref.at[dst_row], sem)


def _dispatch_kernel(pos0_ref, pos1_ref, pad_start_ref, pad_n_ref, nt_ref, h_ref, xs_ref, zero_scr, sem, zsem, tsem,
                     csem):
    i = pl.program_id(0)
    n_tiles_max = xs_ref.shape[0] // EXPERT_TILE

    def tail_copy(t):
        return pltpu.make_async_copy(zero_scr, xs_ref.at[pl.ds(t * EXPERT_TILE, EXPERT_TILE)], tsem)

    @pl.when(i == 0)
    def _():
        zero_scr[...] = jnp.zeros_like(zero_scr)

        def chunk_copy(row):
            return pltpu.make_async_copy(zero_scr.at[pl.ds(0, PAD_CHUNK)], xs_ref.at[pl.ds(row, PAD_CHUNK)], csem)

        for e in range(N_EXPERTS):
            start = pad_start_ref[e]
            n_chunks = pad_n_ref[e] // PAD_CHUNK

            def issue_chunk(k, c, start=start):
                chunk_copy(start + k * PAD_CHUNK).start()
                return c
            lax.fori_loop(0, n_chunks, issue_chunk, 0)

            def issue(r, c, start=start):
                _token_copy(zero_scr, 0, xs_ref, start + r, zsem).start()
                return c
            lax.fori_loop(n_chunks * PAD_CHUNK, pad_n_ref[e], issue, 0)

        def issue_tail(t, c):
            tail_copy(t).start()
            return c
        lax.fori_loop(nt_ref[0], n_tiles_max, issue_tail, 0)

        for e in range(N_EXPERTS):
            n_chunks = pad_n_ref[e] // PAD_CHUNK

            def drain_chunk(k, c):
                chunk_copy(0).wait()
                return c
            lax.fori_loop(0, n_chunks, drain_chunk, 0)

            def drain(r, c):
                _token_copy(zero_scr, 0, xs_ref, 0, zsem).wait()
                return c
            lax.fori_loop(n_chunks * PAD_CHUNK, pad_n_ref[e], drain, 0)

        def drain_tail(t, c):
            tail_copy(t).wait()
            return c
        lax.fori_loop(nt_ref[0], n_tiles_max, drain_tail, 0)

    base = i * DP_TM

    def issue(r, c):
        _token_copy(h_ref, r, xs_ref, pos0_ref[base + r], sem).start(priority=0)
        _token_copy(h_ref, r, xs_ref, pos1_ref[base + r], sem).start(priority=1)
        return c
    lax.fori_loop(0, DP_TM, issue, 0, unroll=DMA_UNROLL)

    for _ in range(2):
        pltpu.make_async_copy(h_ref, xs_ref.at[pl.ds(0, DP_TM)], sem).wait()


def _dispatch(h2, pos0, pos1, pad_start, pad_n, n_tiles, rows_out):
    n = h2.shape[0]
    grid_spec = pltpu.PrefetchScalarGridSpec(
        num_scalar_prefetch=5,
        grid=(n // DP_TM,),
        in_specs=[pl.BlockSpec((DP_TM, TOKEN_ROWS, LANES), lambda i, *_: (i, 0, 0))],
        out_specs=pl.BlockSpec(memory_space=pl.ANY),
        scratch_shapes=[pltpu.VMEM((EXPERT_TILE, TOKEN_ROWS, LANES), BF16),
                        pltpu.SemaphoreType.DMA(()), pltpu.SemaphoreType.DMA(()),
                        pltpu.SemaphoreType.DMA(()), pltpu.SemaphoreType.DMA(())],
    )
    return pl.pallas_call(
        _dispatch_kernel,
        grid_spec=grid_spec,
        out_shape=jax.ShapeDtypeStruct((rows_out, TOKEN_ROWS, LANES), BF16),
        compiler_params=pltpu.CompilerParams(dimension_semantics=("arbitrary",),
                                             vmem_limit_bytes=VMEM_LIMIT, has_side_effects=True),
        name="dispatch",
    )(pos0, pos1, pad_start, pad_n, n_tiles, h2)


EXPERT_CAST_ROWS = 256


def _cast_rows(src_ref, dst_ref):
    rows_total = dst_ref.shape[0]

    def body(r, c):
        rows = pl.ds(pl.multiple_of(r * EXPERT_CAST_ROWS, EXPERT_CAST_ROWS), EXPERT_CAST_ROWS)
        dst_ref[rows, :] = src_ref[rows, :].astype(BF16)
        return c
    lax.fori_loop(0, rows_total // EXPERT_CAST_ROWS, body, 0)


def _experts_kernel(te_ref, tr_ref, nt_ref, first_ref, slot_ref, next_ref, xs_ref, w1_hbm, w3_hbm, w2_hbm, ys_ref,
                    w1_stage, w3_stage, w2_stage, w1_scr, w3_scr, w2_scr, sems):
    i = pl.program_id(0)

    def fetch(expert, slot):
        return (pltpu.make_async_copy(w1_hbm.at[expert], w1_stage.at[slot], sems.at[slot]),
                pltpu.make_async_copy(w3_hbm.at[expert], w3_stage.at[slot], sems.at[slot]),
                pltpu.make_async_copy(w2_hbm.at[expert], w2_stage.at[slot], sems.at[slot]))

    @pl.when(i == 0)
    def _():
        for copy in fetch(te_ref[0], 0):
            copy.start()

    @pl.when(first_ref[i] == 1)
    def _():
        slot = slot_ref[i]
        for copy in fetch(te_ref[i], slot):
            copy.wait()

        @pl.when(next_ref[i] >= 0)
        def _():
            for copy in fetch(next_ref[i], 1 - slot):
                copy.start()

        _cast_rows(w1_stage.at[slot], w1_scr)
        _cast_rows(w3_stage.at[slot], w3_scr)
        _cast_rows(w2_stage.at[slot], w2_scr)

    @pl.when(i < nt_ref[0])
    def _():
        x = xs_ref[...].reshape(EXPERT_TILE, D_MODEL)
        a = jnp.dot(x, w1_scr[...], preferred_element_type=F32)
        b = jnp.dot(x, w3_scr[...], preferred_element_type=F32)
        act = (a * jax.nn.sigmoid(a)) * b
        y = jnp.dot(act.astype(BF16), w2_scr[...], preferred_element_type=F32)
        ys_ref[...] = y.astype(BF16).reshape(ys_ref.shape)

    @pl.when(i >= nt_ref[0])
    def _():
        ys_ref[...] = jnp.zeros_like(ys_ref)


def _experts(xs, tile_expert, tile_row, n_tiles, tile_first, tile_slot, tile_next, w1, w3, w2):
    rows = xs.shape[0]
    _, d, f = w1.shape
    tile = (EXPERT_TILE, TOKEN_ROWS, LANES)
    hbm = pl.BlockSpec(memory_space=pl.ANY)
    grid_spec = pltpu.PrefetchScalarGridSpec(
        num_scalar_prefetch=6,
        grid=(rows // EXPERT_TILE,),
        in_specs=[pl.BlockSpec(tile, lambda i, te, tr, *_: (tr[i], 0, 0)), hbm, hbm, hbm],
        out_specs=pl.BlockSpec(tile, lambda i, *_: (i, 0, 0)),
        scratch_shapes=[pltpu.VMEM((2, d, f), F32), pltpu.VMEM((2, d, f), F32), pltpu.VMEM((2, f, d), F32),
                        pltpu.VMEM((d, f), BF16), pltpu.VMEM((d, f), BF16), pltpu.VMEM((f, d), BF16),
                        pltpu.SemaphoreType.DMA((2,))],
    )
    return pl.pallas_call(
        _experts_kernel,
        grid_spec=grid_spec,
        out_shape=jax.ShapeDtypeStruct(xs.shape, BF16),
        compiler_params=_params(("arbitrary",)),
        name="experts",
    )(tile_expert, tile_row, n_tiles, tile_first, tile_slot, tile_next, xs, w1, w3, w2)


CB_TM = 256


def _combine_kernel(pos0_ref, pos1_ref, mod_ref, fg_ref, x1_ref, w0_ref, w1_ref, ys_ref, o_ref,
                    y0_scr, y1_scr, sems):
    i = pl.program_id(0)
    slot = i % 2

    def gather(step, buf):
        base = step * CB_TM

        def issue(r, c):
            _token_copy(ys_ref, pos0_ref[base + r], y0_scr.at[buf], r, sems.at[buf]).start(priority=0)
            _token_copy(ys_ref, pos1_ref[base + r], y1_scr.at[buf], r, sems.at[buf]).start(priority=1)
            return c
        lax.fori_loop(0, CB_TM, issue, 0, unroll=DMA_UNROLL)

    @pl.when(i == 0)
    def _():
        gather(i, slot)

    @pl.when(i + 1 < pl.num_programs(0))
    def _():
        gather(i + 1, 1 - slot)

    pltpu.make_async_copy(ys_ref.at[pl.ds(0, CB_TM)], y0_scr.at[slot], sems.at[slot]).wait()
    pltpu.make_async_copy(ys_ref.at[pl.ds(0, CB_TM)], y1_scr.at[slot], sems.at[slot]).wait()

    gate2 = mod_ref[0, 5:6, :]
    y0 = y0_scr[slot].reshape(CB_TM, D_MODEL).astype(F32)
    y1 = y1_scr[slot].reshape(CB_TM, D_MODEL).astype(F32)
    y = w0_ref[...] * y0 + w1_ref[...] * y1
    x2 = x1_ref[...] + gate2 * y
    o_ref[...] = (x2 * lax.rsqrt(jnp.mean(x2 * x2, axis=-1, keepdims=True) + EPS)) * fg_ref[...]


def _combine(x1, mod3, final_g, ys, pos0, pos1, cw0, cw1, seq):
    n, d = x1.shape
    tps = seq // CB_TM
    grid_spec = pltpu.PrefetchScalarGridSpec(
        num_scalar_prefetch=2,
        grid=(n // CB_TM,),
        in_specs=[pl.BlockSpec((1, N_MOD, d), lambda i, *_: (i // tps, 0, 0)),
                  pl.BlockSpec((1, d), lambda i, *_: (0, 0)),
                  pl.BlockSpec((CB_TM, d), lambda i, *_: (i, 0)),
                  pl.BlockSpec((CB_TM, 1), lambda i, *_: (i, 0)),
                  pl.BlockSpec((CB_TM, 1), lambda i, *_: (i, 0)),
                  pl.BlockSpec(memory_space=pl.ANY)],
        out_specs=pl.BlockSpec((CB_TM, d), lambda i, *_: (i, 0)),
        scratch_shapes=[pltpu.VMEM((2, CB_TM, TOKEN_ROWS, LANES), BF16),
                        pltpu.VMEM((2, CB_TM, TOKEN_ROWS, LANES), BF16),
                        pltpu.SemaphoreType.DMA((2,))],
    )
    return pl.pallas_call(
        _combine_kernel,
        grid_spec=grid_spec,
        out_shape=jax.ShapeDtypeStruct((n, d), F32),
        compiler_params=_params(("arbitrary",)),
        name="combine",
    )(pos0, pos1, mod3, final_g.reshape(1, d), x1, cw0, cw1, ys)


def _layer(x2d, c, seq, ada_w, ada_b, norm1_g, w_in, gmlp_ln_g, gmlp_ln_b, gmlp_w_s, gmlp_b_s, rel_bias,
           w_branch_a, w_branch_b, w_out, norm2_g, w_group, w_expert, w1, w3, w2, final_g):
    n, d = x2d.shape
    nb = c.shape[0]
    mod3 = _ada(c, ada_w, ada_b).reshape(nb, N_MOD, d)

    proj = _in_proj(x2d, mod3, norm1_g, w_in.astype(BF16), seq)
    ya = _gmlp(proj, gmlp_ln_g, gmlp_ln_b, gmlp_w_s, gmlp_b_s)
    yb = _attn(proj, rel_bias, seq)

    wr_t = jnp.concatenate([w_group.T, jnp.zeros((SUBLANES - N_GROUPS, d), F32),
                            w_expert.transpose(0, 2, 1).reshape(N_EXPERTS, d)], axis=0).astype(BF16)
    x1, h2, logits_t = _merge(x2d, mod3, norm2_g, ya, yb, proj, w_branch_a.astype(BF16),
                              w_branch_b.astype(BF16), w_out.astype(BF16), wr_t, seq)

    eidx, rank, cw, counts = _route(logits_t)
    counts = counts[:, 0]
    padded = ((counts + EXPERT_TILE - 1) // EXPERT_TILE) * EXPERT_TILE
    ends = jnp.cumsum(padded)
    offs = ends - padded
    experts = jnp.arange(N_EXPERTS, dtype=I32)
    pos = jnp.sum(jnp.where(eidx[:, :, None] == experts, offs, 0), axis=-1) + rank
    rows_out = 2 * n + N_EXPERTS * EXPERT_TILE
    n_tiles_max = rows_out // EXPERT_TILE
    n_tiles = (ends[-1] // EXPERT_TILE).astype(I32)
    tile_row = jnp.minimum(jnp.arange(n_tiles_max, dtype=I32), n_tiles - 1)
    tile_expert = jnp.minimum(jnp.sum(ends[None, :] <= (tile_row * EXPERT_TILE)[:, None], axis=-1),
                              N_EXPERTS - 1).astype(I32)

    tiles = jnp.arange(n_tiles_max, dtype=I32)
    prev_expert = jnp.concatenate([jnp.full((1,), -1, I32), tile_expert[:-1]])
    tile_first = ((tiles < n_tiles) & (tile_expert != prev_expert)).astype(I32)
    tile_slot = ((jnp.cumsum(tile_first) - 1) % 2).astype(I32)
    later_nonempty = (experts[None, :] > experts[:, None]) & (padded > 0)[None, :]
    next_nonempty = jnp.min(jnp.where(later_nonempty, experts[None, :], N_EXPERTS), axis=1)
    next_nonempty = jnp.where(next_nonempty == N_EXPERTS, -1, next_nonempty)
    tile_next = jnp.sum(jnp.where(tile_expert[:, None] == experts, next_nonempty, 0), axis=-1).astype(I32)

    n_tiles = n_tiles.reshape(1)
    xs = _dispatch(h2, pos[0], pos[1], (offs + counts).astype(I32), (padded - counts).astype(I32), n_tiles,
                   rows_out)
    ys = _experts(xs, tile_expert, tile_row, n_tiles, tile_first, tile_slot, tile_next,
                  w1.reshape(N_EXPERTS, d, D_EXPERT), w3.reshape(N_EXPERTS, d, D_EXPERT),
                  w2.reshape(N_EXPERTS, D_EXPERT, d))
    return _combine(x1, mod3, final_g, ys, pos[0], pos[1], cw[0].reshape(n, 1), cw[1].reshape(n, 1), seq)


def kernel(x, c, ada_w, ada_b, norm1_g, w_in, gmlp_ln_g, gmlp_ln_b, gmlp_w_s, gmlp_b_s, rel_bias, w_branch_a,
           w_branch_b, w_out, norm2_g, w_group, w_expert, w1, w3, w2, final_g):
    b, s, d = x.shape
    out = _layer(x.reshape(b * s, d), c, s, ada_w[0], ada_b[0], norm1_g[0], w_in[0], gmlp_ln_g[0], gmlp_ln_b[0],
                 gmlp_w_s[0], gmlp_b_s[0], rel_bias[0], w_branch_a[0], w_branch_b[0], w_out[0], norm2_g[0],
                 w_group[0], w_expert[0], w1[0], w3[0], w2[0], final_g)
    return out.reshape(b, s, d)
```

```python
import functools

import numpy as np
import jax
import jax.numpy as jnp
from jax import lax
from jax.experimental import pallas as pl
from jax.experimental.pallas import tpu as pltpu

F32 = jnp.float32
BF16 = jnp.bfloat16
I32 = jnp.int32

D_MODEL = 2048
CHUNK = 64
EPS = 1e-6
NEG_INF = -1e30
LOG2E = float(np.log2(np.e))
GMLP_BLOCK = 128
GMLP_GROUPS = 8
GMLP_WIDTH = 1024
ATT_HEADS = 16
ATT_HEAD_DIM = 64
ATT_WIDTH = ATT_HEADS * ATT_HEAD_DIM
LEFT_CHUNKS = 8
MAX_REL = 256
N_GROUPS = 4
EXPERTS_PER_GROUP = 8
N_EXPERTS = N_GROUPS * EXPERTS_PER_GROUP
D_EXPERT = 512
N_MOD = 6

LANES = 128
SUBLANES = 8
VMEM_LIMIT = 56 * 1024 * 1024

COL_TILE = 1024
PROJ_COLS = 2 * D_MODEL + 2 * GMLP_WIDTH + 3 * ATT_WIDTH
GATE_TILES = 2 * D_MODEL // COL_TILE
U_TILE = GATE_TILES
V_TILE = GATE_TILES + 1
Q_TILE = GATE_TILES + 2
K_TILE = GATE_TILES + 3
VB_TILE = GATE_TILES + 4

ROUTER_ROWS = SUBLANES + N_EXPERTS

ATT_QBLK = 1024
ATT_PREV = LEFT_CHUNKS * CHUNK
ATT_SUB = 2 * CHUNK
ATT_NSUB = ATT_QBLK // ATT_SUB
ATT_NFIRST = ATT_PREV // ATT_SUB
ATT_PAIRS = 2
ATT_WIN = ATT_SUB + LEFT_CHUNKS * CHUNK

EXPERT_TILE = 256
TOKEN_ROWS = D_MODEL // LANES


def _params(sem, vmem=VMEM_LIMIT):
    return pltpu.CompilerParams(dimension_semantics=sem, vmem_limit_bytes=vmem)


def _rms_mod(x, g, scale, shift):
    y = x * lax.rsqrt(jnp.mean(x * x, axis=-1, keepdims=True) + EPS)
    return (y * g) * (1.0 + scale) + shift


def _ada_kernel(cb_ref, w_ref, b_ref, o_ref, s_scr):
    nb, d, _ = cb_ref.shape
    tn = w_ref.shape[1]
    reps = tn // LANES

    @pl.when(pl.program_id(0) == 0)
    def _():
        cb = cb_ref[...]
        s_scr[...] = cb * jax.nn.sigmoid(cb)

    def body(i, accs):
        r = pl.ds(pl.multiple_of(i * SUBLANES, SUBLANES), SUBLANES)
        w8 = w_ref[r, :]
        return tuple(accs[b] + w8 * jnp.concatenate([s_scr[b, r, :]] * reps, axis=1) for b in range(nb))

    init = tuple(jnp.zeros((SUBLANES, tn), F32) for _ in range(nb))
    accs = lax.fori_loop(0, d // SUBLANES, body, init, unroll=4)
    for b in range(nb):
        o_ref[b:b + 1, :] = jnp.sum(accs[b], axis=0, keepdims=True) + b_ref[...]


ADA_TN = 1024


def _ada(c, ada_w, ada_b):
    nb, d = c.shape
    n = ada_w.shape[1]
    tn = ADA_TN
    cb = jnp.broadcast_to(c[:, :, None], (nb, d, LANES))
    return pl.pallas_call(
        _ada_kernel,
        grid=(n // tn,),
        in_specs=[pl.BlockSpec((nb, d, LANES), lambda j: (0, 0, 0)),
                  pl.BlockSpec((d, tn), lambda j: (0, j)),
                  pl.BlockSpec((1, tn), lambda j: (0, j))],
        out_specs=pl.BlockSpec((nb, tn), lambda j: (0, j)),
        out_shape=jax.ShapeDtypeStruct((nb, n), F32),
        scratch_shapes=[pltpu.VMEM((nb, d, LANES), F32)],
        compiler_params=_params(("arbitrary",)),
        name="ada",
    )(cb, ada_w, ada_b.reshape(1, n))


IN_TM = 1024
IN_STAT_RB = 128
IN_NORM_RB = 16


def _gelu(a):
    return 0.5 * a * (1.0 + lax.erf(a * np.float32(np.sqrt(0.5))))


def _sigmoid(a):
    return 0.5 * jnp.tanh(0.5 * a) + 0.5


def _in_proj_kernel(mod_ref, g_ref, x_ref, w_ref, o_ref, h_scr, gain_scr, shift_scr, inv_scr):
    j = pl.program_id(1)

    @pl.when(j == 0)
    def _():
        d = x_ref.shape[1]
        gain_scr[...] = jnp.broadcast_to(g_ref[...] * (1.0 + mod_ref[0, 1:2, :]), gain_scr.shape)
        shift_scr[...] = jnp.broadcast_to(mod_ref[0, 0:1, :], shift_scr.shape)

        def stats(rb, c):
            rows = pl.ds(pl.multiple_of(rb * IN_STAT_RB, IN_STAT_RB), IN_STAT_RB)
            sq = jnp.zeros((IN_STAT_RB, LANES), F32)
            for k in range(d // LANES):
                xk = x_ref[rows, k * LANES:(k + 1) * LANES]
                sq = sq + xk * xk
            inv = lax.rsqrt(jnp.sum(sq, axis=-1, keepdims=True) * (1.0 / d) + EPS)
            inv_scr[rows, :] = jnp.broadcast_to(inv, (IN_STAT_RB, LANES))
            return c
        lax.fori_loop(0, IN_TM // IN_STAT_RB, stats, 0, unroll=2)

        def apply(rb, c):
            rows = pl.ds(pl.multiple_of(rb * IN_NORM_RB, IN_NORM_RB), IN_NORM_RB)
            inv = inv_scr[rows, :]
            for k in range(d // LANES):
                cols = slice(k * LANES, (k + 1) * LANES)
                y = (x_ref[rows, cols] * inv) * gain_scr[:, cols] + shift_scr[:, cols]
                h_scr[rows, cols] = y.astype(BF16)
            return c
        lax.fori_loop(0, IN_TM // IN_NORM_RB, apply, 0, unroll=2)

    def run(epilogue):
        acc = jnp.dot(h_scr[...], w_ref[...], preferred_element_type=F32)
        o_ref[...] = epilogue(acc).astype(o_ref.dtype)

    @pl.when(j < GATE_TILES)
    def _():
        run(_sigmoid)

    @pl.when((j == U_TILE) | (j == V_TILE))
    def _():
        run(_gelu)

    @pl.when(j == Q_TILE)
    def _():
        run(lambda a: a * np.float32(ATT_HEAD_DIM ** -0.5 * LOG2E))

    @pl.when(j > Q_TILE)
    def _():
        run(lambda a: a)


def _in_proj(x2d, mod3, norm_g, w_in_bf16, seq):
    n, d = x2d.shape
    cols = w_in_bf16.shape[1]
    n_tiles = cols // COL_TILE
    tiles_per_seq = seq // IN_TM
    return pl.pallas_call(
        _in_proj_kernel,
        grid=(n // IN_TM, n_tiles),
        in_specs=[pl.BlockSpec((1, N_MOD, d), lambda i, j: (i // tiles_per_seq, 0, 0)),
                  pl.BlockSpec((1, d), lambda i, j: (0, 0)),
                  pl.BlockSpec((IN_TM, d), lambda i, j: (i, 0)),
                  pl.BlockSpec((d, COL_TILE), lambda i, j: (0, (j + n_tiles - GATE_TILES) % n_tiles))],
        out_specs=pl.BlockSpec((IN_TM, COL_TILE), lambda i, j: (i, j)),
        out_shape=jax.ShapeDtypeStruct((n, cols), BF16),
        scratch_shapes=[pltpu.VMEM((IN_TM, d), BF16), pltpu.VMEM((IN_NORM_RB, d), F32),
                        pltpu.VMEM((IN_NORM_RB, d), F32), pltpu.VMEM((IN_TM, LANES), F32)],
        compiler_params=_params(("arbitrary", "arbitrary")),
        name="in_proj",
    )(mod3, norm_g.reshape(1, d), x2d, w_in_bf16)


GM_TM = 1024


def _gmlp_kernel(u_ref, v_ref, lng_ref, lnb_ref, ws_ref, bs_ref, o_ref):
    t = lax.broadcasted_iota(I32, (GMLP_BLOCK, GMLP_BLOCK), 0)
    s = lax.broadcasted_iota(I32, (GMLP_BLOCK, GMLP_BLOCK), 1)
    causal = (s // CHUNK) <= (t // CHUNK)
    lng = lng_ref[...]
    lnb = lnb_ref[...]
    for blk in range(GM_TM // GMLP_BLOCK):
        rows = slice(blk * GMLP_BLOCK, (blk + 1) * GMLP_BLOCK)
        v = v_ref[rows, :].astype(F32)
        mu = jnp.mean(v, axis=-1, keepdims=True)
        vc = v - mu
        var = jnp.mean(vc * vc, axis=-1, keepdims=True)
        vln = ((vc * lax.rsqrt(var + EPS)) * lng + lnb).astype(BF16)
        for g in range(GMLP_GROUPS):
            cols = slice(g * LANES, (g + 1) * LANES)
            w = jnp.where(causal, ws_ref[g], 0.0).astype(BF16)
            mixed = jnp.dot(w, vln[:, cols], preferred_element_type=F32) + bs_ref[g]
            o_ref[rows, cols] = (u_ref[rows, cols].astype(F32) * mixed).astype(o_ref.dtype)


def _gmlp(proj, ln_g, ln_b, w_s, b_s):
    n = proj.shape[0]
    return pl.pallas_call(
        _gmlp_kernel,
        grid=(n // GM_TM,),
        in_specs=[pl.BlockSpec((GM_TM, GMLP_WIDTH), lambda i: (i, U_TILE)),
                  pl.BlockSpec((GM_TM, GMLP_WIDTH), lambda i: (i, V_TILE)),
                  pl.BlockSpec((1, GMLP_WIDTH), lambda i: (0, 0)),
                  pl.BlockSpec((1, GMLP_WIDTH), lambda i: (0, 0)),
                  pl.BlockSpec((GMLP_GROUPS, GMLP_BLOCK, GMLP_BLOCK), lambda i: (0, 0, 0)),
                  pl.BlockSpec((GMLP_GROUPS, GMLP_BLOCK, 1), lambda i: (0, 0, 0))],
        out_specs=pl.BlockSpec((GM_TM, GMLP_WIDTH), lambda i: (i, 0)),
        out_shape=jax.ShapeDtypeStruct((n, GMLP_WIDTH), BF16),
        compiler_params=_params(("arbitrary",)),
        name="gmlp",
    )(proj, proj, ln_g.reshape(1, -1), ln_b.reshape(1, -1), w_s,
      b_s.reshape(GMLP_GROUPS, GMLP_BLOCK, 1))


def _band_bias(rel_table):
    heads = rel_table.shape[0]
    r = np.arange(ATT_SUB)[:, None]
    w = np.arange(ATT_WIN)[None, :]
    j = w // CHUNK - r // CHUNK
    in_band = (j >= 0) & (j <= LEFT_CHUNKS)
    a = np.arange(ATT_NFIRST)[:, None, None]
    in_seq = np.broadcast_to(w[None] >= ATT_PREV - a * ATT_SUB, (ATT_NFIRST, ATT_SUB, ATT_WIN))
    visible = np.concatenate([in_band[None], in_band[None] & in_seq])
    far = LEFT_CHUNKS * CHUNK + ATT_SUB - 1
    n_clipped = far - MAX_REL + 1
    table = rel_table.astype(F32) * np.float32(LOG2E)
    lo = MAX_REL - (ATT_WIN - 1 - LEFT_CHUNKS * CHUNK)
    diag = jnp.concatenate([jnp.broadcast_to(table[:, 2 * MAX_REL:], (heads, n_clipped)),
                            jnp.flip(table[:, lo:2 * MAX_REL], axis=1),
                            jnp.zeros((heads, 1), F32)], axis=1)
    span = diag.shape[1] - 1
    shifted = jnp.tile(diag, (1, ATT_SUB))[:, :ATT_SUB * span].reshape(heads, ATT_SUB, span)
    bias = shifted[:, :, ATT_SUB - 1:ATT_SUB - 1 + ATT_WIN]
    return jnp.where(jnp.asarray(visible)[None], bias[:, None], NEG_INF)


def _attn_kernel(q_ref, kp_ref, kc_ref, vp_ref, vc_ref, bias_ref, o_ref, k_scr, v_scr, s_scr, *, blocks_per_seq):
    first = pl.program_id(1) % blocks_per_seq == 0
    k_scr[0:ATT_PREV, :] = kp_ref[...]
    k_scr[ATT_PREV:, :] = kc_ref[...]
    v_scr[0:ATT_PREV, :] = vp_ref[...]
    v_scr[ATT_PREV:, :] = vc_ref[...]
    lane = lax.broadcasted_iota(I32, (ATT_SUB, LANES), 1)
    low = lane < ATT_HEAD_DIM

    units = [(pair, a) for pair in range(ATT_PAIRS) for a in range(ATT_NSUB)]
    for u, (pair, a) in enumerate(units):
        lanes = slice(pair * LANES, (pair + 1) * LANES)
        q = q_ref[a * ATT_SUB:(a + 1) * ATT_SUB, lanes]
        zero = jnp.zeros_like(q)
        q2 = jnp.concatenate([jnp.where(low, q, zero), jnp.where(low, zero, q)], axis=0)
        win = slice(a * ATT_SUB, a * ATT_SUB + ATT_WIN)
        s = lax.dot_general(q2, k_scr[win, lanes], (((1,), (1,)), ((), ())), preferred_element_type=F32)
        slab = jnp.where(first, a + 1, 0) if a < ATT_NFIRST else 0
        bias = bias_ref[2 * pair:2 * pair + 2, pl.ds(slab, 1)]
        s_scr[u] = s + bias.reshape(2 * ATT_SUB, ATT_WIN)

    for u, (pair, a) in enumerate(units):
        lanes = slice(pair * LANES, (pair + 1) * LANES)
        rows = slice(a * ATT_SUB, (a + 1) * ATT_SUB)
        win = slice(a * ATT_SUB, a * ATT_SUB + ATT_WIN)
        s = s_scr[u]
        m = jnp.max(s, axis=-1, keepdims=True)
        p = jnp.exp2(s - m)
        l = jnp.sum(p, axis=-1, keepdims=True)
        pv = jnp.dot(p.astype(BF16), v_scr[win, lanes], preferred_element_type=F32) / l
        o_ref[rows, lanes] = jnp.where(low, pv[:ATT_SUB], pv[ATT_SUB:]).astype(o_ref.dtype)


def _attn(proj, rel_table, seq):
    n = proj.shape[0]
    bps = seq // ATT_QBLK
    pairs = ATT_WIDTH // LANES
    width = ATT_PAIRS * LANES
    qc, kc, vc = (Q_TILE * COL_TILE // width, K_TILE * COL_TILE // width, VB_TILE * COL_TILE // width)
    bias = _band_bias(rel_table)

    ratio = ATT_QBLK // ATT_PREV

    def prev(i):
        return jnp.where(i % bps == 0, i * ratio, i * ratio - 1)

    blk = (ATT_QBLK, width)
    pblk = (ATT_PREV, width)
    return pl.pallas_call(
        functools.partial(_attn_kernel, blocks_per_seq=bps),
        grid=(pairs // ATT_PAIRS, n // ATT_QBLK),
        in_specs=[pl.BlockSpec(blk, lambda h, i: (i, qc + h)),
                  pl.BlockSpec(pblk, lambda h, i: (prev(i), kc + h)),
                  pl.BlockSpec(blk, lambda h, i: (i, kc + h)),
                  pl.BlockSpec(pblk, lambda h, i: (prev(i), vc + h)),
                  pl.BlockSpec(blk, lambda h, i: (i, vc + h)),
                  pl.BlockSpec((2 * ATT_PAIRS, 1 + ATT_NFIRST, ATT_SUB, ATT_WIN), lambda h, i: (h, 0, 0, 0))],
        out_specs=pl.BlockSpec(blk, lambda h, i: (i, h)),
        out_shape=jax.ShapeDtypeStruct((n, ATT_WIDTH), BF16),
        scratch_shapes=[pltpu.VMEM((ATT_PREV + ATT_QBLK, width), BF16),
                        pltpu.VMEM((ATT_PREV + ATT_QBLK, width), BF16),
                        pltpu.VMEM((ATT_PAIRS * ATT_NSUB, 2 * ATT_SUB, ATT_WIN), F32)],
        compiler_params=_params(("arbitrary", "arbitrary")),
        name="attn",
    )(proj, proj, proj, proj, proj, bias)


MG_TM = 256


def _merge_kernel(mod_ref, g_ref, x_ref, ya_ref, yb_ref, ga_ref, gb_ref, wa_ref, wb_ref, wo_ref, wr_ref,
                  x1_ref, h2_ref, lt_ref, mixed_even, mixed_odd):
    i = pl.program_id(0)

    @pl.when(i == 0)
    def _():
        mixed_odd[...] = jnp.zeros_like(mixed_odd)

    def run(mixed_new, mixed_done):
        ya = jnp.dot(ya_ref[...], wa_ref[...], preferred_element_type=F32)
        yb = jnp.dot(yb_ref[...], wb_ref[...], preferred_element_type=F32)
        m = ga_ref[...].astype(F32) * ya + gb_ref[...].astype(F32) * yb
        mixed_new[...] = jnp.dot(m.astype(BF16), wo_ref[...], preferred_element_type=F32)

        gate1 = mod_ref[0, 2:3, :]
        shift2 = mod_ref[0, 3:4, :]
        scale2 = mod_ref[0, 4:5, :]
        x1 = x_ref[...] + gate1 * mixed_done[...]
        x1_ref[...] = x1
        h2 = _rms_mod(x1, g_ref[...], scale2, shift2).astype(BF16)
        h2_ref[...] = h2.reshape(h2_ref.shape)
        lt_ref[...] = lax.dot_general(wr_ref[...], h2, (((1,), (1,)), ((), ())), preferred_element_type=F32)

    @pl.when(i % 2 == 0)
    def _():
        run(mixed_even, mixed_odd)

    @pl.when(i % 2 == 1)
    def _():
        run(mixed_odd, mixed_even)


def _merge(x2d, mod3, norm_g, ya, yb, proj, wa, wb, wo, wr_t, seq):
    n, d = x2d.shape
    tps = seq // MG_TM
    tiles = n // MG_TM
    const = lambda shape: pl.BlockSpec(shape, lambda i: (0,) * len(shape), pipeline_mode=pl.Buffered(1))

    def cur(i):
        return jnp.minimum(i, tiles - 1)

    def done(i):
        return jnp.maximum(i - 1, 0)

    return pl.pallas_call(
        _merge_kernel,
        grid=(tiles + 1,),
        in_specs=[pl.BlockSpec((1, N_MOD, d), lambda i: (done(i) // tps, 0, 0)),
                  pl.BlockSpec((1, d), lambda i: (0, 0)),
                  pl.BlockSpec((MG_TM, d), lambda i: (done(i), 0)),
                  pl.BlockSpec((MG_TM, GMLP_WIDTH), lambda i: (cur(i), 0)),
                  pl.BlockSpec((MG_TM, ATT_WIDTH), lambda i: (cur(i), 0)),
                  pl.BlockSpec((MG_TM, d), lambda i: (cur(i), 0)),
                  pl.BlockSpec((MG_TM, d), lambda i: (cur(i), 1)),
                  const((GMLP_WIDTH, d)), const((ATT_WIDTH, d)), const((d, d)),
                  const((ROUTER_ROWS, d))],
        out_specs=[pl.BlockSpec((MG_TM, d), lambda i: (done(i), 0)),
                   pl.BlockSpec((MG_TM, TOKEN_ROWS, LANES), lambda i: (done(i), 0, 0)),
                   pl.BlockSpec((ROUTER_ROWS, MG_TM), lambda i: (0, done(i)))],
        out_shape=[jax.ShapeDtypeStruct((n, d), F32),
                   jax.ShapeDtypeStruct((n, TOKEN_ROWS, LANES), BF16),
                   jax.ShapeDtypeStruct((ROUTER_ROWS, n), F32)],
        scratch_shapes=[pltpu.VMEM((MG_TM, d), F32), pltpu.VMEM((MG_TM, d), F32)],
        compiler_params=_params(("arbitrary",)),
        name="merge",
    )(mod3, norm_g.reshape(1, d), x2d, ya, yb, proj, proj, wa, wb, wo, wr_t)


RT_TN = 512


def _first_argmax(vals, vmax, nrows):
    rows = lax.broadcasted_iota(I32, vals.shape, 0)
    return jnp.min(jnp.where(vals == vmax, rows, nrows), axis=0, keepdims=True)


def _route_kernel(lt_ref, e_ref, r_ref, w_ref, cnt_ref, carry_scr):
    @pl.when(pl.program_id(0) == 0)
    def _():
        carry_scr[...] = jnp.zeros_like(carry_scr)

    gl = lt_ref[0:N_GROUPS, :]
    gmax = jnp.max(gl, axis=0, keepdims=True)
    gidx = _first_argmax(gl, gmax, N_GROUPS)
    gw = 1.0 / jnp.sum(jnp.exp(gl - gmax), axis=0, keepdims=True)

    esel = lt_ref[SUBLANES:SUBLANES + EXPERTS_PER_GROUP, :]
    for g in range(1, N_GROUPS):
        lo = SUBLANES + g * EXPERTS_PER_GROUP
        esel = jnp.where(gidx == g, lt_ref[lo:lo + EXPERTS_PER_GROUP, :], esel)
    rows8 = lax.broadcasted_iota(I32, esel.shape, 0)
    m1 = jnp.max(esel, axis=0, keepdims=True)
    i1 = _first_argmax(esel, m1, EXPERTS_PER_GROUP)
    rest = jnp.where(rows8 == i1, -jnp.inf, esel)
    m2 = jnp.max(rest, axis=0, keepdims=True)
    i2 = _first_argmax(rest, m2, EXPERTS_PER_GROUP)
    z = jnp.exp(m2 - m1)
    w_top = 1.0 / (1.0 + z)
    e0 = gidx * EXPERTS_PER_GROUP + i1
    e1 = gidx * EXPERTS_PER_GROUP + i2
    e_ref[0:1, :] = e0
    e_ref[1:2, :] = e1
    w_ref[0:1, :] = gw * w_top
    w_ref[1:2, :] = gw * (z * w_top)

    rows_e = lax.broadcasted_iota(I32, (N_EXPERTS, RT_TN), 0)
    oh0 = rows_e == e0
    oh1 = rows_e == e1
    oh = jnp.where(oh0 | oh1, 1.0, 0.0)
    src = lax.broadcasted_iota(I32, (RT_TN, RT_TN), 0)
    dst = lax.broadcasted_iota(I32, (RT_TN, RT_TN), 1)
    before = jnp.where(src < dst, 1.0, 0.0).astype(BF16)
    carry = carry_scr[...]
    prefix = jnp.dot(oh.astype(BF16), before, preferred_element_type=F32) + carry[:, 0:1]
    r_ref[0:1, :] = jnp.sum(jnp.where(oh0, prefix, 0.0), axis=0, keepdims=True).astype(I32)
    r_ref[1:2, :] = jnp.sum(jnp.where(oh1, prefix, 0.0), axis=0, keepdims=True).astype(I32)
    carry = carry + jnp.sum(oh, axis=1, keepdims=True)
    carry_scr[...] = carry
    cnt_ref[...] = carry.astype(I32)


def _route(logits_t):
    n = logits_t.shape[1]
    slot = pl.BlockSpec((2, RT_TN), lambda i: (0, i))
    return pl.pallas_call(
        _route_kernel,
        grid=(n // RT_TN,),
        in_specs=[pl.BlockSpec((ROUTER_ROWS, RT_TN), lambda i: (0, i))],
        out_specs=[slot, slot, slot, pl.BlockSpec((N_EXPERTS, LANES), lambda i: (0, 0))],
        out_shape=[jax.ShapeDtypeStruct((2, n), I32),
                   jax.ShapeDtypeStruct((2, n), I32),
                   jax.ShapeDtypeStruct((2, n), F32),
                   jax.ShapeDtypeStruct((N_EXPERTS, LANES), I32)],
        scratch_shapes=[pltpu.VMEM((N_EXPERTS, LANES), F32)],
        compiler_params=_params(("arbitrary",)),
        name="route",
    )(logits_t)


DP_TM = 4096
DMA_UNROLL = 8
PAD_CHUNK = 32


def _token_copy(src_ref, src_row, dst_ref, dst_row, sem):
    return pltpu.make_async_copy(src_ref.at[src_row], dst_ref.at[dst_row], sem)


def _dispatch_kernel(pos0_ref, pos1_ref, pad_start_ref, pad_n_ref, nt_ref, h_ref, xs_ref, zero_scr, sem, zsem, tsem,
                     csem):
    i = pl.program_id(0)
    n_tiles_max = xs_ref.shape[0] // EXPERT_TILE

    def tail_copy(t):
        return pltpu.make_async_copy(zero_scr, xs_ref.at[pl.ds(t * EXPERT_TILE, EXPERT_TILE)], tsem)

    @pl.when(i == 0)
    def _():
        zero_scr[...] = jnp.zeros_like(zero_scr)

        def chunk_copy(row):
            return pltpu.make_async_copy(zero_scr.at[pl.ds(0, PAD_CHUNK)], xs_ref.at[pl.ds(row, PAD_CHUNK)], csem)

        for e in range(N_EXPERTS):
            start = pad_start_ref[e]
            n_chunks = pad_n_ref[e] // PAD_CHUNK

            def issue_chunk(k, c, start=start):
                chunk_copy(start + k * PAD_CHUNK).start()
                return c
            lax.fori_loop(0, n_chunks, issue_chunk, 0)

            def issue(r, c, start=start):
                _token_copy(zero_scr, 0, xs_ref, start + r, zsem).start()
                return c
            lax.fori_loop(n_chunks * PAD_CHUNK, pad_n_ref[e], issue, 0)

        def issue_tail(t, c):
            tail_copy(t).start()
            return c
        lax.fori_loop(nt_ref[0], n_tiles_max, issue_tail, 0)

        for e in range(N_EXPERTS):
            n_chunks = pad_n_ref[e] // PAD_CHUNK

            def drain_chunk(k, c):
                chunk_copy(0).wait()
                return c
            lax.fori_loop(0, n_chunks, drain_chunk, 0)

            def drain(r, c):
                _token_copy(zero_scr, 0, xs_ref, 0, zsem).wait()
                return c
            lax.fori_loop(n_chunks * PAD_CHUNK, pad_n_ref[e], drain, 0)

        def drain_tail(t, c):
            tail_copy(t).wait()
            return c
        lax.fori_loop(nt_ref[0], n_tiles_max, drain_tail, 0)

    base = i * DP_TM

    def issue(r, c):
        _token_copy(h_ref, r, xs_ref, pos0_ref[base + r], sem).start(priority=0)
        _token_copy(h_ref, r, xs_ref, pos1_ref[base + r], sem).start(priority=1)
        return c
    lax.fori_loop(0, DP_TM, issue, 0, unroll=DMA_UNROLL)

    for _ in range(2):
        pltpu.make_async_copy(h_ref, xs_ref.at[pl.ds(0, DP_TM)], sem).wait()


def _dispatch(h2, pos0, pos1, pad_start, pad_n, n_tiles, rows_out):
    n = h2.shape[0]
    grid_spec = pltpu.PrefetchScalarGridSpec(
        num_scalar_prefetch=5,
        grid=(n // DP_TM,),
        in_specs=[pl.BlockSpec((DP_TM, TOKEN_ROWS, LANES), lambda i, *_: (i, 0, 0))],
        out_specs=pl.BlockSpec(memory_space=pl.ANY),
        scratch_shapes=[pltpu.VMEM((EXPERT_TILE, TOKEN_ROWS, LANES), BF16),
                        pltpu.SemaphoreType.DMA(()), pltpu.SemaphoreType.DMA(()),
                        pltpu.SemaphoreType.DMA(()), pltpu.SemaphoreType.DMA(())],
    )
    return pl.pallas_call(
        _dispatch_kernel,
        grid_spec=grid_spec,
        out_shape=jax.ShapeDtypeStruct((rows_out, TOKEN_ROWS, LANES), BF16),
        compiler_params=pltpu.CompilerParams(dimension_semantics=("arbitrary",),
                                             vmem_limit_bytes=VMEM_LIMIT, has_side_effects=True),
        name="dispatch",
    )(pos0, pos1, pad_start, pad_n, n_tiles, h2)


EXPERT_CAST_ROWS = 256


def _cast_rows(src_ref, dst_ref):
    rows_total = dst_ref.shape[0]

    def body(r, c):
        rows = pl.ds(pl.multiple_of(r * EXPERT_CAST_ROWS, EXPERT_CAST_ROWS), EXPERT_CAST_ROWS)
        dst_ref[rows, :] = src_ref[rows, :].astype(BF16)
        return c
    lax.fori_loop(0, rows_total // EXPERT_CAST_ROWS, body, 0)


def _experts_kernel(te_ref, tr_ref, nt_ref, first_ref, slot_ref, next_ref, xs_ref, w1_hbm, w3_hbm, w2_hbm, ys_ref,
                    w1_stage, w3_stage, w2_stage, w1_scr, w3_scr, w2_scr, sems):
    i = pl.program_id(0)

    def fetch(expert, slot):
        return (pltpu.make_async_copy(w1_hbm.at[expert], w1_stage.at[slot], sems.at[slot]),
                pltpu.make_async_copy(w3_hbm.at[expert], w3_stage.at[slot], sems.at[slot]),
                pltpu.make_async_copy(w2_hbm.at[expert], w2_stage.at[slot], sems.at[slot]))

    @pl.when(i == 0)
    def _():
        for copy in fetch(te_ref[0], 0):
            copy.start()

    @pl.when(first_ref[i] == 1)
    def _():
        slot = slot_ref[i]
        for copy in fetch(te_ref[i], slot):
            copy.wait()

        @pl.when(next_ref[i] >= 0)
        def _():
            for copy in fetch(next_ref[i], 1 - slot):
                copy.start()

        _cast_rows(w1_stage.at[slot], w1_scr)
        _cast_rows(w3_stage.at[slot], w3_scr)
        _cast_rows(w2_stage.at[slot], w2_scr)

    @pl.when(i < nt_ref[0])
    def _():
        x = xs_ref[...].reshape(EXPERT_TILE, D_MODEL)
        a = jnp.dot(x, w1_scr[...], preferred_element_type=F32)
        b = jnp.dot(x, w3_scr[...], preferred_element_type=F32)
        act = (a * jax.nn.sigmoid(a)) * b
        y = jnp.dot(act.astype(BF16), w2_scr[...], preferred_element_type=F32)
        ys_ref[...] = y.astype(BF16).reshape(ys_ref.shape)

    @pl.when(i >= nt_ref[0])
    def _():
        ys_ref[...] = jnp.zeros_like(ys_ref)


def _experts(xs, tile_expert, tile_row, n_tiles, tile_first, tile_slot, tile_next, w1, w3, w2):
    rows = xs.shape[0]
    _, d, f = w1.shape
    tile = (EXPERT_TILE, TOKEN_ROWS, LANES)
    hbm = pl.BlockSpec(memory_space=pl.ANY)
    grid_spec = pltpu.PrefetchScalarGridSpec(
        num_scalar_prefetch=6,
        grid=(rows // EXPERT_TILE,),
        in_specs=[pl.BlockSpec(tile, lambda i, te, tr, *_: (tr[i], 0, 0)), hbm, hbm, hbm],
        out_specs=pl.BlockSpec(tile, lambda i, *_: (i, 0, 0)),
        scratch_shapes=[pltpu.VMEM((2, d, f), F32), pltpu.VMEM((2, d, f), F32), pltpu.VMEM((2, f, d), F32),
                        pltpu.VMEM((d, f), BF16), pltpu.VMEM((d, f), BF16), pltpu.VMEM((f, d), BF16),
                        pltpu.SemaphoreType.DMA((2,))],
    )
    return pl.pallas_call(
        _experts_kernel,
        grid_spec=grid_spec,
        out_shape=jax.ShapeDtypeStruct(xs.shape, BF16),
        compiler_params=_params(("arbitrary",)),
        name="experts",
    )(tile_expert, tile_row, n_tiles, tile_first, tile_slot, tile_next, xs, w1, w3, w2)


CB_TM = 256


def _combine_kernel(pos0_ref, pos1_ref, mod_ref, fg_ref, x1_ref, w0_ref, w1_ref, ys_ref, o_ref,
                    y0_scr, y1_scr, sems):
    i = pl.program_id(0)
    slot = i % 2

    def gather(step, buf):
        base = step * CB_TM

        def issue(r, c):
            _token_copy(ys_ref, pos0_ref[base + r], y0_scr.at[buf], r, sems.at[buf]).start(priority=0)
            _token_copy(ys_ref, pos1_ref[base + r], y1_scr.at[buf], r, sems.at[buf]).start(priority=1)
            return c
        lax.fori_loop(0, CB_TM, issue, 0, unroll=DMA_UNROLL)

    @pl.when(i == 0)
    def _():
        gather(i, slot)

    @pl.when(i + 1 < pl.num_programs(0))
    def _():
        gather(i + 1, 1 - slot)

    pltpu.make_async_copy(ys_ref.at[pl.ds(0, CB_TM)], y0_scr.at[slot], sems.at[slot]).wait()
    pltpu.make_async_copy(ys_ref.at[pl.ds(0, CB_TM)], y1_scr.at[slot], sems.at[slot]).wait()

    gate2 = mod_ref[0, 5:6, :]
    y0 = y0_scr[slot].reshape(CB_TM, D_MODEL).astype(F32)
    y1 = y1_scr[slot].reshape(CB_TM, D_MODEL).astype(F32)
    y = w0_ref[...] * y0 + w1_ref[...] * y1
    x2 = x1_ref[...] + gate2 * y
    o_ref[...] = (x2 * lax.rsqrt(jnp.mean(x2 * x2, axis=-1, keepdims=True) + EPS)) * fg_ref[...]


def _combine(x1, mod3, final_g, ys, pos0, pos1, cw0, cw1, seq):
    n, d = x1.shape
    tps = seq // CB_TM
    grid_spec = pltpu.PrefetchScalarGridSpec(
        num_scalar_prefetch=2,
        grid=(n // CB_TM,),
        in_specs=[pl.BlockSpec((1, N_MOD, d), lambda i, *_: (i // tps, 0, 0)),
                  pl.BlockSpec((1, d), lambda i, *_: (0, 0)),
                  pl.BlockSpec((CB_TM, d), lambda i, *_: (i, 0)),
                  pl.BlockSpec((CB_TM, 1), lambda i, *_: (i, 0)),
                  pl.BlockSpec((CB_TM, 1), lambda i, *_: (i, 0)),
                  pl.BlockSpec(memory_space=pl.ANY)],
        out_specs=pl.BlockSpec((CB_TM, d), lambda i, *_: (i, 0)),
        scratch_shapes=[pltpu.VMEM((2, CB_TM, TOKEN_ROWS, LANES), BF16),
                        pltpu.VMEM((2, CB_TM, TOKEN_ROWS, LANES), BF16),
                        pltpu.SemaphoreType.DMA((2,))],
    )
    return pl.pallas_call(
        _combine_kernel,
        grid_spec=grid_spec,
        out_shape=jax.ShapeDtypeStruct((n, d), F32),
        compiler_params=_params(("arbitrary",)),
        name="combine",
    )(pos0, pos1, mod3, final_g.reshape(1, d), x1, cw0, cw1, ys)


def _layer(x2d, c, seq, ada_w, ada_b, norm1_g, w_in, gmlp_ln_g, gmlp_ln_b, gmlp_w_s, gmlp_b_s, rel_bias,
           w_branch_a, w_branch_b, w_out, norm2_g, w_group, w_expert, w1, w3, w2, final_g):
    n, d = x2d.shape
    nb = c.shape[0]
    mod3 = _ada(c, ada_w, ada_b).reshape(nb, N_MOD, d)

    proj = _in_proj(x2d, mod3, norm1_g, w_in.astype(BF16), seq)
    ya = _gmlp(proj, gmlp_ln_g, gmlp_ln_b, gmlp_w_s, gmlp_b_s)
    yb = _attn(proj, rel_bias, seq)

    wr_t = jnp.concatenate([w_group.T, jnp.zeros((SUBLANES - N_GROUPS, d), F32),
                            w_expert.transpose(0, 2, 1).reshape(N_EXPERTS, d)], axis=0).astype(BF16)
    x1, h2, logits_t = _merge(x2d, mod3, norm2_g, ya, yb, proj, w_branch_a.astype(BF16),
                              w_branch_b.astype(BF16), w_out.astype(BF16), wr_t, seq)

    eidx, rank, cw, counts = _route(logits_t)
    counts = counts[:, 0]
    padded = ((counts + EXPERT_TILE - 1) // EXPERT_TILE) * EXPERT_TILE
    ends = jnp.cumsum(padded)
    offs = ends - padded
    experts = jnp.arange(N_EXPERTS, dtype=I32)
    pos = jnp.sum(jnp.where(eidx[:, :, None] == experts, offs, 0), axis=-1) + rank
    rows_out = 2 * n + N_EXPERTS * EXPERT_TILE
    n_tiles_max = rows_out // EXPERT_TILE
    n_tiles = (ends[-1] // EXPERT_TILE).astype(I32)
    tile_row = jnp.minimum(jnp.arange(n_tiles_max, dtype=I32), n_tiles - 1)
    tile_expert = jnp.minimum(jnp.sum(ends[None, :] <= (tile_row * EXPERT_TILE)[:, None], axis=-1),
                              N_EXPERTS - 1).astype(I32)

    tiles = jnp.arange(n_tiles_max, dtype=I32)
    prev_expert = jnp.concatenate([jnp.full((1,), -1, I32), tile_expert[:-1]])
    tile_first = ((tiles < n_tiles) & (tile_expert != prev_expert)).astype(I32)
    tile_slot = ((jnp.cumsum(tile_first) - 1) % 2).astype(I32)
    later_nonempty = (experts[None, :] > experts[:, None]) & (padded > 0)[None, :]
    next_nonempty = jnp.min(jnp.where(later_nonempty, experts[None, :], N_EXPERTS), axis=1)
    next_nonempty = jnp.where(next_nonempty == N_EXPERTS, -1, next_nonempty)
    tile_next = jnp.sum(jnp.where(tile_expert[:, None] == experts, next_nonempty, 0), axis=-1).astype(I32)

    n_tiles = n_tiles.reshape(1)
    xs = _dispatch(h2, pos[0], pos[1], (offs + counts).astype(I32), (padded - counts).astype(I32), n_tiles,
                   rows_out)
    ys = _experts(xs, tile_expert, tile_row, n_tiles, tile_first, tile_slot, tile_next,
                  w1.reshape(N_EXPERTS, d, D_EXPERT), w3.reshape(N_EXPERTS, d, D_EXPERT),
                  w2.reshape(N_EXPERTS, D_EXPERT, d))
    return _combine(x1, mod3, final_g, ys, pos[0], pos[1], cw[0].reshape(n, 1), cw[1].reshape(n, 1), seq)


def kernel(x, c, ada_w, ada_b, norm1_g, w_in, gmlp_ln_g, gmlp_ln_b, gmlp_w_s, gmlp_b_s, rel_bias, w_branch_a,
           w_branch_b, w_out, norm2_g, w_group, w_expert, w1, w3, w2, final_g):
    b, s, d = x.shape
    out = _layer(x.reshape(b * s, d), c, s, ada_w[0], ada_b[0], norm1_g[0], w_in[0], gmlp_ln_g[0], gmlp_ln_b[0],
                 gmlp_w_s[0], gmlp_b_s[0], rel_bias[0], w_branch_a[0], w_branch_b[0], w_out[0], norm2_g[0],
                 w_group[0], w_expert[0], w1[0], w3[0], w2[0], final_g)
    return out.reshape(b, s, d)
```

```python
import functools

import numpy as np
import jax
import jax.numpy as jnp
from jax import lax
from jax.experimental import pallas as pl
from jax.experimental.pallas import tpu as pltpu

F32 = jnp.float32
BF16 = jnp.bfloat16
I32 = jnp.int32

D_MODEL = 2048
CHUNK = 64
EPS = 1e-6
NEG_INF = -1e30
LOG2E = float(np.log2(np.e))
GMLP_BLOCK = 128
GMLP_GROUPS = 8
GMLP_WIDTH = 1024
ATT_HEADS = 16
ATT_HEAD_DIM = 64
ATT_WIDTH = ATT_HEADS * ATT_HEAD_DIM
LEFT_CHUNKS = 8
MAX_REL = 256
N_GROUPS = 4
EXPERTS_PER_GROUP = 8
N_EXPERTS = N_GROUPS * EXPERTS_PER_GROUP
D_EXPERT = 512
N_MOD = 6

LANES = 128
SUBLANES = 8
VMEM_LIMIT = 56 * 1024 * 1024

COL_TILE = 1024
PROJ_COLS = 2 * D_MODEL + 2 * GMLP_WIDTH + 3 * ATT_WIDTH
GATE_TILES = 2 * D_MODEL // COL_TILE
U_TILE = GATE_TILES
V_TILE = GATE_TILES + 1
Q_TILE = GATE_TILES + 2
K_TILE = GATE_TILES + 3
VB_TILE = GATE_TILES + 4

ROUTER_ROWS = SUBLANES + N_EXPERTS

ATT_QBLK = 1024
ATT_PREV = LEFT_CHUNKS * CHUNK
ATT_SUB = 2 * CHUNK
ATT_NSUB = ATT_QBLK // ATT_SUB
ATT_NFIRST = ATT_PREV // ATT_SUB
ATT_PAIRS = 2
ATT_WIN = ATT_SUB + LEFT_CHUNKS * CHUNK

EXPERT_TILE = 256
TOKEN_ROWS = D_MODEL // LANES


def _params(sem, vmem=VMEM_LIMIT):
    return pltpu.CompilerParams(dimension_semantics=sem, vmem_limit_bytes=vmem)


def _rms_mod(x, g, scale, shift):
    y = x * lax.rsqrt(jnp.mean(x * x, axis=-1, keepdims=True) + EPS)
    return (y * g) * (1.0 + scale) + shift


def _ada_kernel(cb_ref, w_ref, b_ref, o_ref, s_scr):
    nb, d, _ = cb_ref.shape
    tn = w_ref.shape[1]
    reps = tn // LANES

    @pl.when(pl.program_id(0) == 0)
    def _():
        cb = cb_ref[...]
        s_scr[...] = cb * jax.nn.sigmoid(cb)

    def body(i, accs):
        r = pl.ds(pl.multiple_of(i * SUBLANES, SUBLANES), SUBLANES)
        w8 = w_ref[r, :]
        return tuple(accs[b] + w8 * jnp.concatenate([s_scr[b, r, :]] * reps, axis=1) for b in range(nb))

    init = tuple(jnp.zeros((SUBLANES, tn), F32) for _ in range(nb))
    accs = lax.fori_loop(0, d // SUBLANES, body, init, unroll=4)
    for b in range(nb):
        o_ref[b:b + 1, :] = jnp.sum(accs[b], axis=0, keepdims=True) + b_ref[...]


ADA_TN = 1024


def _ada(c, ada_w, ada_b):
    nb, d = c.shape
    n = ada_w.shape[1]
    tn = ADA_TN
    cb = jnp.broadcast_to(c[:, :, None], (nb, d, LANES))
    return pl.pallas_call(
        _ada_kernel,
        grid=(n // tn,),
        in_specs=[pl.BlockSpec((nb, d, LANES), lambda j: (0, 0, 0)),
                  pl.BlockSpec((d, tn), lambda j: (0, j)),
                  pl.BlockSpec((1, tn), lambda j: (0, j))],
        out_specs=pl.BlockSpec((nb, tn), lambda j: (0, j)),
        out_shape=jax.ShapeDtypeStruct((nb, n), F32),
        scratch_shapes=[pltpu.VMEM((nb, d, LANES), F32)],
        compiler_params=_params(("arbitrary",)),
        name="ada",
    )(cb, ada_w, ada_b.reshape(1, n))


IN_TM = 1024
IN_STAT_RB = 128
IN_NORM_RB = 16


def _gelu(a):
    return 0.5 * a * (1.0 + lax.erf(a * np.float32(np.sqrt(0.5))))


def _sigmoid(a):
    return 0.5 * jnp.tanh(0.5 * a) + 0.5


def _in_proj_kernel(mod_ref, g_ref, x_ref, w_ref, o_ref, h_scr, gain_scr, shift_scr, inv_scr):
    j = pl.program_id(1)

    @pl.when(j == 0)
    def _():
        d = x_ref.shape[1]
        gain_scr[...] = jnp.broadcast_to(g_ref[...] * (1.0 + mod_ref[0, 1:2, :]), gain_scr.shape)
        shift_scr[...] = jnp.broadcast_to(mod_ref[0, 0:1, :], shift_scr.shape)

        def stats(rb, c):
            rows = pl.ds(pl.multiple_of(rb * IN_STAT_RB, IN_STAT_RB), IN_STAT_RB)
            sq = jnp.zeros((IN_STAT_RB, LANES), F32)
            for k in range(d // LANES):
                xk = x_ref[rows, k * LANES:(k + 1) * LANES]
                sq = sq + xk * xk
            inv = lax.rsqrt(jnp.sum(sq, axis=-1, keepdims=True) * (1.0 / d) + EPS)
            inv_scr[rows, :] = jnp.broadcast_to(inv, (IN_STAT_RB, LANES))
            return c
        lax.fori_loop(0, IN_TM // IN_STAT_RB, stats, 0, unroll=2)

        def apply(rb, c):
            rows = pl.ds(pl.multiple_of(rb * IN_NORM_RB, IN_NORM_RB), IN_NORM_RB)
            inv = inv_scr[rows, :]
            for k in range(d // LANES):
                cols = slice(k * LANES, (k + 1) * LANES)
                y = (x_ref[rows, cols] * inv) * gain_scr[:, cols] + shift_scr[:, cols]
                h_scr[rows, cols] = y.astype(BF16)
            return c
        lax.fori_loop(0, IN_TM // IN_NORM_RB, apply, 0, unroll=4)

    def run(epilogue):
        acc = jnp.dot(h_scr[...], w_ref[...], preferred_element_type=F32)
        o_ref[...] = epilogue(acc).astype(o_ref.dtype)

    @pl.when(j < GATE_TILES)
    def _():
        run(_sigmoid)

    @pl.when((j == U_TILE) | (j == V_TILE))
    def _():
        run(_gelu)

    @pl.when(j == Q_TILE)
    def _():
        run(lambda a: a * np.float32(ATT_HEAD_DIM ** -0.5 * LOG2E))

    @pl.when(j > Q_TILE)
    def _():
        run(lambda a: a)


def _in_proj(x2d, mod3, norm_g, w_in_bf16, seq):
    n, d = x2d.shape
    cols = w_in_bf16.shape[1]
    n_tiles = cols // COL_TILE
    tiles_per_seq = seq // IN_TM
    return pl.pallas_call(
        _in_proj_kernel,
        grid=(n // IN_TM, n_tiles),
        in_specs=[pl.BlockSpec((1, N_MOD, d), lambda i, j: (i // tiles_per_seq, 0, 0)),
                  pl.BlockSpec((1, d), lambda i, j: (0, 0)),
                  pl.BlockSpec((IN_TM, d), lambda i, j: (i, 0)),
                  pl.BlockSpec((d, COL_TILE), lambda i, j: (0, (j + n_tiles - GATE_TILES) % n_tiles))],
        out_specs=pl.BlockSpec((IN_TM, COL_TILE), lambda i, j: (i, j)),
        out_shape=jax.ShapeDtypeStruct((n, cols), BF16),
        scratch_shapes=[pltpu.VMEM((IN_TM, d), BF16), pltpu.VMEM((IN_NORM_RB, d), F32),
                        pltpu.VMEM((IN_NORM_RB, d), F32), pltpu.VMEM((IN_TM, LANES), F32)],
        compiler_params=_params(("arbitrary", "arbitrary")),
        name="in_proj",
    )(mod3, norm_g.reshape(1, d), x2d, w_in_bf16)


GM_TM = 1024


def _gmlp_kernel(u_ref, v_ref, lng_ref, lnb_ref, ws_ref, bs_ref, o_ref):
    t = lax.broadcasted_iota(I32, (GMLP_BLOCK, GMLP_BLOCK), 0)
    s = lax.broadcasted_iota(I32, (GMLP_BLOCK, GMLP_BLOCK), 1)
    causal = (s // CHUNK) <= (t // CHUNK)
    lng = lng_ref[...]
    lnb = lnb_ref[...]
    for blk in range(GM_TM // GMLP_BLOCK):
        rows = slice(blk * GMLP_BLOCK, (blk + 1) * GMLP_BLOCK)
        v = v_ref[rows, :].astype(F32)
        mu = jnp.mean(v, axis=-1, keepdims=True)
        vc = v - mu
        var = jnp.mean(vc * vc, axis=-1, keepdims=True)
        vln = ((vc * lax.rsqrt(var + EPS)) * lng + lnb).astype(BF16)
        for g in range(GMLP_GROUPS):
            cols = slice(g * LANES, (g + 1) * LANES)
            w = jnp.where(causal, ws_ref[g], 0.0).astype(BF16)
            mixed = jnp.dot(w, vln[:, cols], preferred_element_type=F32) + bs_ref[g]
            o_ref[rows, cols] = (u_ref[rows, cols].astype(F32) * mixed).astype(o_ref.dtype)


def _gmlp(proj, ln_g, ln_b, w_s, b_s):
    n = proj.shape[0]
    return pl.pallas_call(
        _gmlp_kernel,
        grid=(n // GM_TM,),
        in_specs=[pl.BlockSpec((GM_TM, GMLP_WIDTH), lambda i: (i, U_TILE)),
                  pl.BlockSpec((GM_TM, GMLP_WIDTH), lambda i: (i, V_TILE)),
                  pl.BlockSpec((1, GMLP_WIDTH), lambda i: (0, 0)),
                  pl.BlockSpec((1, GMLP_WIDTH), lambda i: (0, 0)),
                  pl.BlockSpec((GMLP_GROUPS, GMLP_BLOCK, GMLP_BLOCK), lambda i: (0, 0, 0)),
                  pl.BlockSpec((GMLP_GROUPS, GMLP_BLOCK, 1), lambda i: (0, 0, 0))],
        out_specs=pl.BlockSpec((GM_TM, GMLP_WIDTH), lambda i: (i, 0)),
        out_shape=jax.ShapeDtypeStruct((n, GMLP_WIDTH), BF16),
        compiler_params=_params(("arbitrary",)),
        name="gmlp",
    )(proj, proj, ln_g.reshape(1, -1), ln_b.reshape(1, -1), w_s,
      b_s.reshape(GMLP_GROUPS, GMLP_BLOCK, 1))


def _band_bias(rel_table):
    heads = rel_table.shape[0]
    r = np.arange(ATT_SUB)[:, None]
    w = np.arange(ATT_WIN)[None, :]
    j = w // CHUNK - r // CHUNK
    in_band = (j >= 0) & (j <= LEFT_CHUNKS)
    a = np.arange(ATT_NFIRST)[:, None, None]
    in_seq = np.broadcast_to(w[None] >= ATT_PREV - a * ATT_SUB, (ATT_NFIRST, ATT_SUB, ATT_WIN))
    visible = np.concatenate([in_band[None], in_band[None] & in_seq])
    far = LEFT_CHUNKS * CHUNK + ATT_SUB - 1
    n_clipped = far - MAX_REL + 1
    table = rel_table.astype(F32) * np.float32(LOG2E)
    lo = MAX_REL - (ATT_WIN - 1 - LEFT_CHUNKS * CHUNK)
    diag = jnp.concatenate([jnp.broadcast_to(table[:, 2 * MAX_REL:], (heads, n_clipped)),
                            jnp.flip(table[:, lo:2 * MAX_REL], axis=1),
                            jnp.zeros((heads, 1), F32)], axis=1)
    span = diag.shape[1] - 1
    shifted = jnp.tile(diag, (1, ATT_SUB))[:, :ATT_SUB * span].reshape(heads, ATT_SUB, span)
    bias = shifted[:, :, ATT_SUB - 1:ATT_SUB - 1 + ATT_WIN]
    return jnp.where(jnp.asarray(visible)[None], bias[:, None], NEG_INF)


def _attn_kernel(q_ref, kp_ref, kc_ref, vp_ref, vc_ref, bias_ref, o_ref, k_scr, v_scr, s_scr, *, blocks_per_seq):
    first = pl.program_id(1) % blocks_per_seq == 0
    k_scr[0:ATT_PREV, :] = kp_ref[...]
    k_scr[ATT_PREV:, :] = kc_ref[...]
    v_scr[0:ATT_PREV, :] = vp_ref[...]
    v_scr[ATT_PREV:, :] = vc_ref[...]
    lane = lax.broadcasted_iota(I32, (ATT_SUB, LANES), 1)
    low = lane < ATT_HEAD_DIM

    units = [(pair, a) for pair in range(ATT_PAIRS) for a in range(ATT_NSUB)]
    for u, (pair, a) in enumerate(units):
        lanes = slice(pair * LANES, (pair + 1) * LANES)
        q = q_ref[a * ATT_SUB:(a + 1) * ATT_SUB, lanes]
        zero = jnp.zeros_like(q)
        q2 = jnp.concatenate([jnp.where(low, q, zero), jnp.where(low, zero, q)], axis=0)
        win = slice(a * ATT_SUB, a * ATT_SUB + ATT_WIN)
        s = lax.dot_general(q2, k_scr[win, lanes], (((1,), (1,)), ((), ())), preferred_element_type=F32)
        slab = jnp.where(first, a + 1, 0) if a < ATT_NFIRST else 0
        bias = bias_ref[2 * pair:2 * pair + 2, pl.ds(slab, 1)]
        s_scr[u] = s + bias.reshape(2 * ATT_SUB, ATT_WIN)

    for u, (pair, a) in enumerate(units):
        lanes = slice(pair * LANES, (pair + 1) * LANES)
        rows = slice(a * ATT_SUB, (a + 1) * ATT_SUB)
        win = slice(a * ATT_SUB, a * ATT_SUB + ATT_WIN)
        s = s_scr[u]
        m = jnp.max(s, axis=-1, keepdims=True)
        p = jnp.exp2(s - m)
        l = jnp.sum(p, axis=-1, keepdims=True)
        pv = jnp.dot(p.astype(BF16), v_scr[win, lanes], preferred_element_type=F32) / l
        o_ref[rows, lanes] = jnp.where(low, pv[:ATT_SUB], pv[ATT_SUB:]).astype(o_ref.dtype)


def _attn(proj, rel_table, seq):
    n = proj.shape[0]
    bps = seq // ATT_QBLK
    pairs = ATT_WIDTH // LANES
    width = ATT_PAIRS * LANES
    qc, kc, vc = (Q_TILE * COL_TILE // width, K_TILE * COL_TILE // width, VB_TILE * COL_TILE // width)
    bias = _band_bias(rel_table)

    ratio = ATT_QBLK // ATT_PREV

    def prev(i):
        return jnp.where(i % bps == 0, i * ratio, i * ratio - 1)

    blk = (ATT_QBLK, width)
    pblk = (ATT_PREV, width)
    return pl.pallas_call(
        functools.partial(_attn_kernel, blocks_per_seq=bps),
        grid=(pairs // ATT_PAIRS, n // ATT_QBLK),
        in_specs=[pl.BlockSpec(blk, lambda h, i: (i, qc + h)),
                  pl.BlockSpec(pblk, lambda h, i: (prev(i), kc + h)),
                  pl.BlockSpec(blk, lambda h, i: (i, kc + h)),
                  pl.BlockSpec(pblk, lambda h, i: (prev(i), vc + h)),
                  pl.BlockSpec(blk, lambda h, i: (i, vc + h)),
                  pl.BlockSpec((2 * ATT_PAIRS, 1 + ATT_NFIRST, ATT_SUB, ATT_WIN), lambda h, i: (h, 0, 0, 0))],
        out_specs=pl.BlockSpec(blk, lambda h, i: (i, h)),
        out_shape=jax.ShapeDtypeStruct((n, ATT_WIDTH), BF16),
        scratch_shapes=[pltpu.VMEM((ATT_PREV + ATT_QBLK, width), BF16),
                        pltpu.VMEM((ATT_PREV + ATT_QBLK, width), BF16),
                        pltpu.VMEM((ATT_PAIRS * ATT_NSUB, 2 * ATT_SUB, ATT_WIN), F32)],
        compiler_params=_params(("arbitrary", "arbitrary")),
        name="attn",
    )(proj, proj, proj, proj, proj, bias)


MG_TM = 256


def _merge_kernel(mod_ref, g_ref, x_ref, ya_ref, yb_ref, ga_ref, gb_ref, wa_ref, wb_ref, wo_ref, wr_ref,
                  x1_ref, h2_ref, lt_ref, mixed_even, mixed_odd):
    i = pl.program_id(0)

    @pl.when(i == 0)
    def _():
        mixed_odd[...] = jnp.zeros_like(mixed_odd)

    def run(mixed_new, mixed_done):
        ya = jnp.dot(ya_ref[...], wa_ref[...], preferred_element_type=F32)
        yb = jnp.dot(yb_ref[...], wb_ref[...], preferred_element_type=F32)
        m = ga_ref[...].astype(F32) * ya + gb_ref[...].astype(F32) * yb
        mixed_new[...] = jnp.dot(m.astype(BF16), wo_ref[...], preferred_element_type=F32)

        gate1 = mod_ref[0, 2:3, :]
        shift2 = mod_ref[0, 3:4, :]
        scale2 = mod_ref[0, 4:5, :]
        x1 = x_ref[...] + gate1 * mixed_done[...]
        x1_ref[...] = x1
        h2 = _rms_mod(x1, g_ref[...], scale2, shift2).astype(BF16)
        h2_ref[...] = h2.reshape(h2_ref.shape)
        lt_ref[...] = lax.dot_general(wr_ref[...], h2, (((1,), (1,)), ((), ())), preferred_element_type=F32)

    @pl.when(i % 2 == 0)
    def _():
        run(mixed_even, mixed_odd)

    @pl.when(i % 2 == 1)
    def _():
        run(mixed_odd, mixed_even)


def _merge(x2d, mod3, norm_g, ya, yb, proj, wa, wb, wo, wr_t, seq):
    n, d = x2d.shape
    tps = seq // MG_TM
    tiles = n // MG_TM
    const = lambda shape: pl.BlockSpec(shape, lambda i: (0,) * len(shape), pipeline_mode=pl.Buffered(1))

    def cur(i):
        return jnp.minimum(i, tiles - 1)

    def done(i):
        return jnp.maximum(i - 1, 0)

    return pl.pallas_call(
        _merge_kernel,
        grid=(tiles + 1,),
        in_specs=[pl.BlockSpec((1, N_MOD, d), lambda i: (done(i) // tps, 0, 0)),
                  pl.BlockSpec((1, d), lambda i: (0, 0)),
                  pl.BlockSpec((MG_TM, d), lambda i: (done(i), 0)),
                  pl.BlockSpec((MG_TM, GMLP_WIDTH), lambda i: (cur(i), 0)),
                  pl.BlockSpec((MG_TM, ATT_WIDTH), lambda i: (cur(i), 0)),
                  pl.BlockSpec((MG_TM, d), lambda i: (cur(i), 0)),
                  pl.BlockSpec((MG_TM, d), lambda i: (cur(i), 1)),
                  const((GMLP_WIDTH, d)), const((ATT_WIDTH, d)), const((d, d)),
                  const((ROUTER_ROWS, d))],
        out_specs=[pl.BlockSpec((MG_TM, d), lambda i: (done(i), 0)),
                   pl.BlockSpec((MG_TM, TOKEN_ROWS, LANES), lambda i: (done(i), 0, 0)),
                   pl.BlockSpec((ROUTER_ROWS, MG_TM), lambda i: (0, done(i)))],
        out_shape=[jax.ShapeDtypeStruct((n, d), F32),
                   jax.ShapeDtypeStruct((n, TOKEN_ROWS, LANES), BF16),
                   jax.ShapeDtypeStruct((ROUTER_ROWS, n), F32)],
        scratch_shapes=[pltpu.VMEM((MG_TM, d), F32), pltpu.VMEM((MG_TM, d), F32)],
        compiler_params=_params(("arbitrary",)),
        name="merge",
    )(mod3, norm_g.reshape(1, d), x2d, ya, yb, proj, proj, wa, wb, wo, wr_t)


RT_TN = 1024


def _first_argmax(vals, vmax, nrows):
    rows = lax.broadcasted_iota(I32, vals.shape, 0)
    return jnp.min(jnp.where(vals == vmax, rows, nrows), axis=0, keepdims=True)


def _route_kernel(lt_ref, e_ref, r_ref, w_ref, cnt_ref, carry_scr):
    @pl.when(pl.program_id(0) == 0)
    def _():
        carry_scr[...] = jnp.zeros_like(carry_scr)

    gl = lt_ref[0:N_GROUPS, :]
    gmax = jnp.max(gl, axis=0, keepdims=True)
    gidx = _first_argmax(gl, gmax, N_GROUPS)
    gw = 1.0 / jnp.sum(jnp.exp(gl - gmax), axis=0, keepdims=True)

    esel = lt_ref[SUBLANES:SUBLANES + EXPERTS_PER_GROUP, :]
    for g in range(1, N_GROUPS):
        lo = SUBLANES + g * EXPERTS_PER_GROUP
        esel = jnp.where(gidx == g, lt_ref[lo:lo + EXPERTS_PER_GROUP, :], esel)
    rows8 = lax.broadcasted_iota(I32, esel.shape, 0)
    m1 = jnp.max(esel, axis=0, keepdims=True)
    i1 = _first_argmax(esel, m1, EXPERTS_PER_GROUP)
    rest = jnp.where(rows8 == i1, -jnp.inf, esel)
    m2 = jnp.max(rest, axis=0, keepdims=True)
    i2 = _first_argmax(rest, m2, EXPERTS_PER_GROUP)
    z = jnp.exp(m2 - m1)
    w_top = 1.0 / (1.0 + z)
    e0 = gidx * EXPERTS_PER_GROUP + i1
    e1 = gidx * EXPERTS_PER_GROUP + i2
    e_ref[0:1, :] = e0
    e_ref[1:2, :] = e1
    w_ref[0:1, :] = gw * w_top
    w_ref[1:2, :] = gw * (z * w_top)

    rows_e = lax.broadcasted_iota(I32, (N_EXPERTS, RT_TN), 0)
    oh0 = rows_e == e0
    oh1 = rows_e == e1
    oh = jnp.where(oh0 | oh1, 1.0, 0.0)
    src = lax.broadcasted_iota(I32, (RT_TN, RT_TN), 0)
    dst = lax.broadcasted_iota(I32, (RT_TN, RT_TN), 1)
    before = jnp.where(src < dst, 1.0, 0.0).astype(BF16)
    carry = carry_scr[...]
    prefix = jnp.dot(oh.astype(BF16), before, preferred_element_type=F32) + carry[:, 0:1]
    r_ref[0:1, :] = jnp.sum(jnp.where(oh0, prefix, 0.0), axis=0, keepdims=True).astype(I32)
    r_ref[1:2, :] = jnp.sum(jnp.where(oh1, prefix, 0.0), axis=0, keepdims=True).astype(I32)
    carry = carry + jnp.sum(oh, axis=1, keepdims=True)
    carry_scr[...] = carry
    cnt_ref[...] = carry.astype(I32)


def _route(logits_t):
    n = logits_t.shape[1]
    slot = pl.BlockSpec((2, RT_TN), lambda i: (0, i))
    return pl.pallas_call(
        _route_kernel,
        grid=(n // RT_TN,),
        in_specs=[pl.BlockSpec((ROUTER_ROWS, RT_TN), lambda i: (0, i))],
        out_specs=[slot, slot, slot, pl.BlockSpec((N_EXPERTS, LANES), lambda i: (0, 0))],
        out_shape=[jax.ShapeDtypeStruct((2, n), I32),
                   jax.ShapeDtypeStruct((2, n), I32),
                   jax.ShapeDtypeStruct((2, n), F32),
                   jax.ShapeDtypeStruct((N_EXPERTS, LANES), I32)],
        scratch_shapes=[pltpu.VMEM((N_EXPERTS, LANES), F32)],
        compiler_params=_params(("arbitrary",)),
        name="route",
    )(logits_t)


DP_TM = 4096
DMA_UNROLL = 8
PAD_CHUNK = 32


def _token_copy(src_ref, src_row, dst_ref, dst_row, sem):
    return pltpu.make_async_copy(src_ref.at[src_row], dst_ref.at[dst_row], sem)


def _dispatch_kernel(pos0_ref, pos1_ref, pad_start_ref, pad_n_ref, nt_ref, h_ref, xs_ref, zero_scr, sem, zsem, tsem,
                     csem):
    i = pl.program_id(0)
    n_tiles_max = xs_ref.shape[0] // EXPERT_TILE

    def tail_copy(t):
        return pltpu.make_async_copy(zero_scr, xs_ref.at[pl.ds(t * EXPERT_TILE, EXPERT_TILE)], tsem)

    @pl.when(i == 0)
    def _():
        zero_scr[...] = jnp.zeros_like(zero_scr)

        def chunk_copy(row):
            return pltpu.make_async_copy(zero_scr.at[pl.ds(0, PAD_CHUNK)], xs_ref.at[pl.ds(row, PAD_CHUNK)], csem)

        for e in range(N_EXPERTS):
            start = pad_start_ref[e]
            n_chunks = pad_n_ref[e] // PAD_CHUNK

            def issue_chunk(k, c, start=start):
                chunk_copy(start + k * PAD_CHUNK).start()
                return c
            lax.fori_loop(0, n_chunks, issue_chunk, 0)

            def issue(r, c, start=start):
                _token_copy(zero_scr, 0, xs_ref, start + r, zsem).start()
                return c
            lax.fori_loop(n_chunks * PAD_CHUNK, pad_n_ref[e], issue, 0)

        def issue_tail(t, c):
            tail_copy(t).start()
            return c
        lax.fori_loop(nt_ref[0], n_tiles_max, issue_tail, 0)

        for e in range(N_EXPERTS):
            n_chunks = pad_n_ref[e] // PAD_CHUNK

            def drain_chunk(k, c):
                chunk_copy(0).wait()
                return c
            lax.fori_loop(0, n_chunks, drain_chunk, 0)

            def drain(r, c):
                _token_copy(zero_scr, 0, xs_ref, 0, zsem).wait()
                return c
            lax.fori_loop(n_chunks * PAD_CHUNK, pad_n_ref[e], drain, 0)

        def drain_tail(t, c):
            tail_copy(t).wait()
            return c
        lax.fori_loop(nt_ref[0], n_tiles_max, drain_tail, 0)

    base = i * DP_TM

    def issue(r, c):
        _token_copy(h_ref, r, xs_ref, pos0_ref[base + r], sem).start(priority=0)
        _token_copy(h_ref, r, xs_ref, pos1_ref[base + r], sem).start(priority=1)
        return c
    lax.fori_loop(0, DP_TM, issue, 0, unroll=DMA_UNROLL)

    for _ in range(2):
        pltpu.make_async_copy(h_ref, xs_ref.at[pl.ds(0, DP_TM)], sem).wait()


def _dispatch(h2, pos0, pos1, pad_start, pad_n, n_tiles, rows_out):
    n = h2.shape[0]
    grid_spec = pltpu.PrefetchScalarGridSpec(
        num_scalar_prefetch=5,
        grid=(n // DP_TM,),
        in_specs=[pl.BlockSpec((DP_TM, TOKEN_ROWS, LANES), lambda i, *_: (i, 0, 0))],
        out_specs=pl.BlockSpec(memory_space=pl.ANY),
        scratch_shapes=[pltpu.VMEM((EXPERT_TILE, TOKEN_ROWS, LANES), BF16),
                        pltpu.SemaphoreType.DMA(()), pltpu.SemaphoreType.DMA(()),
                        pltpu.SemaphoreType.DMA(()), pltpu.SemaphoreType.DMA(())],
    )
    return pl.pallas_call(
        _dispatch_kernel,
        grid_spec=grid_spec,
        out_shape=jax.ShapeDtypeStruct((rows_out, TOKEN_ROWS, LANES), BF16),
        compiler_params=pltpu.CompilerParams(dimension_semantics=("arbitrary",),
                                             vmem_limit_bytes=VMEM_LIMIT, has_side_effects=True),
        name="dispatch",
    )(pos0, pos1, pad_start, pad_n, n_tiles, h2)


EXPERT_CAST_ROWS = 256


def _cast_rows(src_ref, dst_ref):
    rows_total = dst_ref.shape[0]

    def body(r, c):
        rows = pl.ds(pl.multiple_of(r * EXPERT_CAST_ROWS, EXPERT_CAST_ROWS), EXPERT_CAST_ROWS)
        dst_ref[rows, :] = src_ref[rows, :].astype(BF16)
        return c
    lax.fori_loop(0, rows_total // EXPERT_CAST_ROWS, body, 0)


def _experts_kernel(te_ref, tr_ref, nt_ref, first_ref, slot_ref, next_ref, xs_ref, w1_hbm, w3_hbm, w2_hbm, ys_ref,
                    w1_stage, w3_stage, w2_stage, w1_scr, w3_scr, w2_scr, sems):
    i = pl.program_id(0)

    def fetch(expert, slot):
        return (pltpu.make_async_copy(w1_hbm.at[expert], w1_stage.at[slot], sems.at[slot]),
                pltpu.make_async_copy(w3_hbm.at[expert], w3_stage.at[slot], sems.at[slot]),
                pltpu.make_async_copy(w2_hbm.at[expert], w2_stage.at[slot], sems.at[slot]))

    @pl.when(i == 0)
    def _():
        for copy in fetch(te_ref[0], 0):
            copy.start()

    @pl.when(first_ref[i] == 1)
    def _():
        slot = slot_ref[i]
        for copy in fetch(te_ref[i], slot):
            copy.wait()

        @pl.when(next_ref[i] >= 0)
        def _():
            for copy in fetch(next_ref[i], 1 - slot):
                copy.start()

        _cast_rows(w1_stage.at[slot], w1_scr)
        _cast_rows(w3_stage.at[slot], w3_scr)
        _cast_rows(w2_stage.at[slot], w2_scr)

    @pl.when(i < nt_ref[0])
    def _():
        x = xs_ref[...].reshape(EXPERT_TILE, D_MODEL)
        a = jnp.dot(x, w1_scr[...], preferred_element_type=F32)
        b = jnp.dot(x, w3_scr[...], preferred_element_type=F32)
        act = (a * jax.nn.sigmoid(a)) * b
        y = jnp.dot(act.astype(BF16), w2_scr[...], preferred_element_type=F32)
        ys_ref[...] = y.astype(BF16).reshape(ys_ref.shape)

    @pl.when(i >= nt_ref[0])
    def _():
        ys_ref[...] = jnp.zeros_like(ys_ref)


def _experts(xs, tile_expert, tile_row, n_tiles, tile_first, tile_slot, tile_next, w1, w3, w2):
    rows = xs.shape[0]
    _, d, f = w1.shape
    tile = (EXPERT_TILE, TOKEN_ROWS, LANES)
    hbm = pl.BlockSpec(memory_space=pl.ANY)
    grid_spec = pltpu.PrefetchScalarGridSpec(
        num_scalar_prefetch=6,
        grid=(rows // EXPERT_TILE,),
        in_specs=[pl.BlockSpec(tile, lambda i, te, tr, *_: (tr[i], 0, 0)), hbm, hbm, hbm],
        out_specs=pl.BlockSpec(tile, lambda i, *_: (i, 0, 0)),
        scratch_shapes=[pltpu.VMEM((2, d, f), F32), pltpu.VMEM((2, d, f), F32), pltpu.VMEM((2, f, d), F32),
                        pltpu.VMEM((d, f), BF16), pltpu.VMEM((d, f), BF16), pltpu.VMEM((f, d), BF16),
                        pltpu.SemaphoreType.DMA((2,))],
    )
    return pl.pallas_call(
        _experts_kernel,
        grid_spec=grid_spec,
        out_shape=jax.ShapeDtypeStruct(xs.shape, BF16),
        compiler_params=_params(("arbitrary",)),
        name="experts",
    )(tile_expert, tile_row, n_tiles, tile_first, tile_slot, tile_next, xs, w1, w3, w2)


CB_TM = 256


def _combine_kernel(pos0_ref, pos1_ref, mod_ref, fg_ref, x1_ref, w0_ref, w1_ref, ys_ref, o_ref,
                    y0_scr, y1_scr, sems):
    i = pl.program_id(0)
    slot = i % 2

    def gather(step, buf):
        base = step * CB_TM

        def issue(r, c):
            _token_copy(ys_ref, pos0_ref[base + r], y0_scr.at[buf], r, sems.at[buf]).start(priority=0)
            _token_copy(ys_ref, pos1_ref[base + r], y1_scr.at[buf], r, sems.at[buf]).start(priority=1)
            return c
        lax.fori_loop(0, CB_TM, issue, 0, unroll=DMA_UNROLL)

    @pl.when(i == 0)
    def _():
        gather(i, slot)

    @pl.when(i + 1 < pl.num_programs(0))
    def _():
        gather(i + 1, 1 - slot)

    pltpu.make_async_copy(ys_ref.at[pl.ds(0, CB_TM)], y0_scr.at[slot], sems.at[slot]).wait()
    pltpu.make_async_copy(ys_ref.at[pl.ds(0, CB_TM)], y1_scr.at[slot], sems.at[slot]).wait()

    gate2 = mod_ref[0, 5:6, :]
    y0 = y0_scr[slot].reshape(CB_TM, D_MODEL).astype(F32)
    y1 = y1_scr[slot].reshape(CB_TM, D_MODEL).astype(F32)
    y = w0_ref[...] * y0 + w1_ref[...] * y1
    x2 = x1_ref[...] + gate2 * y
    o_ref[...] = (x2 * lax.rsqrt(jnp.mean(x2 * x2, axis=-1, keepdims=True) + EPS)) * fg_ref[...]


def _combine(x1, mod3, final_g, ys, pos0, pos1, cw0, cw1, seq):
    n, d = x1.shape
    tps = seq // CB_TM
    grid_spec = pltpu.PrefetchScalarGridSpec(
        num_scalar_prefetch=2,
        grid=(n // CB_TM,),
        in_specs=[pl.BlockSpec((1, N_MOD, d), lambda i, *_: (i // tps, 0, 0)),
                  pl.BlockSpec((1, d), lambda i, *_: (0, 0)),
                  pl.BlockSpec((CB_TM, d), lambda i, *_: (i, 0)),
                  pl.BlockSpec((CB_TM, 1), lambda i, *_: (i, 0)),
                  pl.BlockSpec((CB_TM, 1), lambda i, *_: (i, 0)),
                  pl.BlockSpec(memory_space=pl.ANY)],
        out_specs=pl.BlockSpec((CB_TM, d), lambda i, *_: (i, 0)),
        scratch_shapes=[pltpu.VMEM((2, CB_TM, TOKEN_ROWS, LANES), BF16),
                        pltpu.VMEM((2, CB_TM, TOKEN_ROWS, LANES), BF16),
                        pltpu.SemaphoreType.DMA((2,))],
    )
    return pl.pallas_call(
        _combine_kernel,
        grid_spec=grid_spec,
        out_shape=jax.ShapeDtypeStruct((n, d), F32),
        compiler_params=_params(("arbitrary",)),
        name="combine",
    )(pos0, pos1, mod3, final_g.reshape(1, d), x1, cw0, cw1, ys)


def _layer(x2d, c, seq, ada_w, ada_b, norm1_g, w_in, gmlp_ln_g, gmlp_ln_b, gmlp_w_s, gmlp_b_s, rel_bias,
           w_branch_a, w_branch_b, w_out, norm2_g, w_group, w_expert, w1, w3, w2, final_g):
    n, d = x2d.shape
    nb = c.shape[0]
    mod3 = _ada(c, ada_w, ada_b).reshape(nb, N_MOD, d)

    proj = _in_proj(x2d, mod3, norm1_g, w_in.astype(BF16), seq)
    ya = _gmlp(proj, gmlp_ln_g, gmlp_ln_b, gmlp_w_s, gmlp_b_s)
    yb = _attn(proj, rel_bias, seq)

    wr_t = jnp.concatenate([w_group.T, jnp.zeros((SUBLANES - N_GROUPS, d), F32),
                            w_expert.transpose(0, 2, 1).reshape(N_EXPERTS, d)], axis=0).astype(BF16)
    x1, h2, logits_t = _merge(x2d, mod3, norm2_g, ya, yb, proj, w_branch_a.astype(BF16),
                              w_branch_b.astype(BF16), w_out.astype(BF16), wr_t, seq)

    eidx, rank, cw, counts = _route(logits_t)
    counts = counts[:, 0]
    padded = ((counts + EXPERT_TILE - 1) // EXPERT_TILE) * EXPERT_TILE
    ends = jnp.cumsum(padded)
    offs = ends - padded
    experts = jnp.arange(N_EXPERTS, dtype=I32)
    pos = jnp.sum(jnp.where(eidx[:, :, None] == experts, offs, 0), axis=-1) + rank
    rows_out = 2 * n + N_EXPERTS * EXPERT_TILE
    n_tiles_max = rows_out // EXPERT_TILE
    n_tiles = (ends[-1] // EXPERT_TILE).astype(I32)
    tile_row = jnp.minimum(jnp.arange(n_tiles_max, dtype=I32), n_tiles - 1)
    tile_expert = jnp.minimum(jnp.sum(ends[None, :] <= (tile_row * EXPERT_TILE)[:, None], axis=-1),
                              N_EXPERTS - 1).astype(I32)

    tiles = jnp.arange(n_tiles_max, dtype=I32)
    prev_expert = jnp.concatenate([jnp.full((1,), -1, I32), tile_expert[:-1]])
    tile_first = ((tiles < n_tiles) & (tile_expert != prev_expert)).astype(I32)
    tile_slot = ((jnp.cumsum(tile_first) - 1) % 2).astype(I32)
    later_nonempty = (experts[None, :] > experts[:, None]) & (padded > 0)[None, :]
    next_nonempty = jnp.min(jnp.where(later_nonempty, experts[None, :], N_EXPERTS), axis=1)
    next_nonempty = jnp.where(next_nonempty == N_EXPERTS, -1, next_nonempty)
    tile_next = jnp.sum(jnp.where(tile_expert[:, None] == experts, next_nonempty, 0), axis=-1).astype(I32)

    n_tiles = n_tiles.reshape(1)
    xs = _dispatch(h2, pos[0], pos[1], (offs + counts).astype(I32), (padded - counts).astype(I32), n_tiles,
                   rows_out)
    ys = _experts(xs, tile_expert, tile_row, n_tiles, tile_first, tile_slot, tile_next,
                  w1.reshape(N_EXPERTS, d, D_EXPERT), w3.reshape(N_EXPERTS, d, D_EXPERT),
                  w2.reshape(N_EXPERTS, D_EXPERT, d))
    return _combine(x1, mod3, final_g, ys, pos[0], pos[1], cw[0].reshape(n, 1), cw[1].reshape(n, 1), seq)


def kernel(x, c, ada_w, ada_b, norm1_g, w_in, gmlp_ln_g, gmlp_ln_b, gmlp_w_s, gmlp_b_s, rel_bias, w_branch_a,
           w_branch_b, w_out, norm2_g, w_group, w_expert, w1, w3, w2, final_g):
    b, s, d = x.shape
    out = _layer(x.reshape(b * s, d), c, s, ada_w[0], ada_b[0], norm1_g[0], w_in[0], gmlp_ln_g[0], gmlp_ln_b[0],
                 gmlp_w_s[0], gmlp_b_s[0], rel_bias[0], w_branch_a[0], w_branch_b[0], w_out[0], norm2_g[0],
                 w_group[0], w_expert[0], w1[0], w3[0], w2[0], final_g)
    return out.reshape(b, s, d)
```

```python
import functools

import numpy as np
import jax
import jax.numpy as jnp
from jax import lax
from jax.experimental import pallas as pl
from jax.experimental.pallas import tpu as pltpu

F32 = jnp.float32
BF16 = jnp.bfloat16
I32 = jnp.int32

D_MODEL = 2048
CHUNK = 64
EPS = 1e-6
NEG_INF = -1e30
LOG2E = float(np.log2(np.e))
GMLP_BLOCK = 128
GMLP_GROUPS = 8
GMLP_WIDTH = 1024
ATT_HEADS = 16
ATT_HEAD_DIM = 64
ATT_WIDTH = ATT_HEADS * ATT_HEAD_DIM
LEFT_CHUNKS = 8
MAX_REL = 256
N_GROUPS = 4
EXPERTS_PER_GROUP = 8
N_EXPERTS = N_GROUPS * EXPERTS_PER_GROUP
D_EXPERT = 512
N_MOD = 6

LANES = 128
SUBLANES = 8
VMEM_LIMIT = 56 * 1024 * 1024

COL_TILE = 1024
PROJ_COLS = 2 * D_MODEL + 2 * GMLP_WIDTH + 3 * ATT_WIDTH
GATE_TILES = 2 * D_MODEL // COL_TILE
U_TILE = GATE_TILES
V_TILE = GATE_TILES + 1
Q_TILE = GATE_TILES + 2
K_TILE = GATE_TILES + 3
VB_TILE = GATE_TILES + 4

ROUTER_ROWS = SUBLANES + N_EXPERTS

ATT_QBLK = 1024
ATT_PREV = LEFT_CHUNKS * CHUNK
ATT_SUB = 2 * CHUNK
ATT_NSUB = ATT_QBLK // ATT_SUB
ATT_NFIRST = ATT_PREV // ATT_SUB
ATT_PAIRS = 2
ATT_WIN = ATT_SUB + LEFT_CHUNKS * CHUNK

EXPERT_TILE = 256
TOKEN_ROWS = D_MODEL // LANES


def _params(sem, vmem=VMEM_LIMIT):
    return pltpu.CompilerParams(dimension_semantics=sem, vmem_limit_bytes=vmem)


def _rms_mod(x, g, scale, shift):
    y = x * lax.rsqrt(jnp.mean(x * x, axis=-1, keepdims=True) + EPS)
    return (y * g) * (1.0 + scale) + shift


def _ada_kernel(cb_ref, w_ref, b_ref, o_ref, s_scr):
    nb, d, _ = cb_ref.shape
    tn = w_ref.shape[1]
    reps = tn // LANES

    @pl.when(pl.program_id(0) == 0)
    def _():
        cb = cb_ref[...]
        s_scr[...] = cb * jax.nn.sigmoid(cb)

    def body(i, accs):
        r = pl.ds(pl.multiple_of(i * SUBLANES, SUBLANES), SUBLANES)
        w8 = w_ref[r, :]
        return tuple(accs[b] + w8 * jnp.concatenate([s_scr[b, r, :]] * reps, axis=1) for b in range(nb))

    init = tuple(jnp.zeros((SUBLANES, tn), F32) for _ in range(nb))
    accs = lax.fori_loop(0, d // SUBLANES, body, init, unroll=4)
    for b in range(nb):
        o_ref[b:b + 1, :] = jnp.sum(accs[b], axis=0, keepdims=True) + b_ref[...]


ADA_TN = 1024


def _ada(c, ada_w, ada_b):
    nb, d = c.shape
    n = ada_w.shape[1]
    tn = ADA_TN
    cb = jnp.broadcast_to(c[:, :, None], (nb, d, LANES))
    return pl.pallas_call(
        _ada_kernel,
        grid=(n // tn,),
        in_specs=[pl.BlockSpec((nb, d, LANES), lambda j: (0, 0, 0)),
                  pl.BlockSpec((d, tn), lambda j: (0, j)),
                  pl.BlockSpec((1, tn), lambda j: (0, j))],
        out_specs=pl.BlockSpec((nb, tn), lambda j: (0, j)),
        out_shape=jax.ShapeDtypeStruct((nb, n), F32),
        scratch_shapes=[pltpu.VMEM((nb, d, LANES), F32)],
        compiler_params=_params(("arbitrary",)),
        name="ada",
    )(cb, ada_w, ada_b.reshape(1, n))


IN_TM = 1024
IN_STAT_RB = 128
IN_NORM_RB = 16


def _gelu(a):
    return 0.5 * a * (1.0 + lax.erf(a * np.float32(np.sqrt(0.5))))


def _sigmoid(a):
    return 0.5 * jnp.tanh(0.5 * a) + 0.5


def _in_proj_kernel(mod_ref, g_ref, x_ref, w_ref, o_ref, h_scr, gain_scr, shift_scr, inv_scr):
    j = pl.program_id(1)

    @pl.when(j == 0)
    def _():
        d = x_ref.shape[1]
        gain_scr[...] = jnp.broadcast_to(g_ref[...] * (1.0 + mod_ref[0, 1:2, :]), gain_scr.shape)
        shift_scr[...] = jnp.broadcast_to(mod_ref[0, 0:1, :], shift_scr.shape)

        def stats(rb, c):
            rows = pl.ds(pl.multiple_of(rb * IN_STAT_RB, IN_STAT_RB), IN_STAT_RB)
            sq = jnp.zeros((IN_STAT_RB, LANES), F32)
            for k in range(d // LANES):
                xk = x_ref[rows, k * LANES:(k + 1) * LANES]
                sq = sq + xk * xk
            inv = lax.rsqrt(jnp.sum(sq, axis=-1, keepdims=True) * (1.0 / d) + EPS)
            inv_scr[rows, :] = jnp.broadcast_to(inv, (IN_STAT_RB, LANES))
            return c
        lax.fori_loop(0, IN_TM // IN_STAT_RB, stats, 0, unroll=2)

        def apply(rb, c):
            rows = pl.ds(pl.multiple_of(rb * IN_NORM_RB, IN_NORM_RB), IN_NORM_RB)
            inv = inv_scr[rows, :]
            for k in range(d // LANES):
                cols = slice(k * LANES, (k + 1) * LANES)
                y = (x_ref[rows, cols] * inv) * gain_scr[:, cols] + shift_scr[:, cols]
                h_scr[rows, cols] = y.astype(BF16)
            return c
        lax.fori_loop(0, IN_TM // IN_NORM_RB, apply, 0, unroll=4)

    def run(epilogue):
        acc = jnp.dot(h_scr[...], w_ref[...], preferred_element_type=F32)
        o_ref[...] = epilogue(acc).astype(o_ref.dtype)

    @pl.when(j < GATE_TILES)
    def _():
        run(_sigmoid)

    @pl.when((j == U_TILE) | (j == V_TILE))
    def _():
        run(_gelu)

    @pl.when(j == Q_TILE)
    def _():
        run(lambda a: a * np.float32(ATT_HEAD_DIM ** -0.5 * LOG2E))

    @pl.when(j > Q_TILE)
    def _():
        run(lambda a: a)


def _in_proj(x2d, mod3, norm_g, w_in_bf16, seq):
    n, d = x2d.shape
    cols = w_in_bf16.shape[1]
    n_tiles = cols // COL_TILE
    tiles_per_seq = seq // IN_TM
    return pl.pallas_call(
        _in_proj_kernel,
        grid=(n // IN_TM, n_tiles),
        in_specs=[pl.BlockSpec((1, N_MOD, d), lambda i, j: (i // tiles_per_seq, 0, 0)),
                  pl.BlockSpec((1, d), lambda i, j: (0, 0)),
                  pl.BlockSpec((IN_TM, d), lambda i, j: (i, 0)),
                  pl.BlockSpec((d, COL_TILE), lambda i, j: (0, (j + n_tiles - GATE_TILES) % n_tiles))],
        out_specs=pl.BlockSpec((IN_TM, COL_TILE), lambda i, j: (i, j)),
        out_shape=jax.ShapeDtypeStruct((n, cols), BF16),
        scratch_shapes=[pltpu.VMEM((IN_TM, d), BF16), pltpu.VMEM((IN_NORM_RB, d), F32),
                        pltpu.VMEM((IN_NORM_RB, d), F32), pltpu.VMEM((IN_TM, LANES), F32)],
        compiler_params=_params(("arbitrary", "arbitrary")),
        name="in_proj",
    )(mod3, norm_g.reshape(1, d), x2d, w_in_bf16)


GM_TM = 1024


def _gmlp_kernel(u_ref, v_ref, lng_ref, lnb_ref, ws_ref, bs_ref, o_ref):
    t = lax.broadcasted_iota(I32, (GMLP_BLOCK, GMLP_BLOCK), 0)
    s = lax.broadcasted_iota(I32, (GMLP_BLOCK, GMLP_BLOCK), 1)
    causal = (s // CHUNK) <= (t // CHUNK)
    lng = lng_ref[...]
    lnb = lnb_ref[...]
    for blk in range(GM_TM // GMLP_BLOCK):
        rows = slice(blk * GMLP_BLOCK, (blk + 1) * GMLP_BLOCK)
        v = v_ref[rows, :].astype(F32)
        mu = jnp.mean(v, axis=-1, keepdims=True)
        vc = v - mu
        var = jnp.mean(vc * vc, axis=-1, keepdims=True)
        vln = ((vc * lax.rsqrt(var + EPS)) * lng + lnb).astype(BF16)
        for g in range(GMLP_GROUPS):
            cols = slice(g * LANES, (g + 1) * LANES)
            w = jnp.where(causal, ws_ref[g], 0.0).astype(BF16)
            mixed = jnp.dot(w, vln[:, cols], preferred_element_type=F32) + bs_ref[g]
            o_ref[rows, cols] = (u_ref[rows, cols].astype(F32) * mixed).astype(o_ref.dtype)


def _gmlp(proj, ln_g, ln_b, w_s, b_s):
    n = proj.shape[0]
    return pl.pallas_call(
        _gmlp_kernel,
        grid=(n // GM_TM,),
        in_specs=[pl.BlockSpec((GM_TM, GMLP_WIDTH), lambda i: (i, U_TILE)),
                  pl.BlockSpec((GM_TM, GMLP_WIDTH), lambda i: (i, V_TILE)),
                  pl.BlockSpec((1, GMLP_WIDTH), lambda i: (0, 0)),
                  pl.BlockSpec((1, GMLP_WIDTH), lambda i: (0, 0)),
                  pl.BlockSpec((GMLP_GROUPS, GMLP_BLOCK, GMLP_BLOCK), lambda i: (0, 0, 0)),
                  pl.BlockSpec((GMLP_GROUPS, GMLP_BLOCK, 1), lambda i: (0, 0, 0))],
        out_specs=pl.BlockSpec((GM_TM, GMLP_WIDTH), lambda i: (i, 0)),
        out_shape=jax.ShapeDtypeStruct((n, GMLP_WIDTH), BF16),
        compiler_params=_params(("arbitrary",)),
        name="gmlp",
    )(proj, proj, ln_g.reshape(1, -1), ln_b.reshape(1, -1), w_s,
      b_s.reshape(GMLP_GROUPS, GMLP_BLOCK, 1))


def _band_bias(rel_table):
    heads = rel_table.shape[0]
    r = np.arange(ATT_SUB)[:, None]
    w = np.arange(ATT_WIN)[None, :]
    j = w // CHUNK - r // CHUNK
    in_band = (j >= 0) & (j <= LEFT_CHUNKS)
    a = np.arange(ATT_NFIRST)[:, None, None]
    in_seq = np.broadcast_to(w[None] >= ATT_PREV - a * ATT_SUB, (ATT_NFIRST, ATT_SUB, ATT_WIN))
    visible = np.concatenate([in_band[None], in_band[None] & in_seq])
    far = LEFT_CHUNKS * CHUNK + ATT_SUB - 1
    n_clipped = far - MAX_REL + 1
    table = rel_table.astype(F32) * np.float32(LOG2E)
    lo = MAX_REL - (ATT_WIN - 1 - LEFT_CHUNKS * CHUNK)
    diag = jnp.concatenate([jnp.broadcast_to(table[:, 2 * MAX_REL:], (heads, n_clipped)),
                            jnp.flip(table[:, lo:2 * MAX_REL], axis=1),
                            jnp.zeros((heads, 1), F32)], axis=1)
    span = diag.shape[1] - 1
    shifted = jnp.tile(diag, (1, ATT_SUB))[:, :ATT_SUB * span].reshape(heads, ATT_SUB, span)
    bias = shifted[:, :, ATT_SUB - 1:ATT_SUB - 1 + ATT_WIN]
    return jnp.where(jnp.asarray(visible)[None], bias[:, None], NEG_INF)


def _attn_kernel(q_ref, kp_ref, kc_ref, vp_ref, vc_ref, bias_ref, o_ref, k_scr, v_scr, s_scr, *, blocks_per_seq):
    first = pl.program_id(1) % blocks_per_seq == 0
    k_scr[0:ATT_PREV, :] = kp_ref[...]
    k_scr[ATT_PREV:, :] = kc_ref[...]
    v_scr[0:ATT_PREV, :] = vp_ref[...]
    v_scr[ATT_PREV:, :] = vc_ref[...]
    lane = lax.broadcasted_iota(I32, (ATT_SUB, LANES), 1)
    low = lane < ATT_HEAD_DIM

    units = [(pair, a) for pair in range(ATT_PAIRS) for a in range(ATT_NSUB)]
    for u, (pair, a) in enumerate(units):
        lanes = slice(pair * LANES, (pair + 1) * LANES)
        q = q_ref[a * ATT_SUB:(a + 1) * ATT_SUB, lanes]
        zero = jnp.zeros_like(q)
        q2 = jnp.concatenate([jnp.where(low, q, zero), jnp.where(low, zero, q)], axis=0)
        win = slice(a * ATT_SUB, a * ATT_SUB + ATT_WIN)
        s = lax.dot_general(q2, k_scr[win, lanes], (((1,), (1,)), ((), ())), preferred_element_type=F32)
        slab = jnp.where(first, a + 1, 0) if a < ATT_NFIRST else 0
        bias = bias_ref[2 * pair:2 * pair + 2, pl.ds(slab, 1)]
        s_scr[u] = s + bias.reshape(2 * ATT_SUB, ATT_WIN)

    for u, (pair, a) in enumerate(units):
        lanes = slice(pair * LANES, (pair + 1) * LANES)
        rows = slice(a * ATT_SUB, (a + 1) * ATT_SUB)
        win = slice(a * ATT_SUB, a * ATT_SUB + ATT_WIN)
        s = s_scr[u]
        m = jnp.max(s, axis=-1, keepdims=True)
        p = jnp.exp2(s - m)
        l = jnp.sum(p, axis=-1, keepdims=True)
        pv = jnp.dot(p.astype(BF16), v_scr[win, lanes], preferred_element_type=F32) / l
        o_ref[rows, lanes] = jnp.where(low, pv[:ATT_SUB], pv[ATT_SUB:]).astype(o_ref.dtype)


def _attn(proj, rel_table, seq):
    n = proj.shape[0]
    bps = seq // ATT_QBLK
    pairs = ATT_WIDTH // LANES
    width = ATT_PAIRS * LANES
    qc, kc, vc = (Q_TILE * COL_TILE // width, K_TILE * COL_TILE // width, VB_TILE * COL_TILE // width)
    bias = _band_bias(rel_table)

    ratio = ATT_QBLK // ATT_PREV

    def prev(i):
        return jnp.where(i % bps == 0, i * ratio, i * ratio - 1)

    blk = (ATT_QBLK, width)
    pblk = (ATT_PREV, width)
    return pl.pallas_call(
        functools.partial(_attn_kernel, blocks_per_seq=bps),
        grid=(pairs // ATT_PAIRS, n // ATT_QBLK),
        in_specs=[pl.BlockSpec(blk, lambda h, i: (i, qc + h)),
                  pl.BlockSpec(pblk, lambda h, i: (prev(i), kc + h)),
                  pl.BlockSpec(blk, lambda h, i: (i, kc + h)),
                  pl.BlockSpec(pblk, lambda h, i: (prev(i), vc + h)),
                  pl.BlockSpec(blk, lambda h, i: (i, vc + h)),
                  pl.BlockSpec((2 * ATT_PAIRS, 1 + ATT_NFIRST, ATT_SUB, ATT_WIN), lambda h, i: (h, 0, 0, 0))],
        out_specs=pl.BlockSpec(blk, lambda h, i: (i, h)),
        out_shape=jax.ShapeDtypeStruct((n, ATT_WIDTH), BF16),
        scratch_shapes=[pltpu.VMEM((ATT_PREV + ATT_QBLK, width), BF16),
                        pltpu.VMEM((ATT_PREV + ATT_QBLK, width), BF16),
                        pltpu.VMEM((ATT_PAIRS * ATT_NSUB, 2 * ATT_SUB, ATT_WIN), F32)],
        compiler_params=_params(("arbitrary", "arbitrary")),
        name="attn",
    )(proj, proj, proj, proj, proj, bias)


MG_TM = 256


def _merge_kernel(mod_ref, g_ref, x_ref, ya_ref, yb_ref, ga_ref, gb_ref, wa_ref, wb_ref, wo_ref, wr_ref,
                  x1_ref, h2_ref, lt_ref, mixed_even, mixed_odd):
    i = pl.program_id(0)

    @pl.when(i == 0)
    def _():
        mixed_odd[...] = jnp.zeros_like(mixed_odd)

    def run(mixed_new, mixed_done):
        ya = jnp.dot(ya_ref[...], wa_ref[...], preferred_element_type=F32)
        yb = jnp.dot(yb_ref[...], wb_ref[...], preferred_element_type=F32)
        m = ga_ref[...].astype(F32) * ya + gb_ref[...].astype(F32) * yb
        mixed_new[...] = jnp.dot(m.astype(BF16), wo_ref[...], preferred_element_type=F32)

        gate1 = mod_ref[0, 2:3, :]
        shift2 = mod_ref[0, 3:4, :]
        scale2 = mod_ref[0, 4:5, :]
        x1 = x_ref[...] + gate1 * mixed_done[...]
        x1_ref[...] = x1
        h2 = _rms_mod(x1, g_ref[...], scale2, shift2).astype(BF16)
        h2_ref[...] = h2.reshape(h2_ref.shape)
        lt_ref[...] = lax.dot_general(wr_ref[...], h2, (((1,), (1,)), ((), ())), preferred_element_type=F32)

    @pl.when(i % 2 == 0)
    def _():
        run(mixed_even, mixed_odd)

    @pl.when(i % 2 == 1)
    def _():
        run(mixed_odd, mixed_even)


def _merge(x2d, mod3, norm_g, ya, yb, proj, wa, wb, wo, wr_t, seq):
    n, d = x2d.shape
    tps = seq // MG_TM
    tiles = n // MG_TM
    const = lambda shape: pl.BlockSpec(shape, lambda i: (0,) * len(shape), pipeline_mode=pl.Buffered(1))

    def cur(i):
        return jnp.minimum(i, tiles - 1)

    def done(i):
        return jnp.maximum(i - 1, 0)

    return pl.pallas_call(
        _merge_kernel,
        grid=(tiles + 1,),
        in_specs=[pl.BlockSpec((1, N_MOD, d), lambda i: (done(i) // tps, 0, 0)),
                  pl.BlockSpec((1, d), lambda i: (0, 0)),
                  pl.BlockSpec((MG_TM, d), lambda i: (done(i), 0)),
                  pl.BlockSpec((MG_TM, GMLP_WIDTH), lambda i: (cur(i), 0)),
                  pl.BlockSpec((MG_TM, ATT_WIDTH), lambda i: (cur(i), 0)),
                  pl.BlockSpec((MG_TM, d), lambda i: (cur(i), 0)),
                  pl.BlockSpec((MG_TM, d), lambda i: (cur(i), 1)),
                  const((GMLP_WIDTH, d)), const((ATT_WIDTH, d)), const((d, d)),
                  const((ROUTER_ROWS, d))],
        out_specs=[pl.BlockSpec((MG_TM, d), lambda i: (done(i), 0)),
                   pl.BlockSpec((MG_TM, TOKEN_ROWS, LANES), lambda i: (done(i), 0, 0)),
                   pl.BlockSpec((ROUTER_ROWS, MG_TM), lambda i: (0, done(i)))],
        out_shape=[jax.ShapeDtypeStruct((n, d), F32),
                   jax.ShapeDtypeStruct((n, TOKEN_ROWS, LANES), BF16),
                   jax.ShapeDtypeStruct((ROUTER_ROWS, n), F32)],
        scratch_shapes=[pltpu.VMEM((MG_TM, d), F32), pltpu.VMEM((MG_TM, d), F32)],
        compiler_params=_params(("arbitrary",)),
        name="merge",
    )(mod3, norm_g.reshape(1, d), x2d, ya, yb, proj, proj, wa, wb, wo, wr_t)


RT_TN = 1024


def _first_argmax(vals, vmax, nrows):
    rows = lax.broadcasted_iota(I32, vals.shape, 0)
    return jnp.min(jnp.where(vals == vmax, rows, nrows), axis=0, keepdims=True)


def _route_kernel(lt_ref, e_ref, r_ref, w_ref, cnt_ref, carry_scr):
    @pl.when(pl.program_id(0) == 0)
    def _():
        carry_scr[...] = jnp.zeros_like(carry_scr)

    gl = lt_ref[0:N_GROUPS, :]
    gmax = jnp.max(gl, axis=0, keepdims=True)
    gidx = _first_argmax(gl, gmax, N_GROUPS)
    gw = 1.0 / jnp.sum(jnp.exp(gl - gmax), axis=0, keepdims=True)

    esel = lt_ref[SUBLANES:SUBLANES + EXPERTS_PER_GROUP, :]
    for g in range(1, N_GROUPS):
        lo = SUBLANES + g * EXPERTS_PER_GROUP
        esel = jnp.where(gidx == g, lt_ref[lo:lo + EXPERTS_PER_GROUP, :], esel)
    rows8 = lax.broadcasted_iota(I32, esel.shape, 0)
    m1 = jnp.max(esel, axis=0, keepdims=True)
    i1 = _first_argmax(esel, m1, EXPERTS_PER_GROUP)
    rest = jnp.where(rows8 == i1, -jnp.inf, esel)
    m2 = jnp.max(rest, axis=0, keepdims=True)
    i2 = _first_argmax(rest, m2, EXPERTS_PER_GROUP)
    z = jnp.exp(m2 - m1)
    w_top = 1.0 / (1.0 + z)
    e0 = gidx * EXPERTS_PER_GROUP + i1
    e1 = gidx * EXPERTS_PER_GROUP + i2
    e_ref[0:1, :] = e0
    e_ref[1:2, :] = e1
    w_ref[0:1, :] = gw * w_top
    w_ref[1:2, :] = gw * (z * w_top)

    rows_e = lax.broadcasted_iota(I32, (N_EXPERTS, RT_TN), 0)
    oh0 = rows_e == e0
    oh1 = rows_e == e1
    oh = jnp.where(oh0 | oh1, 1.0, 0.0)
    src = lax.broadcasted_iota(I32, (RT_TN, RT_TN), 0)
    dst = lax.broadcasted_iota(I32, (RT_TN, RT_TN), 1)
    before = jnp.where(src < dst, 1.0, 0.0).astype(BF16)
    carry = carry_scr[...]
    prefix = jnp.dot(oh.astype(BF16), before, preferred_element_type=F32) + carry[:, 0:1]
    r_ref[0:1, :] = jnp.sum(jnp.where(oh0, prefix, 0.0), axis=0, keepdims=True).astype(I32)
    r_ref[1:2, :] = jnp.sum(jnp.where(oh1, prefix, 0.0), axis=0, keepdims=True).astype(I32)
    carry = carry + jnp.sum(oh, axis=1, keepdims=True)
    carry_scr[...] = carry
    cnt_ref[...] = carry.astype(I32)


def _route(logits_t):
    n = logits_t.shape[1]
    slot = pl.BlockSpec((2, RT_TN), lambda i: (0, i))
    return pl.pallas_call(
        _route_kernel,
        grid=(n // RT_TN,),
        in_specs=[pl.BlockSpec((ROUTER_ROWS, RT_TN), lambda i: (0, i))],
        out_specs=[slot, slot, slot, pl.BlockSpec((N_EXPERTS, LANES), lambda i: (0, 0))],
        out_shape=[jax.ShapeDtypeStruct((2, n), I32),
                   jax.ShapeDtypeStruct((2, n), I32),
                   jax.ShapeDtypeStruct((2, n), F32),
                   jax.ShapeDtypeStruct((N_EXPERTS, LANES), I32)],
        scratch_shapes=[pltpu.VMEM((N_EXPERTS, LANES), F32)],
        compiler_params=_params(("arbitrary",)),
        name="route",
    )(logits_t)


DP_TM = 4096
DMA_UNROLL = 8
PAD_CHUNK = 32


def _token_copy(src_ref, src_row, dst_ref, dst_row, sem):
    return pltpu.make_async_copy(src_ref.at[src_row], dst_ref.at[dst_row], sem)


def _dispatch_kernel(pos0_ref, pos1_ref, pad_start_ref, pad_n_ref, nt_ref, h_ref, xs_ref, zero_scr, sem, zsem, tsem,
                     csem):
    i = pl.program_id(0)
    n_tiles_max = xs_ref.shape[0] // EXPERT_TILE

    def tail_copy(t):
        return pltpu.make_async_copy(zero_scr, xs_ref.at[pl.ds(t * EXPERT_TILE, EXPERT_TILE)], tsem)

    @pl.when(i == 0)
    def _():
        zero_scr[...] = jnp.zeros_like(zero_scr)

        def chunk_copy(row):
            return pltpu.make_async_copy(zero_scr.at[pl.ds(0, PAD_CHUNK)], xs_ref.at[pl.ds(row, PAD_CHUNK)], csem)

        for e in range(N_EXPERTS):
            start = pad_start_ref[e]
            n_chunks = pad_n_ref[e] // PAD_CHUNK

            def issue_chunk(k, c, start=start):
                chunk_copy(start + k * PAD_CHUNK).start()
                return c
            lax.fori_loop(0, n_chunks, issue_chunk, 0)

            def issue(r, c, start=start):
                _token_copy(zero_scr, 0, xs_ref, start + r, zsem).start()
                return c
            lax.fori_loop(n_chunks * PAD_CHUNK, pad_n_ref[e], issue, 0)

        def issue_tail(t, c):
            tail_copy(t).start()
            return c
        lax.fori_loop(nt_ref[0], n_tiles_max, issue_tail, 0)

        for e in range(N_EXPERTS):
            n_chunks = pad_n_ref[e] // PAD_CHUNK

            def drain_chunk(k, c):
                chunk_copy(0).wait()
                return c
            lax.fori_loop(0, n_chunks, drain_chunk, 0)

            def drain(r, c):
                _token_copy(zero_scr, 0, xs_ref, 0, zsem).wait()
                return c
            lax.fori_loop(n_chunks * PAD_CHUNK, pad_n_ref[e], drain, 0)

        def drain_tail(t, c):
            tail_copy(t).wait()
            return c
        lax.fori_loop(nt_ref[0], n_tiles_max, drain_tail, 0)

    base = i * DP_TM

    def issue(r, c):
        _token_copy(h_ref, r, xs_ref, pos0_ref[base + r], sem).start(priority=0)
        _token_copy(h_ref, r, xs_ref, pos1_ref[base + r], sem).start(priority=1)
        return c
    lax.fori_loop(0, DP_TM, issue, 0, unroll=DMA_UNROLL)

    for _ in range(2):
        pltpu.make_async_copy(h_ref, xs_ref.at[pl.ds(0, DP_TM)], sem).wait()


def _dispatch(h2, pos0, pos1, pad_start, pad_n, n_tiles, rows_out):
    n = h2.shape[0]
    grid_spec = pltpu.PrefetchScalarGridSpec(
        num_scalar_prefetch=5,
        grid=(n // DP_TM,),
        in_specs=[pl.BlockSpec((DP_TM, TOKEN_ROWS, LANES), lambda i, *_: (i, 0, 0))],
        out_specs=pl.BlockSpec(memory_space=pl.ANY),
        scratch_shapes=[pltpu.VMEM((EXPERT_TILE, TOKEN_ROWS, LANES), BF16),
                        pltpu.SemaphoreType.DMA(()), pltpu.SemaphoreType.DMA(()),
                        pltpu.SemaphoreType.DMA(()), pltpu.SemaphoreType.DMA(())],
    )
    return pl.pallas_call(
        _dispatch_kernel,
        grid_spec=grid_spec,
        out_shape=jax.ShapeDtypeStruct((rows_out, TOKEN_ROWS, LANES), BF16),
        compiler_params=pltpu.CompilerParams(dimension_semantics=("arbitrary",),
                                             vmem_limit_bytes=VMEM_LIMIT, has_side_effects=True),
        name="dispatch",
    )(pos0, pos1, pad_start, pad_n, n_tiles, h2)


EXPERT_CAST_ROWS = 256


def _cast_rows(src_ref, dst_ref):
    rows_total = dst_ref.shape[0]

    def body(r, c):
        rows = pl.ds(pl.multiple_of(r * EXPERT_CAST_ROWS, EXPERT_CAST_ROWS), EXPERT_CAST_ROWS)
        dst_ref[rows, :] = src_ref[rows, :].astype(BF16)
        return c
    lax.fori_loop(0, rows_total // EXPERT_CAST_ROWS, body, 0)


def _experts_kernel(te_ref, tr_ref, nt_ref, first_ref, slot_ref, next_ref, xs_ref, w1_hbm, w3_hbm, w2_hbm, ys_ref,
                    w1_stage, w3_stage, w2_stage, w1_scr, w3_scr, w2_scr, sems):
    i = pl.program_id(0)

    def fetch(expert, slot):
        return (pltpu.make_async_copy(w1_hbm.at[expert], w1_stage.at[slot], sems.at[slot]),
                pltpu.make_async_copy(w3_hbm.at[expert], w3_stage.at[slot], sems.at[slot]),
                pltpu.make_async_copy(w2_hbm.at[expert], w2_stage.at[slot], sems.at[slot]))

    @pl.when(i == 0)
    def _():
        for copy in fetch(te_ref[0], 0):
            copy.start()

    @pl.when(first_ref[i] == 1)
    def _():
        slot = slot_ref[i]
        for copy in fetch(te_ref[i], slot):
            copy.wait()

        @pl.when(next_ref[i] >= 0)
        def _():
            for copy in fetch(next_ref[i], 1 - slot):
                copy.start()

        _cast_rows(w1_stage.at[slot], w1_scr)
        _cast_rows(w3_stage.at[slot], w3_scr)
        _cast_rows(w2_stage.at[slot], w2_scr)

    @pl.when(i < nt_ref[0])
    def _():
        x = xs_ref[...].reshape(EXPERT_TILE, D_MODEL)
        a = jnp.dot(x, w1_scr[...], preferred_element_type=F32)
        b = jnp.dot(x, w3_scr[...], preferred_element_type=F32)
        act = (a * jax.nn.sigmoid(a)) * b
        y = jnp.dot(act.astype(BF16), w2_scr[...], preferred_element_type=F32)
        ys_ref[...] = y.astype(BF16).reshape(ys_ref.shape)

    @pl.when(i >= nt_ref[0])
    def _():
        ys_ref[...] = jnp.zeros_like(ys_ref)


def _experts(xs, tile_expert, tile_row, n_tiles, tile_first, tile_slot, tile_next, w1, w3, w2):
    rows = xs.shape[0]
    _, d, f = w1.shape
    tile = (EXPERT_TILE, TOKEN_ROWS, LANES)
    hbm = pl.BlockSpec(memory_space=pl.ANY)
    grid_spec = pltpu.PrefetchScalarGridSpec(
        num_scalar_prefetch=6,
        grid=(rows // EXPERT_TILE,),
        in_specs=[pl.BlockSpec(tile, lambda i, te, tr, *_: (tr[i], 0, 0)), hbm, hbm, hbm],
        out_specs=pl.BlockSpec(tile, lambda i, *_: (i, 0, 0)),
        scratch_shapes=[pltpu.VMEM((2, d, f), F32), pltpu.VMEM((2, d, f), F32), pltpu.VMEM((2, f, d), F32),
                        pltpu.VMEM((d, f), BF16), pltpu.VMEM((d, f), BF16), pltpu.VMEM((f, d), BF16),
                        pltpu.SemaphoreType.DMA((2,))],
    )
    return pl.pallas_call(
        _experts_kernel,
        grid_spec=grid_spec,
        out_shape=jax.ShapeDtypeStruct(xs.shape, BF16),
        compiler_params=_params(("arbitrary",)),
        name="experts",
    )(tile_expert, tile_row, n_tiles, tile_first, tile_slot, tile_next, xs, w1, w3, w2)


CB_TM = 256
CB_RING = 3


def _combine_kernel(pos0_ref, pos1_ref, mod_ref, fg_ref, x1_ref, w0_ref, w1_ref, ys_ref, o_ref,
                    y0_scr, y1_scr, sems):
    i = pl.program_id(0)
    steps = pl.num_programs(0)
    slot = i % CB_RING

    def gather(step, buf):
        base = step * CB_TM

        def issue(r, c):
            _token_copy(ys_ref, pos0_ref[base + r], y0_scr.at[buf], r, sems.at[buf]).start(priority=0)
            _token_copy(ys_ref, pos1_ref[base + r], y1_scr.at[buf], r, sems.at[buf]).start(priority=1)
            return c
        lax.fori_loop(0, CB_TM, issue, 0, unroll=DMA_UNROLL)

    @pl.when(i == 0)
    def _():
        for ahead in range(CB_RING - 1):
            gather(ahead, ahead)

    @pl.when(i + CB_RING - 1 < steps)
    def _():
        gather(i + CB_RING - 1, (i + CB_RING - 1) % CB_RING)

    pltpu.make_async_copy(ys_ref.at[pl.ds(0, CB_TM)], y0_scr.at[slot], sems.at[slot]).wait()
    pltpu.make_async_copy(ys_ref.at[pl.ds(0, CB_TM)], y1_scr.at[slot], sems.at[slot]).wait()

    gate2 = mod_ref[0, 5:6, :]
    y0 = y0_scr[slot].reshape(CB_TM, D_MODEL).astype(F32)
    y1 = y1_scr[slot].reshape(CB_TM, D_MODEL).astype(F32)
    y = w0_ref[...] * y0 + w1_ref[...] * y1
    x2 = x1_ref[...] + gate2 * y
    o_ref[...] = (x2 * lax.rsqrt(jnp.mean(x2 * x2, axis=-1, keepdims=True) + EPS)) * fg_ref[...]


def _combine(x1, mod3, final_g, ys, pos0, pos1, cw0, cw1, seq):
    n, d = x1.shape
    tps = seq // CB_TM
    grid_spec = pltpu.PrefetchScalarGridSpec(
        num_scalar_prefetch=2,
        grid=(n // CB_TM,),
        in_specs=[pl.BlockSpec((1, N_MOD, d), lambda i, *_: (i // tps, 0, 0)),
                  pl.BlockSpec((1, d), lambda i, *_: (0, 0)),
                  pl.BlockSpec((CB_TM, d), lambda i, *_: (i, 0)),
                  pl.BlockSpec((CB_TM, 1), lambda i, *_: (i, 0)),
                  pl.BlockSpec((CB_TM, 1), lambda i, *_: (i, 0)),
                  pl.BlockSpec(memory_space=pl.ANY)],
        out_specs=pl.BlockSpec((CB_TM, d), lambda i, *_: (i, 0)),
        scratch_shapes=[pltpu.VMEM((CB_RING, CB_TM, TOKEN_ROWS, LANES), BF16),
                        pltpu.VMEM((CB_RING, CB_TM, TOKEN_ROWS, LANES), BF16),
                        pltpu.SemaphoreType.DMA((CB_RING,))],
    )
    return pl.pallas_call(
        _combine_kernel,
        grid_spec=grid_spec,
        out_shape=jax.ShapeDtypeStruct((n, d), F32),
        compiler_params=_params(("arbitrary",)),
        name="combine",
    )(pos0, pos1, mod3, final_g.reshape(1, d), x1, cw0, cw1, ys)


def _layer(x2d, c, seq, ada_w, ada_b, norm1_g, w_in, gmlp_ln_g, gmlp_ln_b, gmlp_w_s, gmlp_b_s, rel_bias,
           w_branch_a, w_branch_b, w_out, norm2_g, w_group, w_expert, w1, w3, w2, final_g):
    n, d = x2d.shape
    nb = c.shape[0]
    mod3 = _ada(c, ada_w, ada_b).reshape(nb, N_MOD, d)

    proj = _in_proj(x2d, mod3, norm1_g, w_in.astype(BF16), seq)
    ya = _gmlp(proj, gmlp_ln_g, gmlp_ln_b, gmlp_w_s, gmlp_b_s)
    yb = _attn(proj, rel_bias, seq)

    wr_t = jnp.concatenate([w_group.T, jnp.zeros((SUBLANES - N_GROUPS, d), F32),
                            w_expert.transpose(0, 2, 1).reshape(N_EXPERTS, d)], axis=0).astype(BF16)
    x1, h2, logits_t = _merge(x2d, mod3, norm2_g, ya, yb, proj, w_branch_a.astype(BF16),
                              w_branch_b.astype(BF16), w_out.astype(BF16), wr_t, seq)

    eidx, rank, cw, counts = _route(logits_t)
    counts = counts[:, 0]
    padded = ((counts + EXPERT_TILE - 1) // EXPERT_TILE) * EXPERT_TILE
    ends = jnp.cumsum(padded)
    offs = ends - padded
    experts = jnp.arange(N_EXPERTS, dtype=I32)
    pos = jnp.sum(jnp.where(eidx[:, :, None] == experts, offs, 0), axis=-1) + rank
    rows_out = 2 * n + N_EXPERTS * EXPERT_TILE
    n_tiles_max = rows_out // EXPERT_TILE
    n_tiles = (ends[-1] // EXPERT_TILE).astype(I32)
    tile_row = jnp.minimum(jnp.arange(n_tiles_max, dtype=I32), n_tiles - 1)
    tile_expert = jnp.minimum(jnp.sum(ends[None, :] <= (tile_row * EXPERT_TILE)[:, None], axis=-1),
                              N_EXPERTS - 1).astype(I32)

    tiles = jnp.arange(n_tiles_max, dtype=I32)
    prev_expert = jnp.concatenate([jnp.full((1,), -1, I32), tile_expert[:-1]])
    tile_first = ((tiles < n_tiles) & (tile_expert != prev_expert)).astype(I32)
    tile_slot = ((jnp.cumsum(tile_first) - 1) % 2).astype(I32)
    later_nonempty = (experts[None, :] > experts[:, None]) & (padded > 0)[None, :]
    next_nonempty = jnp.min(jnp.where(later_nonempty, experts[None, :], N_EXPERTS), axis=1)
    next_nonempty = jnp.where(next_nonempty == N_EXPERTS, -1, next_nonempty)
    tile_next = jnp.sum(jnp.where(tile_expert[:, None] == experts, next_nonempty, 0), axis=-1).astype(I32)

    n_tiles = n_tiles.reshape(1)
    xs = _dispatch(h2, pos[0], pos[1], (offs + counts).astype(I32), (padded - counts).astype(I32), n_tiles,
                   rows_out)
    ys = _experts(xs, tile_expert, tile_row, n_tiles, tile_first, tile_slot, tile_next,
                  w1.reshape(N_EXPERTS, d, D_EXPERT), w3.reshape(N_EXPERTS, d, D_EXPERT),
                  w2.reshape(N_EXPERTS, D_EXPERT, d))
    return _combine(x1, mod3, final_g, ys, pos[0], pos[1], cw[0].reshape(n, 1), cw[1].reshape(n, 1), seq)


def kernel(x, c, ada_w, ada_b, norm1_g, w_in, gmlp_ln_g, gmlp_ln_b, gmlp_w_s, gmlp_b_s, rel_bias, w_branch_a,
           w_branch_b, w_out, norm2_g, w_group, w_expert, w1, w3, w2, final_g):
    b, s, d = x.shape
    out = _layer(x.reshape(b * s, d), c, s, ada_w[0], ada_b[0], norm1_g[0], w_in[0], gmlp_ln_g[0], gmlp_ln_b[0],
                 gmlp_w_s[0], gmlp_b_s[0], rel_bias[0], w_branch_a[0], w_branch_b[0], w_out[0], norm2_g[0],
                 w_group[0], w_expert[0], w1[0], w3[0], w2[0], final_g)
    return out.reshape(b, s, d)
```

```python
import functools

import numpy as np
import jax
import jax.numpy as jnp
from jax import lax
from jax.experimental import pallas as pl
from jax.experimental.pallas import tpu as pltpu

F32 = jnp.float32
BF16 = jnp.bfloat16
I32 = jnp.int32

D_MODEL = 2048
CHUNK = 64
EPS = 1e-6
NEG_INF = -1e30
LOG2E = float(np.log2(np.e))
GMLP_BLOCK = 128
GMLP_GROUPS = 8
GMLP_WIDTH = 1024
ATT_HEADS = 16
ATT_HEAD_DIM = 64
ATT_WIDTH = ATT_HEADS * ATT_HEAD_DIM
LEFT_CHUNKS = 8
MAX_REL = 256
N_GROUPS = 4
EXPERTS_PER_GROUP = 8
N_EXPERTS = N_GROUPS * EXPERTS_PER_GROUP
D_EXPERT = 512
N_MOD = 6

LANES = 128
SUBLANES = 8
VMEM_LIMIT = 56 * 1024 * 1024

COL_TILE = 1024
PROJ_COLS = 2 * D_MODEL + 2 * GMLP_WIDTH + 3 * ATT_WIDTH
GATE_TILES = 2 * D_MODEL // COL_TILE
U_TILE = GATE_TILES
V_TILE = GATE_TILES + 1
Q_TILE = GATE_TILES + 2
K_TILE = GATE_TILES + 3
VB_TILE = GATE_TILES + 4

ROUTER_ROWS = SUBLANES + N_EXPERTS

ATT_QBLK = 1024
ATT_PREV = LEFT_CHUNKS * CHUNK
ATT_SUB = 2 * CHUNK
ATT_NSUB = ATT_QBLK // ATT_SUB
ATT_NFIRST = ATT_PREV // ATT_SUB
ATT_PAIRS = 2
ATT_WIN = ATT_SUB + LEFT_CHUNKS * CHUNK

EXPERT_TILE = 256
TOKEN_ROWS = D_MODEL // LANES


def _params(sem, vmem=VMEM_LIMIT):
    return pltpu.CompilerParams(dimension_semantics=sem, vmem_limit_bytes=vmem)


def _rms_mod(x, g, scale, shift):
    y = x * lax.rsqrt(jnp.mean(x * x, axis=-1, keepdims=True) + EPS)
    return (y * g) * (1.0 + scale) + shift


def _ada_kernel(cb_ref, w_ref, b_ref, o_ref, s_scr):
    nb, d, _ = cb_ref.shape
    tn = w_ref.shape[1]
    reps = tn // LANES

    @pl.when(pl.program_id(0) == 0)
    def _():
        cb = cb_ref[...]
        s_scr[...] = cb * jax.nn.sigmoid(cb)

    def body(i, accs):
        r = pl.ds(pl.multiple_of(i * SUBLANES, SUBLANES), SUBLANES)
        w8 = w_ref[r, :]
        return tuple(accs[b] + w8 * jnp.concatenate([s_scr[b, r, :]] * reps, axis=1) for b in range(nb))

    init = tuple(jnp.zeros((SUBLANES, tn), F32) for _ in range(nb))
    accs = lax.fori_loop(0, d // SUBLANES, body, init, unroll=4)
    for b in range(nb):
        o_ref[b:b + 1, :] = jnp.sum(accs[b], axis=0, keepdims=True) + b_ref[...]


ADA_TN = 1024


def _ada(c, ada_w, ada_b):
    nb, d = c.shape
    n = ada_w.shape[1]
    tn = ADA_TN
    cb = jnp.broadcast_to(c[:, :, None], (nb, d, LANES))
    return pl.pallas_call(
        _ada_kernel,
        grid=(n // tn,),
        in_specs=[pl.BlockSpec((nb, d, LANES), lambda j: (0, 0, 0)),
                  pl.BlockSpec((d, tn), lambda j: (0, j)),
                  pl.BlockSpec((1, tn), lambda j: (0, j))],
        out_specs=pl.BlockSpec((nb, tn), lambda j: (0, j)),
        out_shape=jax.ShapeDtypeStruct((nb, n), F32),
        scratch_shapes=[pltpu.VMEM((nb, d, LANES), F32)],
        compiler_params=_params(("arbitrary",)),
        name="ada",
    )(cb, ada_w, ada_b.reshape(1, n))


IN_TM = 1024
IN_STAT_RB = 128
IN_NORM_RB = 16


def _gelu(a):
    return 0.5 * a * (1.0 + lax.erf(a * np.float32(np.sqrt(0.5))))


def _sigmoid(a):
    return 0.5 * jnp.tanh(0.5 * a) + 0.5


def _in_proj_kernel(mod_ref, g_ref, x_ref, w_ref, o_ref, h_scr, gain_scr, shift_scr, inv_scr):
    j = pl.program_id(1)

    @pl.when(j == 0)
    def _():
        d = x_ref.shape[1]
        gain_scr[...] = jnp.broadcast_to(g_ref[...] * (1.0 + mod_ref[0, 1:2, :]), gain_scr.shape)
        shift_scr[...] = jnp.broadcast_to(mod_ref[0, 0:1, :], shift_scr.shape)

        def stats(rb, c):
            rows = pl.ds(pl.multiple_of(rb * IN_STAT_RB, IN_STAT_RB), IN_STAT_RB)
            sq = jnp.zeros((IN_STAT_RB, LANES), F32)
            for k in range(d // LANES):
                xk = x_ref[rows, k * LANES:(k + 1) * LANES]
                sq = sq + xk * xk
            inv = lax.rsqrt(jnp.sum(sq, axis=-1, keepdims=True) * (1.0 / d) + EPS)
            inv_scr[rows, :] = jnp.broadcast_to(inv, (IN_STAT_RB, LANES))
            return c
        lax.fori_loop(0, IN_TM // IN_STAT_RB, stats, 0, unroll=2)

        def apply(rb, c):
            rows = pl.ds(pl.multiple_of(rb * IN_NORM_RB, IN_NORM_RB), IN_NORM_RB)
            inv = inv_scr[rows, :]
            for k in range(d // LANES):
                cols = slice(k * LANES, (k + 1) * LANES)
                y = (x_ref[rows, cols] * inv) * gain_scr[:, cols] + shift_scr[:, cols]
                h_scr[rows, cols] = y.astype(BF16)
            return c
        lax.fori_loop(0, IN_TM // IN_NORM_RB, apply, 0, unroll=4)

    def run(epilogue):
        acc = jnp.dot(h_scr[...], w_ref[...], preferred_element_type=F32)
        o_ref[...] = epilogue(acc).astype(o_ref.dtype)

    @pl.when(j < GATE_TILES)
    def _():
        run(_sigmoid)

    @pl.when((j == U_TILE) | (j == V_TILE))
    def _():
        run(_gelu)

    @pl.when(j == Q_TILE)
    def _():
        run(lambda a: a * np.float32(ATT_HEAD_DIM ** -0.5 * LOG2E))

    @pl.when(j > Q_TILE)
    def _():
        run(lambda a: a)


def _in_proj(x2d, mod3, norm_g, w_in_bf16, seq):
    n, d = x2d.shape
    cols = w_in_bf16.shape[1]
    n_tiles = cols // COL_TILE
    tiles_per_seq = seq // IN_TM
    return pl.pallas_call(
        _in_proj_kernel,
        grid=(n // IN_TM, n_tiles),
        in_specs=[pl.BlockSpec((1, N_MOD, d), lambda i, j: (i // tiles_per_seq, 0, 0)),
                  pl.BlockSpec((1, d), lambda i, j: (0, 0)),
                  pl.BlockSpec((IN_TM, d), lambda i, j: (i, 0)),
                  pl.BlockSpec((d, COL_TILE), lambda i, j: (0, (j + n_tiles - GATE_TILES) % n_tiles))],
        out_specs=pl.BlockSpec((IN_TM, COL_TILE), lambda i, j: (i, j)),
        out_shape=jax.ShapeDtypeStruct((n, cols), BF16),
        scratch_shapes=[pltpu.VMEM((IN_TM, d), BF16), pltpu.VMEM((IN_NORM_RB, d), F32),
                        pltpu.VMEM((IN_NORM_RB, d), F32), pltpu.VMEM((IN_TM, LANES), F32)],
        compiler_params=_params(("arbitrary", "arbitrary")),
        name="in_proj",
    )(mod3, norm_g.reshape(1, d), x2d, w_in_bf16)


GM_TM = 1024


def _gmlp_kernel(u_ref, v_ref, lng_ref, lnb_ref, ws_ref, bs_ref, o_ref):
    t = lax.broadcasted_iota(I32, (GMLP_BLOCK, GMLP_BLOCK), 0)
    s = lax.broadcasted_iota(I32, (GMLP_BLOCK, GMLP_BLOCK), 1)
    causal = (s // CHUNK) <= (t // CHUNK)
    lng = lng_ref[...]
    lnb = lnb_ref[...]
    for blk in range(GM_TM // GMLP_BLOCK):
        rows = slice(blk * GMLP_BLOCK, (blk + 1) * GMLP_BLOCK)
        v = v_ref[rows, :].astype(F32)
        mu = jnp.mean(v, axis=-1, keepdims=True)
        vc = v - mu
        var = jnp.mean(vc * vc, axis=-1, keepdims=True)
        vln = ((vc * lax.rsqrt(var + EPS)) * lng + lnb).astype(BF16)
        for g in range(GMLP_GROUPS):
            cols = slice(g * LANES, (g + 1) * LANES)
            w = jnp.where(causal, ws_ref[g], 0.0).astype(BF16)
            mixed = jnp.dot(w, vln[:, cols], preferred_element_type=F32) + bs_ref[g]
            o_ref[rows, cols] = (u_ref[rows, cols].astype(F32) * mixed).astype(o_ref.dtype)


def _gmlp(proj, ln_g, ln_b, w_s, b_s):
    n = proj.shape[0]
    return pl.pallas_call(
        _gmlp_kernel,
        grid=(n // GM_TM,),
        in_specs=[pl.BlockSpec((GM_TM, GMLP_WIDTH), lambda i: (i, U_TILE)),
                  pl.BlockSpec((GM_TM, GMLP_WIDTH), lambda i: (i, V_TILE)),
                  pl.BlockSpec((1, GMLP_WIDTH), lambda i: (0, 0)),
                  pl.BlockSpec((1, GMLP_WIDTH), lambda i: (0, 0)),
                  pl.BlockSpec((GMLP_GROUPS, GMLP_BLOCK, GMLP_BLOCK), lambda i: (0, 0, 0)),
                  pl.BlockSpec((GMLP_GROUPS, GMLP_BLOCK, 1), lambda i: (0, 0, 0))],
        out_specs=pl.BlockSpec((GM_TM, GMLP_WIDTH), lambda i: (i, 0)),
        out_shape=jax.ShapeDtypeStruct((n, GMLP_WIDTH), BF16),
        compiler_params=_params(("arbitrary",)),
        name="gmlp",
    )(proj, proj, ln_g.reshape(1, -1), ln_b.reshape(1, -1), w_s,
      b_s.reshape(GMLP_GROUPS, GMLP_BLOCK, 1))


def _band_bias(rel_table):
    heads = rel_table.shape[0]
    r = np.arange(ATT_SUB)[:, None]
    w = np.arange(ATT_WIN)[None, :]
    j = w // CHUNK - r // CHUNK
    in_band = (j >= 0) & (j <= LEFT_CHUNKS)
    a = np.arange(ATT_NFIRST)[:, None, None]
    in_seq = np.broadcast_to(w[None] >= ATT_PREV - a * ATT_SUB, (ATT_NFIRST, ATT_SUB, ATT_WIN))
    visible = np.concatenate([in_band[None], in_band[None] & in_seq])
    far = LEFT_CHUNKS * CHUNK + ATT_SUB - 1
    n_clipped = far - MAX_REL + 1
    table = rel_table.astype(F32) * np.float32(LOG2E)
    lo = MAX_REL - (ATT_WIN - 1 - LEFT_CHUNKS * CHUNK)
    diag = jnp.concatenate([jnp.broadcast_to(table[:, 2 * MAX_REL:], (heads, n_clipped)),
                            jnp.flip(table[:, lo:2 * MAX_REL], axis=1),
                            jnp.zeros((heads, 1), F32)], axis=1)
    mask = jnp.asarray(np.where(visible, 0.0, NEG_INF).astype(np.float32))
    return diag.reshape(heads // (2 * ATT_PAIRS), 2 * ATT_PAIRS, diag.shape[1]), mask


def _attn_kernel(q_ref, kp_ref, kc_ref, vp_ref, vc_ref, diag_ref, mask_ref, o_ref, k_scr, v_scr, s_scr, bias_ref,
                 *, blocks_per_seq):
    first = pl.program_id(1) % blocks_per_seq == 0

    @pl.when(pl.program_id(1) == 0)
    def _():
        span = diag_ref.shape[2]
        for hd in range(2 * ATT_PAIRS):
            rows = jnp.broadcast_to(diag_ref[0, hd:hd + 1, :], (ATT_SUB, span))
            tile = pltpu.roll(rows, span - ATT_SUB + 1, 1, stride=1, stride_axis=0)[:, :ATT_WIN]
            for slab in range(1 + ATT_NFIRST):
                bias_ref[hd, slab] = tile + mask_ref[slab]
    k_scr[0:ATT_PREV, :] = kp_ref[...]
    k_scr[ATT_PREV:, :] = kc_ref[...]
    v_scr[0:ATT_PREV, :] = vp_ref[...]
    v_scr[ATT_PREV:, :] = vc_ref[...]
    lane = lax.broadcasted_iota(I32, (ATT_SUB, LANES), 1)
    low = lane < ATT_HEAD_DIM

    units = [(pair, a) for pair in range(ATT_PAIRS) for a in range(ATT_NSUB)]
    for u, (pair, a) in enumerate(units):
        lanes = slice(pair * LANES, (pair + 1) * LANES)
        q = q_ref[a * ATT_SUB:(a + 1) * ATT_SUB, lanes]
        zero = jnp.zeros_like(q)
        q2 = jnp.concatenate([jnp.where(low, q, zero), jnp.where(low, zero, q)], axis=0)
        win = slice(a * ATT_SUB, a * ATT_SUB + ATT_WIN)
        s = lax.dot_general(q2, k_scr[win, lanes], (((1,), (1,)), ((), ())), preferred_element_type=F32)
        slab = jnp.where(first, a + 1, 0) if a < ATT_NFIRST else 0
        bias = bias_ref[2 * pair:2 * pair + 2, pl.ds(slab, 1)]
        s_scr[u] = s + bias.reshape(2 * ATT_SUB, ATT_WIN)

    for u, (pair, a) in enumerate(units):
        lanes = slice(pair * LANES, (pair + 1) * LANES)
        rows = slice(a * ATT_SUB, (a + 1) * ATT_SUB)
        win = slice(a * ATT_SUB, a * ATT_SUB + ATT_WIN)
        s = s_scr[u]
        m = jnp.max(s, axis=-1, keepdims=True)
        p = jnp.exp2(s - m)
        l = jnp.sum(p, axis=-1, keepdims=True)
        pv = jnp.dot(p.astype(BF16), v_scr[win, lanes], preferred_element_type=F32) / l
        o_ref[rows, lanes] = jnp.where(low, pv[:ATT_SUB], pv[ATT_SUB:]).astype(o_ref.dtype)


def _attn(proj, rel_table, seq):
    n = proj.shape[0]
    bps = seq // ATT_QBLK
    pairs = ATT_WIDTH // LANES
    width = ATT_PAIRS * LANES
    qc, kc, vc = (Q_TILE * COL_TILE // width, K_TILE * COL_TILE // width, VB_TILE * COL_TILE // width)
    diag, mask = _band_bias(rel_table)

    ratio = ATT_QBLK // ATT_PREV

    def prev(i):
        return jnp.where(i % bps == 0, i * ratio, i * ratio - 1)

    blk = (ATT_QBLK, width)
    pblk = (ATT_PREV, width)
    return pl.pallas_call(
        functools.partial(_attn_kernel, blocks_per_seq=bps),
        grid=(pairs // ATT_PAIRS, n // ATT_QBLK),
        in_specs=[pl.BlockSpec(blk, lambda h, i: (i, qc + h)),
                  pl.BlockSpec(pblk, lambda h, i: (prev(i), kc + h)),
                  pl.BlockSpec(blk, lambda h, i: (i, kc + h)),
                  pl.BlockSpec(pblk, lambda h, i: (prev(i), vc + h)),
                  pl.BlockSpec(blk, lambda h, i: (i, vc + h)),
                  pl.BlockSpec((1,) + diag.shape[1:], lambda h, i: (h, 0, 0)),
                  pl.BlockSpec(mask.shape, lambda h, i: (0, 0, 0))],
        out_specs=pl.BlockSpec(blk, lambda h, i: (i, h)),
        out_shape=jax.ShapeDtypeStruct((n, ATT_WIDTH), BF16),
        scratch_shapes=[pltpu.VMEM((ATT_PREV + ATT_QBLK, width), BF16),
                        pltpu.VMEM((ATT_PREV + ATT_QBLK, width), BF16),
                        pltpu.VMEM((ATT_PAIRS * ATT_NSUB, 2 * ATT_SUB, ATT_WIN), F32),
                        pltpu.VMEM((2 * ATT_PAIRS, 1 + ATT_NFIRST, ATT_SUB, ATT_WIN), F32)],
        compiler_params=_params(("arbitrary", "arbitrary")),
        name="attn",
    )(proj, proj, proj, proj, proj, diag, mask)


MG_TM = 256


def _merge_kernel(mod_ref, g_ref, x_ref, ya_ref, yb_ref, ga_ref, gb_ref, wa_ref, wb_ref, wo_ref, wr_ref,
                  x1_ref, h2_ref, lt_ref, mixed_even, mixed_odd):
    i = pl.program_id(0)

    @pl.when(i == 0)
    def _():
        mixed_odd[...] = jnp.zeros_like(mixed_odd)

    def run(mixed_new, mixed_done):
        ya = jnp.dot(ya_ref[...], wa_ref[...], preferred_element_type=F32)
        yb = jnp.dot(yb_ref[...], wb_ref[...], preferred_element_type=F32)
        m = ga_ref[...].astype(F32) * ya + gb_ref[...].astype(F32) * yb
        mixed_new[...] = jnp.dot(m.astype(BF16), wo_ref[...], preferred_element_type=F32)

        gate1 = mod_ref[0, 2:3, :]
        shift2 = mod_ref[0, 3:4, :]
        scale2 = mod_ref[0, 4:5, :]
        x1 = x_ref[...] + gate1 * mixed_done[...]
        x1_ref[...] = x1
        h2 = _rms_mod(x1, g_ref[...], scale2, shift2).astype(BF16)
        h2_ref[...] = h2.reshape(h2_ref.shape)
        lt_ref[...] = lax.dot_general(wr_ref[...], h2, (((1,), (1,)), ((), ())), preferred_element_type=F32)

    @pl.when(i % 2 == 0)
    def _():
        run(mixed_even, mixed_odd)

    @pl.when(i % 2 == 1)
    def _():
        run(mixed_odd, mixed_even)


def _merge(x2d, mod3, norm_g, ya, yb, proj, wa, wb, wo, wr_t, seq):
    n, d = x2d.shape
    tps = seq // MG_TM
    tiles = n // MG_TM
    const = lambda shape: pl.BlockSpec(shape, lambda i: (0,) * len(shape), pipeline_mode=pl.Buffered(1))

    def cur(i):
        return jnp.minimum(i, tiles - 1)

    def done(i):
        return jnp.maximum(i - 1, 0)

    return pl.pallas_call(
        _merge_kernel,
        grid=(tiles + 1,),
        in_specs=[pl.BlockSpec((1, N_MOD, d), lambda i: (done(i) // tps, 0, 0)),
                  pl.BlockSpec((1, d), lambda i: (0, 0)),
                  pl.BlockSpec((MG_TM, d), lambda i: (done(i), 0)),
                  pl.BlockSpec((MG_TM, GMLP_WIDTH), lambda i: (cur(i), 0)),
                  pl.BlockSpec((MG_TM, ATT_WIDTH), lambda i: (cur(i), 0)),
                  pl.BlockSpec((MG_TM, d), lambda i: (cur(i), 0)),
                  pl.BlockSpec((MG_TM, d), lambda i: (cur(i), 1)),
                  const((GMLP_WIDTH, d)), const((ATT_WIDTH, d)), const((d, d)),
                  const((ROUTER_ROWS, d))],
        out_specs=[pl.BlockSpec((MG_TM, d), lambda i: (done(i), 0)),
                   pl.BlockSpec((MG_TM, TOKEN_ROWS, LANES), lambda i: (done(i), 0, 0)),
                   pl.BlockSpec((ROUTER_ROWS, MG_TM), lambda i: (0, done(i)))],
        out_shape=[jax.ShapeDtypeStruct((n, d), F32),
                   jax.ShapeDtypeStruct((n, TOKEN_ROWS, LANES), BF16),
                   jax.ShapeDtypeStruct((ROUTER_ROWS, n), F32)],
        scratch_shapes=[pltpu.VMEM((MG_TM, d), F32), pltpu.VMEM((MG_TM, d), F32)],
        compiler_params=_params(("arbitrary",)),
        name="merge",
    )(mod3, norm_g.reshape(1, d), x2d, ya, yb, proj, proj, wa, wb, wo, wr_t)


RT_TN = 1024


def _first_argmax(vals, vmax, nrows):
    rows = lax.broadcasted_iota(I32, vals.shape, 0)
    return jnp.min(jnp.where(vals == vmax, rows, nrows), axis=0, keepdims=True)


def _route_kernel(lt_ref, e_ref, r_ref, w_ref, cnt_ref, carry_scr):
    @pl.when(pl.program_id(0) == 0)
    def _():
        carry_scr[...] = jnp.zeros_like(carry_scr)

    gl = lt_ref[0:N_GROUPS, :]
    gmax = jnp.max(gl, axis=0, keepdims=True)
    gidx = _first_argmax(gl, gmax, N_GROUPS)
    gw = 1.0 / jnp.sum(jnp.exp(gl - gmax), axis=0, keepdims=True)

    esel = lt_ref[SUBLANES:SUBLANES + EXPERTS_PER_GROUP, :]
    for g in range(1, N_GROUPS):
        lo = SUBLANES + g * EXPERTS_PER_GROUP
        esel = jnp.where(gidx == g, lt_ref[lo:lo + EXPERTS_PER_GROUP, :], esel)
    rows8 = lax.broadcasted_iota(I32, esel.shape, 0)
    m1 = jnp.max(esel, axis=0, keepdims=True)
    i1 = _first_argmax(esel, m1, EXPERTS_PER_GROUP)
    rest = jnp.where(rows8 == i1, -jnp.inf, esel)
    m2 = jnp.max(rest, axis=0, keepdims=True)
    i2 = _first_argmax(rest, m2, EXPERTS_PER_GROUP)
    z = jnp.exp(m2 - m1)
    w_top = 1.0 / (1.0 + z)
    e0 = gidx * EXPERTS_PER_GROUP + i1
    e1 = gidx * EXPERTS_PER_GROUP + i2
    e_ref[0:1, :] = e0
    e_ref[1:2, :] = e1
    w_ref[0:1, :] = gw * w_top
    w_ref[1:2, :] = gw * (z * w_top)

    rows_e = lax.broadcasted_iota(I32, (N_EXPERTS, RT_TN), 0)
    oh0 = rows_e == e0
    oh1 = rows_e == e1
    oh = jnp.where(oh0 | oh1, 1.0, 0.0)
    src = lax.broadcasted_iota(I32, (RT_TN, RT_TN), 0)
    dst = lax.broadcasted_iota(I32, (RT_TN, RT_TN), 1)
    before = jnp.where(src < dst, 1.0, 0.0).astype(BF16)
    carry = carry_scr[...]
    prefix = jnp.dot(oh.astype(BF16), before, preferred_element_type=F32) + carry[:, 0:1]
    r_ref[0:1, :] = jnp.sum(jnp.where(oh0, prefix, 0.0), axis=0, keepdims=True).astype(I32)
    r_ref[1:2, :] = jnp.sum(jnp.where(oh1, prefix, 0.0), axis=0, keepdims=True).astype(I32)
    carry = carry + jnp.sum(oh, axis=1, keepdims=True)
    carry_scr[...] = carry
    cnt_ref[...] = carry.astype(I32)


def _route(logits_t):
    n = logits_t.shape[1]
    slot = pl.BlockSpec((2, RT_TN), lambda i: (0, i))
    return pl.pallas_call(
        _route_kernel,
        grid=(n // RT_TN,),
        in_specs=[pl.BlockSpec((ROUTER_ROWS, RT_TN), lambda i: (0, i))],
        out_specs=[slot, slot, slot, pl.BlockSpec((N_EXPERTS, LANES), lambda i: (0, 0))],
        out_shape=[jax.ShapeDtypeStruct((2, n), I32),
                   jax.ShapeDtypeStruct((2, n), I32),
                   jax.ShapeDtypeStruct((2, n), F32),
                   jax.ShapeDtypeStruct((N_EXPERTS, LANES), I32)],
        scratch_shapes=[pltpu.VMEM((N_EXPERTS, LANES), F32)],
        compiler_params=_params(("arbitrary",)),
        name="route",
    )(logits_t)


DP_TM = 4096
DMA_UNROLL = 8
PAD_CHUNK = 32


def _token_copy(src_ref, src_row, dst_ref, dst_row, sem):
    return pltpu.make_async_copy(src_ref.at[src_row], dst_ref.at[dst_row], sem)


def _dispatch_kernel(pos0_ref, pos1_ref, pad_start_ref, pad_n_ref, nt_ref, h_ref, xs_ref, zero_scr, sem, zsem, tsem,
                     csem):
    i = pl.program_id(0)
    n_tiles_max = xs_ref.shape[0] // EXPERT_TILE

    def tail_copy(t):
        return pltpu.make_async_copy(zero_scr, xs_ref.at[pl.ds(t * EXPERT_TILE, EXPERT_TILE)], tsem)

    @pl.when(i == 0)
    def _():
        zero_scr[...] = jnp.zeros_like(zero_scr)

        def chunk_copy(row):
            return pltpu.make_async_copy(zero_scr.at[pl.ds(0, PAD_CHUNK)], xs_ref.at[pl.ds(row, PAD_CHUNK)], csem)

        for e in range(N_EXPERTS):
            start = pad_start_ref[e]
            n_chunks = pad_n_ref[e] // PAD_CHUNK

            def issue_chunk(k, c, start=start):
                chunk_copy(start + k * PAD_CHUNK).start()
                return c
            lax.fori_loop(0, n_chunks, issue_chunk, 0)

            def issue(r, c, start=start):
                _token_copy(zero_scr, 0, xs_ref, start + r, zsem).start()
                return c
            lax.fori_loop(n_chunks * PAD_CHUNK, pad_n_ref[e], issue, 0)

        def issue_tail(t, c):
            tail_copy(t).start()
            return c
        lax.fori_loop(nt_ref[0], n_tiles_max, issue_tail, 0)

        for e in range(N_EXPERTS):
            n_chunks = pad_n_ref[e] // PAD_CHUNK

            def drain_chunk(k, c):
                chunk_copy(0).wait()
                return c
            lax.fori_loop(0, n_chunks, drain_chunk, 0)

            def drain(r, c):
                _token_copy(zero_scr, 0, xs_ref, 0, zsem).wait()
                return c
            lax.fori_loop(n_chunks * PAD_CHUNK, pad_n_ref[e], drain, 0)

        def drain_tail(t, c):
            tail_copy(t).wait()
            return c
        lax.fori_loop(nt_ref[0], n_tiles_max, drain_tail, 0)

    base = i * DP_TM

    def issue(r, c):
        _token_copy(h_ref, r, xs_ref, pos0_ref[base + r], sem).start(priority=0)
        _token_copy(h_ref, r, xs_ref, pos1_ref[base + r], sem).start(priority=1)
        return c
    lax.fori_loop(0, DP_TM, issue, 0, unroll=DMA_UNROLL)

    for _ in range(2):
        pltpu.make_async_copy(h_ref, xs_ref.at[pl.ds(0, DP_TM)], sem).wait()


def _dispatch(h2, pos0, pos1, pad_start, pad_n, n_tiles, rows_out):
    n = h2.shape[0]
    grid_spec = pltpu.PrefetchScalarGridSpec(
        num_scalar_prefetch=5,
        grid=(n // DP_TM,),
        in_specs=[pl.BlockSpec((DP_TM, TOKEN_ROWS, LANES), lambda i, *_: (i, 0, 0))],
        out_specs=pl.BlockSpec(memory_space=pl.ANY),
        scratch_shapes=[pltpu.VMEM((EXPERT_TILE, TOKEN_ROWS, LANES), BF16),
                        pltpu.SemaphoreType.DMA(()), pltpu.SemaphoreType.DMA(()),
                        pltpu.SemaphoreType.DMA(()), pltpu.SemaphoreType.DMA(())],
    )
    return pl.pallas_call(
        _dispatch_kernel,
        grid_spec=grid_spec,
        out_shape=jax.ShapeDtypeStruct((rows_out, TOKEN_ROWS, LANES), BF16),
        compiler_params=pltpu.CompilerParams(dimension_semantics=("arbitrary",),
                                             vmem_limit_bytes=VMEM_LIMIT, has_side_effects=True),
        name="dispatch",
    )(pos0, pos1, pad_start, pad_n, n_tiles, h2)


EXPERT_CAST_ROWS = 256


def _cast_rows(src_ref, dst_ref):
    rows_total = dst_ref.shape[0]

    def body(r, c):
        rows = pl.ds(pl.multiple_of(r * EXPERT_CAST_ROWS, EXPERT_CAST_ROWS), EXPERT_CAST_ROWS)
        dst_ref[rows, :] = src_ref[rows, :].astype(BF16)
        return c
    lax.fori_loop(0, rows_total // EXPERT_CAST_ROWS, body, 0)


def _experts_kernel(te_ref, tr_ref, nt_ref, first_ref, slot_ref, next_ref, xs_ref, w1_hbm, w3_hbm, w2_hbm, ys_ref,
                    w1_stage, w3_stage, w2_stage, w1_scr, w3_scr, w2_scr, sems):
    i = pl.program_id(0)

    def fetch(expert, slot):
        return (pltpu.make_async_copy(w1_hbm.at[expert], w1_stage.at[slot], sems.at[slot]),
                pltpu.make_async_copy(w3_hbm.at[expert], w3_stage.at[slot], sems.at[slot]),
                pltpu.make_async_copy(w2_hbm.at[expert], w2_stage.at[slot], sems.at[slot]))

    @pl.when(i == 0)
    def _():
        for copy in fetch(te_ref[0], 0):
            copy.start()

    @pl.when(first_ref[i] == 1)
    def _():
        slot = slot_ref[i]
        for copy in fetch(te_ref[i], slot):
            copy.wait()

        @pl.when(next_ref[i] >= 0)
        def _():
            for copy in fetch(next_ref[i], 1 - slot):
                copy.start()

        _cast_rows(w1_stage.at[slot], w1_scr)
        _cast_rows(w3_stage.at[slot], w3_scr)
        _cast_rows(w2_stage.at[slot], w2_scr)

    @pl.when(i < nt_ref[0])
    def _():
        x = xs_ref[...].reshape(EXPERT_TILE, D_MODEL)
        a = jnp.dot(x, w1_scr[...], preferred_element_type=F32)
        b = jnp.dot(x, w3_scr[...], preferred_element_type=F32)
        act = (a * jax.nn.sigmoid(a)) * b
        y = jnp.dot(act.astype(BF16), w2_scr[...], preferred_element_type=F32)
        ys_ref[...] = y.astype(BF16).reshape(ys_ref.shape)

    @pl.when(i >= nt_ref[0])
    def _():
        ys_ref[...] = jnp.zeros_like(ys_ref)


def _experts(xs, tile_expert, tile_row, n_tiles, tile_first, tile_slot, tile_next, w1, w3, w2):
    rows = xs.shape[0]
    _, d, f = w1.shape
    tile = (EXPERT_TILE, TOKEN_ROWS, LANES)
    hbm = pl.BlockSpec(memory_space=pl.ANY)
    grid_spec = pltpu.PrefetchScalarGridSpec(
        num_scalar_prefetch=6,
        grid=(rows // EXPERT_TILE,),
        in_specs=[pl.BlockSpec(tile, lambda i, te, tr, *_: (tr[i], 0, 0)), hbm, hbm, hbm],
        out_specs=pl.BlockSpec(tile, lambda i, *_: (i, 0, 0)),
        scratch_shapes=[pltpu.VMEM((2, d, f), F32), pltpu.VMEM((2, d, f), F32), pltpu.VMEM((2, f, d), F32),
                        pltpu.VMEM((d, f), BF16), pltpu.VMEM((d, f), BF16), pltpu.VMEM((f, d), BF16),
                        pltpu.SemaphoreType.DMA((2,))],
    )
    return pl.pallas_call(
        _experts_kernel,
        grid_spec=grid_spec,
        out_shape=jax.ShapeDtypeStruct(xs.shape, BF16),
        compiler_params=_params(("arbitrary",)),
        name="experts",
    )(tile_expert, tile_row, n_tiles, tile_first, tile_slot, tile_next, xs, w1, w3, w2)


CB_TM = 256
CB_RING = 3


def _combine_kernel(pos0_ref, pos1_ref, mod_ref, fg_ref, x1_ref, w0_ref, w1_ref, ys_ref, o_ref,
                    y0_scr, y1_scr, sems):
    i = pl.program_id(0)
    steps = pl.num_programs(0)
    slot = i % CB_RING

    def gather(step, buf):
        base = step * CB_TM

        def issue(r, c):
            _token_copy(ys_ref, pos0_ref[base + r], y0_scr.at[buf], r, sems.at[buf]).start(priority=0)
            _token_copy(ys_ref, pos1_ref[base + r], y1_scr.at[buf], r, sems.at[buf]).start(priority=1)
            return c
        lax.fori_loop(0, CB_TM, issue, 0, unroll=DMA_UNROLL)

    @pl.when(i == 0)
    def _():
        for ahead in range(CB_RING - 1):
            gather(ahead, ahead)

    @pl.when(i + CB_RING - 1 < steps)
    def _():
        gather(i + CB_RING - 1, (i + CB_RING - 1) % CB_RING)

    pltpu.make_async_copy(ys_ref.at[pl.ds(0, CB_TM)], y0_scr.at[slot], sems.at[slot]).wait()
    pltpu.make_async_copy(ys_ref.at[pl.ds(0, CB_TM)], y1_scr.at[slot], sems.at[slot]).wait()

    gate2 = mod_ref[0, 5:6, :]
    y0 = y0_scr[slot].reshape(CB_TM, D_MODEL).astype(F32)
    y1 = y1_scr[slot].reshape(CB_TM, D_MODEL).astype(F32)
    y = w0_ref[...] * y0 + w1_ref[...] * y1
    x2 = x1_ref[...] + gate2 * y
    o_ref[...] = (x2 * lax.rsqrt(jnp.mean(x2 * x2, axis=-1, keepdims=True) + EPS)) * fg_ref[...]


def _combine(x1, mod3, final_g, ys, pos0, pos1, cw0, cw1, seq):
    n, d = x1.shape
    tps = seq // CB_TM
    grid_spec = pltpu.PrefetchScalarGridSpec(
        num_scalar_prefetch=2,
        grid=(n // CB_TM,),
        in_specs=[pl.BlockSpec((1, N_MOD, d), lambda i, *_: (i // tps, 0, 0)),
                  pl.BlockSpec((1, d), lambda i, *_: (0, 0)),
                  pl.BlockSpec((CB_TM, d), lambda i, *_: (i, 0)),
                  pl.BlockSpec((CB_TM, 1), lambda i, *_: (i, 0)),
                  pl.BlockSpec((CB_TM, 1), lambda i, *_: (i, 0)),
                  pl.BlockSpec(memory_space=pl.ANY)],
        out_specs=pl.BlockSpec((CB_TM, d), lambda i, *_: (i, 0)),
        scratch_shapes=[pltpu.VMEM((CB_RING, CB_TM, TOKEN_ROWS, LANES), BF16),
                        pltpu.VMEM((CB_RING, CB_TM, TOKEN_ROWS, LANES), BF16),
                        pltpu.SemaphoreType.DMA((CB_RING,))],
    )
    return pl.pallas_call(
        _combine_kernel,
        grid_spec=grid_spec,
        out_shape=jax.ShapeDtypeStruct((n, d), F32),
        compiler_params=_params(("arbitrary",)),
        name="combine",
    )(pos0, pos1, mod3, final_g.reshape(1, d), x1, cw0, cw1, ys)


def _layer(x2d, c, seq, ada_w, ada_b, norm1_g, w_in, gmlp_ln_g, gmlp_ln_b, gmlp_w_s, gmlp_b_s, rel_bias,
           w_branch_a, w_branch_b, w_out, norm2_g, w_group, w_expert, w1, w3, w2, final_g):
    n, d = x2d.shape
    nb = c.shape[0]
    mod3 = _ada(c, ada_w, ada_b).reshape(nb, N_MOD, d)

    proj = _in_proj(x2d, mod3, norm1_g, w_in.astype(BF16), seq)
    ya = _gmlp(proj, gmlp_ln_g, gmlp_ln_b, gmlp_w_s, gmlp_b_s)
    yb = _attn(proj, rel_bias, seq)

    wr_t = jnp.concatenate([w_group.T, jnp.zeros((SUBLANES - N_GROUPS, d), F32),
                            w_expert.transpose(0, 2, 1).reshape(N_EXPERTS, d)], axis=0).astype(BF16)
    x1, h2, logits_t = _merge(x2d, mod3, norm2_g, ya, yb, proj, w_branch_a.astype(BF16),
                              w_branch_b.astype(BF16), w_out.astype(BF16), wr_t, seq)

    eidx, rank, cw, counts = _route(logits_t)
    counts = counts[:, 0]
    padded = ((counts + EXPERT_TILE - 1) // EXPERT_TILE) * EXPERT_TILE
    ends = jnp.cumsum(padded)
    offs = ends - padded
    experts = jnp.arange(N_EXPERTS, dtype=I32)
    pos = jnp.sum(jnp.where(eidx[:, :, None] == experts, offs, 0), axis=-1) + rank
    rows_out = 2 * n + N_EXPERTS * EXPERT_TILE
    n_tiles_max = rows_out // EXPERT_TILE
    n_tiles = (ends[-1] // EXPERT_TILE).astype(I32)
    tile_row = jnp.minimum(jnp.arange(n_tiles_max, dtype=I32), n_tiles - 1)
    tile_expert = jnp.minimum(jnp.sum(ends[None, :] <= (tile_row * EXPERT_TILE)[:, None], axis=-1),
                              N_EXPERTS - 1).astype(I32)

    tiles = jnp.arange(n_tiles_max, dtype=I32)
    prev_expert = jnp.concatenate([jnp.full((1,), -1, I32), tile_expert[:-1]])
    tile_first = ((tiles < n_tiles) & (tile_expert != prev_expert)).astype(I32)
    tile_slot = ((jnp.cumsum(tile_first) - 1) % 2).astype(I32)
    later_nonempty = (experts[None, :] > experts[:, None]) & (padded > 0)[None, :]
    next_nonempty = jnp.min(jnp.where(later_nonempty, experts[None, :], N_EXPERTS), axis=1)
    next_nonempty = jnp.where(next_nonempty == N_EXPERTS, -1, next_nonempty)
    tile_next = jnp.sum(jnp.where(tile_expert[:, None] == experts, next_nonempty, 0), axis=-1).astype(I32)

    n_tiles = n_tiles.reshape(1)
    xs = _dispatch(h2, pos[0], pos[1], (offs + counts).astype(I32), (padded - counts).astype(I32), n_tiles,
                   rows_out)
    ys = _experts(xs, tile_expert, tile_row, n_tiles, tile_first, tile_slot, tile_next,
                  w1.reshape(N_EXPERTS, d, D_EXPERT), w3.reshape(N_EXPERTS, d, D_EXPERT),
                  w2.reshape(N_EXPERTS, D_EXPERT, d))
    return _combine(x1, mod3, final_g, ys, pos[0], pos[1], cw[0].reshape(n, 1), cw[1].reshape(n, 1), seq)


def kernel(x, c, ada_w, ada_b, norm1_g, w_in, gmlp_ln_g, gmlp_ln_b, gmlp_w_s, gmlp_b_s, rel_bias, w_branch_a,
           w_branch_b, w_out, norm2_g, w_group, w_expert, w1, w3, w2, final_g):
    b, s, d = x.shape
    out = _layer(x.reshape(b * s, d), c, s, ada_w[0], ada_b[0], norm1_g[0], w_in[0], gmlp_ln_g[0], gmlp_ln_b[0],
                 gmlp_w_s[0], gmlp_b_s[0], rel_bias[0], w_branch_a[0], w_branch_b[0], w_out[0], norm2_g[0],
                 w_group[0], w_expert[0], w1[0], w3[0], w2[0], final_g)
    return out.reshape(b, s, d)
```
